```python
import jax, jax.numpy as jnp
from jax import lax
import numpy as np

D_MODEL = 1024
BATCH = 8
SEQ = 4096
DEPTH = 1

LRU_WIDTH = D_MODEL
LRU_BLOCKS = 16
LRU_BLOCK = LRU_WIDTH // LRU_BLOCKS
CONV_WIDTH = 4
LRU_C = 8.0
ATT_HEADS = 8
ATT_HEAD_DIM = D_MODEL // ATT_HEADS
ATT_KV_HEADS = 2
ATT_GROUP = ATT_HEADS // ATT_KV_HEADS
IDX_HEADS = 8
IDX_DIM = 64
TOPK_MAX = 256
Q_BLOCK = 128
MEM_HEADS = 4
MEM_HEAD_DIM = D_MODEL // MEM_HEADS
N_BRANCH = 3
PEER_HEADS = 8
PEER_KEYS = 128
PEER_EXPERTS = PEER_KEYS * PEER_KEYS
PEER_QDIM = 256
PEER_SUBDIM = PEER_QDIM // 2
PEER_TOPK = 16
PEER_CHUNK = 128
ROPE_THETA = 10000.0
EPS = 1e-6

SPLIT_SIZES = (LRU_WIDTH, LRU_WIDTH,
               ATT_HEADS * ATT_HEAD_DIM, ATT_KV_HEADS * ATT_HEAD_DIM, ATT_KV_HEADS * ATT_HEAD_DIM,
               IDX_HEADS * IDX_DIM, IDX_DIM, IDX_HEADS,
               MEM_HEADS * MEM_HEAD_DIM,
               N_BRANCH * D_MODEL)
IN_WIDTH = sum(SPLIT_SIZES)
SPLIT_POINTS = tuple(sum(SPLIT_SIZES[:i + 1]) for i in range(len(SPLIT_SIZES) - 1))

kernel_name = "hybrid_rglru_dsa_mem_peer_block"


def rmsnorm(x, g):
    xf = x.astype(jnp.float32)
    y = xf * lax.rsqrt(jnp.mean(xf * xf, axis=-1, keepdims=True) + EPS)
    return (y * g.astype(jnp.float32)).astype(x.dtype)


def rope(x, pos):
    half = x.shape[-1] // 2
    freq = ROPE_THETA ** (-jnp.arange(half, dtype=jnp.float32) / half)
    ang = pos.astype(jnp.float32)[:, None] * freq[None, :]
    cos = jnp.cos(ang)[None, :, None, :]
    sin = jnp.sin(ang)[None, :, None, :]
    xf = x.astype(jnp.float32)
    x1, x2 = xf[..., :half], xf[..., half:]
    return jnp.concatenate([x1 * cos - x2 * sin, x2 * cos + x1 * sin], axis=-1).astype(x.dtype)


def causal_depthwise_conv(x, w, b):
    y = lax.conv_general_dilated(x, w[:, None, :].astype(x.dtype), window_strides=(1,),
                                 padding=[(CONV_WIDTH - 1, 0)],
                                 dimension_numbers=('NWC', 'WIO', 'NWC'),
                                 feature_group_count=x.shape[-1])
    return y + b.astype(x.dtype)


def rg_lru(x, w_a, b_a, w_i, b_i, lam):
    B, S, C = x.shape
    xb = x.reshape(B, S, LRU_BLOCKS, LRU_BLOCK)
    r = jax.nn.sigmoid(jnp.einsum('bsnc,ncd->bsnd', xb, w_a) + b_a).reshape(B, S, C)
    i = jax.nn.sigmoid(jnp.einsum('bsnc,ncd->bsnd', xb, w_i) + b_i).reshape(B, S, C)
    log_a = -LRU_C * r.astype(jnp.float32) * jax.nn.softplus(-lam.astype(jnp.float32))
    a = jnp.exp(log_a)
    mult = jnp.sqrt(-jnp.expm1(2.0 * log_a))
    u = mult * (i * x).astype(jnp.float32)

    def combine(left, right):
        a1, b1 = left
        a2, b2 = right
        return a1 * a2, a2 * b1 + b2

    _, h = lax.associative_scan(combine, (a, u), axis=1)
    return h.astype(x.dtype)


def dsa_attention(q, k, v, qi, ki, wi):
    B, S = q.shape[0], q.shape[1]
    topk = min(TOPK_MAX, S // 4)
    nb = S // Q_BLOCK
    kpos = jnp.arange(S)
    qpos = kpos.reshape(nb, Q_BLOCK)
    ki_f = ki.astype(jnp.float32)

    def to_blocks(t):
        return jnp.moveaxis(t.reshape((B, nb, Q_BLOCK) + t.shape[2:]), 1, 0)

    def block(args):
        qb, qib, wib, pos = args
        logits = jnp.einsum('bqhd,bsd->bqhs', qib.astype(jnp.float32), ki_f)
        score = jnp.einsum('bqhs,bqh->bqs', jax.nn.relu(logits), wib.astype(jnp.float32))
        causal = kpos[None, :] <= pos[:, None]
        score = jnp.where(causal[None], score, -jnp.inf)
        _, idx = lax.top_k(score, topk)
        valid = idx <= pos[None, :, None]
        kg = jax.vmap(lambda kb, ib: kb[ib])(k, idx)
        vg = jax.vmap(lambda vb, ib: vb[ib])(v, idx)
        qg = qb.reshape(B, Q_BLOCK, ATT_KV_HEADS, ATT_GROUP, ATT_HEAD_DIM)
        s = jnp.einsum('bqgrd,bqkgd->bqgrk', qg, kg).astype(jnp.float32) * (ATT_HEAD_DIM ** -0.5)
        s = jnp.where(valid[:, :, None, None, :], s, -jnp.inf)
        p = jax.nn.softmax(s, axis=-1).astype(vg.dtype)
        o = jnp.einsum('bqgrk,bqkgd->bqgrd', p, vg)
        return o.reshape(B, Q_BLOCK, ATT_HEADS * ATT_HEAD_DIM)

    out = lax.map(block, (to_blocks(q), to_blocks(qi), to_blocks(wi), qpos))
    return jnp.moveaxis(out, 0, 1).reshape(B, S, ATT_HEADS * ATT_HEAD_DIM)


def memory_attention(q, mk, mv):
    B, S = q.shape[0], q.shape[1]
    s = jnp.einsum('bshd,bmhd->bhsm', q, mk).astype(jnp.float32) * (MEM_HEAD_DIM ** -0.5)
    p = jax.nn.softmax(s, axis=-1).astype(mv.dtype)
    o = jnp.einsum('bhsm,bmhd->bshd', p, mv)
    return o.reshape(B, S, MEM_HEADS * MEM_HEAD_DIM)


def peer(x, w_q, sub_keys, u, v):
    B, S, D = x.shape
    q = (x @ w_q).reshape(B, S, PEER_HEADS, 2, PEER_SUBDIM)
    sc = jnp.einsum('bshpc,hpkc->bshpk', q, sub_keys).astype(jnp.float32)
    s1, i1 = lax.top_k(sc[..., 0, :], PEER_TOPK)
    s2, i2 = lax.top_k(sc[..., 1, :], PEER_TOPK)
    cand = (s1[..., :, None] + s2[..., None, :]).reshape(B, S, PEER_HEADS, PEER_TOPK * PEER_TOPK)
    cand_idx = (i1[..., :, None] * PEER_KEYS + i2[..., None, :]).reshape(B, S, PEER_HEADS, PEER_TOPK * PEER_TOPK)
    top_s, sel = lax.top_k(cand, PEER_TOPK)
    eidx = jnp.take_along_axis(cand_idx, sel, axis=-1)
    g = jax.nn.softmax(top_s, axis=-1)
    T = B * S
    nc = T // PEER_CHUNK
    xs = x.reshape(nc, PEER_CHUNK, D)
    es = eidx.reshape(nc, PEER_CHUNK, PEER_HEADS, PEER_TOPK)
    gs = g.reshape(nc, PEER_CHUNK, PEER_HEADS, PEER_TOPK)

    def chunk(args):
        xc, ic, gc = args
        uc = u[ic]
        act = jax.nn.gelu(jnp.einsum('chkd,cd->chk', uc, xc).astype(jnp.float32)) * gc
        vc = v[ic]
        return jnp.einsum('chk,chkd->cd', act.astype(vc.dtype), vc)

    out = lax.map(chunk, (xs, es, gs))
    return out.reshape(B, S, D)


def setup_inputs(seed: int = 0) -> dict:
    key = jax.random.key(seed)
    ks = jax.random.split(key, 26)
    f32 = jnp.float32
    nrm = lambda k, shape, s: jax.random.normal(k, shape, f32) * s
    gain = lambda k, shape: 1.0 + 0.02 * jax.random.normal(k, shape, f32)
    a_c = jax.random.uniform(ks[9], (DEPTH, LRU_WIDTH), f32, minval=0.9, maxval=0.999)
    p = a_c ** (1.0 / LRU_C)
    lam = jnp.log(p) - jnp.log1p(-p)
    return {
        "x": nrm(ks[0], (BATCH, SEQ, D_MODEL), 1.0),
        "mem": nrm(ks[1], (BATCH, 256, D_MODEL), 1.0),
        "norm_mix": gain(ks[2], (DEPTH, D_MODEL)),
        "w_in": nrm(ks[3], (DEPTH, D_MODEL, IN_WIDTH), D_MODEL ** -0.5),
        "conv_w": nrm(ks[4], (DEPTH, CONV_WIDTH, LRU_WIDTH), CONV_WIDTH ** -0.5),
        "conv_b": nrm(ks[5], (DEPTH, LRU_WIDTH), 0.01),
        "lru_wa": nrm(ks[6], (DEPTH, LRU_BLOCKS, LRU_BLOCK, LRU_BLOCK), LRU_BLOCK ** -0.5),
        "lru_ba": nrm(ks[7], (DEPTH, LRU_BLOCKS, LRU_BLOCK), 0.01),
        "lru_wi": nrm(ks[8], (DEPTH, LRU_BLOCKS, LRU_BLOCK, LRU_BLOCK), LRU_BLOCK ** -0.5),
        "lru_bi": nrm(ks[10], (DEPTH, LRU_BLOCKS, LRU_BLOCK), 0.01),
        "lru_lambda": lam,
        "q_norm": gain(ks[11], (DEPTH, ATT_HEAD_DIM)),
        "k_norm": gain(ks[12], (DEPTH, ATT_HEAD_DIM)),
        "idx_k_norm": gain(ks[13], (DEPTH, IDX_DIM)),
        "mem_norm": gain(ks[14], (DEPTH, D_MODEL)),
        "w_mem_kv": nrm(ks[15], (DEPTH, D_MODEL, 2 * MEM_HEADS * MEM_HEAD_DIM), D_MODEL ** -0.5),
        "mem_q_norm": gain(ks[16], (DEPTH, MEM_HEAD_DIM)),
        "mem_k_norm": gain(ks[17], (DEPTH, MEM_HEAD_DIM)),
        "w_out": nrm(ks[18], (DEPTH, D_MODEL, D_MODEL), D_MODEL ** -0.5),
        "norm_ffn": gain(ks[19], (DEPTH, D_MODEL)),
        "peer_wq": nrm(ks[20], (DEPTH, D_MODEL, PEER_HEADS * PEER_QDIM), D_MODEL ** -0.5),
        "peer_subkeys": nrm(ks[21], (DEPTH, PEER_HEADS, 2, PEER_KEYS, PEER_SUBDIM), PEER_SUBDIM ** -0.5),
        "peer_u": nrm(ks[22], (DEPTH, PEER_EXPERTS, D_MODEL), D_MODEL ** -0.5),
        "peer_v": nrm(ks[23], (DEPTH, PEER_EXPERTS, D_MODEL), PEER_HEADS ** -0.5),
    }


def reference(x, mem, norm_mix, w_in, conv_w, conv_b, lru_wa, lru_ba, lru_wi, lru_bi, lru_lambda,
              q_norm, k_norm, idx_k_norm, mem_norm, w_mem_kv, mem_q_norm, mem_k_norm, w_out,
              norm_ffn, peer_wq, peer_subkeys, peer_u, peer_v):
    B, S, D = x.shape
    M = mem.shape[1]
    pos = jnp.arange(S)
    for l in range(DEPTH):
        h = rmsnorm(x, norm_mix[l])
        (lru_x, lru_gate, q, k, v, qi, ki, wi, mq, gates) = jnp.split(h @ w_in[l], SPLIT_POINTS, axis=-1)

        xc = causal_depthwise_conv(lru_x, conv_w[l], conv_b[l])
        y_lru = rg_lru(xc, lru_wa[l], lru_ba[l], lru_wi[l], lru_bi[l], lru_lambda[l]) * jax.nn.gelu(lru_gate)

        q = rope(rmsnorm(q.reshape(B, S, ATT_HEADS, ATT_HEAD_DIM), q_norm[l]), pos)
        k = rope(rmsnorm(k.reshape(B, S, ATT_KV_HEADS, ATT_HEAD_DIM), k_norm[l]), pos)
        v = v.reshape(B, S, ATT_KV_HEADS, ATT_HEAD_DIM)
        qi = rope(qi.reshape(B, S, IDX_HEADS, IDX_DIM), pos) * (IDX_DIM ** -0.5)
        ki = rope(rmsnorm(ki, idx_k_norm[l])[:, :, None, :], pos)[:, :, 0, :]
        wi = wi * (IDX_HEADS ** -0.5)
        y_att = dsa_attention(q, k, v, qi, ki, wi)

        m = rmsnorm(mem, mem_norm[l])
        mk, mv = jnp.split(m @ w_mem_kv[l], 2, axis=-1)
        mk = rmsnorm(mk.reshape(B, M, MEM_HEADS, MEM_HEAD_DIM), mem_k_norm[l])
        mv = mv.reshape(B, M, MEM_HEADS, MEM_HEAD_DIM)
        mq = rmsnorm(mq.reshape(B, S, MEM_HEADS, MEM_HEAD_DIM), mem_q_norm[l])
        y_mem = memory_attention(mq, mk, mv)

        g = jax.nn.sigmoid(gates.reshape(B, S, N_BRANCH, D))
        merged = g[:, :, 0, :] * y_lru + g[:, :, 1, :] * y_att + g[:, :, 2, :] * y_mem
        x = x + merged @ w_out[l]

        x = x + peer(rmsnorm(x, norm_ffn[l]), peer_wq[l], peer_subkeys[l], peer_u[l], peer_v[l])
    return x
```

```python
import functools

import jax
import jax.numpy as jnp
from jax import lax
from jax.experimental import pallas as pl
from jax.experimental.pallas import tpu as pltpu

_F32 = jnp.float32
_BF = jnp.bfloat16

EPS = 1e-6
ROPE_THETA = 10000.0
LRU_C = 8.0
LRU_BLOCKS = 16
CONV_WIDTH = 4
ATT_HEADS = 8
ATT_KV_HEADS = 2
ATT_GROUP = ATT_HEADS // ATT_KV_HEADS
IDX_HEADS = 8
IDX_DIM = 64
TOPK_MAX = 256
MEM_HEADS = 4
PEER_HEADS = 8
PEER_KEYS = 128
PEER_TOPK = 16

LANES = 128
SUBLANES = 8
VMEM_LIMIT_BYTES = 56 * 1024 * 1024
MASK_VALUE = -1e30
FLT_MAX = 3.4028234663852886e38
INT_MIN = -(2 ** 31)


def _cparams(*sem):
    return pltpu.CompilerParams(dimension_semantics=sem, vmem_limit_bytes=VMEM_LIMIT_BYTES)


def _rms(x, g):
    return x * lax.rsqrt(jnp.mean(x * x, axis=-1, keepdims=True) + EPS) * g


def _gelu(x):
    return 0.5 * x * (1.0 + jnp.tanh(0.7978845608028654 * (x + 0.044715 * (x * x * x))))


def _dot(a, b):
    return jnp.dot(a, b, preferred_element_type=_F32)


def _full_spec(shape):
    n = len(shape)
    return pl.BlockSpec(shape, lambda *_: (0,) * n)


def _lru_body(x_ref, nm_ref, wlx_ref, wlg_ref, wg0_ref, cw_ref, cb_ref, wa_ref, ba_ref, wi_ref,
              bi_ref, lam_ref, out_ref, buf_ref, hc_ref, a_ref, b_ref):
    ts = x_ref.shape[1]
    c_dim = out_ref.shape[2]

    @pl.when(pl.program_id(1) == 0)
    def _():
        buf_ref[0:SUBLANES, :] = jnp.zeros((SUBLANES, c_dim), _F32)
        hc_ref[...] = jnp.zeros_like(hc_ref)

    h = _rms(x_ref[0], nm_ref[...]).astype(_BF)
    lx = _dot(h, wlx_ref[...])
    buf_ref[SUBLANES:SUBLANES + ts, :] = lx
    cw = cw_ref[...]
    xc = cb_ref[...] + cw[3:4] * lx
    for j in range(CONV_WIDTH - 1):
        off = SUBLANES - (CONV_WIDTH - 1 - j)
        xc = xc + cw[j:j + 1] * buf_ref[off:off + ts, :]
    buf_ref[0:SUBLANES, :] = buf_ref[ts:ts + SUBLANES, :]

    xcb = xc.astype(_BF)
    r = jax.nn.sigmoid(_dot(xcb, wa_ref[...]) + ba_ref[...])
    ig = jax.nn.sigmoid(_dot(xcb, wi_ref[...]) + bi_ref[...])
    lam = lam_ref[...]
    softplus_neg_lam = jnp.maximum(-lam, 0.0) + jnp.log1p(jnp.exp(-jnp.abs(lam)))
    log_a = (-LRU_C) * r * softplus_neg_lam
    a = jnp.exp(log_a)
    u = jnp.sqrt(jnp.tanh(-log_a) * (a * a + 1.0)) * (ig * xc)

    rid = lax.broadcasted_iota(jnp.int32, (ts, c_dim), 0) & (SUBLANES - 1)
    for d in (1, 2, 4):
        ok = rid >= d
        a_s = pltpu.roll(a, d, 0)
        u_s = pltpu.roll(u, d, 0)
        u = jnp.where(ok, a * u_s + u, u)
        a = jnp.where(ok, a * a_s, a)
    a_ref[...] = a
    b_ref[...] = u

    def step(g, hp):
        r0 = pl.multiple_of(g * SUBLANES, SUBLANES)
        hcur = a_ref[pl.ds(r0, SUBLANES), :] * hp + b_ref[pl.ds(r0, SUBLANES), :]
        b_ref[pl.ds(r0, SUBLANES), :] = hcur
        return jnp.broadcast_to(hcur[SUBLANES - 1:SUBLANES, :], (SUBLANES, c_dim))

    hc_ref[...] = lax.fori_loop(0, ts // SUBLANES, step, hc_ref[...])
    gate = _gelu(_dot(h, wlg_ref[...])) * jax.nn.sigmoid(_dot(h, wg0_ref[...]))
    out_ref[0] = (b_ref[...] * gate).astype(_BF)


def _lru_branch(x, nm, wlx, wlg, wg0, cw, cb, wa, ba, wi, bi, lam, ts):
    b, s, d = x.shape
    c = wlx.shape[1]
    row = lambda v: v.reshape(1, -1)
    args = (x, row(nm), wlx, wlg, wg0, cw, row(cb), wa, row(ba), wi, row(bi), row(lam))
    in_specs = [pl.BlockSpec((1, ts, d), lambda i, j: (i, j, 0))]
    in_specs += [_full_spec(a.shape) for a in args[1:]]
    return pl.pallas_call(
        _lru_body,
        grid=(b, s // ts),
        in_specs=in_specs,
        out_specs=pl.BlockSpec((1, ts, c), lambda i, j: (i, j, 0)),
        out_shape=jax.ShapeDtypeStruct((b, s, c), _BF),
        scratch_shapes=[pltpu.VMEM((ts + SUBLANES, c), _F32), pltpu.VMEM((SUBLANES, c), _F32),
                        pltpu.VMEM((ts, c), _F32), pltpu.VMEM((ts, c), _F32)],
        compiler_params=_cparams("arbitrary", "arbitrary"),
        name="lru",
    )(*args)


def _rope128(x, cos, sin):
    return x * cos + pltpu.roll(x, LANES // 2, 1) * sin


def _rope64(x, cos, sin, first_half):
    rot = jnp.where(first_half, pltpu.roll(x, LANES - IDX_DIM // 2, 1), pltpu.roll(x, IDX_DIM // 2, 1))
    return x * cos + rot * sin


def _dsa_proj_body(x_ref, nm_ref, wq_ref, wk_ref, wv_ref, wqi_ref, wki_ref, wwi_ref, wg1_ref,
                   qn_ref, kn_ref, ikn_ref, c128_ref, s128_ref, c64_ref, s64_ref,
                   qT_ref, k_ref, vT_ref, qiT_ref, ki_ref, wT_ref, g1_ref):
    ts = x_ref.shape[1]
    h = _rms(x_ref[0], nm_ref[...]).astype(_BF)
    c128, s128 = c128_ref[...], s128_ref[...]
    c64, s64 = c64_ref[...], s64_ref[...]
    first_half = (lax.broadcasted_iota(jnp.int32, (ts, LANES), 1) & (IDX_DIM - 1)) < IDX_DIM // 2

    def head_norm_rope(t, g, scale):
        outs = []
        for i in range(t.shape[1] // LANES):
            th = _rms(t[:, i * LANES:(i + 1) * LANES], g)
            outs.append(_rope128(th, c128, s128) * scale)
        return jnp.concatenate(outs, axis=1)

    q = head_norm_rope(_dot(h, wq_ref[...]), qn_ref[...], float(LANES) ** -0.5)
    qT_ref[0] = q.T.astype(_BF)
    k_ref[0] = head_norm_rope(_dot(h, wk_ref[...]), kn_ref[...], 1.0).astype(_BF)
    vT_ref[0, 0] = _dot(h, wv_ref[...]).T.astype(_BF)

    qi = _dot(h, wqi_ref[...])
    qi = jnp.concatenate(
        [_rope64(qi[:, i * LANES:(i + 1) * LANES], c64, s64, first_half) for i in range(qi.shape[1] // LANES)],
        axis=1) * (float(IDX_DIM) ** -0.5)
    qiT_ref[0] = qi.T.astype(_BF)
    ki = _rope64(_rms(_dot(h, wki_ref[...]), ikn_ref[...]), c64, s64, first_half)
    ki_ref[0] = ki[:, :IDX_DIM].astype(_BF)
    wi = _dot(h, wwi_ref[...]) * (float(IDX_HEADS) ** -0.5)
    wT_ref[0] = wi.T[:IDX_HEADS, :]
    g1_ref[0] = jax.nn.sigmoid(_dot(h, wg1_ref[...])).astype(_BF)


def _dsa_proj(x, nm, wq, wk, wv, wqi, wki2, wwi, wg1, qn, kn, ikn2, c128, s128, c64, s64, ts):
    b, s, d = x.shape
    row = lambda v: v.reshape(1, -1)
    nq, nk, nv, nqi = wq.shape[1], wk.shape[1], wv.shape[1], wqi.shape[1]
    args = (x, row(nm), wq, wk, wv, wqi, wki2, wwi, wg1, row(qn), row(kn), row(ikn2), c128, s128, c64, s64)
    tile = lambda n: pl.BlockSpec((1, ts, n), lambda i, j: (i, j, 0))
    tile_t = lambda n: pl.BlockSpec((1, n, ts), lambda i, j: (i, 0, j))
    tab = pl.BlockSpec((ts, LANES), lambda i, j: (j, 0))
    in_specs = [tile(d)] + [_full_spec(a.shape) for a in args[1:12]] + [tab] * 4
    out_shape = (
        jax.ShapeDtypeStruct((b, nq, s), _BF),
        jax.ShapeDtypeStruct((b, s, nk), _BF),
        jax.ShapeDtypeStruct((b, s // ts, nv, ts), _BF),
        jax.ShapeDtypeStruct((b, nqi, s), _BF),
        jax.ShapeDtypeStruct((b, s, IDX_DIM), _BF),
        jax.ShapeDtypeStruct((b, IDX_HEADS, s), _F32),
        jax.ShapeDtypeStruct((b, s, d), _BF),
    )
    out_specs = (tile_t(nq), tile(nk), pl.BlockSpec((1, 1, nv, ts), lambda i, j: (i, j, 0, 0)), tile_t(nqi),
                 tile(IDX_DIM), tile_t(IDX_HEADS), tile(d))
    return pl.pallas_call(
        _dsa_proj_body, grid=(b, s // ts), in_specs=in_specs, out_specs=out_specs, out_shape=out_shape,
        compiler_params=_cparams("arbitrary", "arbitrary"), name="dsa_proj",
    )(*args)


def _key_to_float(u):
    key = u ^ jnp.int32(INT_MIN)
    bits = key ^ ((key >> 31) & jnp.int32(0x7FFFFFFF))
    return jnp.where((u >> 23) == 0, -jnp.inf, lax.bitcast_convert_type(bits, _F32))


def _dsa_body(qT_ref, qiT_ref, wT_ref, k_ref, vT_ref, ki_ref, g1_ref, out_ref,
              sc_ref, m_ref, l_ref, acc_ref, *, topk, tk):
    tq = qT_ref.shape[2]
    dh = LANES
    q0 = pl.program_id(1) * tq
    n_chunks = (q0 + tq + tk - 1) // tk

    qicat = jnp.concatenate([qiT_ref[0, h * IDX_DIM:(h + 1) * IDX_DIM, :] for h in range(IDX_HEADS)], axis=1)
    w = wT_ref[0]
    qpos = q0 + lax.broadcasted_iota(jnp.int32, (tk, tq), 1)
    krow = lax.broadcasted_iota(jnp.int32, (tk, tq), 0)

    def score_chunk(c, carry):
        r0 = pl.multiple_of(c * tk, tk)
        lg = _dot(ki_ref[0, pl.ds(r0, tk), :], qicat)
        s = jnp.zeros((tk, tq), _F32)
        for h in range(IDX_HEADS):
            s = s + jnp.maximum(lg[:, h * tq:(h + 1) * tq], 0.0) * w[h:h + 1, :]
        sc_ref[pl.ds(r0, tk), :] = jnp.where(krow + r0 <= qpos, s, -jnp.inf)
        return carry

    lax.fori_loop(0, n_chunks, score_chunk, 0)

    def count_ge(cand):
        def body(c, cnt):
            r0 = pl.multiple_of(c * tk, tk)
            hit = jnp.where(sc_ref[pl.ds(r0, tk), :] >= cand, 1.0, 0.0)
            return cnt + jnp.sum(hit, axis=0, keepdims=True)
        return lax.fori_loop(0, n_chunks, body, jnp.zeros((1, tq), _F32))

    def bisect(i, u):
        trial = u | lax.shift_left(jnp.int32(1), 31 - i)
        return jnp.where(count_ge(_key_to_float(trial)) >= float(topk), trial, u)

    u = lax.fori_loop(0, 32, bisect, jnp.zeros((1, tq), jnp.int32))
    tau = jnp.maximum(_key_to_float(u), -FLT_MAX)

    m_ref[...] = jnp.full(m_ref.shape, MASK_VALUE, _F32)
    l_ref[...] = jnp.zeros_like(l_ref)
    acc_ref[...] = jnp.zeros_like(acc_ref)

    def att_chunk(c, carry):
        r0 = pl.multiple_of(c * tk, tk)
        bias = jnp.where(sc_ref[pl.ds(r0, tk), :] >= tau, 0.0, MASK_VALUE)
        for g in range(ATT_KV_HEADS):
            kc = k_ref[0, pl.ds(r0, tk), g * dh:(g + 1) * dh]
            vc = vT_ref[0, c, g * dh:(g + 1) * dh, :]
            for r in range(ATT_GROUP):
                h = g * ATT_GROUP + r
                s = _dot(kc, qT_ref[0, h * dh:(h + 1) * dh, :]) + bias
                m_old = m_ref[h:h + 1, :]
                m_new = jnp.maximum(m_old, jnp.max(s, axis=0, keepdims=True))
                alpha = jnp.exp(m_old - m_new)
                p = jnp.exp(s - m_new)
                l_ref[h:h + 1, :] = alpha * l_ref[h:h + 1, :] + jnp.sum(p, axis=0, keepdims=True)
                acc_ref[h * dh:(h + 1) * dh, :] = (alpha * acc_ref[h * dh:(h + 1) * dh, :]
                                                   + _dot(vc, p.astype(_BF)))
                m_ref[h:h + 1, :] = m_new
        return carry

    lax.fori_loop(0, n_chunks, att_chunk, 0)
    outs = []
    for h in range(ATT_HEADS):
        o = acc_ref[h * dh:(h + 1) * dh, :] / l_ref[h:h + 1, :]
        outs.append(o.T)
    out_ref[0] = (jnp.concatenate(outs, axis=1) * g1_ref[0].astype(_F32)).astype(_BF)


def _dsa(qT, qiT, wT, k, vT, ki, g1, topk, tq, tk):
    b, nq, s = qT.shape
    d = g1.shape[2]
    in_specs = [
        pl.BlockSpec((1, nq, tq), lambda i, j: (i, 0, j)),
        pl.BlockSpec((1, qiT.shape[1], tq), lambda i, j: (i, 0, j)),
        pl.BlockSpec((1, IDX_HEADS, tq), lambda i, j: (i, 0, j)),
        pl.BlockSpec((1, s, k.shape[2]), lambda i, j: (i, 0, 0)),
        pl.BlockSpec((1,) + vT.shape[1:], lambda i, j: (i, 0, 0, 0)),
        pl.BlockSpec((1, s, IDX_DIM), lambda i, j: (i, 0, 0)),
        pl.BlockSpec((1, tq, d), lambda i, j: (i, j, 0)),
    ]
    return pl.pallas_call(
        functools.partial(_dsa_body, topk=topk, tk=tk),
        grid=(b, s // tq), in_specs=in_specs,
        out_specs=pl.BlockSpec((1, tq, d), lambda i, j: (i, j, 0)),
        out_shape=jax.ShapeDtypeStruct((b, s, d), _BF),
        scratch_shapes=[pltpu.VMEM((s, tq), _F32), pltpu.VMEM((ATT_HEADS, tq), _F32),
                        pltpu.VMEM((ATT_HEADS, tq), _F32), pltpu.VMEM((nq, tq), _F32)],
        compiler_params=_cparams("arbitrary", "arbitrary"), name="dsa",
    )(qT, qiT, wT, k, vT, ki, g1)


def _mem_kv_body(mem_ref, mn_ref, wkv_ref, mkn_ref, mkT_ref, mv_ref):
    d = mem_ref.shape[2]
    dm = d // MEM_HEADS
    m = _rms(mem_ref[0], mn_ref[...]).astype(_BF)
    kv = _dot(m, wkv_ref[...])
    mk = jnp.concatenate([_rms(kv[:, i * dm:(i + 1) * dm], mkn_ref[...]) for i in range(MEM_HEADS)], axis=1)
    mkT_ref[0] = mk.T.astype(_BF)
    mv_ref[0] = kv[:, d:].astype(_BF)


def _mem_kv(mem, mn, wkv, mkn):
    b, m, d = mem.shape
    row = lambda v: v.reshape(1, -1)
    return pl.pallas_call(
        _mem_kv_body, grid=(b,),
        in_specs=[pl.BlockSpec((1, m, d), lambda i: (i, 0, 0)), _full_spec((1, d)), _full_spec(wkv.shape),
                  _full_spec((1, d // MEM_HEADS))],
        out_specs=(pl.BlockSpec((1, d, m), lambda i: (i, 0, 0)), pl.BlockSpec((1, m, d), lambda i: (i, 0, 0))),
        out_shape=(jax.ShapeDtypeStruct((b, d, m), _BF), jax.ShapeDtypeStruct((b, m, d), _BF)),
        compiler_params=_cparams("arbitrary"), name="mem_kv",
    )(mem, row(mn), wkv, row(mkn))


def _mem_body(x_ref, nm_ref, wmq_ref, wg2_ref, mqn_ref, mkT_ref, mv_ref, out_ref):
    d = x_ref.shape[2]
    dm = d // MEM_HEADS
    h = _rms(x_ref[0], nm_ref[...]).astype(_BF)
    mq = _dot(h, wmq_ref[...])
    outs = []
    for i in range(MEM_HEADS):
        qh = (_rms(mq[:, i * dm:(i + 1) * dm], mqn_ref[...]) * (float(dm) ** -0.5)).astype(_BF)
        s = _dot(qh, mkT_ref[0, i * dm:(i + 1) * dm, :])
        p = jnp.exp(s - jnp.max(s, axis=-1, keepdims=True))
        o = _dot(p.astype(_BF), mv_ref[0, :, i * dm:(i + 1) * dm])
        outs.append(o / jnp.sum(p, axis=-1, keepdims=True))
    out_ref[0] = (jnp.concatenate(outs, axis=1) * jax.nn.sigmoid(_dot(h, wg2_ref[...]))).astype(_BF)


def _mem_branch(x, nm, wmq, wg2, mqn, mkT, mv, ts):
    b, s, d = x.shape
    m = mv.shape[1]
    row = lambda v: v.reshape(1, -1)
    return pl.pallas_call(
        _mem_body, grid=(b, s // ts),
        in_specs=[pl.BlockSpec((1, ts, d), lambda i, j: (i, j, 0)), _full_spec((1, d)), _full_spec(wmq.shape),
                  _full_spec(wg2.shape), _full_spec((1, d // MEM_HEADS)),
                  pl.BlockSpec((1, d, m), lambda i, j: (i, 0, 0)), pl.BlockSpec((1, m, d), lambda i, j: (i, 0, 0))],
        out_specs=pl.BlockSpec((1, ts, d), lambda i, j: (i, j, 0)),
        out_shape=jax.ShapeDtypeStruct((b, s, d), _BF),
        compiler_params=_cparams("arbitrary", "arbitrary"), name="mem",
    )(x, row(nm), wmq, wg2, row(mqn), mkT, mv)


def _mid_body(x_ref, a_ref, b_ref, c_ref, wo_ref, nf_ref, wpqT_ref, sk_ref, x1_ref, h2T_ref, scT_ref):
    merged = (a_ref[...].astype(_F32) + b_ref[...].astype(_F32) + c_ref[...].astype(_F32)).astype(_BF)
    x1 = x_ref[...] + _dot(merged, wo_ref[...])
    x1_ref[...] = x1
    h2T = _rms(x1, nf_ref[...]).T.astype(_BF)
    h2T_ref[...] = h2T
    pqT = _dot(wpqT_ref[...], h2T).astype(_BF)
    for c in range(sk_ref.shape[0]):
        scT_ref[c * PEER_KEYS:(c + 1) * PEER_KEYS, :] = _dot(sk_ref[c], pqT[c * LANES:(c + 1) * LANES, :])


def _mid(x2, ma, mb, mc, wo, nf, wpqT, sk, ts):
    t, d = x2.shape
    nsc = sk.shape[0] * PEER_KEYS
    tile = pl.BlockSpec((ts, d), lambda i: (i, 0))
    return pl.pallas_call(
        _mid_body, grid=(t // ts,),
        in_specs=[tile, tile, tile, tile, _full_spec(wo.shape), _full_spec((1, d)), _full_spec(wpqT.shape),
                  _full_spec(sk.shape)],
        out_specs=(tile, pl.BlockSpec((d, ts), lambda i: (0, i)), pl.BlockSpec((nsc, ts), lambda i: (0, i))),
        out_shape=(jax.ShapeDtypeStruct((t, d), _F32), jax.ShapeDtypeStruct((d, t), _BF),
                   jax.ShapeDtypeStruct((nsc, t), _F32)),
        compiler_params=_cparams("arbitrary"), name="mid",
    )(x2, ma, mb, mc, wo, nf.reshape(1, -1), wpqT, sk)


def _top_values(s, n):
    vals = []
    cur = s
    for i in range(n):
        m = jnp.max(cur, axis=0, keepdims=True)
        vals.append(m)
        if i + 1 < n:
            cur = jnp.where(cur == m, -jnp.inf, cur)
    return vals


def _route_body(scT_ref, tau_ref, c_ref):
    def head(h, carry):
        r1 = pl.multiple_of(h * 2 * PEER_KEYS, PEER_KEYS)
        r2 = pl.multiple_of(h * 2 * PEER_KEYS + PEER_KEYS, PEER_KEYS)
        n = PEER_TOPK + 1
        v1 = _top_values(scT_ref[pl.ds(r1, PEER_KEYS), :], n)
        v2 = _top_values(scT_ref[pl.ds(r2, PEER_KEYS), :], n)
        edge = 4
        assert (edge + 1) ** 2 > n
        pad = [jnp.full_like(v1[0], -jnp.inf)] * (-n % SUBLANES)
        col2 = jnp.concatenate(v2 + pad, axis=0)
        col1 = jnp.concatenate([jnp.full_like(v1[0], -jnp.inf)] * edge + v1[edge:] + pad, axis=0)
        cand = jnp.concatenate([v1[a] + col2 for a in range(edge)] + [v2[b] + col1 for b in range(edge)], axis=0)
        best = _top_values(cand, n)
        tau = best[PEER_TOPK - 1]
        top = v1[0] + v2[0]
        z = jnp.sum(jnp.where(cand >= tau, jnp.exp(cand - top), 0.0), axis=0, keepdims=True)
        tau_ref[pl.ds(h, 1), :] = 0.5 * (tau + best[PEER_TOPK])
        c_ref[pl.ds(h, 1), :] = top + jnp.log(z)
        return carry

    lax.fori_loop(0, PEER_HEADS, head, 0)


def _route(scT, te):
    nsc, t = scT.shape
    spec = pl.BlockSpec((PEER_HEADS, te), lambda i: (0, i))
    return pl.pallas_call(
        _route_body, grid=(t // te,),
        in_specs=[pl.BlockSpec((nsc, te), lambda i: (0, i))],
        out_specs=(spec, spec),
        out_shape=(jax.ShapeDtypeStruct((PEER_HEADS, t), _F32),) * 2,
        compiler_params=_cparams("arbitrary"), name="route",
    )(scT)


def _experts_body(h2T_ref, scT_ref, tau_ref, c_ref, u_ref, vT_ref, x1_ref, out_ref,
                  acc_ref, e2_ref, m2_ref, *, rows_per_step):
    e = pl.program_id(1)
    nk = PEER_KEYS

    @pl.when(e == 0)
    def _():
        acc_ref[...] = jnp.zeros_like(acc_ref)
        for h in range(PEER_HEADS):
            s2 = scT_ref[(2 * h + 1) * nk:(2 * h + 2) * nk, :]
            m2 = jnp.max(s2, axis=0, keepdims=True)
            m2_ref[h:h + 1, :] = m2
            e2_ref[h * nk:(h + 1) * nk, :] = jnp.exp(s2 - m2)

    h2T = h2T_ref[...]
    for i in range(rows_per_step):
        i_glob = e * rows_per_step + i
        act = _gelu(_dot(u_ref[i * nk:(i + 1) * nk, :], h2T))
        wgt = jnp.zeros(act.shape, _F32)
        for h in range(PEER_HEADS):
            s1 = scT_ref[pl.ds(2 * h * nk + i_glob, 1), :]
            thr = tau_ref[h:h + 1, :] - s1
            scale = jnp.exp(s1 + m2_ref[h:h + 1, :] - c_ref[h:h + 1, :])
            s2 = scT_ref[(2 * h + 1) * nk:(2 * h + 2) * nk, :]
            wgt = wgt + jnp.where(s2 >= thr, e2_ref[h * nk:(h + 1) * nk, :], 0.0) * scale
        acc_ref[...] += _dot(vT_ref[:, i * nk:(i + 1) * nk], (act * wgt).astype(_BF))

    @pl.when(e == pl.num_programs(1) - 1)
    def _():
        out_ref[...] = x1_ref[...] + acc_ref[...].T


def _experts(h2T, scT, tau, cn, u, vT, x1, ts, rows_per_step):
    d, t = h2T.shape
    nsc = scT.shape[0]
    ne = u.shape[0]
    eb = rows_per_step * PEER_KEYS
    col = lambda n: pl.BlockSpec((n, ts), lambda i, e: (0, i))
    tile = pl.BlockSpec((ts, d), lambda i, e: (i, 0))
    return pl.pallas_call(
        functools.partial(_experts_body, rows_per_step=rows_per_step),
        grid=(t // ts, ne // eb),
        in_specs=[col(d), col(nsc), col(PEER_HEADS), col(PEER_HEADS),
                  pl.BlockSpec((eb, d), lambda i, e: (e, 0)), pl.BlockSpec((d, eb), lambda i, e: (0, e)), tile],
        out_specs=tile,
        out_shape=jax.ShapeDtypeStruct((t, d), _F32),
        scratch_shapes=[pltpu.VMEM((d, ts), _F32), pltpu.VMEM((PEER_HEADS * PEER_KEYS, ts), _F32),
                        pltpu.VMEM((PEER_HEADS, ts), _F32)],
        compiler_params=_cparams("arbitrary", "arbitrary"), name="experts",
    )(h2T, scT, tau, cn, u, vT, x1)


def _block_diag(w):
    n, c, d = w.shape
    return jnp.einsum("ncd,nm->ncmd", w, jnp.eye(n, dtype=w.dtype)).reshape(n * c, n * d)


def _rope_tables(s, dim, reps):
    half = dim // 2
    freq = ROPE_THETA ** (-jnp.arange(half, dtype=_F32) / half)
    ang = jnp.arange(s, dtype=_F32)[:, None] * freq[None, :]
    cos = jnp.tile(jnp.concatenate([jnp.cos(ang), jnp.cos(ang)], axis=1), (1, reps))
    sin = jnp.tile(jnp.concatenate([-jnp.sin(ang), jnp.sin(ang)], axis=1), (1, reps))
    return cos, sin


def _tile(n, pref):
    return pref if n % pref == 0 else n


def _layer(x, mem, p, l):
    b, s, d = x.shape
    bf = lambda a: a.astype(_BF)
    w_in = p["w_in"][l]
    n_q, n_kv = ATT_HEADS * LANES, ATT_KV_HEADS * LANES
    n_qi = IDX_HEADS * IDX_DIM
    sizes = (d, d, n_q, n_kv, n_kv, n_qi, IDX_DIM, IDX_HEADS, d, 3 * d)
    offs = [0]
    for n in sizes:
        offs.append(offs[-1] + n)
    wlx, wlg, wq, wk, wv, wqi, wki, wwi, wmq, wgates = (w_in[:, offs[i]:offs[i + 1]] for i in range(len(sizes)))
    wki2 = jnp.concatenate([wki, wki], axis=1)
    wwi_p = jnp.pad(wwi, ((0, 0), (0, LANES - IDX_HEADS)))
    wg0, wg1, wg2 = wgates[:, :d], wgates[:, d:2 * d], wgates[:, 2 * d:]
    nm = p["norm_mix"][l]

    ts_a = _tile(s, 256)
    m_lru = _lru_branch(x, nm, bf(wlx), bf(wlg), bf(wg0), p["conv_w"][l], p["conv_b"][l],
                        bf(_block_diag(p["lru_wa"][l])), p["lru_ba"][l].reshape(-1),
                        bf(_block_diag(p["lru_wi"][l])), p["lru_bi"][l].reshape(-1), p["lru_lambda"][l], ts_a)

    tk = _tile(s, 512)
    c128, s128 = _rope_tables(s, LANES, 1)
    c64, s64 = _rope_tables(s, IDX_DIM, LANES // IDX_DIM)
    ikn2 = jnp.concatenate([p["idx_k_norm"][l]] * 2)
    qT, k, vT, qiT, ki, wT, g1 = _dsa_proj(x, nm, bf(wq), bf(wk), bf(wv), bf(wqi), bf(wki2), bf(wwi_p), bf(wg1),
                                           p["q_norm"][l], p["k_norm"][l], ikn2, c128, s128, c64, s64, tk)
    m_att = _dsa(qT, qiT, wT, k, vT, ki, g1, min(TOPK_MAX, s // 4), _tile(s, 128), tk)

    mkT, mv = _mem_kv(mem, p["mem_norm"][l], bf(p["w_mem_kv"][l]), p["mem_k_norm"][l])
    m_mem = _mem_branch(x, nm, bf(wmq), bf(wg2), p["mem_q_norm"][l], mkT, mv, _tile(s, 512))

    t = b * s
    flat = lambda a: a.reshape(t, d)
    sk = bf(p["peer_subkeys"][l].reshape(2 * PEER_HEADS, PEER_KEYS, -1))
    x1, h2T, scT = _mid(flat(x), flat(m_lru), flat(m_att), flat(m_mem), bf(p["w_out"][l]), p["norm_ffn"][l],
                        bf(p["peer_wq"][l].T), sk, _tile(t, 512))
    tau, cn = _route(scT, _tile(t, 256))
    out = _experts(h2T, scT, tau, cn, bf(p["peer_u"][l]), bf(p["peer_v"][l].T), x1, _tile(t, 512), 4)
    return out.reshape(b, s, d)


def kernel(x, mem, norm_mix, w_in, conv_w, conv_b, lru_wa, lru_ba, lru_wi, lru_bi, lru_lambda, q_norm, k_norm,
           idx_k_norm, mem_norm, w_mem_kv, mem_q_norm, mem_k_norm, w_out, norm_ffn, peer_wq, peer_subkeys,
           peer_u, peer_v):
    p = dict(norm_mix=norm_mix, w_in=w_in, conv_w=conv_w, conv_b=conv_b, lru_wa=lru_wa, lru_ba=lru_ba,
             lru_wi=lru_wi, lru_bi=lru_bi, lru_lambda=lru_lambda, q_norm=q_norm, k_norm=k_norm,
             idx_k_norm=idx_k_norm, mem_norm=mem_norm, w_mem_kv=w_mem_kv, mem_q_norm=mem_q_norm,
             mem_k_norm=mem_k_norm, w_out=w_out, norm_ffn=norm_ffn, peer_wq=peer_wq, peer_subkeys=peer_subkeys,
             peer_u=peer_u, peer_v=peer_v)
    for l in range(norm_mix.shape[0]):
        x = _layer(x, mem, p, l)
    return x
```

```python
import functools

import jax
import jax.numpy as jnp
from jax import lax
from jax.experimental import pallas as pl
from jax.experimental.pallas import tpu as pltpu

_F32 = jnp.float32
_BF = jnp.bfloat16

EPS = 1e-6
ROPE_THETA = 10000.0
LRU_C = 8.0
LRU_BLOCKS = 16
CONV_WIDTH = 4
ATT_HEADS = 8
ATT_KV_HEADS = 2
ATT_GROUP = ATT_HEADS // ATT_KV_HEADS
IDX_HEADS = 8
IDX_DIM = 64
TOPK_MAX = 256
MEM_HEADS = 4
PEER_HEADS = 8
PEER_KEYS = 128
PEER_TOPK = 16

LANES = 128
SUBLANES = 8
ROW_GROUP = 4 * SUBLANES
VMEM_LIMIT_BYTES = 56 * 1024 * 1024
MASK_VALUE = -1e30
FLT_MAX = 3.4028234663852886e38
INT_MIN = -(2 ** 31)


def _cparams(*sem):
    return pltpu.CompilerParams(dimension_semantics=sem, vmem_limit_bytes=VMEM_LIMIT_BYTES)


def _rms(x, g):
    return x * lax.rsqrt(jnp.mean(x * x, axis=-1, keepdims=True) + EPS) * g


GELU_C0 = 0.7978845608028654
GELU_C1 = GELU_C0 * 0.044715


def _gelu(x):
    return 0.5 * x * (1.0 + jnp.tanh(x * (GELU_C0 + GELU_C1 * (x * x))))


def _dot(a, b):
    return jnp.dot(a, b, preferred_element_type=_F32)


def _full_spec(shape):
    n = len(shape)
    return pl.BlockSpec(shape, lambda *_: (0,) * n)


def _lru_body(x_ref, nm_ref, wlx_ref, wlg_ref, wg0_ref, cw_ref, cb_ref, wa_ref, ba_ref, wi_ref,
              bi_ref, lam_ref, out_ref, buf_ref, hc_ref, a_ref, b_ref):
    ts = x_ref.shape[1]
    c_dim = out_ref.shape[2]

    @pl.when(pl.program_id(1) == 0)
    def _():
        buf_ref[0:SUBLANES, :] = jnp.zeros((SUBLANES, c_dim), _F32)
        hc_ref[...] = jnp.zeros_like(hc_ref)

    h = _rms(x_ref[0], nm_ref[...]).astype(_BF)
    lx = _dot(h, wlx_ref[...])
    buf_ref[SUBLANES:SUBLANES + ts, :] = lx
    cw = cw_ref[...]
    xc = cb_ref[...] + cw[3:4] * lx
    for j in range(CONV_WIDTH - 1):
        off = SUBLANES - (CONV_WIDTH - 1 - j)
        xc = xc + cw[j:j + 1] * buf_ref[off:off + ts, :]
    buf_ref[0:SUBLANES, :] = buf_ref[ts:ts + SUBLANES, :]

    xcb = xc.astype(_BF)
    r = jax.nn.sigmoid(_dot(xcb, wa_ref[...]) + ba_ref[...])
    ig = jax.nn.sigmoid(_dot(xcb, wi_ref[...]) + bi_ref[...])
    lam = lam_ref[...]
    softplus_neg_lam = jnp.maximum(-lam, 0.0) + jnp.log1p(jnp.exp(-jnp.abs(lam)))
    log_a = (-LRU_C) * r * softplus_neg_lam
    a = jnp.exp(log_a)
    u = jnp.sqrt(jnp.tanh(-log_a) * (a * a + 1.0)) * (ig * xc)

    rid = lax.broadcasted_iota(jnp.int32, (ts, c_dim), 0) & (SUBLANES - 1)
    for d in (1, 2, 4):
        ok = rid >= d
        a_s = pltpu.roll(a, d, 0)
        u_s = pltpu.roll(u, d, 0)
        u = jnp.where(ok, a * u_s + u, u)
        a = jnp.where(ok, a * a_s, a)
    a_ref[...] = a
    b_ref[...] = u

    def step(g, hp):
        r0 = pl.multiple_of(g * SUBLANES, SUBLANES)
        hcur = a_ref[pl.ds(r0, SUBLANES), :] * hp + b_ref[pl.ds(r0, SUBLANES), :]
        b_ref[pl.ds(r0, SUBLANES), :] = hcur
        return jnp.broadcast_to(hcur[SUBLANES - 1:SUBLANES, :], (SUBLANES, c_dim))

    hc_ref[...] = lax.fori_loop(0, ts // SUBLANES, step, hc_ref[...])
    gate = _gelu(_dot(h, wlg_ref[...])) * jax.nn.sigmoid(_dot(h, wg0_ref[...]))
    out_ref[0] = (b_ref[...] * gate).astype(_BF)


def _lru_branch(x, nm, wlx, wlg, wg0, cw, cb, wa, ba, wi, bi, lam, ts):
    b, s, d = x.shape
    c = wlx.shape[1]
    row = lambda v: v.reshape(1, -1)
    args = (x, row(nm), wlx, wlg, wg0, cw, row(cb), wa, row(ba), wi, row(bi), row(lam))
    in_specs = [pl.BlockSpec((1, ts, d), lambda i, j: (i, j, 0))]
    in_specs += [_full_spec(a.shape) for a in args[1:]]
    return pl.pallas_call(
        _lru_body,
        grid=(b, s // ts),
        in_specs=in_specs,
        out_specs=pl.BlockSpec((1, ts, c), lambda i, j: (i, j, 0)),
        out_shape=jax.ShapeDtypeStruct((b, s, c), _BF),
        scratch_shapes=[pltpu.VMEM((ts + SUBLANES, c), _F32), pltpu.VMEM((SUBLANES, c), _F32),
                        pltpu.VMEM((ts, c), _F32), pltpu.VMEM((ts, c), _F32)],
        compiler_params=_cparams("arbitrary", "arbitrary"),
        name="lru",
    )(*args)


def _rope128(x, cos, sin):
    return x * cos + pltpu.roll(x, LANES // 2, 1) * sin


def _rope64(x, cos, sin, first_half):
    rot = jnp.where(first_half, pltpu.roll(x, LANES - IDX_DIM // 2, 1), pltpu.roll(x, IDX_DIM // 2, 1))
    return x * cos + rot * sin


def _dsa_proj_body(x_ref, nm_ref, wq_ref, wk_ref, wv_ref, wqi_ref, wki_ref, wwi_ref, wg1_ref,
                   qn_ref, kn_ref, ikn_ref, c128_ref, s128_ref, c64_ref, s64_ref,
                   qT_ref, k_ref, vT_ref, qiT_ref, ki_ref, wT_ref, g1_ref):
    ts = x_ref.shape[1]
    h = _rms(x_ref[0], nm_ref[...]).astype(_BF)
    c128, s128 = c128_ref[...], s128_ref[...]
    c64, s64 = c64_ref[...], s64_ref[...]
    first_half = (lax.broadcasted_iota(jnp.int32, (ts, LANES), 1) & (IDX_DIM - 1)) < IDX_DIM // 2

    def head_norm_rope(t, g, scale):
        outs = []
        for i in range(t.shape[1] // LANES):
            th = _rms(t[:, i * LANES:(i + 1) * LANES], g)
            outs.append(_rope128(th, c128, s128) * scale)
        return jnp.concatenate(outs, axis=1)

    q = head_norm_rope(_dot(h, wq_ref[...]), qn_ref[...], float(LANES) ** -0.5)
    qT_ref[0] = q.T.astype(_BF)
    k_ref[0] = head_norm_rope(_dot(h, wk_ref[...]), kn_ref[...], 1.0).astype(_BF)
    vT_ref[0, 0] = _dot(h, wv_ref[...]).T.astype(_BF)

    qi = _dot(h, wqi_ref[...])
    qi = jnp.concatenate(
        [_rope64(qi[:, i * LANES:(i + 1) * LANES], c64, s64, first_half) for i in range(qi.shape[1] // LANES)],
        axis=1) * (float(IDX_DIM) ** -0.5)
    qiT_ref[0] = qi.T.astype(_BF)
    ki = _rope64(_rms(_dot(h, wki_ref[...]), ikn_ref[...]), c64, s64, first_half)
    ki_ref[0] = ki[:, :IDX_DIM].astype(_BF)
    wi = _dot(h, wwi_ref[...]) * (float(IDX_HEADS) ** -0.5)
    wT_ref[0] = wi.T[:IDX_HEADS, :]
    g1_ref[0] = jax.nn.sigmoid(_dot(h, wg1_ref[...])).astype(_BF)


def _dsa_proj(x, nm, wq, wk, wv, wqi, wki2, wwi, wg1, qn, kn, ikn2, c128, s128, c64, s64, ts):
    b, s, d = x.shape
    row = lambda v: v.reshape(1, -1)
    nq, nk, nv, nqi = wq.shape[1], wk.shape[1], wv.shape[1], wqi.shape[1]
    args = (x, row(nm), wq, wk, wv, wqi, wki2, wwi, wg1, row(qn), row(kn), row(ikn2), c128, s128, c64, s64)
    tile = lambda n: pl.BlockSpec((1, ts, n), lambda i, j: (i, j, 0))
    tile_t = lambda n: pl.BlockSpec((1, n, ts), lambda i, j: (i, 0, j))
    tab = pl.BlockSpec((ts, LANES), lambda i, j: (j, 0))
    in_specs = [tile(d)] + [_full_spec(a.shape) for a in args[1:12]] + [tab] * 4
    out_shape = (
        jax.ShapeDtypeStruct((b, nq, s), _BF),
        jax.ShapeDtypeStruct((b, s, nk), _BF),
        jax.ShapeDtypeStruct((b, s // ts, nv, ts), _BF),
        jax.ShapeDtypeStruct((b, nqi, s), _BF),
        jax.ShapeDtypeStruct((b, s, IDX_DIM), _BF),
        jax.ShapeDtypeStruct((b, IDX_HEADS, s), _F32),
        jax.ShapeDtypeStruct((b, s, d), _BF),
    )
    out_specs = (tile_t(nq), tile(nk), pl.BlockSpec((1, 1, nv, ts), lambda i, j: (i, j, 0, 0)), tile_t(nqi),
                 tile(IDX_DIM), tile_t(IDX_HEADS), tile(d))
    return pl.pallas_call(
        _dsa_proj_body, grid=(b, s // ts), in_specs=in_specs, out_specs=out_specs, out_shape=out_shape,
        compiler_params=_cparams("arbitrary", "arbitrary"), name="dsa_proj",
    )(*args)


def _key_to_float(u):
    key = u ^ jnp.int32(INT_MIN)
    bits = key ^ ((key >> 31) & jnp.int32(0x7FFFFFFF))
    return jnp.where((u >> 23) == 0, -jnp.inf, lax.bitcast_convert_type(bits, _F32))


def _dsa_body(qT_ref, qiT_ref, wT_ref, k_ref, vT_ref, ki_ref, g1_ref, out_ref,
              sc_ref, m_ref, l_ref, acc_ref, bias_ref, s_ref, p_ref, *, topk, tk):
    tq = qT_ref.shape[2]
    dh = LANES
    q0 = pl.program_id(1) * tq
    n_chunks = (q0 + tq + tk - 1) // tk

    qicat = jnp.concatenate([qiT_ref[0, h * IDX_DIM:(h + 1) * IDX_DIM, :] for h in range(IDX_HEADS)], axis=1)
    w = wT_ref[0]
    qpos = q0 + lax.broadcasted_iota(jnp.int32, (tk, tq), 1)
    krow = lax.broadcasted_iota(jnp.int32, (tk, tq), 0)

    def score_chunk(c, carry):
        r0 = pl.multiple_of(c * tk, tk)
        lg = _dot(ki_ref[0, pl.ds(r0, tk), :], qicat)
        s = jnp.zeros((tk, tq), _F32)
        for h in range(IDX_HEADS):
            s = s + jnp.maximum(lg[:, h * tq:(h + 1) * tq], 0.0) * w[h:h + 1, :]
        sc_ref[pl.ds(r0, tk), :] = jnp.where(krow + r0 <= qpos, s, -jnp.inf)
        return carry

    lax.fori_loop(0, n_chunks, score_chunk, 0)

    def make_count(nc):
        def count_ge(cand):
            parts = [jnp.zeros((SUBLANES, tq), _F32) for _ in range(4)]
            n = 0
            for c in range(nc):
                hit = jnp.where(sc_ref[c * tk:(c + 1) * tk, :] >= cand, 1.0, 0.0)
                for r in range(tk // SUBLANES):
                    parts[n % 4] = parts[n % 4] + hit[r * SUBLANES:(r + 1) * SUBLANES, :]
                    n += 1
            return jnp.sum((parts[0] + parts[1]) + (parts[2] + parts[3]), axis=0, keepdims=True)
        return count_ge

    counters = [make_count(nc) for nc in range(1, sc_ref.shape[0] // tk + 1)]

    def bisect(i, u):
        trial = u | lax.shift_left(jnp.int32(1), 31 - i)
        cnt = lax.switch(n_chunks - 1, counters, _key_to_float(trial))
        return jnp.where(cnt >= float(topk), trial, u)

    u = lax.fori_loop(0, 32, bisect, jnp.zeros((1, tq), jnp.int32))
    tau = jnp.maximum(_key_to_float(u), -FLT_MAX)

    m_ref[...] = jnp.full(m_ref.shape, MASK_VALUE, _F32)
    l_ref[...] = jnp.zeros_like(l_ref)
    acc_ref[...] = jnp.zeros_like(acc_ref)

    sub = 8 * SUBLANES
    groups = [slice(r * SUBLANES, (r + 1) * SUBLANES) for r in range(sub // SUBLANES)]

    def att_chunk(c, carry):
        r0 = pl.multiple_of(c * tk, tk)
        bias_ref[...] = jnp.where(sc_ref[pl.ds(r0, tk), :] >= tau, 0.0, MASK_VALUE)

        def logits(h):
            g = h // ATT_GROUP
            s_ref[h] = _dot(k_ref[0, pl.ds(r0, tk), g * dh:(g + 1) * dh], qT_ref[0, h * dh:(h + 1) * dh, :])

        logits(0)
        for h in range(ATT_HEADS):
            if h + 1 < ATT_HEADS:
                logits(h + 1)
            m_old = m_ref[h:h + 1, :]
            mx = jnp.broadcast_to(m_old, (SUBLANES, tq))
            for i in range(tk // sub):
                rows = slice(i * sub, (i + 1) * sub)
                sb = s_ref[h, rows, :] + bias_ref[rows, :]
                s_ref[h, rows, :] = sb
                for gr in groups:
                    mx = jnp.maximum(mx, sb[gr, :])
            m_new = jnp.max(mx, axis=0, keepdims=True)
            ls = jnp.zeros((SUBLANES, tq), _F32)
            for i in range(tk // sub):
                rows = slice(i * sub, (i + 1) * sub)
                p = jnp.exp(s_ref[h, rows, :] - m_new)
                p_ref[h, rows, :] = p.astype(_BF)
                for gr in groups:
                    ls = ls + p[gr, :]
            alpha = jnp.exp(m_old - m_new)
            l_ref[h:h + 1, :] = alpha * l_ref[h:h + 1, :] + jnp.sum(ls, axis=0, keepdims=True)
            g = h // ATT_GROUP
            acc_ref[h * dh:(h + 1) * dh, :] = (alpha * acc_ref[h * dh:(h + 1) * dh, :]
                                               + _dot(vT_ref[0, c, g * dh:(g + 1) * dh, :], p_ref[h]))
            m_ref[h:h + 1, :] = m_new
        return carry

    lax.fori_loop(0, n_chunks, att_chunk, 0)
    outs = []
    for h in range(ATT_HEADS):
        o = acc_ref[h * dh:(h + 1) * dh, :] / l_ref[h:h + 1, :]
        outs.append(o.T)
    out_ref[0] = (jnp.concatenate(outs, axis=1) * g1_ref[0].astype(_F32)).astype(_BF)


def _dsa(qT, qiT, wT, k, vT, ki, g1, topk, tq, tk):
    b, nq, s = qT.shape
    d = g1.shape[2]
    in_specs = [
        pl.BlockSpec((1, nq, tq), lambda i, j: (i, 0, j)),
        pl.BlockSpec((1, qiT.shape[1], tq), lambda i, j: (i, 0, j)),
        pl.BlockSpec((1, IDX_HEADS, tq), lambda i, j: (i, 0, j)),
        pl.BlockSpec((1, s, k.shape[2]), lambda i, j: (i, 0, 0)),
        pl.BlockSpec((1,) + vT.shape[1:], lambda i, j: (i, 0, 0, 0)),
        pl.BlockSpec((1, s, IDX_DIM), lambda i, j: (i, 0, 0)),
        pl.BlockSpec((1, tq, d), lambda i, j: (i, j, 0)),
    ]
    return pl.pallas_call(
        functools.partial(_dsa_body, topk=topk, tk=tk),
        grid=(b, s // tq), in_specs=in_specs,
        out_specs=pl.BlockSpec((1, tq, d), lambda i, j: (i, j, 0)),
        out_shape=jax.ShapeDtypeStruct((b, s, d), _BF),
        scratch_shapes=[pltpu.VMEM((s, tq), _F32), pltpu.VMEM((ATT_HEADS, tq), _F32),
                        pltpu.VMEM((ATT_HEADS, tq), _F32), pltpu.VMEM((nq, tq), _F32),
                        pltpu.VMEM((tk, tq), _F32), pltpu.VMEM((ATT_HEADS, tk, tq), _F32),
                        pltpu.VMEM((ATT_HEADS, tk, tq), _BF)],
        compiler_params=_cparams("arbitrary", "arbitrary"), name="dsa",
    )(qT, qiT, wT, k, vT, ki, g1)


def _mem_kv_body(mem_ref, mn_ref, wkv_ref, mkn_ref, mkT_ref, mv_ref):
    d = mem_ref.shape[2]
    dm = d // MEM_HEADS
    m = _rms(mem_ref[0], mn_ref[...]).astype(_BF)
    kv = _dot(m, wkv_ref[...])
    mk = jnp.concatenate([_rms(kv[:, i * dm:(i + 1) * dm], mkn_ref[...]) for i in range(MEM_HEADS)], axis=1)
    mkT_ref[0] = mk.T.astype(_BF)
    mv_ref[0] = kv[:, d:].astype(_BF)


def _mem_kv(mem, mn, wkv, mkn):
    b, m, d = mem.shape
    row = lambda v: v.reshape(1, -1)
    return pl.pallas_call(
        _mem_kv_body, grid=(b,),
        in_specs=[pl.BlockSpec((1, m, d), lambda i: (i, 0, 0)), _full_spec((1, d)), _full_spec(wkv.shape),
                  _full_spec((1, d // MEM_HEADS))],
        out_specs=(pl.BlockSpec((1, d, m), lambda i: (i, 0, 0)), pl.BlockSpec((1, m, d), lambda i: (i, 0, 0))),
        out_shape=(jax.ShapeDtypeStruct((b, d, m), _BF), jax.ShapeDtypeStruct((b, m, d), _BF)),
        compiler_params=_cparams("arbitrary"), name="mem_kv",
    )(mem, row(mn), wkv, row(mkn))


def _mem_body(x_ref, nm_ref, wmq_ref, wg2_ref, mqn_ref, mkT_ref, mv_ref, out_ref):
    d = x_ref.shape[2]
    dm = d // MEM_HEADS
    h = _rms(x_ref[0], nm_ref[...]).astype(_BF)
    mq = _dot(h, wmq_ref[...])
    outs = []
    for i in range(MEM_HEADS):
        qh = (_rms(mq[:, i * dm:(i + 1) * dm], mqn_ref[...]) * (float(dm) ** -0.5)).astype(_BF)
        s = _dot(qh, mkT_ref[0, i * dm:(i + 1) * dm, :])
        p = jnp.exp(s - jnp.max(s, axis=-1, keepdims=True))
        o = _dot(p.astype(_BF), mv_ref[0, :, i * dm:(i + 1) * dm])
        outs.append(o / jnp.sum(p, axis=-1, keepdims=True))
    out_ref[0] = (jnp.concatenate(outs, axis=1) * jax.nn.sigmoid(_dot(h, wg2_ref[...]))).astype(_BF)


def _mem_branch(x, nm, wmq, wg2, mqn, mkT, mv, ts):
    b, s, d = x.shape
    m = mv.shape[1]
    row = lambda v: v.reshape(1, -1)
    return pl.pallas_call(
        _mem_body, grid=(b, s // ts),
        in_specs=[pl.BlockSpec((1, ts, d), lambda i, j: (i, j, 0)), _full_spec((1, d)), _full_spec(wmq.shape),
                  _full_spec(wg2.shape), _full_spec((1, d // MEM_HEADS)),
                  pl.BlockSpec((1, d, m), lambda i, j: (i, 0, 0)), pl.BlockSpec((1, m, d), lambda i, j: (i, 0, 0))],
        out_specs=pl.BlockSpec((1, ts, d), lambda i, j: (i, j, 0)),
        out_shape=jax.ShapeDtypeStruct((b, s, d), _BF),
        compiler_params=_cparams("arbitrary", "arbitrary"), name="mem",
    )(x, row(nm), wmq, wg2, row(mqn), mkT, mv)


def _mid_body(x_ref, a_ref, b_ref, c_ref, wo_ref, nf_ref, wpqT_ref, sk_ref, x1_ref, h2T_ref, scT_ref):
    merged = (a_ref[...].astype(_F32) + b_ref[...].astype(_F32) + c_ref[...].astype(_F32)).astype(_BF)
    x1 = x_ref[...] + _dot(merged, wo_ref[...])
    x1_ref[...] = x1
    h2T = _rms(x1, nf_ref[...]).T.astype(_BF)
    h2T_ref[...] = h2T
    pqT = _dot(wpqT_ref[...], h2T).astype(_BF)
    for c in range(sk_ref.shape[0]):
        scT_ref[c * PEER_KEYS:(c + 1) * PEER_KEYS, :] = _dot(sk_ref[c], pqT[c * LANES:(c + 1) * LANES, :])


def _mid(x2, ma, mb, mc, wo, nf, wpqT, sk, ts):
    t, d = x2.shape
    nsc = sk.shape[0] * PEER_KEYS
    tile = pl.BlockSpec((ts, d), lambda i: (i, 0))
    return pl.pallas_call(
        _mid_body, grid=(t // ts,),
        in_specs=[tile, tile, tile, tile, _full_spec(wo.shape), _full_spec((1, d)), _full_spec(wpqT.shape),
                  _full_spec(sk.shape)],
        out_specs=(tile, pl.BlockSpec((d, ts), lambda i: (0, i)), pl.BlockSpec((nsc, ts), lambda i: (0, i))),
        out_shape=(jax.ShapeDtypeStruct((t, d), _F32), jax.ShapeDtypeStruct((d, t), _BF),
                   jax.ShapeDtypeStruct((nsc, t), _F32)),
        compiler_params=_cparams("arbitrary"), name="mid",
    )(x2, ma, mb, mc, wo, nf.reshape(1, -1), wpqT, sk)


def _top_values(s, n):
    vals = []
    cur = s
    for i in range(n):
        m = jnp.max(cur, axis=0, keepdims=True)
        vals.append(m)
        if i + 1 < n:
            cur = jnp.where(cur == m, -jnp.inf, cur)
    return vals


def _route_body(scT_ref, tau_ref, c_ref):
    def head(h, carry):
        r1 = pl.multiple_of(h * 2 * PEER_KEYS, PEER_KEYS)
        r2 = pl.multiple_of(h * 2 * PEER_KEYS + PEER_KEYS, PEER_KEYS)
        n = PEER_TOPK + 1
        v1 = _top_values(scT_ref[pl.ds(r1, PEER_KEYS), :], n)
        v2 = _top_values(scT_ref[pl.ds(r2, PEER_KEYS), :], n)
        edge = 4
        assert (edge + 1) ** 2 > n
        pad = [jnp.full_like(v1[0], -jnp.inf)] * (-n % SUBLANES)
        col2 = jnp.concatenate(v2 + pad, axis=0)
        col1 = jnp.concatenate([jnp.full_like(v1[0], -jnp.inf)] * edge + v1[edge:] + pad, axis=0)
        cand = jnp.concatenate([v1[a] + col2 for a in range(edge)] + [v2[b] + col1 for b in range(edge)], axis=0)
        best = _top_values(cand, n)
        tau = best[PEER_TOPK - 1]
        top = v1[0] + v2[0]
        z = jnp.sum(jnp.where(cand >= tau, jnp.exp(cand - top), 0.0), axis=0, keepdims=True)
        tau_ref[pl.ds(h, 1), :] = 0.5 * (tau + best[PEER_TOPK])
        c_ref[pl.ds(h, 1), :] = top + jnp.log(z)
        return carry

    lax.fori_loop(0, PEER_HEADS, head, 0)


def _route(scT, te):
    nsc, t = scT.shape
    spec = pl.BlockSpec((PEER_HEADS, te), lambda i: (0, i))
    return pl.pallas_call(
        _route_body, grid=(t // te,),
        in_specs=[pl.BlockSpec((nsc, te), lambda i: (0, i))],
        out_specs=(spec, spec),
        out_shape=(jax.ShapeDtypeStruct((PEER_HEADS, t), _F32),) * 2,
        compiler_params=_cparams("arbitrary"), name="route",
    )(scT)


def _experts_body(h2T_ref, scT_ref, tau_ref, c_ref, u_ref, vT_ref, x1_ref, out_ref,
                  acc_ref, e2_ref, m2_ref, act_ref, w_ref, g_ref, rows_ref, *, rows_per_step):
    e = pl.program_id(1)
    nk = PEER_KEYS

    n_blocks = pl.num_programs(1) - 1

    @pl.when(e == 0)
    def _():
        acc_ref[...] = jnp.zeros_like(acc_ref)
        g_ref[...] = jnp.zeros_like(g_ref)
        for h in range(PEER_HEADS):
            s2 = scT_ref[(2 * h + 1) * nk:(2 * h + 2) * nk, :]
            m2 = jnp.max(s2, axis=0, keepdims=True)
            m2_ref[h:h + 1, :] = m2
            e2_ref[h * nk:(h + 1) * nk, :] = jnp.exp(s2 - m2)

    blk = jnp.minimum(e, n_blocks - 1)
    n_rows = rows_per_step * PEER_HEADS
    for i in range(rows_per_step):
        for h in range(PEER_HEADS):
            s1 = scT_ref[pl.ds(2 * h * nk + blk * rows_per_step + i, 1), :]
            rows_ref[i * PEER_HEADS + h:i * PEER_HEADS + h + 1, :] = tau_ref[h:h + 1, :] - s1
            rows_ref[n_rows + i * PEER_HEADS + h:n_rows + i * PEER_HEADS + h + 1, :] = (
                0.5 * jnp.exp(s1 + m2_ref[h:h + 1, :] - c_ref[h:h + 1, :]))
    acc_ref[...] += _dot(vT_ref[...], g_ref[...])
    act_ref[...] = _dot(u_ref[...], h2T_ref[...])
    for i in range(rows_per_step):
        wgt = jnp.zeros((nk, w_ref.shape[1]), _F32)
        for h in range(PEER_HEADS):
            r = i * PEER_HEADS + h
            s2 = scT_ref[(2 * h + 1) * nk:(2 * h + 2) * nk, :]
            hit = jnp.where(s2 >= rows_ref[r:r + 1, :], e2_ref[h * nk:(h + 1) * nk, :], 0.0)
            wgt = wgt + hit * rows_ref[n_rows + r:n_rows + r + 1, :]
        w_ref[i * nk:(i + 1) * nk, :] = wgt
    x = act_ref[...]
    t = jnp.tanh(x * (GELU_C0 + GELU_C1 * (x * x)))
    g_ref[...] = ((x + x * t) * w_ref[...]).astype(_BF)

    @pl.when(e == n_blocks)
    def _():
        out_ref[...] = x1_ref[...] + acc_ref[...].T


def _experts(h2T, scT, tau, cn, u, vT, x1, ts, rows_per_step):
    d, t = h2T.shape
    nsc = scT.shape[0]
    ne = u.shape[0]
    eb = rows_per_step * PEER_KEYS
    nb = ne // eb
    col = lambda n: pl.BlockSpec((n, ts), lambda i, e: (0, i))
    tile = pl.BlockSpec((ts, d), lambda i, e: (i, 0))
    return pl.pallas_call(
        functools.partial(_experts_body, rows_per_step=rows_per_step),
        grid=(t // ts, nb + 1),
        in_specs=[col(d), col(nsc), col(PEER_HEADS), col(PEER_HEADS),
                  pl.BlockSpec((eb, d), lambda i, e: (jnp.minimum(e, nb - 1), 0)),
                  pl.BlockSpec((d, eb), lambda i, e: (0, jnp.maximum(e - 1, 0))), tile],
        out_specs=tile,
        out_shape=jax.ShapeDtypeStruct((t, d), _F32),
        scratch_shapes=[pltpu.VMEM((d, ts), _F32), pltpu.VMEM((PEER_HEADS * PEER_KEYS, ts), _F32),
                        pltpu.VMEM((PEER_HEADS, ts), _F32), pltpu.VMEM((eb, ts), _F32),
                        pltpu.VMEM((eb, ts), _F32), pltpu.VMEM((eb, ts), _BF),
                        pltpu.VMEM((2 * rows_per_step * PEER_HEADS, ts), _F32)],
        compiler_params=_cparams("arbitrary", "arbitrary"), name="experts",
    )(h2T, scT, tau, cn, u, vT, x1)


def _block_diag(w):
    n, c, d = w.shape
    return jnp.einsum("ncd,nm->ncmd", w, jnp.eye(n, dtype=w.dtype)).reshape(n * c, n * d)


def _rope_tables(s, dim, reps):
    half = dim // 2
    freq = ROPE_THETA ** (-jnp.arange(half, dtype=_F32) / half)
    ang = jnp.arange(s, dtype=_F32)[:, None] * freq[None, :]
    cos = jnp.tile(jnp.concatenate([jnp.cos(ang), jnp.cos(ang)], axis=1), (1, reps))
    sin = jnp.tile(jnp.concatenate([-jnp.sin(ang), jnp.sin(ang)], axis=1), (1, reps))
    return cos, sin


def _tile(n, pref):
    return pref if n % pref == 0 else n


def _layer(x, mem, p, l):
    b, s, d = x.shape
    bf = lambda a: a.astype(_BF)
    w_in = p["w_in"][l]
    n_q, n_kv = ATT_HEADS * LANES, ATT_KV_HEADS * LANES
    n_qi = IDX_HEADS * IDX_DIM
    sizes = (d, d, n_q, n_kv, n_kv, n_qi, IDX_DIM, IDX_HEADS, d, 3 * d)
    offs = [0]
    for n in sizes:
        offs.append(offs[-1] + n)
    wlx, wlg, wq, wk, wv, wqi, wki, wwi, wmq, wgates = (w_in[:, offs[i]:offs[i + 1]] for i in range(len(sizes)))
    wki2 = jnp.concatenate([wki, wki], axis=1)
    wwi_p = jnp.pad(wwi, ((0, 0), (0, LANES - IDX_HEADS)))
    wg0, wg1, wg2 = wgates[:, :d], wgates[:, d:2 * d], wgates[:, 2 * d:]
    nm = p["norm_mix"][l]

    ts_a = _tile(s, 256)
    m_lru = _lru_branch(x, nm, bf(wlx), bf(wlg), bf(wg0), p["conv_w"][l], p["conv_b"][l],
                        bf(_block_diag(p["lru_wa"][l])), p["lru_ba"][l].reshape(-1),
                        bf(_block_diag(p["lru_wi"][l])), p["lru_bi"][l].reshape(-1), p["lru_lambda"][l], ts_a)

    tk = _tile(s, 512)
    c128, s128 = _rope_tables(s, LANES, 1)
    c64, s64 = _rope_tables(s, IDX_DIM, LANES // IDX_DIM)
    ikn2 = jnp.concatenate([p["idx_k_norm"][l]] * 2)
    qT, k, vT, qiT, ki, wT, g1 = _dsa_proj(x, nm, bf(wq), bf(wk), bf(wv), bf(wqi), bf(wki2), bf(wwi_p), bf(wg1),
                                           p["q_norm"][l], p["k_norm"][l], ikn2, c128, s128, c64, s64, tk)
    m_att = _dsa(qT, qiT, wT, k, vT, ki, g1, min(TOPK_MAX, s // 4), _tile(s, 256), tk)

    mkT, mv = _mem_kv(mem, p["mem_norm"][l], bf(p["w_mem_kv"][l]), p["mem_k_norm"][l])
    m_mem = _mem_branch(x, nm, bf(wmq), bf(wg2), p["mem_q_norm"][l], mkT, mv, _tile(s, 512))

    t = b * s
    flat = lambda a: a.reshape(t, d)
    sk = bf(p["peer_subkeys"][l].reshape(2 * PEER_HEADS, PEER_KEYS, -1))
    x1, h2T, scT = _mid(flat(x), flat(m_lru), flat(m_att), flat(m_mem), bf(p["w_out"][l]), p["norm_ffn"][l],
                        bf(p["peer_wq"][l].T), sk, _tile(t, 512))
    tau, cn = _route(scT, _tile(t, 256))
    out = _experts(h2T, scT, tau, cn, bf(p["peer_u"][l]), bf(p["peer_v"][l].T), x1, _tile(t, 512), 4)
    return out.reshape(b, s, d)


def kernel(x, mem, norm_mix, w_in, conv_w, conv_b, lru_wa, lru_ba, lru_wi, lru_bi, lru_lambda, q_norm, k_norm,
           idx_k_norm, mem_norm, w_mem_kv, mem_q_norm, mem_k_norm, w_out, norm_ffn, peer_wq, peer_subkeys,
           peer_u, peer_v):
    p = dict(norm_mix=norm_mix, w_in=w_in, conv_w=conv_w, conv_b=conv_b, lru_wa=lru_wa, lru_ba=lru_ba,
             lru_wi=lru_wi, lru_bi=lru_bi, lru_lambda=lru_lambda, q_norm=q_norm, k_norm=k_norm,
             idx_k_norm=idx_k_norm, mem_norm=mem_norm, w_mem_kv=w_mem_kv, mem_q_norm=mem_q_norm,
             mem_k_norm=mem_k_norm, w_out=w_out, norm_ffn=norm_ffn, peer_wq=peer_wq, peer_subkeys=peer_subkeys,
             peer_u=peer_u, peer_v=peer_v)
    for l in range(norm_mix.shape[0]):
        x = _layer(x, mem, p, l)
    return x
```

```python
import functools

import jax
import jax.numpy as jnp
from jax import lax
from jax.experimental import pallas as pl
from jax.experimental.pallas import tpu as pltpu

_F32 = jnp.float32
_BF = jnp.bfloat16

EPS = 1e-6
ROPE_THETA = 10000.0
LRU_C = 8.0
LRU_BLOCKS = 16
CONV_WIDTH = 4
ATT_HEADS = 8
ATT_KV_HEADS = 2
ATT_GROUP = ATT_HEADS // ATT_KV_HEADS
IDX_HEADS = 8
IDX_DIM = 64
TOPK_MAX = 256
MEM_HEADS = 4
PEER_HEADS = 8
PEER_KEYS = 128
PEER_TOPK = 16

LANES = 128
SUBLANES = 8
ROW_GROUP = 4 * SUBLANES
VMEM_LIMIT_BYTES = 56 * 1024 * 1024
MASK_VALUE = -1e30
FLT_MAX = 3.4028234663852886e38
INT_MIN = -(2 ** 31)


def _cparams(*sem):
    return pltpu.CompilerParams(dimension_semantics=sem, vmem_limit_bytes=VMEM_LIMIT_BYTES)


def _rms(x, g):
    return x * lax.rsqrt(jnp.mean(x * x, axis=-1, keepdims=True) + EPS) * g


GELU_C0 = 0.7978845608028654
GELU_C1 = GELU_C0 * 0.044715


def _gelu(x):
    return 0.5 * x * (1.0 + jnp.tanh(x * (GELU_C0 + GELU_C1 * (x * x))))


def _dot(a, b):
    return jnp.dot(a, b, preferred_element_type=_F32)


def _full_spec(shape):
    n = len(shape)
    return pl.BlockSpec(shape, lambda *_: (0,) * n)


def _lru_body(x_ref, nm_ref, wlx_ref, wlg_ref, wg0_ref, cw_ref, cb_ref, wa_ref, ba_ref, wi_ref,
              bi_ref, lam_ref, out_ref, buf_ref, hc_ref, a_ref, b_ref):
    ts = x_ref.shape[1]
    c_dim = out_ref.shape[2]

    @pl.when(pl.program_id(1) == 0)
    def _():
        buf_ref[0:SUBLANES, :] = jnp.zeros((SUBLANES, c_dim), _F32)
        hc_ref[...] = jnp.zeros_like(hc_ref)

    h = _rms(x_ref[0], nm_ref[...]).astype(_BF)
    lx = _dot(h, wlx_ref[...])
    buf_ref[SUBLANES:SUBLANES + ts, :] = lx
    cw = cw_ref[...]
    xc = cb_ref[...] + cw[3:4] * lx
    for j in range(CONV_WIDTH - 1):
        off = SUBLANES - (CONV_WIDTH - 1 - j)
        xc = xc + cw[j:j + 1] * buf_ref[off:off + ts, :]
    buf_ref[0:SUBLANES, :] = buf_ref[ts:ts + SUBLANES, :]

    xcb = xc.astype(_BF)
    r = jax.nn.sigmoid(_dot(xcb, wa_ref[...]) + ba_ref[...])
    ig = jax.nn.sigmoid(_dot(xcb, wi_ref[...]) + bi_ref[...])
    lam = lam_ref[...]
    softplus_neg_lam = jnp.maximum(-lam, 0.0) + jnp.log1p(jnp.exp(-jnp.abs(lam)))
    log_a = (-LRU_C) * r * softplus_neg_lam
    a = jnp.exp(log_a)
    u = jnp.sqrt(jnp.tanh(-log_a) * (a * a + 1.0)) * (ig * xc)

    rid = lax.broadcasted_iota(jnp.int32, (ts, c_dim), 0) & (SUBLANES - 1)
    for d in (1, 2, 4):
        ok = rid >= d
        a_s = pltpu.roll(a, d, 0)
        u_s = pltpu.roll(u, d, 0)
        u = jnp.where(ok, a * u_s + u, u)
        a = jnp.where(ok, a * a_s, a)
    a_ref[...] = a
    b_ref[...] = u

    def step(g, hp):
        r0 = pl.multiple_of(g * SUBLANES, SUBLANES)
        hcur = a_ref[pl.ds(r0, SUBLANES), :] * hp + b_ref[pl.ds(r0, SUBLANES), :]
        b_ref[pl.ds(r0, SUBLANES), :] = hcur
        return jnp.broadcast_to(hcur[SUBLANES - 1:SUBLANES, :], (SUBLANES, c_dim))

    hc_ref[...] = lax.fori_loop(0, ts // SUBLANES, step, hc_ref[...])
    gate = _gelu(_dot(h, wlg_ref[...])) * jax.nn.sigmoid(_dot(h, wg0_ref[...]))
    out_ref[0] = (b_ref[...] * gate).astype(_BF)


def _lru_branch(x, nm, wlx, wlg, wg0, cw, cb, wa, ba, wi, bi, lam, ts):
    b, s, d = x.shape
    c = wlx.shape[1]
    row = lambda v: v.reshape(1, -1)
    args = (x, row(nm), wlx, wlg, wg0, cw, row(cb), wa, row(ba), wi, row(bi), row(lam))
    in_specs = [pl.BlockSpec((1, ts, d), lambda i, j: (i, j, 0))]
    in_specs += [_full_spec(a.shape) for a in args[1:]]
    return pl.pallas_call(
        _lru_body,
        grid=(b, s // ts),
        in_specs=in_specs,
        out_specs=pl.BlockSpec((1, ts, c), lambda i, j: (i, j, 0)),
        out_shape=jax.ShapeDtypeStruct((b, s, c), _BF),
        scratch_shapes=[pltpu.VMEM((ts + SUBLANES, c), _F32), pltpu.VMEM((SUBLANES, c), _F32),
                        pltpu.VMEM((ts, c), _F32), pltpu.VMEM((ts, c), _F32)],
        compiler_params=_cparams("arbitrary", "arbitrary"),
        name="lru",
    )(*args)


def _rope128(x, cos, sin):
    return x * cos + pltpu.roll(x, LANES // 2, 1) * sin


def _rope64(x, cos, sin, first_half):
    rot = jnp.where(first_half, pltpu.roll(x, LANES - IDX_DIM // 2, 1), pltpu.roll(x, IDX_DIM // 2, 1))
    return x * cos + rot * sin


def _dsa_proj_body(x_ref, nm_ref, wq_ref, wk_ref, wv_ref, wqi_ref, wki_ref, wwi_ref, wg1_ref,
                   qn_ref, kn_ref, ikn_ref, c128_ref, s128_ref, c64_ref, s64_ref,
                   qT_ref, k_ref, vT_ref, qiT_ref, ki_ref, wT_ref, g1_ref):
    ts = x_ref.shape[1]
    h = _rms(x_ref[0], nm_ref[...]).astype(_BF)
    c128, s128 = c128_ref[...], s128_ref[...]
    c64, s64 = c64_ref[...], s64_ref[...]
    first_half = (lax.broadcasted_iota(jnp.int32, (ts, LANES), 1) & (IDX_DIM - 1)) < IDX_DIM // 2

    def head_norm_rope(t, g, scale):
        outs = []
        for i in range(t.shape[1] // LANES):
            th = _rms(t[:, i * LANES:(i + 1) * LANES], g)
            outs.append(_rope128(th, c128, s128) * scale)
        return jnp.concatenate(outs, axis=1)

    q = head_norm_rope(_dot(h, wq_ref[...]), qn_ref[...], float(LANES) ** -0.5)
    qT_ref[0] = q.T.astype(_BF)
    k_ref[0] = head_norm_rope(_dot(h, wk_ref[...]), kn_ref[...], 1.0).astype(_BF)
    vT_ref[0, 0] = _dot(h, wv_ref[...]).T.astype(_BF)

    qi = _dot(h, wqi_ref[...])
    qi = jnp.concatenate(
        [_rope64(qi[:, i * LANES:(i + 1) * LANES], c64, s64, first_half) for i in range(qi.shape[1] // LANES)],
        axis=1) * (float(IDX_DIM) ** -0.5)
    qiT_ref[0] = qi.T.astype(_BF)
    ki = _rope64(_rms(_dot(h, wki_ref[...]), ikn_ref[...]), c64, s64, first_half)
    ki_ref[0] = ki[:, :IDX_DIM].astype(_BF)
    wi = _dot(h, wwi_ref[...]) * (float(IDX_HEADS) ** -0.5)
    wT_ref[0] = wi.T[:IDX_HEADS, :]
    g1_ref[0] = jax.nn.sigmoid(_dot(h, wg1_ref[...])).astype(_BF)


def _dsa_proj(x, nm, wq, wk, wv, wqi, wki2, wwi, wg1, qn, kn, ikn2, c128, s128, c64, s64, ts):
    b, s, d = x.shape
    row = lambda v: v.reshape(1, -1)
    nq, nk, nv, nqi = wq.shape[1], wk.shape[1], wv.shape[1], wqi.shape[1]
    args = (x, row(nm), wq, wk, wv, wqi, wki2, wwi, wg1, row(qn), row(kn), row(ikn2), c128, s128, c64, s64)
    tile = lambda n: pl.BlockSpec((1, ts, n), lambda i, j: (i, j, 0))
    tile_t = lambda n: pl.BlockSpec((1, n, ts), lambda i, j: (i, 0, j))
    tab = pl.BlockSpec((ts, LANES), lambda i, j: (j, 0))
    in_specs = [tile(d)] + [_full_spec(a.shape) for a in args[1:12]] + [tab] * 4
    out_shape = (
        jax.ShapeDtypeStruct((b, nq, s), _BF),
        jax.ShapeDtypeStruct((b, s, nk), _BF),
        jax.ShapeDtypeStruct((b, s // ts, nv, ts), _BF),
        jax.ShapeDtypeStruct((b, nqi, s), _BF),
        jax.ShapeDtypeStruct((b, s, IDX_DIM), _BF),
        jax.ShapeDtypeStruct((b, IDX_HEADS, s), _F32),
        jax.ShapeDtypeStruct((b, s, d), _BF),
    )
    out_specs = (tile_t(nq), tile(nk), pl.BlockSpec((1, 1, nv, ts), lambda i, j: (i, j, 0, 0)), tile_t(nqi),
                 tile(IDX_DIM), tile_t(IDX_HEADS), tile(d))
    return pl.pallas_call(
        _dsa_proj_body, grid=(b, s // ts), in_specs=in_specs, out_specs=out_specs, out_shape=out_shape,
        compiler_params=_cparams("arbitrary", "arbitrary"), name="dsa_proj",
    )(*args)


def _key_to_float(u):
    key = u ^ jnp.int32(INT_MIN)
    bits = key ^ ((key >> 31) & jnp.int32(0x7FFFFFFF))
    return jnp.where((u >> 23) == 0, -jnp.inf, lax.bitcast_convert_type(bits, _F32))


def _dsa_body(qT_ref, qiT_ref, wT_ref, k_ref, vT_ref, ki_ref, g1_ref, out_ref,
              sc_ref, m_ref, l_ref, acc_ref, bias_ref, s_ref, p_ref, *, topk, tk):
    tq = qT_ref.shape[2]
    dh = LANES
    q0 = pl.program_id(1) * tq
    n_chunks = (q0 + tq + tk - 1) // tk

    qicat = jnp.concatenate([qiT_ref[0, h * IDX_DIM:(h + 1) * IDX_DIM, :] for h in range(IDX_HEADS)], axis=1)
    w = wT_ref[0]
    qpos = q0 + lax.broadcasted_iota(jnp.int32, (tk, tq), 1)
    krow = lax.broadcasted_iota(jnp.int32, (tk, tq), 0)

    def score_chunk(c, carry):
        r0 = pl.multiple_of(c * tk, tk)
        lg = _dot(ki_ref[0, pl.ds(r0, tk), :], qicat)
        s = jnp.zeros((tk, tq), _F32)
        for h in range(IDX_HEADS):
            s = s + jnp.maximum(lg[:, h * tq:(h + 1) * tq], 0.0) * w[h:h + 1, :]
        sc_ref[pl.ds(r0, tk), :] = jnp.where(krow + r0 <= qpos, s, -jnp.inf)
        return carry

    lax.fori_loop(0, n_chunks, score_chunk, 0)

    def make_count(nc):
        def count_ge(cand):
            parts = [jnp.zeros((SUBLANES, tq), _F32) for _ in range(4)]
            n = 0
            for c in range(nc):
                hit = jnp.where(sc_ref[c * tk:(c + 1) * tk, :] >= cand, 1.0, 0.0)
                for r in range(tk // SUBLANES):
                    parts[n % 4] = parts[n % 4] + hit[r * SUBLANES:(r + 1) * SUBLANES, :]
                    n += 1
            return jnp.sum((parts[0] + parts[1]) + (parts[2] + parts[3]), axis=0, keepdims=True)
        return count_ge

    counters = [make_count(nc) for nc in range(1, sc_ref.shape[0] // tk + 1)]

    def bisect(i, u):
        trial = u | lax.shift_left(jnp.int32(1), 31 - i)
        cnt = lax.switch(n_chunks - 1, counters, _key_to_float(trial))
        return jnp.where(cnt >= float(topk), trial, u)

    u = lax.fori_loop(0, 32, bisect, jnp.zeros((1, tq), jnp.int32))
    tau = jnp.maximum(_key_to_float(u), -FLT_MAX)

    m_ref[...] = jnp.full(m_ref.shape, MASK_VALUE, _F32)
    l_ref[...] = jnp.zeros_like(l_ref)
    acc_ref[...] = jnp.zeros_like(acc_ref)

    sub = 8 * SUBLANES
    groups = [slice(r * SUBLANES, (r + 1) * SUBLANES) for r in range(sub // SUBLANES)]

    def att_chunk(c, carry):
        r0 = pl.multiple_of(c * tk, tk)
        bias_ref[...] = jnp.where(sc_ref[pl.ds(r0, tk), :] >= tau, 0.0, MASK_VALUE)

        def logits(h):
            g = h // ATT_GROUP
            s_ref[h] = _dot(k_ref[0, pl.ds(r0, tk), g * dh:(g + 1) * dh], qT_ref[0, h * dh:(h + 1) * dh, :])

        logits(0)
        for h in range(ATT_HEADS):
            if h + 1 < ATT_HEADS:
                logits(h + 1)
            m_old = m_ref[h:h + 1, :]
            mx = jnp.broadcast_to(m_old, (SUBLANES, tq))
            for i in range(tk // sub):
                rows = slice(i * sub, (i + 1) * sub)
                sb = s_ref[h, rows, :] + bias_ref[rows, :]
                s_ref[h, rows, :] = sb
                for gr in groups:
                    mx = jnp.maximum(mx, sb[gr, :])
            m_new = jnp.max(mx, axis=0, keepdims=True)
            ls = jnp.zeros((SUBLANES, tq), _F32)
            for i in range(tk // sub):
                rows = slice(i * sub, (i + 1) * sub)
                p = jnp.exp(s_ref[h, rows, :] - m_new)
                p_ref[h, rows, :] = p.astype(_BF)
                for gr in groups:
                    ls = ls + p[gr, :]
            alpha = jnp.exp(m_old - m_new)
            l_ref[h:h + 1, :] = alpha * l_ref[h:h + 1, :] + jnp.sum(ls, axis=0, keepdims=True)
            g = h // ATT_GROUP
            acc_ref[h * dh:(h + 1) * dh, :] = (alpha * acc_ref[h * dh:(h + 1) * dh, :]
                                               + _dot(vT_ref[0, c, g * dh:(g + 1) * dh, :], p_ref[h]))
            m_ref[h:h + 1, :] = m_new
        return carry

    lax.fori_loop(0, n_chunks, att_chunk, 0)
    outs = []
    for h in range(ATT_HEADS):
        o = acc_ref[h * dh:(h + 1) * dh, :] / l_ref[h:h + 1, :]
        outs.append(o.T)
    out_ref[0] = (jnp.concatenate(outs, axis=1) * g1_ref[0].astype(_F32)).astype(_BF)


def _dsa(qT, qiT, wT, k, vT, ki, g1, topk, tq, tk):
    b, nq, s = qT.shape
    d = g1.shape[2]
    in_specs = [
        pl.BlockSpec((1, nq, tq), lambda i, j: (i, 0, j)),
        pl.BlockSpec((1, qiT.shape[1], tq), lambda i, j: (i, 0, j)),
        pl.BlockSpec((1, IDX_HEADS, tq), lambda i, j: (i, 0, j)),
        pl.BlockSpec((1, s, k.shape[2]), lambda i, j: (i, 0, 0)),
        pl.BlockSpec((1,) + vT.shape[1:], lambda i, j: (i, 0, 0, 0)),
        pl.BlockSpec((1, s, IDX_DIM), lambda i, j: (i, 0, 0)),
        pl.BlockSpec((1, tq, d), lambda i, j: (i, j, 0)),
    ]
    return pl.pallas_call(
        functools.partial(_dsa_body, topk=topk, tk=tk),
        grid=(b, s // tq), in_specs=in_specs,
        out_specs=pl.BlockSpec((1, tq, d), lambda i, j: (i, j, 0)),
        out_shape=jax.ShapeDtypeStruct((b, s, d), _BF),
        scratch_shapes=[pltpu.VMEM((s, tq), _F32), pltpu.VMEM((ATT_HEADS, tq), _F32),
                        pltpu.VMEM((ATT_HEADS, tq), _F32), pltpu.VMEM((nq, tq), _F32),
                        pltpu.VMEM((tk, tq), _F32), pltpu.VMEM((ATT_HEADS, tk, tq), _F32),
                        pltpu.VMEM((ATT_HEADS, tk, tq), _BF)],
        compiler_params=_cparams("arbitrary", "arbitrary"), name="dsa",
    )(qT, qiT, wT, k, vT, ki, g1)


def _mem_kv_body(mem_ref, mn_ref, wkv_ref, mkn_ref, mkT_ref, mv_ref):
    d = mem_ref.shape[2]
    dm = d // MEM_HEADS
    m = _rms(mem_ref[0], mn_ref[...]).astype(_BF)
    kv = _dot(m, wkv_ref[...])
    mk = jnp.concatenate([_rms(kv[:, i * dm:(i + 1) * dm], mkn_ref[...]) for i in range(MEM_HEADS)], axis=1)
    mkT_ref[0] = mk.T.astype(_BF)
    mv_ref[0] = kv[:, d:].astype(_BF)


def _mem_kv(mem, mn, wkv, mkn):
    b, m, d = mem.shape
    row = lambda v: v.reshape(1, -1)
    return pl.pallas_call(
        _mem_kv_body, grid=(b,),
        in_specs=[pl.BlockSpec((1, m, d), lambda i: (i, 0, 0)), _full_spec((1, d)), _full_spec(wkv.shape),
                  _full_spec((1, d // MEM_HEADS))],
        out_specs=(pl.BlockSpec((1, d, m), lambda i: (i, 0, 0)), pl.BlockSpec((1, m, d), lambda i: (i, 0, 0))),
        out_shape=(jax.ShapeDtypeStruct((b, d, m), _BF), jax.ShapeDtypeStruct((b, m, d), _BF)),
        compiler_params=_cparams("arbitrary"), name="mem_kv",
    )(mem, row(mn), wkv, row(mkn))


def _mem_body(x_ref, nm_ref, wmq_ref, wg2_ref, mqn_ref, mkT_ref, mv_ref, out_ref):
    d = x_ref.shape[2]
    dm = d // MEM_HEADS
    h = _rms(x_ref[0], nm_ref[...]).astype(_BF)
    mq = _dot(h, wmq_ref[...])
    outs = []
    for i in range(MEM_HEADS):
        qh = (_rms(mq[:, i * dm:(i + 1) * dm], mqn_ref[...]) * (float(dm) ** -0.5)).astype(_BF)
        s = _dot(qh, mkT_ref[0, i * dm:(i + 1) * dm, :])
        p = jnp.exp(s - jnp.max(s, axis=-1, keepdims=True))
        o = _dot(p.astype(_BF), mv_ref[0, :, i * dm:(i + 1) * dm])
        outs.append(o / jnp.sum(p, axis=-1, keepdims=True))
    out_ref[0] = (jnp.concatenate(outs, axis=1) * jax.nn.sigmoid(_dot(h, wg2_ref[...]))).astype(_BF)


def _mem_branch(x, nm, wmq, wg2, mqn, mkT, mv, ts):
    b, s, d = x.shape
    m = mv.shape[1]
    row = lambda v: v.reshape(1, -1)
    return pl.pallas_call(
        _mem_body, grid=(b, s // ts),
        in_specs=[pl.BlockSpec((1, ts, d), lambda i, j: (i, j, 0)), _full_spec((1, d)), _full_spec(wmq.shape),
                  _full_spec(wg2.shape), _full_spec((1, d // MEM_HEADS)),
                  pl.BlockSpec((1, d, m), lambda i, j: (i, 0, 0)), pl.BlockSpec((1, m, d), lambda i, j: (i, 0, 0))],
        out_specs=pl.BlockSpec((1, ts, d), lambda i, j: (i, j, 0)),
        out_shape=jax.ShapeDtypeStruct((b, s, d), _BF),
        compiler_params=_cparams("arbitrary", "arbitrary"), name="mem",
    )(x, row(nm), wmq, wg2, row(mqn), mkT, mv)


def _mid_body(x_ref, a_ref, b_ref, c_ref, wo_ref, nf_ref, wpqT_ref, sk_ref, x1_ref, h2T_ref, scT_ref):
    merged = (a_ref[...].astype(_F32) + b_ref[...].astype(_F32) + c_ref[...].astype(_F32)).astype(_BF)
    x1 = x_ref[...] + _dot(merged, wo_ref[...])
    x1_ref[...] = x1
    h2T = _rms(x1, nf_ref[...]).T.astype(_BF)
    h2T_ref[...] = h2T
    pqT = _dot(wpqT_ref[...], h2T).astype(_BF)
    for c in range(sk_ref.shape[0]):
        scT_ref[c * PEER_KEYS:(c + 1) * PEER_KEYS, :] = _dot(sk_ref[c], pqT[c * LANES:(c + 1) * LANES, :])


def _mid(x2, ma, mb, mc, wo, nf, wpqT, sk, ts):
    t, d = x2.shape
    nsc = sk.shape[0] * PEER_KEYS
    tile = pl.BlockSpec((ts, d), lambda i: (i, 0))
    return pl.pallas_call(
        _mid_body, grid=(t // ts,),
        in_specs=[tile, tile, tile, tile, _full_spec(wo.shape), _full_spec((1, d)), _full_spec(wpqT.shape),
                  _full_spec(sk.shape)],
        out_specs=(tile, pl.BlockSpec((d, ts), lambda i: (0, i)), pl.BlockSpec((nsc, ts), lambda i: (0, i))),
        out_shape=(jax.ShapeDtypeStruct((t, d), _F32), jax.ShapeDtypeStruct((d, t), _BF),
                   jax.ShapeDtypeStruct((nsc, t), _F32)),
        compiler_params=_cparams("arbitrary"), name="mid",
    )(x2, ma, mb, mc, wo, nf.reshape(1, -1), wpqT, sk)


def _top_values(s, n):
    vals = []
    cur = s
    for i in range(n):
        m = jnp.max(cur, axis=0, keepdims=True)
        vals.append(m)
        if i + 1 < n:
            cur = jnp.where(cur == m, -jnp.inf, cur)
    return vals


def _route_body(scT_ref, tau_ref, c_ref):
    def head(h, carry):
        r1 = pl.multiple_of(h * 2 * PEER_KEYS, PEER_KEYS)
        r2 = pl.multiple_of(h * 2 * PEER_KEYS + PEER_KEYS, PEER_KEYS)
        n = PEER_TOPK + 1
        v1 = _top_values(scT_ref[pl.ds(r1, PEER_KEYS), :], n)
        v2 = _top_values(scT_ref[pl.ds(r2, PEER_KEYS), :], n)
        edge = 4
        assert (edge + 1) ** 2 > n
        pad = [jnp.full_like(v1[0], -jnp.inf)] * (-n % SUBLANES)
        col2 = jnp.concatenate(v2 + pad, axis=0)
        col1 = jnp.concatenate([jnp.full_like(v1[0], -jnp.inf)] * edge + v1[edge:] + pad, axis=0)
        cand = jnp.concatenate([v1[a] + col2 for a in range(edge)] + [v2[b] + col1 for b in range(edge)], axis=0)
        best = _top_values(cand, n)
        tau = best[PEER_TOPK - 1]
        top = v1[0] + v2[0]
        z = jnp.sum(jnp.where(cand >= tau, jnp.exp(cand - top), 0.0), axis=0, keepdims=True)
        tau_ref[pl.ds(h, 1), :] = 0.5 * (tau + best[PEER_TOPK])
        c_ref[pl.ds(h, 1), :] = top + jnp.log(z)
        return carry

    lax.fori_loop(0, PEER_HEADS, head, 0, unroll=2)


def _route(scT, te):
    nsc, t = scT.shape
    spec = pl.BlockSpec((PEER_HEADS, te), lambda i: (0, i))
    return pl.pallas_call(
        _route_body, grid=(t // te,),
        in_specs=[pl.BlockSpec((nsc, te), lambda i: (0, i))],
        out_specs=(spec, spec),
        out_shape=(jax.ShapeDtypeStruct((PEER_HEADS, t), _F32),) * 2,
        compiler_params=_cparams("arbitrary"), name="route",
    )(scT)


def _experts_body(h2T_ref, scT_ref, tau_ref, c_ref, u_ref, vT_ref, x1_ref, out_ref,
                  acc_ref, e2_ref, m2_ref, act_ref, w_ref, g_ref, rows_ref, *, rows_per_step):
    e = pl.program_id(1)
    nk = PEER_KEYS

    n_blocks = pl.num_programs(1) - 1

    @pl.when(e == 0)
    def _():
        acc_ref[...] = jnp.zeros_like(acc_ref)
        g_ref[...] = jnp.zeros_like(g_ref)
        for h in range(PEER_HEADS):
            s2 = scT_ref[(2 * h + 1) * nk:(2 * h + 2) * nk, :]
            m2 = jnp.max(s2, axis=0, keepdims=True)
            m2_ref[h:h + 1, :] = m2
            e2_ref[h * nk:(h + 1) * nk, :] = jnp.exp(s2 - m2)

    blk = jnp.minimum(e, n_blocks - 1)
    n_rows = rows_per_step * PEER_HEADS
    for i in range(rows_per_step):
        for h in range(PEER_HEADS):
            s1 = scT_ref[pl.ds(2 * h * nk + blk * rows_per_step + i, 1), :]
            rows_ref[i * PEER_HEADS + h:i * PEER_HEADS + h + 1, :] = tau_ref[h:h + 1, :] - s1
            rows_ref[n_rows + i * PEER_HEADS + h:n_rows + i * PEER_HEADS + h + 1, :] = (
                0.5 * jnp.exp(s1 + m2_ref[h:h + 1, :] - c_ref[h:h + 1, :]))
    acc_ref[...] += _dot(vT_ref[...], g_ref[...])

    @pl.when(e < n_blocks)
    def _():
        act_ref[...] = _dot(u_ref[...], h2T_ref[...])
        for i in range(rows_per_step):
            wgt = jnp.zeros((nk, w_ref.shape[1]), _F32)
            for h in range(PEER_HEADS):
                r = i * PEER_HEADS + h
                s2 = scT_ref[(2 * h + 1) * nk:(2 * h + 2) * nk, :]
                hit = jnp.where(s2 >= rows_ref[r:r + 1, :], e2_ref[h * nk:(h + 1) * nk, :], 0.0)
                wgt = wgt + hit * rows_ref[n_rows + r:n_rows + r + 1, :]
            w_ref[i * nk:(i + 1) * nk, :] = wgt
        x = act_ref[...]
        t = jnp.tanh(x * (GELU_C0 + GELU_C1 * (x * x)))
        g_ref[...] = ((x + x * t) * w_ref[...]).astype(_BF)

    @pl.when(e == n_blocks)
    def _():
        out_ref[...] = x1_ref[...] + acc_ref[...].T


def _experts(h2T, scT, tau, cn, u, vT, x1, ts, rows_per_step):
    d, t = h2T.shape
    nsc = scT.shape[0]
    ne = u.shape[0]
    eb = rows_per_step * PEER_KEYS
    nb = ne // eb
    col = lambda n: pl.BlockSpec((n, ts), lambda i, e: (0, i))
    tile = pl.BlockSpec((ts, d), lambda i, e: (i, 0))
    return pl.pallas_call(
        functools.partial(_experts_body, rows_per_step=rows_per_step),
        grid=(t // ts, nb + 1),
        in_specs=[col(d), col(nsc), col(PEER_HEADS), col(PEER_HEADS),
                  pl.BlockSpec((eb, d), lambda i, e: (jnp.minimum(e, nb - 1), 0)),
                  pl.BlockSpec((d, eb), lambda i, e: (0, jnp.maximum(e - 1, 0))), tile],
        out_specs=tile,
        out_shape=jax.ShapeDtypeStruct((t, d), _F32),
        scratch_shapes=[pltpu.VMEM((d, ts), _F32), pltpu.VMEM((PEER_HEADS * PEER_KEYS, ts), _F32),
                        pltpu.VMEM((PEER_HEADS, ts), _F32), pltpu.VMEM((eb, ts), _F32),
                        pltpu.VMEM((eb, ts), _F32), pltpu.VMEM((eb, ts), _BF),
                        pltpu.VMEM((2 * rows_per_step * PEER_HEADS, ts), _F32)],
        compiler_params=_cparams("arbitrary", "arbitrary"), name="experts",
    )(h2T, scT, tau, cn, u, vT, x1)


def _block_diag(w):
    n, c, d = w.shape
    return jnp.einsum("ncd,nm->ncmd", w, jnp.eye(n, dtype=w.dtype)).reshape(n * c, n * d)


def _rope_tables(s, dim, reps):
    half = dim // 2
    freq = ROPE_THETA ** (-jnp.arange(half, dtype=_F32) / half)
    ang = jnp.arange(s, dtype=_F32)[:, None] * freq[None, :]
    cos = jnp.tile(jnp.concatenate([jnp.cos(ang), jnp.cos(ang)], axis=1), (1, reps))
    sin = jnp.tile(jnp.concatenate([-jnp.sin(ang), jnp.sin(ang)], axis=1), (1, reps))
    return cos, sin


def _tile(n, pref):
    return pref if n % pref == 0 else n


def _layer(x, mem, p, l):
    b, s, d = x.shape
    bf = lambda a: a.astype(_BF)
    w_in = p["w_in"][l]
    n_q, n_kv = ATT_HEADS * LANES, ATT_KV_HEADS * LANES
    n_qi = IDX_HEADS * IDX_DIM
    sizes = (d, d, n_q, n_kv, n_kv, n_qi, IDX_DIM, IDX_HEADS, d, 3 * d)
    offs = [0]
    for n in sizes:
        offs.append(offs[-1] + n)
    wlx, wlg, wq, wk, wv, wqi, wki, wwi, wmq, wgates = (w_in[:, offs[i]:offs[i + 1]] for i in range(len(sizes)))
    wki2 = jnp.concatenate([wki, wki], axis=1)
    wwi_p = jnp.pad(wwi, ((0, 0), (0, LANES - IDX_HEADS)))
    wg0, wg1, wg2 = wgates[:, :d], wgates[:, d:2 * d], wgates[:, 2 * d:]
    nm = p["norm_mix"][l]

    ts_a = _tile(s, 256)
    m_lru = _lru_branch(x, nm, bf(wlx), bf(wlg), bf(wg0), p["conv_w"][l], p["conv_b"][l],
                        bf(_block_diag(p["lru_wa"][l])), p["lru_ba"][l].reshape(-1),
                        bf(_block_diag(p["lru_wi"][l])), p["lru_bi"][l].reshape(-1), p["lru_lambda"][l], ts_a)

    tk = _tile(s, 512)
    c128, s128 = _rope_tables(s, LANES, 1)
    c64, s64 = _rope_tables(s, IDX_DIM, LANES // IDX_DIM)
    ikn2 = jnp.concatenate([p["idx_k_norm"][l]] * 2)
    qT, k, vT, qiT, ki, wT, g1 = _dsa_proj(x, nm, bf(wq), bf(wk), bf(wv), bf(wqi), bf(wki2), bf(wwi_p), bf(wg1),
                                           p["q_norm"][l], p["k_norm"][l], ikn2, c128, s128, c64, s64, tk)
    m_att = _dsa(qT, qiT, wT, k, vT, ki, g1, min(TOPK_MAX, s // 4), _tile(s, 256), tk)

    mkT, mv = _mem_kv(mem, p["mem_norm"][l], bf(p["w_mem_kv"][l]), p["mem_k_norm"][l])
    m_mem = _mem_branch(x, nm, bf(wmq), bf(wg2), p["mem_q_norm"][l], mkT, mv, _tile(s, 512))

    t = b * s
    flat = lambda a: a.reshape(t, d)
    sk = bf(p["peer_subkeys"][l].reshape(2 * PEER_HEADS, PEER_KEYS, -1))
    x1, h2T, scT = _mid(flat(x), flat(m_lru), flat(m_att), flat(m_mem), bf(p["w_out"][l]), p["norm_ffn"][l],
                        bf(p["peer_wq"][l].T), sk, _tile(t, 512))
    tau, cn = _route(scT, _tile(t, 256))
    out = _experts(h2T, scT, tau, cn, bf(p["peer_u"][l]), bf(p["peer_v"][l].T), x1, _tile(t, 512), 8)
    return out.reshape(b, s, d)


def kernel(x, mem, norm_mix, w_in, conv_w, conv_b, lru_wa, lru_ba, lru_wi, lru_bi, lru_lambda, q_norm, k_norm,
           idx_k_norm, mem_norm, w_mem_kv, mem_q_norm, mem_k_norm, w_out, norm_ffn, peer_wq, peer_subkeys,
           peer_u, peer_v):
    p = dict(norm_mix=norm_mix, w_in=w_in, conv_w=conv_w, conv_b=conv_b, lru_wa=lru_wa, lru_ba=lru_ba,
             lru_wi=lru_wi, lru_bi=lru_bi, lru_lambda=lru_lambda, q_norm=q_norm, k_norm=k_norm,
             idx_k_norm=idx_k_norm, mem_norm=mem_norm, w_mem_kv=w_mem_kv, mem_q_norm=mem_q_norm,
             mem_k_norm=mem_k_norm, w_out=w_out, norm_ffn=norm_ffn, peer_wq=peer_wq, peer_subkeys=peer_subkeys,
             peer_u=peer_u, peer_v=peer_v)
    for l in range(norm_mix.shape[0]):
        x = _layer(x, mem, p, l)
    return x
```

```python
import functools

import jax
import jax.numpy as jnp
from jax import lax
from jax.experimental import pallas as pl
from jax.experimental.pallas import tpu as pltpu

_F32 = jnp.float32
_BF = jnp.bfloat16

EPS = 1e-6
ROPE_THETA = 10000.0
LRU_C = 8.0
LRU_BLOCKS = 16
CONV_WIDTH = 4
ATT_HEADS = 8
ATT_KV_HEADS = 2
ATT_GROUP = ATT_HEADS // ATT_KV_HEADS
IDX_HEADS = 8
IDX_DIM = 64
TOPK_MAX = 256
MEM_HEADS = 4
PEER_HEADS = 8
PEER_KEYS = 128
PEER_TOPK = 16

LANES = 128
SUBLANES = 8
ONES_ROWS = 2 * SUBLANES
VMEM_LIMIT_BYTES = 56 * 1024 * 1024
MASK_VALUE = -1e30
FLT_MAX = 3.4028234663852886e38
INT_MIN = -(2 ** 31)


def _cparams(*sem):
    return pltpu.CompilerParams(dimension_semantics=sem, vmem_limit_bytes=VMEM_LIMIT_BYTES)


def _rms(x, g):
    return x * lax.rsqrt(jnp.mean(x * x, axis=-1, keepdims=True) + EPS) * g


GELU_C0 = 0.7978845608028654
GELU_C1 = GELU_C0 * 0.044715


def _gelu(x):
    return 0.5 * x * (1.0 + jnp.tanh(x * (GELU_C0 + GELU_C1 * (x * x))))


def _dot(a, b):
    return jnp.dot(a, b, preferred_element_type=_F32)


def _full_spec(shape):
    n = len(shape)
    return pl.BlockSpec(shape, lambda *_: (0,) * n)


def _lru_body(x_ref, nm_ref, wlx_ref, wlg_ref, wg0_ref, cw_ref, cb_ref, wa_ref, ba_ref, wi_ref,
              bi_ref, lam_ref, out_ref, buf_ref, hc_ref, a_ref, b_ref):
    ts = x_ref.shape[1]
    c_dim = out_ref.shape[2]

    @pl.when(pl.program_id(1) == 0)
    def _():
        buf_ref[0:SUBLANES, :] = jnp.zeros((SUBLANES, c_dim), _F32)
        hc_ref[...] = jnp.zeros_like(hc_ref)

    h = _rms(x_ref[0], nm_ref[...]).astype(_BF)
    lx = _dot(h, wlx_ref[...])
    buf_ref[SUBLANES:SUBLANES + ts, :] = lx
    cw = cw_ref[...]
    xc = cb_ref[...] + cw[3:4] * lx
    for j in range(CONV_WIDTH - 1):
        off = SUBLANES - (CONV_WIDTH - 1 - j)
        xc = xc + cw[j:j + 1] * buf_ref[off:off + ts, :]
    buf_ref[0:SUBLANES, :] = buf_ref[ts:ts + SUBLANES, :]

    xcb = xc.astype(_BF)
    r = jax.nn.sigmoid(_dot(xcb, wa_ref[...]) + ba_ref[...])
    ig = jax.nn.sigmoid(_dot(xcb, wi_ref[...]) + bi_ref[...])
    lam = lam_ref[...]
    softplus_neg_lam = jnp.maximum(-lam, 0.0) + jnp.log1p(jnp.exp(-jnp.abs(lam)))
    log_a = (-LRU_C) * r * softplus_neg_lam
    a = jnp.exp(log_a)
    u = jnp.sqrt(jnp.tanh(-log_a) * (a * a + 1.0)) * (ig * xc)

    rid = lax.broadcasted_iota(jnp.int32, (ts, c_dim), 0) & (SUBLANES - 1)
    for d in (1, 2, 4):
        ok = rid >= d
        a_s = pltpu.roll(a, d, 0)
        u_s = pltpu.roll(u, d, 0)
        u = jnp.where(ok, a * u_s + u, u)
        a = jnp.where(ok, a * a_s, a)
    a_ref[...] = a
    b_ref[...] = u

    def step(g, hp):
        r0 = pl.multiple_of(g * SUBLANES, SUBLANES)
        hcur = a_ref[pl.ds(r0, SUBLANES), :] * hp + b_ref[pl.ds(r0, SUBLANES), :]
        b_ref[pl.ds(r0, SUBLANES), :] = hcur
        return jnp.broadcast_to(hcur[SUBLANES - 1:SUBLANES, :], (SUBLANES, c_dim))

    hc_ref[...] = lax.fori_loop(0, ts // SUBLANES, step, hc_ref[...])
    gate = _gelu(_dot(h, wlg_ref[...])) * jax.nn.sigmoid(_dot(h, wg0_ref[...]))
    out_ref[0] = (b_ref[...] * gate).astype(_BF)


def _lru_branch(x, nm, wlx, wlg, wg0, cw, cb, wa, ba, wi, bi, lam, ts):
    b, s, d = x.shape
    c = wlx.shape[1]
    row = lambda v: v.reshape(1, -1)
    args = (x, row(nm), wlx, wlg, wg0, cw, row(cb), wa, row(ba), wi, row(bi), row(lam))
    in_specs = [pl.BlockSpec((1, ts, d), lambda i, j: (i, j, 0))]
    in_specs += [_full_spec(a.shape) for a in args[1:]]
    return pl.pallas_call(
        _lru_body,
        grid=(b, s // ts),
        in_specs=in_specs,
        out_specs=pl.BlockSpec((1, ts, c), lambda i, j: (i, j, 0)),
        out_shape=jax.ShapeDtypeStruct((b, s, c), _BF),
        scratch_shapes=[pltpu.VMEM((ts + SUBLANES, c), _F32), pltpu.VMEM((SUBLANES, c), _F32),
                        pltpu.VMEM((ts, c), _F32), pltpu.VMEM((ts, c), _F32)],
        compiler_params=_cparams("arbitrary", "arbitrary"),
        name="lru",
    )(*args)


def _rope128(x, cos, sin):
    return x * cos + pltpu.roll(x, LANES // 2, 1) * sin


def _rope64(x, cos, sin, first_half):
    rot = jnp.where(first_half, pltpu.roll(x, LANES - IDX_DIM // 2, 1), pltpu.roll(x, IDX_DIM // 2, 1))
    return x * cos + rot * sin


def _dsa_proj_body(x_ref, nm_ref, wq_ref, wk_ref, wv_ref, wqi_ref, wki_ref, wwi_ref, wg1_ref,
                   qn_ref, kn_ref, ikn_ref, c128_ref, s128_ref, c64_ref, s64_ref,
                   qT_ref, k_ref, vT_ref, qiT_ref, ki_ref, wT_ref, g1_ref):
    ts = x_ref.shape[1]
    h = _rms(x_ref[0], nm_ref[...]).astype(_BF)
    c128, s128 = c128_ref[...], s128_ref[...]
    c64, s64 = c64_ref[...], s64_ref[...]
    first_half = (lax.broadcasted_iota(jnp.int32, (ts, LANES), 1) & (IDX_DIM - 1)) < IDX_DIM // 2

    def head_norm_rope(t, g, scale):
        outs = []
        for i in range(t.shape[1] // LANES):
            th = _rms(t[:, i * LANES:(i + 1) * LANES], g)
            outs.append(_rope128(th, c128, s128) * scale)
        return jnp.concatenate(outs, axis=1)

    q = head_norm_rope(_dot(h, wq_ref[...]), qn_ref[...], float(LANES) ** -0.5)
    qT_ref[0] = q.T.astype(_BF)
    k_ref[0] = head_norm_rope(_dot(h, wk_ref[...]), kn_ref[...], 1.0).astype(_BF)
    vT = _dot(h, wv_ref[...]).T.astype(_BF)
    for g in range(ATT_KV_HEADS):
        vT_ref[0, 0, g, :LANES, :] = vT[g * LANES:(g + 1) * LANES, :]
        vT_ref[0, 0, g, LANES:, :] = jnp.ones((ONES_ROWS, ts), _BF)

    qi = _dot(h, wqi_ref[...])
    qi = jnp.concatenate(
        [_rope64(qi[:, i * LANES:(i + 1) * LANES], c64, s64, first_half) for i in range(qi.shape[1] // LANES)],
        axis=1) * (float(IDX_DIM) ** -0.5)
    qiT_ref[0] = qi.T.astype(_BF)
    ki = _rope64(_rms(_dot(h, wki_ref[...]), ikn_ref[...]), c64, s64, first_half)
    ki_ref[0] = ki[:, :IDX_DIM].astype(_BF)
    wi = _dot(h, wwi_ref[...]) * (float(IDX_HEADS) ** -0.5)
    wT_ref[0] = wi.T[:IDX_HEADS, :]
    g1_ref[0] = jax.nn.sigmoid(_dot(h, wg1_ref[...])).astype(_BF)


def _dsa_proj(x, nm, wq, wk, wv, wqi, wki2, wwi, wg1, qn, kn, ikn2, c128, s128, c64, s64, ts):
    b, s, d = x.shape
    row = lambda v: v.reshape(1, -1)
    nq, nk, nv, nqi = wq.shape[1], wk.shape[1], wv.shape[1], wqi.shape[1]
    args = (x, row(nm), wq, wk, wv, wqi, wki2, wwi, wg1, row(qn), row(kn), row(ikn2), c128, s128, c64, s64)
    tile = lambda n: pl.BlockSpec((1, ts, n), lambda i, j: (i, j, 0))
    tile_t = lambda n: pl.BlockSpec((1, n, ts), lambda i, j: (i, 0, j))
    tab = pl.BlockSpec((ts, LANES), lambda i, j: (j, 0))
    in_specs = [tile(d)] + [_full_spec(a.shape) for a in args[1:12]] + [tab] * 4
    out_shape = (
        jax.ShapeDtypeStruct((b, nq, s), _BF),
        jax.ShapeDtypeStruct((b, s, nk), _BF),
        jax.ShapeDtypeStruct((b, s // ts, ATT_KV_HEADS, LANES + ONES_ROWS, ts), _BF),
        jax.ShapeDtypeStruct((b, nqi, s), _BF),
        jax.ShapeDtypeStruct((b, s, IDX_DIM), _BF),
        jax.ShapeDtypeStruct((b, IDX_HEADS, s), _F32),
        jax.ShapeDtypeStruct((b, s, d), _BF),
    )
    out_specs = (tile_t(nq), tile(nk),
                 pl.BlockSpec((1, 1, ATT_KV_HEADS, LANES + ONES_ROWS, ts), lambda i, j: (i, j, 0, 0, 0)), tile_t(nqi),
                 tile(IDX_DIM), tile_t(IDX_HEADS), tile(d))
    return pl.pallas_call(
        _dsa_proj_body, grid=(b, s // ts), in_specs=in_specs, out_specs=out_specs, out_shape=out_shape,
        compiler_params=_cparams("arbitrary", "arbitrary"), name="dsa_proj",
    )(*args)


def _key_to_float(u):
    key = u ^ jnp.int32(INT_MIN)
    bits = key ^ ((key >> 31) & jnp.int32(0x7FFFFFFF))
    return jnp.where((u >> 23) == 0, -jnp.inf, lax.bitcast_convert_type(bits, _F32))


def _dsa_body(qT_ref, qiT_ref, wT_ref, k_ref, vT_ref, ki_ref, g1_ref, out_ref,
              sc_ref, m_ref, l_ref, acc_ref, bias_ref, s_ref, p_ref, *, topk, tk):
    tq = qT_ref.shape[2]
    dh = LANES
    q0 = pl.program_id(1) * tq
    n_chunks = (q0 + tq + tk - 1) // tk

    qicat = jnp.concatenate([qiT_ref[0, h * IDX_DIM:(h + 1) * IDX_DIM, :] for h in range(IDX_HEADS)], axis=1)
    w = wT_ref[0]
    qpos = q0 + lax.broadcasted_iota(jnp.int32, (tk, tq), 1)
    krow = lax.broadcasted_iota(jnp.int32, (tk, tq), 0)

    def score_chunk(c, carry):
        r0 = pl.multiple_of(c * tk, tk)
        lg = _dot(ki_ref[0, pl.ds(r0, tk), :], qicat)
        s = jnp.zeros((tk, tq), _F32)
        for h in range(IDX_HEADS):
            s = s + jnp.maximum(lg[:, h * tq:(h + 1) * tq], 0.0) * w[h:h + 1, :]
        sc_ref[pl.ds(r0, tk), :] = jnp.where(krow + r0 <= qpos, s, -jnp.inf)
        return carry

    lax.fori_loop(0, n_chunks, score_chunk, 0)

    def make_count(nc):
        def count_ge(cand):
            parts = [jnp.zeros((SUBLANES, tq), _F32) for _ in range(4)]
            n = 0
            for c in range(nc):
                hit = jnp.where(sc_ref[c * tk:(c + 1) * tk, :] >= cand, 1.0, 0.0)
                for r in range(tk // SUBLANES):
                    parts[n % 4] = parts[n % 4] + hit[r * SUBLANES:(r + 1) * SUBLANES, :]
                    n += 1
            return jnp.sum((parts[0] + parts[1]) + (parts[2] + parts[3]), axis=0, keepdims=True)
        return count_ge

    counters = [make_count(nc) for nc in range(1, sc_ref.shape[0] // tk + 1)]

    def bisect(i, u):
        trial = u | lax.shift_left(jnp.int32(1), 31 - i)
        cnt = lax.switch(n_chunks - 1, counters, _key_to_float(trial))
        return jnp.where(cnt >= float(topk), trial, u)

    u = lax.fori_loop(0, 32, bisect, jnp.zeros((1, tq), jnp.int32))
    tau = jnp.maximum(_key_to_float(u), -FLT_MAX)

    m_ref[...] = jnp.full(m_ref.shape, MASK_VALUE, _F32)
    l_ref[...] = jnp.zeros_like(l_ref)
    acc_ref[...] = jnp.zeros_like(acc_ref)

    sub = 8 * SUBLANES
    groups = [slice(r * SUBLANES, (r + 1) * SUBLANES) for r in range(sub // SUBLANES)]

    def att_chunk(c, carry):
        r0 = pl.multiple_of(c * tk, tk)
        bias_ref[...] = jnp.where(sc_ref[pl.ds(r0, tk), :] >= tau, 0.0, MASK_VALUE)

        def logits(h):
            g = h // ATT_GROUP
            s_ref[h] = _dot(k_ref[0, pl.ds(r0, tk), g * dh:(g + 1) * dh], qT_ref[0, h * dh:(h + 1) * dh, :])

        logits(0)
        for h in range(ATT_HEADS):
            if h + 1 < ATT_HEADS:
                logits(h + 1)
            m_old = m_ref[h:h + 1, :]
            mx = jnp.broadcast_to(m_old, (SUBLANES, tq))
            for i in range(tk // sub):
                rows = slice(i * sub, (i + 1) * sub)
                sb = s_ref[h, rows, :] + bias_ref[rows, :]
                s_ref[h, rows, :] = sb
                for gr in groups:
                    mx = jnp.maximum(mx, sb[gr, :])
            m_new = jnp.max(mx, axis=0, keepdims=True)
            for i in range(tk // sub):
                rows = slice(i * sub, (i + 1) * sub)
                p_ref[h, rows, :] = jnp.exp((s_ref[h, rows, :] - m_new).astype(_BF))
            alpha = jnp.exp(m_old - m_new)
            pv = _dot(vT_ref[0, c, h // ATT_GROUP], p_ref[h])
            l_ref[h:h + 1, :] = alpha * l_ref[h:h + 1, :] + pv[dh:dh + 1, :]
            acc_ref[h * dh:(h + 1) * dh, :] = alpha * acc_ref[h * dh:(h + 1) * dh, :] + pv[:dh, :]
            m_ref[h:h + 1, :] = m_new
        return carry

    lax.fori_loop(0, n_chunks, att_chunk, 0)
    outs = []
    for h in range(ATT_HEADS):
        o = acc_ref[h * dh:(h + 1) * dh, :] / l_ref[h:h + 1, :]
        outs.append(o.T)
    out_ref[0] = (jnp.concatenate(outs, axis=1) * g1_ref[0].astype(_F32)).astype(_BF)


def _dsa(qT, qiT, wT, k, vT, ki, g1, topk, tq, tk):
    b, nq, s = qT.shape
    d = g1.shape[2]
    in_specs = [
        pl.BlockSpec((1, nq, tq), lambda i, j: (i, 0, j)),
        pl.BlockSpec((1, qiT.shape[1], tq), lambda i, j: (i, 0, j)),
        pl.BlockSpec((1, IDX_HEADS, tq), lambda i, j: (i, 0, j)),
        pl.BlockSpec((1, s, k.shape[2]), lambda i, j: (i, 0, 0)),
        pl.BlockSpec((1,) + vT.shape[1:], lambda i, j: (i, 0, 0, 0, 0)),
        pl.BlockSpec((1, s, IDX_DIM), lambda i, j: (i, 0, 0)),
        pl.BlockSpec((1, tq, d), lambda i, j: (i, j, 0)),
    ]
    return pl.pallas_call(
        functools.partial(_dsa_body, topk=topk, tk=tk),
        grid=(b, s // tq), in_specs=in_specs,
        out_specs=pl.BlockSpec((1, tq, d), lambda i, j: (i, j, 0)),
        out_shape=jax.ShapeDtypeStruct((b, s, d), _BF),
        scratch_shapes=[pltpu.VMEM((s, tq), _F32), pltpu.VMEM((ATT_HEADS, tq), _F32),
                        pltpu.VMEM((ATT_HEADS, tq), _F32), pltpu.VMEM((nq, tq), _F32),
                        pltpu.VMEM((tk, tq), _F32), pltpu.VMEM((ATT_HEADS, tk, tq), _F32),
                        pltpu.VMEM((ATT_HEADS, tk, tq), _BF)],
        compiler_params=_cparams("arbitrary", "arbitrary"), name="dsa",
    )(qT, qiT, wT, k, vT, ki, g1)


def _mem_kv_body(mem_ref, mn_ref, wkv_ref, mkn_ref, mkT_ref, mv_ref):
    d = mem_ref.shape[2]
    dm = d // MEM_HEADS
    m = _rms(mem_ref[0], mn_ref[...]).astype(_BF)
    kv = _dot(m, wkv_ref[...])
    mk = jnp.concatenate([_rms(kv[:, i * dm:(i + 1) * dm], mkn_ref[...]) for i in range(MEM_HEADS)], axis=1)
    mkT_ref[0] = mk.T.astype(_BF)
    mv_ref[0] = kv[:, d:].astype(_BF)


def _mem_kv(mem, mn, wkv, mkn):
    b, m, d = mem.shape
    row = lambda v: v.reshape(1, -1)
    return pl.pallas_call(
        _mem_kv_body, grid=(b,),
        in_specs=[pl.BlockSpec((1, m, d), lambda i: (i, 0, 0)), _full_spec((1, d)), _full_spec(wkv.shape),
                  _full_spec((1, d // MEM_HEADS))],
        out_specs=(pl.BlockSpec((1, d, m), lambda i: (i, 0, 0)), pl.BlockSpec((1, m, d), lambda i: (i, 0, 0))),
        out_shape=(jax.ShapeDtypeStruct((b, d, m), _BF), jax.ShapeDtypeStruct((b, m, d), _BF)),
        compiler_params=_cparams("arbitrary"), name="mem_kv",
    )(mem, row(mn), wkv, row(mkn))


def _mem_body(x_ref, nm_ref, wmq_ref, wg2_ref, mqn_ref, mkT_ref, mv_ref, out_ref):
    d = x_ref.shape[2]
    dm = d // MEM_HEADS
    h = _rms(x_ref[0], nm_ref[...]).astype(_BF)
    mq = _dot(h, wmq_ref[...])
    outs = []
    for i in range(MEM_HEADS):
        qh = (_rms(mq[:, i * dm:(i + 1) * dm], mqn_ref[...]) * (float(dm) ** -0.5)).astype(_BF)
        s = _dot(qh, mkT_ref[0, i * dm:(i + 1) * dm, :])
        p = jnp.exp(s - jnp.max(s, axis=-1, keepdims=True))
        o = _dot(p.astype(_BF), mv_ref[0, :, i * dm:(i + 1) * dm])
        outs.append(o / jnp.sum(p, axis=-1, keepdims=True))
    out_ref[0] = (jnp.concatenate(outs, axis=1) * jax.nn.sigmoid(_dot(h, wg2_ref[...]))).astype(_BF)


def _mem_branch(x, nm, wmq, wg2, mqn, mkT, mv, ts):
    b, s, d = x.shape
    m = mv.shape[1]
    row = lambda v: v.reshape(1, -1)
    return pl.pallas_call(
        _mem_body, grid=(b, s // ts),
        in_specs=[pl.BlockSpec((1, ts, d), lambda i, j: (i, j, 0)), _full_spec((1, d)), _full_spec(wmq.shape),
                  _full_spec(wg2.shape), _full_spec((1, d // MEM_HEADS)),
                  pl.BlockSpec((1, d, m), lambda i, j: (i, 0, 0)), pl.BlockSpec((1, m, d), lambda i, j: (i, 0, 0))],
        out_specs=pl.BlockSpec((1, ts, d), lambda i, j: (i, j, 0)),
        out_shape=jax.ShapeDtypeStruct((b, s, d), _BF),
        compiler_params=_cparams("arbitrary", "arbitrary"), name="mem",
    )(x, row(nm), wmq, wg2, row(mqn), mkT, mv)


def _mid_body(x_ref, a_ref, b_ref, c_ref, wo_ref, nf_ref, wpqT_ref, sk_ref, x1_ref, h2T_ref, scT_ref):
    merged = (a_ref[...].astype(_F32) + b_ref[...].astype(_F32) + c_ref[...].astype(_F32)).astype(_BF)
    x1 = x_ref[...] + _dot(merged, wo_ref[...])
    x1_ref[...] = x1
    h2T = _rms(x1, nf_ref[...]).T.astype(_BF)
    h2T_ref[...] = h2T
    pqT = _dot(wpqT_ref[...], h2T).astype(_BF)
    for c in range(sk_ref.shape[0]):
        scT_ref[c * PEER_KEYS:(c + 1) * PEER_KEYS, :] = _dot(sk_ref[c], pqT[c * LANES:(c + 1) * LANES, :])


def _mid(x2, ma, mb, mc, wo, nf, wpqT, sk, ts):
    t, d = x2.shape
    nsc = sk.shape[0] * PEER_KEYS
    tile = pl.BlockSpec((ts, d), lambda i: (i, 0))
    return pl.pallas_call(
        _mid_body, grid=(t // ts,),
        in_specs=[tile, tile, tile, tile, _full_spec(wo.shape), _full_spec((1, d)), _full_spec(wpqT.shape),
                  _full_spec(sk.shape)],
        out_specs=(tile, pl.BlockSpec((d, ts), lambda i: (0, i)), pl.BlockSpec((nsc, ts), lambda i: (0, i))),
        out_shape=(jax.ShapeDtypeStruct((t, d), _F32), jax.ShapeDtypeStruct((d, t), _BF),
                   jax.ShapeDtypeStruct((nsc, t), _F32)),
        compiler_params=_cparams("arbitrary"), name="mid",
    )(x2, ma, mb, mc, wo, nf.reshape(1, -1), wpqT, sk)


def _top_values(s, n):
    vals = []
    cur = s
    for i in range(n):
        m = jnp.max(cur, axis=0, keepdims=True)
        vals.append(m)
        if i + 1 < n:
            cur = jnp.where(cur == m, -jnp.inf, cur)
    return vals


def _route_body(scT_ref, tau_ref, c_ref):
    def head(h, carry):
        r1 = pl.multiple_of(h * 2 * PEER_KEYS, PEER_KEYS)
        r2 = pl.multiple_of(h * 2 * PEER_KEYS + PEER_KEYS, PEER_KEYS)
        n = PEER_TOPK + 1
        v1 = _top_values(scT_ref[pl.ds(r1, PEER_KEYS), :], n)
        v2 = _top_values(scT_ref[pl.ds(r2, PEER_KEYS), :], n)
        edge = SUBLANES
        assert 2 * (edge + 1) > n
        pad = [jnp.full_like(v1[0], -jnp.inf)] * (-n % SUBLANES)
        col1 = jnp.concatenate(v1 + pad, axis=0)
        col2 = jnp.concatenate(v2 + pad, axis=0)
        cand = jnp.concatenate([v1[a] + col2[:edge] for a in range(edge)]
                               + [v1[0] + col2[edge:], v2[0] + col1[edge:]], axis=0)
        best = _top_values(cand, n)
        tau = best[PEER_TOPK - 1]
        top = v1[0] + v2[0]
        z = jnp.sum(jnp.where(cand >= tau, jnp.exp(cand - top), 0.0), axis=0, keepdims=True)
        tau_ref[pl.ds(h, 1), :] = 0.5 * (tau + best[PEER_TOPK])
        c_ref[pl.ds(h, 1), :] = top + jnp.log(z)
        return carry

    lax.fori_loop(0, PEER_HEADS, head, 0, unroll=2)


def _route(scT, te):
    nsc, t = scT.shape
    spec = pl.BlockSpec((PEER_HEADS, te), lambda i: (0, i))
    return pl.pallas_call(
        _route_body, grid=(t // te,),
        in_specs=[pl.BlockSpec((nsc, te), lambda i: (0, i))],
        out_specs=(spec, spec),
        out_shape=(jax.ShapeDtypeStruct((PEER_HEADS, t), _F32),) * 2,
        compiler_params=_cparams("arbitrary"), name="route",
    )(scT)


def _experts_body(h2T_ref, scT_ref, tau_ref, c_ref, u_ref, vT_ref, x1_ref, out_ref,
                  acc_ref, e2_ref, m2_ref, act_ref, g_ref, rows_ref, *, rows_per_step):
    e = pl.program_id(1)
    nk = PEER_KEYS

    @pl.when(e == 0)
    def _():
        acc_ref[...] = jnp.zeros_like(acc_ref)
        for h in range(PEER_HEADS):
            s2 = scT_ref[(2 * h + 1) * nk:(2 * h + 2) * nk, :]
            m2 = jnp.max(s2, axis=0, keepdims=True)
            m2_ref[h:h + 1, :] = m2
            e2_ref[h * nk:(h + 1) * nk, :] = jnp.exp(s2 - m2)

    n_rows = rows_per_step * PEER_HEADS
    for i in range(rows_per_step):
        for h in range(PEER_HEADS):
            s1 = scT_ref[pl.ds(2 * h * nk + e * rows_per_step + i, 1), :]
            rows_ref[i * PEER_HEADS + h:i * PEER_HEADS + h + 1, :] = tau_ref[h:h + 1, :] - s1
            rows_ref[n_rows + i * PEER_HEADS + h:n_rows + i * PEER_HEADS + h + 1, :] = (
                0.5 * jnp.exp(s1 + m2_ref[h:h + 1, :] - c_ref[h:h + 1, :]))

    def first_matmul(i):
        act_ref[i * nk:(i + 1) * nk, :] = _dot(u_ref[i * nk:(i + 1) * nk, :], h2T_ref[...])

    pair = 2 * nk
    first_matmul(0)
    for i in range(rows_per_step):
        if i + 1 < rows_per_step:
            first_matmul(i + 1)
        wgt = jnp.zeros((nk, act_ref.shape[1]), _F32)
        for h in range(PEER_HEADS):
            r = i * PEER_HEADS + h
            s2 = scT_ref[(2 * h + 1) * nk:(2 * h + 2) * nk, :]
            hit = jnp.where(s2 >= rows_ref[r:r + 1, :], e2_ref[h * nk:(h + 1) * nk, :], 0.0)
            wgt = wgt + hit * rows_ref[n_rows + r:n_rows + r + 1, :]
        x = act_ref[i * nk:(i + 1) * nk, :]
        t = jnp.tanh(x * (GELU_C0 + GELU_C1 * (x * x)))
        g_ref[i * nk:(i + 1) * nk, :] = ((x + x * t) * wgt).astype(_BF)
        if i % 2 == 1:
            p = i // 2
            acc_ref[...] += _dot(vT_ref[:, p * pair:(p + 1) * pair], g_ref[p * pair:(p + 1) * pair, :])

    @pl.when(e == pl.num_programs(1) - 1)
    def _():
        out_ref[...] = x1_ref[...] + acc_ref[...].T


def _experts(h2T, scT, tau, cn, u, vT, x1, ts, rows_per_step):
    d, t = h2T.shape
    nsc = scT.shape[0]
    ne = u.shape[0]
    eb = rows_per_step * PEER_KEYS
    nb = ne // eb
    col = lambda n: pl.BlockSpec((n, ts), lambda i, e: (0, i))
    tile = pl.BlockSpec((ts, d), lambda i, e: (i, 0))
    return pl.pallas_call(
        functools.partial(_experts_body, rows_per_step=rows_per_step),
        grid=(t // ts, nb),
        in_specs=[col(d), col(nsc), col(PEER_HEADS), col(PEER_HEADS),
                  pl.BlockSpec((eb, d), lambda i, e: (e, 0)), pl.BlockSpec((d, eb), lambda i, e: (0, e)), tile],
        out_specs=tile,
        out_shape=jax.ShapeDtypeStruct((t, d), _F32),
        scratch_shapes=[pltpu.VMEM((d, ts), _F32), pltpu.VMEM((PEER_HEADS * PEER_KEYS, ts), _F32),
                        pltpu.VMEM((PEER_HEADS, ts), _F32), pltpu.VMEM((eb, ts), _F32),
                        pltpu.VMEM((eb, ts), _BF),
                        pltpu.VMEM((2 * rows_per_step * PEER_HEADS, ts), _F32)],
        compiler_params=_cparams("arbitrary", "arbitrary"), name="experts",
    )(h2T, scT, tau, cn, u, vT, x1)


def _block_diag(w):
    n, c, d = w.shape
    return jnp.einsum("ncd,nm->ncmd", w, jnp.eye(n, dtype=w.dtype)).reshape(n * c, n * d)


def _rope_tables(s, dim, reps):
    half = dim // 2
    freq = ROPE_THETA ** (-jnp.arange(half, dtype=_F32) / half)
    ang = jnp.arange(s, dtype=_F32)[:, None] * freq[None, :]
    cos = jnp.tile(jnp.concatenate([jnp.cos(ang), jnp.cos(ang)], axis=1), (1, reps))
    sin = jnp.tile(jnp.concatenate([-jnp.sin(ang), jnp.sin(ang)], axis=1), (1, reps))
    return cos, sin


def _tile(n, pref):
    return pref if n % pref == 0 else n


def _layer(x, mem, p, l):
    b, s, d = x.shape
    bf = lambda a: a.astype(_BF)
    w_in = p["w_in"][l]
    n_q, n_kv = ATT_HEADS * LANES, ATT_KV_HEADS * LANES
    n_qi = IDX_HEADS * IDX_DIM
    sizes = (d, d, n_q, n_kv, n_kv, n_qi, IDX_DIM, IDX_HEADS, d, 3 * d)
    offs = [0]
    for n in sizes:
        offs.append(offs[-1] + n)
    wlx, wlg, wq, wk, wv, wqi, wki, wwi, wmq, wgates = (w_in[:, offs[i]:offs[i + 1]] for i in range(len(sizes)))
    wki2 = jnp.concatenate([wki, wki], axis=1)
    wwi_p = jnp.pad(wwi, ((0, 0), (0, LANES - IDX_HEADS)))
    wg0, wg1, wg2 = wgates[:, :d], wgates[:, d:2 * d], wgates[:, 2 * d:]
    nm = p["norm_mix"][l]

    ts_a = _tile(s, 256)
    m_lru = _lru_branch(x, nm, bf(wlx), bf(wlg), bf(wg0), p["conv_w"][l], p["conv_b"][l],
                        bf(_block_diag(p["lru_wa"][l])), p["lru_ba"][l].reshape(-1),
                        bf(_block_diag(p["lru_wi"][l])), p["lru_bi"][l].reshape(-1), p["lru_lambda"][l], ts_a)

    tk = _tile(s, 512)
    c128, s128 = _rope_tables(s, LANES, 1)
    c64, s64 = _rope_tables(s, IDX_DIM, LANES // IDX_DIM)
    ikn2 = jnp.concatenate([p["idx_k_norm"][l]] * 2)
    qT, k, vT, qiT, ki, wT, g1 = _dsa_proj(x, nm, bf(wq), bf(wk), bf(wv), bf(wqi), bf(wki2), bf(wwi_p), bf(wg1),
                                           p["q_norm"][l], p["k_norm"][l], ikn2, c128, s128, c64, s64, tk)
    m_att = _dsa(qT, qiT, wT, k, vT, ki, g1, min(TOPK_MAX, s // 4), _tile(s, 256), tk)

    mkT, mv = _mem_kv(mem, p["mem_norm"][l], bf(p["w_mem_kv"][l]), p["mem_k_norm"][l])
    m_mem = _mem_branch(x, nm, bf(wmq), bf(wg2), p["mem_q_norm"][l], mkT, mv, _tile(s, 512))

    t = b * s
    flat = lambda a: a.reshape(t, d)
    sk = bf(p["peer_subkeys"][l].reshape(2 * PEER_HEADS, PEER_KEYS, -1))
    x1, h2T, scT = _mid(flat(x), flat(m_lru), flat(m_att), flat(m_mem), bf(p["w_out"][l]), p["norm_ffn"][l],
                        bf(p["peer_wq"][l].T), sk, _tile(t, 512))
    tau, cn = _route(scT, _tile(t, 256))
    out = _experts(h2T, scT, tau, cn, bf(p["peer_u"][l]), bf(p["peer_v"][l].T), x1, _tile(t, 512), 8)
    return out.reshape(b, s, d)


def kernel(x, mem, norm_mix, w_in, conv_w, conv_b, lru_wa, lru_ba, lru_wi, lru_bi, lru_lambda, q_norm, k_norm,
           idx_k_norm, mem_norm, w_mem_kv, mem_q_norm, mem_k_norm, w_out, norm_ffn, peer_wq, peer_subkeys,
           peer_u, peer_v):
    p = dict(norm_mix=norm_mix, w_in=w_in, conv_w=conv_w, conv_b=conv_b, lru_wa=lru_wa, lru_ba=lru_ba,
             lru_wi=lru_wi, lru_bi=lru_bi, lru_lambda=lru_lambda, q_norm=q_norm, k_norm=k_norm,
             idx_k_norm=idx_k_norm, mem_norm=mem_norm, w_mem_kv=w_mem_kv, mem_q_norm=mem_q_norm,
             mem_k_norm=mem_k_norm, w_out=w_out, norm_ffn=norm_ffn, peer_wq=peer_wq, peer_subkeys=peer_subkeys,
             peer_u=peer_u, peer_v=peer_v)
    for l in range(norm_mix.shape[0]):
        x = _layer(x, mem, p, l)
    return x
```

```python
import functools

import jax
import jax.numpy as jnp
from jax import lax
from jax.experimental import pallas as pl
from jax.experimental.pallas import tpu as pltpu

_F32 = jnp.float32
_BF = jnp.bfloat16

EPS = 1e-6
ROPE_THETA = 10000.0
LRU_C = 8.0
LRU_BLOCKS = 16
CONV_WIDTH = 4
ATT_HEADS = 8
ATT_KV_HEADS = 2
ATT_GROUP = ATT_HEADS // ATT_KV_HEADS
IDX_HEADS = 8
IDX_DIM = 64
TOPK_MAX = 256
MEM_HEADS = 4
PEER_HEADS = 8
PEER_KEYS = 128
PEER_TOPK = 16

LANES = 128
SUBLANES = 8
ONES_ROWS = 2 * SUBLANES
VMEM_LIMIT_BYTES = 56 * 1024 * 1024
MASK_VALUE = -1e30
FLT_MAX = 3.4028234663852886e38
INT_MIN = -(2 ** 31)


def _cparams(*sem):
    return pltpu.CompilerParams(dimension_semantics=sem, vmem_limit_bytes=VMEM_LIMIT_BYTES)


def _rms(x, g):
    return x * lax.rsqrt(jnp.mean(x * x, axis=-1, keepdims=True) + EPS) * g


GELU_C0 = 0.7978845608028654
GELU_C1 = GELU_C0 * 0.044715


def _gelu(x):
    return 0.5 * x * (1.0 + jnp.tanh(x * (GELU_C0 + GELU_C1 * (x * x))))


def _dot(a, b):
    return jnp.dot(a, b, preferred_element_type=_F32)


def _full_spec(shape):
    n = len(shape)
    return pl.BlockSpec(shape, lambda *_: (0,) * n)


def _lru_body(x_ref, nm_ref, wlx_ref, wlg_ref, wg0_ref, cw_ref, cb_ref, wa_ref, ba_ref, wi_ref,
              bi_ref, lam_ref, out_ref, buf_ref, hc_ref, a_ref, b_ref):
    ts = x_ref.shape[1]
    c_dim = out_ref.shape[2]

    @pl.when(pl.program_id(1) == 0)
    def _():
        buf_ref[0:SUBLANES, :] = jnp.zeros((SUBLANES, c_dim), _F32)
        hc_ref[...] = jnp.zeros_like(hc_ref)

    h = _rms(x_ref[0], nm_ref[...]).astype(_BF)
    lx = _dot(h, wlx_ref[...])
    buf_ref[SUBLANES:SUBLANES + ts, :] = lx
    cw = cw_ref[...]
    xc = cb_ref[...] + cw[3:4] * lx
    for j in range(CONV_WIDTH - 1):
        off = SUBLANES - (CONV_WIDTH - 1 - j)
        xc = xc + cw[j:j + 1] * buf_ref[off:off + ts, :]
    buf_ref[0:SUBLANES, :] = buf_ref[ts:ts + SUBLANES, :]

    xcb = xc.astype(_BF)
    r = jax.nn.sigmoid(_dot(xcb, wa_ref[...]) + ba_ref[...])
    ig = jax.nn.sigmoid(_dot(xcb, wi_ref[...]) + bi_ref[...])
    lam = lam_ref[...]
    softplus_neg_lam = jnp.maximum(-lam, 0.0) + jnp.log1p(jnp.exp(-jnp.abs(lam)))
    log_a = (-LRU_C) * r * softplus_neg_lam
    a = jnp.exp(log_a)
    u = jnp.sqrt(jnp.tanh(-log_a) * (a * a + 1.0)) * (ig * xc)

    rid = lax.broadcasted_iota(jnp.int32, (ts, c_dim), 0) & (SUBLANES - 1)
    for d in (1, 2, 4):
        ok = rid >= d
        a_s = pltpu.roll(a, d, 0)
        u_s = pltpu.roll(u, d, 0)
        u = jnp.where(ok, a * u_s + u, u)
        a = jnp.where(ok, a * a_s, a)
    a_ref[...] = a
    b_ref[...] = u

    def step(g, hp):
        r0 = pl.multiple_of(g * SUBLANES, SUBLANES)
        hcur = a_ref[pl.ds(r0, SUBLANES), :] * hp + b_ref[pl.ds(r0, SUBLANES), :]
        b_ref[pl.ds(r0, SUBLANES), :] = hcur
        return jnp.broadcast_to(hcur[SUBLANES - 1:SUBLANES, :], (SUBLANES, c_dim))

    hc_ref[...] = lax.fori_loop(0, ts // SUBLANES, step, hc_ref[...])
    gate = _gelu(_dot(h, wlg_ref[...])) * jax.nn.sigmoid(_dot(h, wg0_ref[...]))
    out_ref[0] = (b_ref[...] * gate).astype(_BF)


def _lru_branch(x, nm, wlx, wlg, wg0, cw, cb, wa, ba, wi, bi, lam, ts):
    b, s, d = x.shape
    c = wlx.shape[1]
    row = lambda v: v.reshape(1, -1)
    args = (x, row(nm), wlx, wlg, wg0, cw, row(cb), wa, row(ba), wi, row(bi), row(lam))
    in_specs = [pl.BlockSpec((1, ts, d), lambda i, j: (i, j, 0))]
    in_specs += [_full_spec(a.shape) for a in args[1:]]
    return pl.pallas_call(
        _lru_body,
        grid=(b, s // ts),
        in_specs=in_specs,
        out_specs=pl.BlockSpec((1, ts, c), lambda i, j: (i, j, 0)),
        out_shape=jax.ShapeDtypeStruct((b, s, c), _BF),
        scratch_shapes=[pltpu.VMEM((ts + SUBLANES, c), _F32), pltpu.VMEM((SUBLANES, c), _F32),
                        pltpu.VMEM((ts, c), _F32), pltpu.VMEM((ts, c), _F32)],
        compiler_params=_cparams("arbitrary", "arbitrary"),
        name="lru",
    )(*args)


def _rope128(x, cos, sin):
    return x * cos + pltpu.roll(x, LANES // 2, 1) * sin


def _rope64(x, cos, sin, first_half):
    rot = jnp.where(first_half, pltpu.roll(x, LANES - IDX_DIM // 2, 1), pltpu.roll(x, IDX_DIM // 2, 1))
    return x * cos + rot * sin


def _dsa_proj_body(x_ref, nm_ref, wq_ref, wk_ref, wv_ref, wqi_ref, wki_ref, wwi_ref, wg1_ref,
                   qn_ref, kn_ref, ikn_ref, c128_ref, s128_ref, c64_ref, s64_ref,
                   qT_ref, k_ref, vT_ref, qiT_ref, ki_ref, wT_ref, g1_ref):
    ts = x_ref.shape[1]
    h = _rms(x_ref[0], nm_ref[...]).astype(_BF)
    c128, s128 = c128_ref[...], s128_ref[...]
    c64, s64 = c64_ref[...], s64_ref[...]
    first_half = (lax.broadcasted_iota(jnp.int32, (ts, LANES), 1) & (IDX_DIM - 1)) < IDX_DIM // 2

    def head_norm_rope(t, g, scale):
        outs = []
        for i in range(t.shape[1] // LANES):
            th = _rms(t[:, i * LANES:(i + 1) * LANES], g)
            outs.append(_rope128(th, c128, s128) * scale)
        return jnp.concatenate(outs, axis=1)

    q = head_norm_rope(_dot(h, wq_ref[...]), qn_ref[...], float(LANES) ** -0.5)
    qT_ref[0] = q.T.astype(_BF)
    k_ref[0] = head_norm_rope(_dot(h, wk_ref[...]), kn_ref[...], 1.0).astype(_BF)
    vT = _dot(h, wv_ref[...]).T.astype(_BF)
    for g in range(ATT_KV_HEADS):
        vT_ref[0, 0, g, :LANES, :] = vT[g * LANES:(g + 1) * LANES, :]
        vT_ref[0, 0, g, LANES:, :] = jnp.ones((ONES_ROWS, ts), _BF)

    qi = _dot(h, wqi_ref[...])
    qi = jnp.concatenate(
        [_rope64(qi[:, i * LANES:(i + 1) * LANES], c64, s64, first_half) for i in range(qi.shape[1] // LANES)],
        axis=1) * (float(IDX_DIM) ** -0.5)
    qiT_ref[0] = qi.T.astype(_BF)
    ki = _rope64(_rms(_dot(h, wki_ref[...]), ikn_ref[...]), c64, s64, first_half)
    ki_ref[0] = ki[:, :IDX_DIM].astype(_BF)
    wi = _dot(h, wwi_ref[...]) * (float(IDX_HEADS) ** -0.5)
    wT_ref[0] = wi.T[:IDX_HEADS, :]
    g1_ref[0] = jax.nn.sigmoid(_dot(h, wg1_ref[...])).astype(_BF)


def _dsa_proj(x, nm, wq, wk, wv, wqi, wki2, wwi, wg1, qn, kn, ikn2, c128, s128, c64, s64, ts):
    b, s, d = x.shape
    row = lambda v: v.reshape(1, -1)
    nq, nk, nv, nqi = wq.shape[1], wk.shape[1], wv.shape[1], wqi.shape[1]
    args = (x, row(nm), wq, wk, wv, wqi, wki2, wwi, wg1, row(qn), row(kn), row(ikn2), c128, s128, c64, s64)
    tile = lambda n: pl.BlockSpec((1, ts, n), lambda i, j: (i, j, 0))
    tile_t = lambda n: pl.BlockSpec((1, n, ts), lambda i, j: (i, 0, j))
    tab = pl.BlockSpec((ts, LANES), lambda i, j: (j, 0))
    in_specs = [tile(d)] + [_full_spec(a.shape) for a in args[1:12]] + [tab] * 4
    out_shape = (
        jax.ShapeDtypeStruct((b, nq, s), _BF),
        jax.ShapeDtypeStruct((b, s, nk), _BF),
        jax.ShapeDtypeStruct((b, s // ts, ATT_KV_HEADS, LANES + ONES_ROWS, ts), _BF),
        jax.ShapeDtypeStruct((b, nqi, s), _BF),
        jax.ShapeDtypeStruct((b, s, IDX_DIM), _BF),
        jax.ShapeDtypeStruct((b, IDX_HEADS, s), _F32),
        jax.ShapeDtypeStruct((b, s, d), _BF),
    )
    out_specs = (tile_t(nq), tile(nk),
                 pl.BlockSpec((1, 1, ATT_KV_HEADS, LANES + ONES_ROWS, ts), lambda i, j: (i, j, 0, 0, 0)), tile_t(nqi),
                 tile(IDX_DIM), tile_t(IDX_HEADS), tile(d))
    return pl.pallas_call(
        _dsa_proj_body, grid=(b, s // ts), in_specs=in_specs, out_specs=out_specs, out_shape=out_shape,
        compiler_params=_cparams("arbitrary", "arbitrary"), name="dsa_proj",
    )(*args)


def _key_to_float(u):
    key = u ^ jnp.int32(INT_MIN)
    bits = key ^ ((key >> 31) & jnp.int32(0x7FFFFFFF))
    return jnp.where((u >> 23) == 0, -jnp.inf, lax.bitcast_convert_type(bits, _F32))


def _dsa_body(qT_ref, qiT_ref, wT_ref, k_ref, vT_ref, ki_ref, g1_ref, out_ref,
              sc_ref, m_ref, l_ref, acc_ref, bias_ref, s_ref, p_ref, *, topk, tk):
    tq = qT_ref.shape[2]
    dh = LANES
    q0 = pl.program_id(1) * tq
    n_chunks = (q0 + tq + tk - 1) // tk

    qicat = jnp.concatenate([qiT_ref[0, h * IDX_DIM:(h + 1) * IDX_DIM, :] for h in range(IDX_HEADS)], axis=1)
    w = wT_ref[0]
    qpos = q0 + lax.broadcasted_iota(jnp.int32, (tk, tq), 1)
    krow = lax.broadcasted_iota(jnp.int32, (tk, tq), 0)

    def score_chunk(c, carry):
        r0 = pl.multiple_of(c * tk, tk)
        lg = _dot(ki_ref[0, pl.ds(r0, tk), :], qicat)
        s = jnp.zeros((tk, tq), _F32)
        for h in range(IDX_HEADS):
            s = s + jnp.maximum(lg[:, h * tq:(h + 1) * tq], 0.0) * w[h:h + 1, :]
        sc_ref[pl.ds(r0, tk), :] = jnp.where(krow + r0 <= qpos, s, -jnp.inf)
        return carry

    lax.fori_loop(0, n_chunks, score_chunk, 0)

    def make_count(nc):
        def count_ge(cand):
            parts = [jnp.zeros((SUBLANES, tq), _F32) for _ in range(4)]
            n = 0
            for c in range(nc):
                hit = jnp.where(sc_ref[c * tk:(c + 1) * tk, :] >= cand, 1.0, 0.0)
                for r in range(tk // SUBLANES):
                    parts[n % 4] = parts[n % 4] + hit[r * SUBLANES:(r + 1) * SUBLANES, :]
                    n += 1
            return jnp.sum((parts[0] + parts[1]) + (parts[2] + parts[3]), axis=0, keepdims=True)
        return count_ge

    counters = [make_count(nc) for nc in range(1, sc_ref.shape[0] // tk + 1)]

    def bisect(i, u):
        trial = u | lax.shift_left(jnp.int32(1), 31 - i)
        cnt = lax.switch(n_chunks - 1, counters, _key_to_float(trial))
        return jnp.where(cnt >= float(topk), trial, u)

    u = lax.fori_loop(0, 32, bisect, jnp.zeros((1, tq), jnp.int32))
    tau = jnp.maximum(_key_to_float(u), -FLT_MAX)

    m_ref[...] = jnp.full(m_ref.shape, MASK_VALUE, _F32)
    l_ref[...] = jnp.zeros_like(l_ref)
    acc_ref[...] = jnp.zeros_like(acc_ref)

    sub = 8 * SUBLANES
    groups = [slice(r * SUBLANES, (r + 1) * SUBLANES) for r in range(sub // SUBLANES)]

    def att_chunk(c, carry):
        r0 = pl.multiple_of(c * tk, tk)
        bias_ref[...] = jnp.where(sc_ref[pl.ds(r0, tk), :] >= tau, 0.0, MASK_VALUE)

        def logits(h):
            g = h // ATT_GROUP
            s_ref[h] = _dot(k_ref[0, pl.ds(r0, tk), g * dh:(g + 1) * dh], qT_ref[0, h * dh:(h + 1) * dh, :])

        logits(0)
        for h in range(ATT_HEADS):
            if h + 1 < ATT_HEADS:
                logits(h + 1)
            m_old = m_ref[h:h + 1, :]
            mx = jnp.broadcast_to(m_old, (SUBLANES, tq))
            for i in range(tk // sub):
                rows = slice(i * sub, (i + 1) * sub)
                sb = s_ref[h, rows, :] + bias_ref[rows, :]
                s_ref[h, rows, :] = sb
                for gr in groups:
                    mx = jnp.maximum(mx, sb[gr, :])
            m_new = jnp.max(mx, axis=0, keepdims=True)
            for i in range(tk // sub):
                rows = slice(i * sub, (i + 1) * sub)
                p_ref[h, rows, :] = jnp.exp((s_ref[h, rows, :] - m_new).astype(_BF))
            alpha = jnp.exp(m_old - m_new)
            pv = _dot(vT_ref[0, c, h // ATT_GROUP], p_ref[h])
            l_ref[h:h + 1, :] = alpha * l_ref[h:h + 1, :] + pv[dh:dh + 1, :]
            acc_ref[h * dh:(h + 1) * dh, :] = alpha * acc_ref[h * dh:(h + 1) * dh, :] + pv[:dh, :]
            m_ref[h:h + 1, :] = m_new
        return carry

    lax.fori_loop(0, n_chunks, att_chunk, 0)
    outs = []
    for h in range(ATT_HEADS):
        o = acc_ref[h * dh:(h + 1) * dh, :] / l_ref[h:h + 1, :]
        outs.append(o.T)
    out_ref[0] = (jnp.concatenate(outs, axis=1) * g1_ref[0].astype(_F32)).astype(_BF)


def _dsa(qT, qiT, wT, k, vT, ki, g1, topk, tq, tk):
    b, nq, s = qT.shape
    d = g1.shape[2]
    in_specs = [
        pl.BlockSpec((1, nq, tq), lambda i, j: (i, 0, j)),
        pl.BlockSpec((1, qiT.shape[1], tq), lambda i, j: (i, 0, j)),
        pl.BlockSpec((1, IDX_HEADS, tq), lambda i, j: (i, 0, j)),
        pl.BlockSpec((1, s, k.shape[2]), lambda i, j: (i, 0, 0)),
        pl.BlockSpec((1,) + vT.shape[1:], lambda i, j: (i, 0, 0, 0, 0)),
        pl.BlockSpec((1, s, IDX_DIM), lambda i, j: (i, 0, 0)),
        pl.BlockSpec((1, tq, d), lambda i, j: (i, j, 0)),
    ]
    return pl.pallas_call(
        functools.partial(_dsa_body, topk=topk, tk=tk),
        grid=(b, s // tq), in_specs=in_specs,
        out_specs=pl.BlockSpec((1, tq, d), lambda i, j: (i, j, 0)),
        out_shape=jax.ShapeDtypeStruct((b, s, d), _BF),
        scratch_shapes=[pltpu.VMEM((s, tq), _F32), pltpu.VMEM((ATT_HEADS, tq), _F32),
                        pltpu.VMEM((ATT_HEADS, tq), _F32), pltpu.VMEM((nq, tq), _F32),
                        pltpu.VMEM((tk, tq), _F32), pltpu.VMEM((ATT_HEADS, tk, tq), _F32),
                        pltpu.VMEM((ATT_HEADS, tk, tq), _BF)],
        compiler_params=_cparams("arbitrary", "arbitrary"), name="dsa",
    )(qT, qiT, wT, k, vT, ki, g1)


def _mem_kv_body(mem_ref, mn_ref, wkv_ref, mkn_ref, mkT_ref, mv_ref):
    d = mem_ref.shape[2]
    dm = d // MEM_HEADS
    m = _rms(mem_ref[0], mn_ref[...]).astype(_BF)
    kv = _dot(m, wkv_ref[...])
    mk = jnp.concatenate([_rms(kv[:, i * dm:(i + 1) * dm], mkn_ref[...]) for i in range(MEM_HEADS)], axis=1)
    mkT_ref[0] = mk.T.astype(_BF)
    mv_ref[0] = kv[:, d:].astype(_BF)


def _mem_kv(mem, mn, wkv, mkn):
    b, m, d = mem.shape
    row = lambda v: v.reshape(1, -1)
    return pl.pallas_call(
        _mem_kv_body, grid=(b,),
        in_specs=[pl.BlockSpec((1, m, d), lambda i: (i, 0, 0)), _full_spec((1, d)), _full_spec(wkv.shape),
                  _full_spec((1, d // MEM_HEADS))],
        out_specs=(pl.BlockSpec((1, d, m), lambda i: (i, 0, 0)), pl.BlockSpec((1, m, d), lambda i: (i, 0, 0))),
        out_shape=(jax.ShapeDtypeStruct((b, d, m), _BF), jax.ShapeDtypeStruct((b, m, d), _BF)),
        compiler_params=_cparams("arbitrary"), name="mem_kv",
    )(mem, row(mn), wkv, row(mkn))


def _mem_body(x_ref, nm_ref, wmq_ref, wg2_ref, mqn_ref, mkT_ref, mv_ref, out_ref):
    d = x_ref.shape[2]
    dm = d // MEM_HEADS
    h = _rms(x_ref[0], nm_ref[...]).astype(_BF)
    mq = _dot(h, wmq_ref[...])
    outs = []
    for i in range(MEM_HEADS):
        qh = (_rms(mq[:, i * dm:(i + 1) * dm], mqn_ref[...]) * (float(dm) ** -0.5)).astype(_BF)
        s = _dot(qh, mkT_ref[0, i * dm:(i + 1) * dm, :])
        p = jnp.exp(s - jnp.max(s, axis=-1, keepdims=True))
        o = _dot(p.astype(_BF), mv_ref[0, :, i * dm:(i + 1) * dm])
        outs.append(o / jnp.sum(p, axis=-1, keepdims=True))
    out_ref[0] = (jnp.concatenate(outs, axis=1) * jax.nn.sigmoid(_dot(h, wg2_ref[...]))).astype(_BF)


def _mem_branch(x, nm, wmq, wg2, mqn, mkT, mv, ts):
    b, s, d = x.shape
    m = mv.shape[1]
    row = lambda v: v.reshape(1, -1)
    return pl.pallas_call(
        _mem_body, grid=(b, s // ts),
        in_specs=[pl.BlockSpec((1, ts, d), lambda i, j: (i, j, 0)), _full_spec((1, d)), _full_spec(wmq.shape),
                  _full_spec(wg2.shape), _full_spec((1, d // MEM_HEADS)),
                  pl.BlockSpec((1, d, m), lambda i, j: (i, 0, 0)), pl.BlockSpec((1, m, d), lambda i, j: (i, 0, 0))],
        out_specs=pl.BlockSpec((1, ts, d), lambda i, j: (i, j, 0)),
        out_shape=jax.ShapeDtypeStruct((b, s, d), _BF),
        compiler_params=_cparams("arbitrary", "arbitrary"), name="mem",
    )(x, row(nm), wmq, wg2, row(mqn), mkT, mv)


def _mid_body(x_ref, a_ref, b_ref, c_ref, wo_ref, nf_ref, wpqT_ref, sk_ref, x1_ref, h2T_ref, scT_ref):
    merged = (a_ref[...].astype(_F32) + b_ref[...].astype(_F32) + c_ref[...].astype(_F32)).astype(_BF)
    x1 = x_ref[...] + _dot(merged, wo_ref[...])
    x1_ref[...] = x1
    h2T = _rms(x1, nf_ref[...]).T.astype(_BF)
    h2T_ref[...] = h2T
    pqT = _dot(wpqT_ref[...], h2T).astype(_BF)
    for c in range(sk_ref.shape[0]):
        scT_ref[c * PEER_KEYS:(c + 1) * PEER_KEYS, :] = _dot(sk_ref[c], pqT[c * LANES:(c + 1) * LANES, :])


def _mid(x2, ma, mb, mc, wo, nf, wpqT, sk, ts):
    t, d = x2.shape
    nsc = sk.shape[0] * PEER_KEYS
    tile = pl.BlockSpec((ts, d), lambda i: (i, 0))
    return pl.pallas_call(
        _mid_body, grid=(t // ts,),
        in_specs=[tile, tile, tile, tile, _full_spec(wo.shape), _full_spec((1, d)), _full_spec(wpqT.shape),
                  _full_spec(sk.shape)],
        out_specs=(tile, pl.BlockSpec((d, ts), lambda i: (0, i)), pl.BlockSpec((nsc, ts), lambda i: (0, i))),
        out_shape=(jax.ShapeDtypeStruct((t, d), _F32), jax.ShapeDtypeStruct((d, t), _BF),
                   jax.ShapeDtypeStruct((nsc, t), _F32)),
        compiler_params=_cparams("arbitrary"), name="mid",
    )(x2, ma, mb, mc, wo, nf.reshape(1, -1), wpqT, sk)


def _top_values(s, n):
    vals = []
    cur = s
    for i in range(n):
        m = jnp.max(cur, axis=0, keepdims=True)
        vals.append(m)
        if i + 1 < n:
            cur = jnp.where(cur == m, -jnp.inf, cur)
    return vals


def _route_body(scT_ref, tau_ref, c_ref):
    def head(h, carry):
        r1 = pl.multiple_of(h * 2 * PEER_KEYS, PEER_KEYS)
        r2 = pl.multiple_of(h * 2 * PEER_KEYS + PEER_KEYS, PEER_KEYS)
        n = PEER_TOPK + 1
        v1 = _top_values(scT_ref[pl.ds(r1, PEER_KEYS), :], n)
        v2 = _top_values(scT_ref[pl.ds(r2, PEER_KEYS), :], n)
        edge = SUBLANES
        assert 2 * (edge + 1) > n
        pad = [jnp.full_like(v1[0], -jnp.inf)] * (-n % SUBLANES)
        col1 = jnp.concatenate(v1 + pad, axis=0)
        col2 = jnp.concatenate(v2 + pad, axis=0)
        cand = jnp.concatenate([v1[a] + col2[:edge] for a in range(edge)]
                               + [v1[0] + col2[edge:], v2[0] + col1[edge:]], axis=0)
        best = _top_values(cand, n)
        tau = best[PEER_TOPK - 1]
        top = v1[0] + v2[0]
        z = jnp.sum(jnp.where(cand >= tau, jnp.exp(cand - top), 0.0), axis=0, keepdims=True)
        tau_ref[pl.ds(h, 1), :] = 0.5 * (tau + best[PEER_TOPK])
        c_ref[pl.ds(h, 1), :] = top + jnp.log(z)
        return carry

    lax.fori_loop(0, PEER_HEADS, head, 0, unroll=2)


def _route(scT, te):
    nsc, t = scT.shape
    spec = pl.BlockSpec((PEER_HEADS, te), lambda i: (0, i))
    return pl.pallas_call(
        _route_body, grid=(t // te,),
        in_specs=[pl.BlockSpec((nsc, te), lambda i: (0, i))],
        out_specs=(spec, spec),
        out_shape=(jax.ShapeDtypeStruct((PEER_HEADS, t), _F32),) * 2,
        compiler_params=_cparams("arbitrary"), name="route",
    )(scT)


def _experts_body(h2T_ref, scT_ref, tau_ref, c_ref, u_ref, vT_ref, x1_ref, out_ref,
                  acc_ref, e2_ref, m2_ref, act_ref, g_ref, rows_ref, *, rows_per_step):
    e = pl.program_id(1)
    nk = PEER_KEYS

    @pl.when(e == 0)
    def _():
        acc_ref[...] = jnp.zeros_like(acc_ref)
        for h in range(PEER_HEADS):
            s2 = scT_ref[(2 * h + 1) * nk:(2 * h + 2) * nk, :]
            m2 = jnp.max(s2, axis=0, keepdims=True)
            m2_ref[h:h + 1, :] = m2
            e2_ref[h * nk:(h + 1) * nk, :] = jnp.exp(s2 - m2)

    n_rows = rows_per_step * PEER_HEADS
    for i in range(rows_per_step):
        for h in range(PEER_HEADS):
            s1 = scT_ref[pl.ds(2 * h * nk + e * rows_per_step + i, 1), :]
            rows_ref[i * PEER_HEADS + h:i * PEER_HEADS + h + 1, :] = tau_ref[h:h + 1, :] - s1
            rows_ref[n_rows + i * PEER_HEADS + h:n_rows + i * PEER_HEADS + h + 1, :] = (
                0.5 * jnp.exp(s1 + m2_ref[h:h + 1, :] - c_ref[h:h + 1, :]))

    def first_matmul(i):
        act_ref[i * nk:(i + 1) * nk, :] = _dot(u_ref[i * nk:(i + 1) * nk, :], h2T_ref[...])

    pair = 2 * nk
    first_matmul(0)
    for i in range(rows_per_step):
        if i + 1 < rows_per_step:
            first_matmul(i + 1)
        wgt = jnp.zeros((nk, act_ref.shape[1]), _F32)
        for h in range(PEER_HEADS):
            r = i * PEER_HEADS + h
            s2 = scT_ref[(2 * h + 1) * nk:(2 * h + 2) * nk, :]
            hit = jnp.where(s2 >= rows_ref[r:r + 1, :], e2_ref[h * nk:(h + 1) * nk, :], 0.0)
            wgt = wgt + hit * rows_ref[n_rows + r:n_rows + r + 1, :]
        x = act_ref[i * nk:(i + 1) * nk, :]
        t = jnp.tanh(x * (GELU_C0 + GELU_C1 * (x * x)))
        g_ref[i * nk:(i + 1) * nk, :] = ((x + x * t) * wgt).astype(_BF)
        if i % 2 == 1:
            p = i // 2
            acc_ref[...] += _dot(vT_ref[0, :, p * pair:(p + 1) * pair], g_ref[p * pair:(p + 1) * pair, :])

    @pl.when(e == pl.num_programs(1) - 1)
    def _():
        out_ref[...] = x1_ref[...] + acc_ref[...].T


def _experts(h2T, scT, tau, cn, u, vT, x1, ts, rows_per_step):
    d, t = h2T.shape
    nsc = scT.shape[0]
    ne = u.shape[0]
    eb = rows_per_step * PEER_KEYS
    nb = ne // eb
    col = lambda n: pl.BlockSpec((n, ts), lambda i, e: (0, i))
    tile = pl.BlockSpec((ts, d), lambda i, e: (i, 0))
    return pl.pallas_call(
        functools.partial(_experts_body, rows_per_step=rows_per_step),
        grid=(t // ts, nb),
        in_specs=[col(d), col(nsc), col(PEER_HEADS), col(PEER_HEADS),
                  pl.BlockSpec((eb, d), lambda i, e: (e, 0)), pl.BlockSpec((1, d, eb), lambda i, e: (e, 0, 0)), tile],
        out_specs=tile,
        out_shape=jax.ShapeDtypeStruct((t, d), _F32),
        scratch_shapes=[pltpu.VMEM((d, ts), _F32), pltpu.VMEM((PEER_HEADS * PEER_KEYS, ts), _F32),
                        pltpu.VMEM((PEER_HEADS, ts), _F32), pltpu.VMEM((eb, ts), _F32),
                        pltpu.VMEM((eb, ts), _BF),
                        pltpu.VMEM((2 * rows_per_step * PEER_HEADS, ts), _F32)],
        compiler_params=_cparams("arbitrary", "arbitrary"), name="experts",
    )(h2T, scT, tau, cn, u, vT, x1)


def _block_diag(w):
    n, c, d = w.shape
    return jnp.einsum("ncd,nm->ncmd", w, jnp.eye(n, dtype=w.dtype)).reshape(n * c, n * d)


def _rope_tables(s, dim, reps):
    half = dim // 2
    freq = ROPE_THETA ** (-jnp.arange(half, dtype=_F32) / half)
    ang = jnp.arange(s, dtype=_F32)[:, None] * freq[None, :]
    cos = jnp.tile(jnp.concatenate([jnp.cos(ang), jnp.cos(ang)], axis=1), (1, reps))
    sin = jnp.tile(jnp.concatenate([-jnp.sin(ang), jnp.sin(ang)], axis=1), (1, reps))
    return cos, sin


def _tile(n, pref):
    return pref if n % pref == 0 else n


def _layer(x, mem, p, l):
    b, s, d = x.shape
    bf = lambda a: a.astype(_BF)
    w_in = p["w_in"][l]
    n_q, n_kv = ATT_HEADS * LANES, ATT_KV_HEADS * LANES
    n_qi = IDX_HEADS * IDX_DIM
    sizes = (d, d, n_q, n_kv, n_kv, n_qi, IDX_DIM, IDX_HEADS, d, 3 * d)
    offs = [0]
    for n in sizes:
        offs.append(offs[-1] + n)
    wlx, wlg, wq, wk, wv, wqi, wki, wwi, wmq, wgates = (w_in[:, offs[i]:offs[i + 1]] for i in range(len(sizes)))
    wki2 = jnp.concatenate([wki, wki], axis=1)
    wwi_p = jnp.pad(wwi, ((0, 0), (0, LANES - IDX_HEADS)))
    wg0, wg1, wg2 = wgates[:, :d], wgates[:, d:2 * d], wgates[:, 2 * d:]
    nm = p["norm_mix"][l]

    ts_a = _tile(s, 256)
    m_lru = _lru_branch(x, nm, bf(wlx), bf(wlg), bf(wg0), p["conv_w"][l], p["conv_b"][l],
                        bf(_block_diag(p["lru_wa"][l])), p["lru_ba"][l].reshape(-1),
                        bf(_block_diag(p["lru_wi"][l])), p["lru_bi"][l].reshape(-1), p["lru_lambda"][l], ts_a)

    tk = _tile(s, 512)
    c128, s128 = _rope_tables(s, LANES, 1)
    c64, s64 = _rope_tables(s, IDX_DIM, LANES // IDX_DIM)
    ikn2 = jnp.concatenate([p["idx_k_norm"][l]] * 2)
    qT, k, vT, qiT, ki, wT, g1 = _dsa_proj(x, nm, bf(wq), bf(wk), bf(wv), bf(wqi), bf(wki2), bf(wwi_p), bf(wg1),
                                           p["q_norm"][l], p["k_norm"][l], ikn2, c128, s128, c64, s64, tk)
    m_att = _dsa(qT, qiT, wT, k, vT, ki, g1, min(TOPK_MAX, s // 4), _tile(s, 256), tk)

    mkT, mv = _mem_kv(mem, p["mem_norm"][l], bf(p["w_mem_kv"][l]), p["mem_k_norm"][l])
    m_mem = _mem_branch(x, nm, bf(wmq), bf(wg2), p["mem_q_norm"][l], mkT, mv, _tile(s, 512))

    t = b * s
    flat = lambda a: a.reshape(t, d)
    sk = bf(p["peer_subkeys"][l].reshape(2 * PEER_HEADS, PEER_KEYS, -1))
    x1, h2T, scT = _mid(flat(x), flat(m_lru), flat(m_att), flat(m_mem), bf(p["w_out"][l]), p["norm_ffn"][l],
                        bf(p["peer_wq"][l].T), sk, _tile(t, 512))
    tau, cn = _route(scT, _tile(t, 256))
    rows_per_step = 8
    vT = bf(p["peer_v"][l]).reshape(-1, rows_per_step * PEER_KEYS, d).transpose(0, 2, 1)
    out = _experts(h2T, scT, tau, cn, bf(p["peer_u"][l]), vT, x1, _tile(t, 512), rows_per_step)
    return out.reshape(b, s, d)


def kernel(x, mem, norm_mix, w_in, conv_w, conv_b, lru_wa, lru_ba, lru_wi, lru_bi, lru_lambda, q_norm, k_norm,
           idx_k_norm, mem_norm, w_mem_kv, mem_q_norm, mem_k_norm, w_out, norm_ffn, peer_wq, peer_subkeys,
           peer_u, peer_v):
    p = dict(norm_mix=norm_mix, w_in=w_in, conv_w=conv_w, conv_b=conv_b, lru_wa=lru_wa, lru_ba=lru_ba,
             lru_wi=lru_wi, lru_bi=lru_bi, lru_lambda=lru_lambda, q_norm=q_norm, k_norm=k_norm,
             idx_k_norm=idx_k_norm, mem_norm=mem_norm, w_mem_kv=w_mem_kv, mem_q_norm=mem_q_norm,
             mem_k_norm=mem_k_norm, w_out=w_out, norm_ffn=norm_ffn, peer_wq=peer_wq, peer_subkeys=peer_subkeys,
             peer_u=peer_u, peer_v=peer_v)
    for l in range(norm_mix.shape[0]):
        x = _layer(x, mem, p, l)
    return x
```

```python
import functools

import jax
import jax.numpy as jnp
from jax import lax
from jax.experimental import pallas as pl
from jax.experimental.pallas import tpu as pltpu

_F32 = jnp.float32
_BF = jnp.bfloat16

EPS = 1e-6
ROPE_THETA = 10000.0
LRU_C = 8.0
LRU_BLOCKS = 16
CONV_WIDTH = 4
ATT_HEADS = 8
ATT_KV_HEADS = 2
ATT_GROUP = ATT_HEADS // ATT_KV_HEADS
IDX_HEADS = 8
IDX_DIM = 64
TOPK_MAX = 256
MEM_HEADS = 4
PEER_HEADS = 8
PEER_KEYS = 128
PEER_TOPK = 16

LANES = 128
SUBLANES = 8
ONES_ROWS = 2 * SUBLANES
VMEM_LIMIT_BYTES = 56 * 1024 * 1024
MASK_VALUE = -1e30
FLT_MAX = 3.4028234663852886e38
INT_MIN = -(2 ** 31)


def _cparams(*sem):
    return pltpu.CompilerParams(dimension_semantics=sem, vmem_limit_bytes=VMEM_LIMIT_BYTES)


def _rms(x, g):
    return x * lax.rsqrt(jnp.mean(x * x, axis=-1, keepdims=True) + EPS) * g


GELU_C0 = 0.7978845608028654
GELU_C1 = GELU_C0 * 0.044715


def _gelu(x):
    return 0.5 * x * (1.0 + jnp.tanh(x * (GELU_C0 + GELU_C1 * (x * x))))


def _dot(a, b):
    return jnp.dot(a, b, preferred_element_type=_F32)


def _full_spec(shape):
    n = len(shape)
    return pl.BlockSpec(shape, lambda *_: (0,) * n)


def _lru_body(x_ref, nm_ref, wlx_ref, wlg_ref, wg0_ref, cw_ref, cb_ref, wa_ref, ba_ref, wi_ref,
              bi_ref, lam_ref, out_ref, buf_ref, hc_ref, a_ref, b_ref):
    ts = x_ref.shape[1]
    c_dim = out_ref.shape[2]

    @pl.when(pl.program_id(1) == 0)
    def _():
        buf_ref[0:SUBLANES, :] = jnp.zeros((SUBLANES, c_dim), _F32)
        hc_ref[...] = jnp.zeros_like(hc_ref)

    h = _rms(x_ref[0], nm_ref[...]).astype(_BF)
    lx = _dot(h, wlx_ref[...])
    buf_ref[SUBLANES:SUBLANES + ts, :] = lx
    ext = buf_ref[...]
    cw = cw_ref[...]
    xc = cb_ref[...] + cw[3:4] * lx
    for j in range(CONV_WIDTH - 1):
        d = CONV_WIDTH - 1 - j
        xc = xc + cw[j:j + 1] * pltpu.roll(ext, d, 0)[SUBLANES:, :]
    buf_ref[0:SUBLANES, :] = lx[ts - SUBLANES:, :]

    xcb = xc.astype(_BF)
    r = jax.nn.sigmoid(_dot(xcb, wa_ref[...]) + ba_ref[...])
    ig = jax.nn.sigmoid(_dot(xcb, wi_ref[...]) + bi_ref[...])
    lam = lam_ref[...]
    softplus_neg_lam = jnp.maximum(-lam, 0.0) + jnp.log1p(jnp.exp(-jnp.abs(lam)))
    log_a = (-LRU_C) * r * softplus_neg_lam
    a = jnp.exp(log_a)
    y = jnp.tanh(-log_a) * (a * a + 1.0)
    u = jnp.where(y > 0.0, y * lax.rsqrt(y), 0.0) * (ig * xc)

    a = a.reshape(ts // SUBLANES, SUBLANES, c_dim)
    u = u.reshape(ts // SUBLANES, SUBLANES, c_dim)
    rid = lax.broadcasted_iota(jnp.int32, a.shape, 1)
    for d in (1, 2, 4):
        ok = rid >= d
        a_s = pltpu.roll(a, d, 1)
        u_s = pltpu.roll(u, d, 1)
        u = jnp.where(ok, a * u_s + u, u)
        a = jnp.where(ok, a * a_s, a)
    a_ref[...] = a.reshape(ts, c_dim)
    b_ref[...] = u.reshape(ts, c_dim)

    def step(g, hp):
        r0 = pl.multiple_of(g * SUBLANES, SUBLANES)
        hcur = a_ref[pl.ds(r0, SUBLANES), :] * hp + b_ref[pl.ds(r0, SUBLANES), :]
        b_ref[pl.ds(r0, SUBLANES), :] = hcur
        return jnp.broadcast_to(hcur[SUBLANES - 1:SUBLANES, :], (SUBLANES, c_dim))

    hc_ref[...] = lax.fori_loop(0, ts // SUBLANES, step, hc_ref[...])
    gate = _gelu(_dot(h, wlg_ref[...])) * jax.nn.sigmoid(_dot(h, wg0_ref[...]))
    out_ref[0] = (b_ref[...] * gate).astype(_BF)


def _lru_branch(x, nm, wlx, wlg, wg0, cw, cb, wa, ba, wi, bi, lam, ts):
    b, s, d = x.shape
    c = wlx.shape[1]
    row = lambda v: v.reshape(1, -1)
    args = (x, row(nm), wlx, wlg, wg0, cw, row(cb), wa, row(ba), wi, row(bi), row(lam))
    in_specs = [pl.BlockSpec((1, ts, d), lambda i, j: (i, j, 0))]
    in_specs += [_full_spec(a.shape) for a in args[1:]]
    return pl.pallas_call(
        _lru_body,
        grid=(b, s // ts),
        in_specs=in_specs,
        out_specs=pl.BlockSpec((1, ts, c), lambda i, j: (i, j, 0)),
        out_shape=jax.ShapeDtypeStruct((b, s, c), _BF),
        scratch_shapes=[pltpu.VMEM((ts + SUBLANES, c), _F32), pltpu.VMEM((SUBLANES, c), _F32),
                        pltpu.VMEM((ts, c), _F32), pltpu.VMEM((ts, c), _F32)],
        compiler_params=_cparams("arbitrary", "arbitrary"),
        name="lru",
    )(*args)


def _rope128(x, cos, sin):
    return x * cos + pltpu.roll(x, LANES // 2, 1) * sin


def _rope64(x, cos, sin, first_half):
    rot = jnp.where(first_half, pltpu.roll(x, LANES - IDX_DIM // 2, 1), pltpu.roll(x, IDX_DIM // 2, 1))
    return x * cos + rot * sin


def _dsa_proj_body(x_ref, nm_ref, wq_ref, wk_ref, wv_ref, wqi_ref, wki_ref, wwi_ref, wg1_ref,
                   qn_ref, kn_ref, ikn_ref, c128_ref, s128_ref, c64_ref, s64_ref,
                   qT_ref, k_ref, vT_ref, qiT_ref, ki_ref, wT_ref, g1_ref):
    ts = x_ref.shape[1]
    h = _rms(x_ref[0], nm_ref[...]).astype(_BF)
    c128, s128 = c128_ref[...], s128_ref[...]
    c64, s64 = c64_ref[...], s64_ref[...]
    first_half = (lax.broadcasted_iota(jnp.int32, (ts, LANES), 1) & (IDX_DIM - 1)) < IDX_DIM // 2

    def head_norm_rope(t, g, scale):
        outs = []
        for i in range(t.shape[1] // LANES):
            th = _rms(t[:, i * LANES:(i + 1) * LANES], g)
            outs.append(_rope128(th, c128, s128) * scale)
        return jnp.concatenate(outs, axis=1)

    q = head_norm_rope(_dot(h, wq_ref[...]), qn_ref[...], float(LANES) ** -0.5)
    qT_ref[0] = q.T.astype(_BF)
    k_ref[0] = head_norm_rope(_dot(h, wk_ref[...]), kn_ref[...], 1.0).astype(_BF)
    vT = _dot(h, wv_ref[...]).T.astype(_BF)
    for g in range(ATT_KV_HEADS):
        vT_ref[0, 0, g, :LANES, :] = vT[g * LANES:(g + 1) * LANES, :]
        vT_ref[0, 0, g, LANES:, :] = jnp.ones((ONES_ROWS, ts), _BF)

    qi = _dot(h, wqi_ref[...])
    qi = jnp.concatenate(
        [_rope64(qi[:, i * LANES:(i + 1) * LANES], c64, s64, first_half) for i in range(qi.shape[1] // LANES)],
        axis=1) * (float(IDX_DIM) ** -0.5)
    qiT_ref[0] = qi.T.astype(_BF)
    ki = _rope64(_rms(_dot(h, wki_ref[...]), ikn_ref[...]), c64, s64, first_half)
    ki_ref[0] = ki[:, :IDX_DIM].astype(_BF)
    wi = _dot(h, wwi_ref[...]) * (float(IDX_HEADS) ** -0.5)
    wT_ref[0] = wi.T[:IDX_HEADS, :]
    g1_ref[0] = jax.nn.sigmoid(_dot(h, wg1_ref[...])).astype(_BF)


def _dsa_proj(x, nm, wq, wk, wv, wqi, wki2, wwi, wg1, qn, kn, ikn2, c128, s128, c64, s64, ts):
    b, s, d = x.shape
    row = lambda v: v.reshape(1, -1)
    nq, nk, nv, nqi = wq.shape[1], wk.shape[1], wv.shape[1], wqi.shape[1]
    args = (x, row(nm), wq, wk, wv, wqi, wki2, wwi, wg1, row(qn), row(kn), row(ikn2), c128, s128, c64, s64)
    tile = lambda n: pl.BlockSpec((1, ts, n), lambda i, j: (i, j, 0))
    tile_t = lambda n: pl.BlockSpec((1, n, ts), lambda i, j: (i, 0, j))
    tab = pl.BlockSpec((ts, LANES), lambda i, j: (j, 0))
    in_specs = [tile(d)] + [_full_spec(a.shape) for a in args[1:12]] + [tab] * 4
    out_shape = (
        jax.ShapeDtypeStruct((b, nq, s), _BF),
        jax.ShapeDtypeStruct((b, s, nk), _BF),
        jax.ShapeDtypeStruct((b, s // ts, ATT_KV_HEADS, LANES + ONES_ROWS, ts), _BF),
        jax.ShapeDtypeStruct((b, nqi, s), _BF),
        jax.ShapeDtypeStruct((b, s, IDX_DIM), _BF),
        jax.ShapeDtypeStruct((b, IDX_HEADS, s), _F32),
        jax.ShapeDtypeStruct((b, s, d), _BF),
    )
    out_specs = (tile_t(nq), tile(nk),
                 pl.BlockSpec((1, 1, ATT_KV_HEADS, LANES + ONES_ROWS, ts), lambda i, j: (i, j, 0, 0, 0)), tile_t(nqi),
                 tile(IDX_DIM), tile_t(IDX_HEADS), tile(d))
    return pl.pallas_call(
        _dsa_proj_body, grid=(b, s // ts), in_specs=in_specs, out_specs=out_specs, out_shape=out_shape,
        compiler_params=_cparams("arbitrary", "arbitrary"), name="dsa_proj",
    )(*args)


def _key_to_float(u):
    key = u ^ jnp.int32(INT_MIN)
    bits = key ^ ((key >> 31) & jnp.int32(0x7FFFFFFF))
    return jnp.where((u >> 23) == 0, -jnp.inf, lax.bitcast_convert_type(bits, _F32))


def _high_half(x):
    bits = lax.bitcast_convert_type(x, jnp.int32) & jnp.int32(-65536)
    return lax.bitcast_convert_type(bits, _F32).astype(_BF)


def _dsa_body(qT_ref, qiT_ref, wT_ref, k_ref, vT_ref, ki_ref, g1_ref, out_ref,
              sc_ref, sh_ref, m_ref, l_ref, acc_ref, bias_ref, s_ref, p_ref, *, topk, tk):
    tq = qT_ref.shape[2]
    dh = LANES
    q0 = pl.program_id(1) * tq
    n_chunks = (q0 + tq + tk - 1) // tk

    qicat = jnp.concatenate([qiT_ref[0, h * IDX_DIM:(h + 1) * IDX_DIM, :] for h in range(IDX_HEADS)], axis=1)
    w = wT_ref[0]
    qpos = q0 + lax.broadcasted_iota(jnp.int32, (tk, tq), 1)
    krow = lax.broadcasted_iota(jnp.int32, (tk, tq), 0)

    def score_chunk(c, carry):
        r0 = pl.multiple_of(c * tk, tk)
        lg = _dot(ki_ref[0, pl.ds(r0, tk), :], qicat)
        s = jnp.zeros((tk, tq), _F32)
        for h in range(IDX_HEADS):
            s = s + jnp.maximum(lg[:, h * tq:(h + 1) * tq], 0.0) * w[h:h + 1, :]
        s = jnp.where(krow + r0 <= qpos, s, -jnp.inf)
        sc_ref[pl.ds(r0, tk), :] = s
        sh_ref[pl.ds(r0, tk), :] = _high_half(s)
        return carry

    lax.fori_loop(0, n_chunks, score_chunk, 0)

    def make_count(nc, ref, rows, dtype):
        def count_ge(cand):
            one, zero = jnp.ones((), dtype), jnp.zeros((), dtype)
            parts = [jnp.zeros((rows, tq), dtype) for _ in range(4)]
            n = 0
            for c in range(nc):
                hit = jnp.where(ref[c * tk:(c + 1) * tk, :] >= cand, one, zero)
                for r in range(tk // rows):
                    parts[n % 4] = parts[n % 4] + hit[r * rows:(r + 1) * rows, :]
                    n += 1
            parts = [p.astype(_F32) for p in parts]
            return jnp.sum((parts[0] + parts[1]) + (parts[2] + parts[3]), axis=0, keepdims=True)
        return count_ge

    max_chunks = sc_ref.shape[0] // tk
    assert max_chunks * tk // (2 * SUBLANES) <= 4 * 64
    coarse = [make_count(nc, sh_ref, 2 * SUBLANES, _BF) for nc in range(1, max_chunks + 1)]
    fine = [make_count(nc, sc_ref, SUBLANES, _F32) for nc in range(1, max_chunks + 1)]

    def bisect_high(i, u):
        trial = u | lax.shift_left(jnp.int32(1), 31 - i)
        cnt = lax.switch(n_chunks - 1, coarse, _high_half(_key_to_float(trial)))
        return jnp.where(cnt >= float(topk), trial, u)

    def bisect_low(i, u):
        trial = u | lax.shift_left(jnp.int32(1), 31 - i)
        cnt = lax.switch(n_chunks - 1, fine, _key_to_float(trial))
        return jnp.where(cnt >= float(topk), trial, u)

    u = lax.fori_loop(0, 16, bisect_high, jnp.zeros((1, tq), jnp.int32))
    u = lax.fori_loop(16, 32, bisect_low, u)
    tau = jnp.maximum(_key_to_float(u), -FLT_MAX)

    m_ref[...] = jnp.full(m_ref.shape, MASK_VALUE, _F32)
    l_ref[...] = jnp.zeros_like(l_ref)
    acc_ref[...] = jnp.zeros_like(acc_ref)

    sub = 8 * SUBLANES
    groups = [slice(r * SUBLANES, (r + 1) * SUBLANES) for r in range(sub // SUBLANES)]

    def att_chunk(c, carry):
        r0 = pl.multiple_of(c * tk, tk)
        bias_ref[...] = jnp.where(sc_ref[pl.ds(r0, tk), :] >= tau, 0.0, MASK_VALUE)

        def logits(h):
            g = h // ATT_GROUP
            s_ref[h] = _dot(k_ref[0, pl.ds(r0, tk), g * dh:(g + 1) * dh], qT_ref[0, h * dh:(h + 1) * dh, :])

        logits(0)
        for h in range(ATT_HEADS):
            if h + 1 < ATT_HEADS:
                logits(h + 1)
            m_old = m_ref[h:h + 1, :]
            mx = jnp.broadcast_to(m_old, (SUBLANES, tq))
            for i in range(tk // sub):
                rows = slice(i * sub, (i + 1) * sub)
                sb = s_ref[h, rows, :] + bias_ref[rows, :]
                s_ref[h, rows, :] = sb
                for gr in groups:
                    mx = jnp.maximum(mx, sb[gr, :])
            m_new = jnp.max(mx, axis=0, keepdims=True)
            for i in range(tk // sub):
                rows = slice(i * sub, (i + 1) * sub)
                p_ref[h, rows, :] = jnp.exp((s_ref[h, rows, :] - m_new).astype(_BF))
            alpha = jnp.exp(m_old - m_new)
            pv = _dot(vT_ref[0, c, h // ATT_GROUP], p_ref[h])
            l_ref[h:h + 1, :] = alpha * l_ref[h:h + 1, :] + pv[dh:dh + 1, :]
            acc_ref[h * dh:(h + 1) * dh, :] = alpha * acc_ref[h * dh:(h + 1) * dh, :] + pv[:dh, :]
            m_ref[h:h + 1, :] = m_new
        return carry

    lax.fori_loop(0, n_chunks, att_chunk, 0)
    outs = []
    for h in range(ATT_HEADS):
        o = acc_ref[h * dh:(h + 1) * dh, :] / l_ref[h:h + 1, :]
        outs.append(o.T)
    out_ref[0] = (jnp.concatenate(outs, axis=1) * g1_ref[0].astype(_F32)).astype(_BF)


def _dsa(qT, qiT, wT, k, vT, ki, g1, topk, tq, tk):
    b, nq, s = qT.shape
    d = g1.shape[2]
    in_specs = [
        pl.BlockSpec((1, nq, tq), lambda i, j: (i, 0, j)),
        pl.BlockSpec((1, qiT.shape[1], tq), lambda i, j: (i, 0, j)),
        pl.BlockSpec((1, IDX_HEADS, tq), lambda i, j: (i, 0, j)),
        pl.BlockSpec((1, s, k.shape[2]), lambda i, j: (i, 0, 0)),
        pl.BlockSpec((1,) + vT.shape[1:], lambda i, j: (i, 0, 0, 0, 0)),
        pl.BlockSpec((1, s, IDX_DIM), lambda i, j: (i, 0, 0)),
        pl.BlockSpec((1, tq, d), lambda i, j: (i, j, 0)),
    ]
    return pl.pallas_call(
        functools.partial(_dsa_body, topk=topk, tk=tk),
        grid=(b, s // tq), in_specs=in_specs,
        out_specs=pl.BlockSpec((1, tq, d), lambda i, j: (i, j, 0)),
        out_shape=jax.ShapeDtypeStruct((b, s, d), _BF),
        scratch_shapes=[pltpu.VMEM((s, tq), _F32), pltpu.VMEM((s, tq), _BF), pltpu.VMEM((ATT_HEADS, tq), _F32),
                        pltpu.VMEM((ATT_HEADS, tq), _F32), pltpu.VMEM((nq, tq), _F32),
                        pltpu.VMEM((tk, tq), _F32), pltpu.VMEM((ATT_HEADS, tk, tq), _F32),
                        pltpu.VMEM((ATT_HEADS, tk, tq), _BF)],
        compiler_params=_cparams("arbitrary", "arbitrary"), name="dsa",
    )(qT, qiT, wT, k, vT, ki, g1)


def _mem_kv_body(mem_ref, mn_ref, wkv_ref, mkn_ref, mkT_ref, mv_ref):
    d = mem_ref.shape[2]
    dm = d // MEM_HEADS
    m = _rms(mem_ref[0], mn_ref[...]).astype(_BF)
    kv = _dot(m, wkv_ref[...])
    mk = jnp.concatenate([_rms(kv[:, i * dm:(i + 1) * dm], mkn_ref[...]) for i in range(MEM_HEADS)], axis=1)
    mkT_ref[0] = mk.T.astype(_BF)
    mv_ref[0] = kv[:, d:].astype(_BF)


def _mem_kv(mem, mn, wkv, mkn):
    b, m, d = mem.shape
    row = lambda v: v.reshape(1, -1)
    return pl.pallas_call(
        _mem_kv_body, grid=(b,),
        in_specs=[pl.BlockSpec((1, m, d), lambda i: (i, 0, 0)), _full_spec((1, d)), _full_spec(wkv.shape),
                  _full_spec((1, d // MEM_HEADS))],
        out_specs=(pl.BlockSpec((1, d, m), lambda i: (i, 0, 0)), pl.BlockSpec((1, m, d), lambda i: (i, 0, 0))),
        out_shape=(jax.ShapeDtypeStruct((b, d, m), _BF), jax.ShapeDtypeStruct((b, m, d), _BF)),
        compiler_params=_cparams("arbitrary"), name="mem_kv",
    )(mem, row(mn), wkv, row(mkn))


def _mem_body(x_ref, nm_ref, wmq_ref, wg2_ref, mqn_ref, mkT_ref, mv_ref, out_ref):
    d = x_ref.shape[2]
    dm = d // MEM_HEADS
    h = _rms(x_ref[0], nm_ref[...]).astype(_BF)
    mq = _dot(h, wmq_ref[...])
    outs = []
    for i in range(MEM_HEADS):
        qh = (_rms(mq[:, i * dm:(i + 1) * dm], mqn_ref[...]) * (float(dm) ** -0.5)).astype(_BF)
        s = _dot(qh, mkT_ref[0, i * dm:(i + 1) * dm, :])
        p = jnp.exp(s - jnp.max(s, axis=-1, keepdims=True))
        o = _dot(p.astype(_BF), mv_ref[0, :, i * dm:(i + 1) * dm])
        outs.append(o / jnp.sum(p, axis=-1, keepdims=True))
    out_ref[0] = (jnp.concatenate(outs, axis=1) * jax.nn.sigmoid(_dot(h, wg2_ref[...]))).astype(_BF)


def _mem_branch(x, nm, wmq, wg2, mqn, mkT, mv, ts):
    b, s, d = x.shape
    m = mv.shape[1]
    row = lambda v: v.reshape(1, -1)
    return pl.pallas_call(
        _mem_body, grid=(b, s // ts),
        in_specs=[pl.BlockSpec((1, ts, d), lambda i, j: (i, j, 0)), _full_spec((1, d)), _full_spec(wmq.shape),
                  _full_spec(wg2.shape), _full_spec((1, d // MEM_HEADS)),
                  pl.BlockSpec((1, d, m), lambda i, j: (i, 0, 0)), pl.BlockSpec((1, m, d), lambda i, j: (i, 0, 0))],
        out_specs=pl.BlockSpec((1, ts, d), lambda i, j: (i, j, 0)),
        out_shape=jax.ShapeDtypeStruct((b, s, d), _BF),
        compiler_params=_cparams("arbitrary", "arbitrary"), name="mem",
    )(x, row(nm), wmq, wg2, row(mqn), mkT, mv)


def _mid_body(x_ref, a_ref, b_ref, c_ref, wo_ref, nf_ref, wpqT_ref, sk_ref, x1_ref, h2T_ref, scT_ref):
    merged = (a_ref[...].astype(_F32) + b_ref[...].astype(_F32) + c_ref[...].astype(_F32)).astype(_BF)
    x1 = x_ref[...] + _dot(merged, wo_ref[...])
    x1_ref[...] = x1
    h2T = _rms(x1, nf_ref[...]).T.astype(_BF)
    h2T_ref[...] = h2T
    pqT = _dot(wpqT_ref[...], h2T).astype(_BF)
    for c in range(sk_ref.shape[0]):
        scT_ref[c * PEER_KEYS:(c + 1) * PEER_KEYS, :] = _dot(sk_ref[c], pqT[c * LANES:(c + 1) * LANES, :])


def _mid(x2, ma, mb, mc, wo, nf, wpqT, sk, ts):
    t, d = x2.shape
    nsc = sk.shape[0] * PEER_KEYS
    tile = pl.BlockSpec((ts, d), lambda i: (i, 0))
    return pl.pallas_call(
        _mid_body, grid=(t // ts,),
        in_specs=[tile, tile, tile, tile, _full_spec(wo.shape), _full_spec((1, d)), _full_spec(wpqT.shape),
                  _full_spec(sk.shape)],
        out_specs=(tile, pl.BlockSpec((d, ts), lambda i: (0, i)), pl.BlockSpec((nsc, ts), lambda i: (0, i))),
        out_shape=(jax.ShapeDtypeStruct((t, d), _F32), jax.ShapeDtypeStruct((d, t), _BF),
                   jax.ShapeDtypeStruct((nsc, t), _F32)),
        compiler_params=_cparams("arbitrary"), name="mid",
    )(x2, ma, mb, mc, wo, nf.reshape(1, -1), wpqT, sk)


def _top_values(s, n):
    groups = [s[r * SUBLANES:(r + 1) * SUBLANES, :] for r in range(s.shape[0] // SUBLANES)]
    size = pl.next_power_of_2(len(groups))
    groups += [jnp.full_like(groups[0], -jnp.inf)] * (size - len(groups))
    k = 2
    while k <= size:
        j = k // 2
        while j >= 1:
            for i in range(size):
                l = i ^ j
                if l > i:
                    hi, lo = jnp.maximum(groups[i], groups[l]), jnp.minimum(groups[i], groups[l])
                    groups[i], groups[l] = (hi, lo) if (i & k) == 0 else (lo, hi)
            j //= 2
        k *= 2
    vals = []
    for it in range(n):
        m = jnp.max(groups[0], axis=0, keepdims=True)
        vals.append(m)
        remaining = n - 1 - it
        head = groups[0] == m
        for d in range(min(remaining, size)):
            below = groups[d + 1] if d + 1 < size else jnp.full_like(m, -jnp.inf)
            groups[d] = jnp.where(head, below, groups[d])
    return vals


def _route_body(scT_ref, tau_ref, c_ref):
    def head(h, carry):
        r1 = pl.multiple_of(h * 2 * PEER_KEYS, PEER_KEYS)
        r2 = pl.multiple_of(h * 2 * PEER_KEYS + PEER_KEYS, PEER_KEYS)
        n = PEER_TOPK + 1
        v1 = _top_values(scT_ref[pl.ds(r1, PEER_KEYS), :], n)
        v2 = _top_values(scT_ref[pl.ds(r2, PEER_KEYS), :], n)
        edge = SUBLANES
        assert 2 * (edge + 1) > n
        pad = [jnp.full_like(v1[0], -jnp.inf)] * (-n % SUBLANES)
        col1 = jnp.concatenate(v1 + pad, axis=0)
        col2 = jnp.concatenate(v2 + pad, axis=0)
        cand = jnp.concatenate([v1[a] + col2[:edge] for a in range(edge)]
                               + [v1[0] + col2[edge:], v2[0] + col1[edge:]], axis=0)
        best = _top_values(cand, n)
        tau = best[PEER_TOPK - 1]
        top = v1[0] + v2[0]
        z = jnp.sum(jnp.where(cand >= tau, jnp.exp(cand - top), 0.0), axis=0, keepdims=True)
        tau_ref[pl.ds(h, 1), :] = 0.5 * (tau + best[PEER_TOPK])
        c_ref[pl.ds(h, 1), :] = top + jnp.log(z)
        return carry

    lax.fori_loop(0, PEER_HEADS, head, 0, unroll=2)


def _route(scT, te):
    nsc, t = scT.shape
    spec = pl.BlockSpec((PEER_HEADS, te), lambda i: (0, i))
    return pl.pallas_call(
        _route_body, grid=(t // te,),
        in_specs=[pl.BlockSpec((nsc, te), lambda i: (0, i))],
        out_specs=(spec, spec),
        out_shape=(jax.ShapeDtypeStruct((PEER_HEADS, t), _F32),) * 2,
        compiler_params=_cparams("arbitrary"), name="route",
    )(scT)


def _experts_body(h2T_ref, scT_ref, tau_ref, c_ref, u_ref, vT_ref, x1_ref, out_ref,
                  acc_ref, e2_ref, m2_ref, act_ref, g_ref, rows_ref, *, rows_per_step):
    e = pl.program_id(1)
    nk = PEER_KEYS

    @pl.when(e == 0)
    def _():
        acc_ref[...] = jnp.zeros_like(acc_ref)
        for h in range(PEER_HEADS):
            s2 = scT_ref[(2 * h + 1) * nk:(2 * h + 2) * nk, :]
            m2 = jnp.max(s2, axis=0, keepdims=True)
            m2_ref[h:h + 1, :] = m2
            e2_ref[h * nk:(h + 1) * nk, :] = jnp.exp(s2 - m2)

    n_rows = rows_per_step * PEER_HEADS
    for i in range(rows_per_step):
        for h in range(PEER_HEADS):
            s1 = scT_ref[pl.ds(2 * h * nk + e * rows_per_step + i, 1), :]
            rows_ref[i * PEER_HEADS + h:i * PEER_HEADS + h + 1, :] = tau_ref[h:h + 1, :] - s1
            rows_ref[n_rows + i * PEER_HEADS + h:n_rows + i * PEER_HEADS + h + 1, :] = (
                0.5 * jnp.exp(s1 + m2_ref[h:h + 1, :] - c_ref[h:h + 1, :]))

    def first_matmul(i):
        act_ref[i * nk:(i + 1) * nk, :] = _dot(u_ref[i * nk:(i + 1) * nk, :], h2T_ref[...])

    pair = 2 * nk
    first_matmul(0)
    for i in range(rows_per_step):
        if i + 1 < rows_per_step:
            first_matmul(i + 1)
        wgt = jnp.zeros((nk, act_ref.shape[1]), _F32)
        for h in range(PEER_HEADS):
            r = i * PEER_HEADS + h
            s2 = scT_ref[(2 * h + 1) * nk:(2 * h + 2) * nk, :]
            hit = jnp.where(s2 >= rows_ref[r:r + 1, :], e2_ref[h * nk:(h + 1) * nk, :], 0.0)
            wgt = wgt + hit * rows_ref[n_rows + r:n_rows + r + 1, :]
        x = act_ref[i * nk:(i + 1) * nk, :]
        t = jnp.tanh(x * (GELU_C0 + GELU_C1 * (x * x)))
        g_ref[i * nk:(i + 1) * nk, :] = ((x + x * t) * wgt).astype(_BF)
        if i % 2 == 1:
            p = i // 2
            acc_ref[...] += _dot(vT_ref[0, :, p * pair:(p + 1) * pair], g_ref[p * pair:(p + 1) * pair, :])

    @pl.when(e == pl.num_programs(1) - 1)
    def _():
        out_ref[...] = x1_ref[...] + acc_ref[...].T


def _experts(h2T, scT, tau, cn, u, vT, x1, ts, rows_per_step):
    d, t = h2T.shape
    nsc = scT.shape[0]
    ne = u.shape[0]
    eb = rows_per_step * PEER_KEYS
    nb = ne // eb
    col = lambda n: pl.BlockSpec((n, ts), lambda i, e: (0, i))
    tile = pl.BlockSpec((ts, d), lambda i, e: (i, 0))
    return pl.pallas_call(
        functools.partial(_experts_body, rows_per_step=rows_per_step),
        grid=(t // ts, nb),
        in_specs=[col(d), col(nsc), col(PEER_HEADS), col(PEER_HEADS),
                  pl.BlockSpec((eb, d), lambda i, e: (e, 0)), pl.BlockSpec((1, d, eb), lambda i, e: (e, 0, 0)), tile],
        out_specs=tile,
        out_shape=jax.ShapeDtypeStruct((t, d), _F32),
        scratch_shapes=[pltpu.VMEM((d, ts), _F32), pltpu.VMEM((PEER_HEADS * PEER_KEYS, ts), _F32),
                        pltpu.VMEM((PEER_HEADS, ts), _F32), pltpu.VMEM((eb, ts), _F32),
                        pltpu.VMEM((eb, ts), _BF),
                        pltpu.VMEM((2 * rows_per_step * PEER_HEADS, ts), _F32)],
        compiler_params=_cparams("arbitrary", "arbitrary"), name="experts",
    )(h2T, scT, tau, cn, u, vT, x1)


def _block_diag(w):
    n, c, d = w.shape
    return jnp.einsum("ncd,nm->ncmd", w, jnp.eye(n, dtype=w.dtype)).reshape(n * c, n * d)


def _rope_tables(s, dim, reps):
    half = dim // 2
    freq = ROPE_THETA ** (-jnp.arange(half, dtype=_F32) / half)
    ang = jnp.arange(s, dtype=_F32)[:, None] * freq[None, :]
    cos = jnp.tile(jnp.concatenate([jnp.cos(ang), jnp.cos(ang)], axis=1), (1, reps))
    sin = jnp.tile(jnp.concatenate([-jnp.sin(ang), jnp.sin(ang)], axis=1), (1, reps))
    return cos, sin


def _tile(n, pref):
    return pref if n % pref == 0 else n


def _layer(x, mem, p, l):
    b, s, d = x.shape
    bf = lambda a: a.astype(_BF)
    w_in = p["w_in"][l]
    n_q, n_kv = ATT_HEADS * LANES, ATT_KV_HEADS * LANES
    n_qi = IDX_HEADS * IDX_DIM
    sizes = (d, d, n_q, n_kv, n_kv, n_qi, IDX_DIM, IDX_HEADS, d, 3 * d)
    offs = [0]
    for n in sizes:
        offs.append(offs[-1] + n)
    wlx, wlg, wq, wk, wv, wqi, wki, wwi, wmq, wgates = (w_in[:, offs[i]:offs[i + 1]] for i in range(len(sizes)))
    wki2 = jnp.concatenate([wki, wki], axis=1)
    wwi_p = jnp.pad(wwi, ((0, 0), (0, LANES - IDX_HEADS)))
    wg0, wg1, wg2 = wgates[:, :d], wgates[:, d:2 * d], wgates[:, 2 * d:]
    nm = p["norm_mix"][l]

    ts_a = _tile(s, 256)
    m_lru = _lru_branch(x, nm, bf(wlx), bf(wlg), bf(wg0), p["conv_w"][l], p["conv_b"][l],
                        bf(_block_diag(p["lru_wa"][l])), p["lru_ba"][l].reshape(-1),
                        bf(_block_diag(p["lru_wi"][l])), p["lru_bi"][l].reshape(-1), p["lru_lambda"][l], ts_a)

    tk = _tile(s, 512)
    c128, s128 = _rope_tables(s, LANES, 1)
    c64, s64 = _rope_tables(s, IDX_DIM, LANES // IDX_DIM)
    ikn2 = jnp.concatenate([p["idx_k_norm"][l]] * 2)
    qT, k, vT, qiT, ki, wT, g1 = _dsa_proj(x, nm, bf(wq), bf(wk), bf(wv), bf(wqi), bf(wki2), bf(wwi_p), bf(wg1),
                                           p["q_norm"][l], p["k_norm"][l], ikn2, c128, s128, c64, s64, tk)
    m_att = _dsa(qT, qiT, wT, k, vT, ki, g1, min(TOPK_MAX, s // 4), _tile(s, 256), tk)

    mkT, mv = _mem_kv(mem, p["mem_norm"][l], bf(p["w_mem_kv"][l]), p["mem_k_norm"][l])
    m_mem = _mem_branch(x, nm, bf(wmq), bf(wg2), p["mem_q_norm"][l], mkT, mv, _tile(s, 512))

    t = b * s
    flat = lambda a: a.reshape(t, d)
    sk = bf(p["peer_subkeys"][l].reshape(2 * PEER_HEADS, PEER_KEYS, -1))
    x1, h2T, scT = _mid(flat(x), flat(m_lru), flat(m_att), flat(m_mem), bf(p["w_out"][l]), p["norm_ffn"][l],
                        bf(p["peer_wq"][l].T), sk, _tile(t, 512))
    tau, cn = _route(scT, _tile(t, 256))
    rows_per_step = 8
    vT = bf(p["peer_v"][l]).reshape(-1, rows_per_step * PEER_KEYS, d).transpose(0, 2, 1)
    out = _experts(h2T, scT, tau, cn, bf(p["peer_u"][l]), vT, x1, _tile(t, 512), rows_per_step)
    return out.reshape(b, s, d)


def kernel(x, mem, norm_mix, w_in, conv_w, conv_b, lru_wa, lru_ba, lru_wi, lru_bi, lru_lambda, q_norm, k_norm,
           idx_k_norm, mem_norm, w_mem_kv, mem_q_norm, mem_k_norm, w_out, norm_ffn, peer_wq, peer_subkeys,
           peer_u, peer_v):
    p = dict(norm_mix=norm_mix, w_in=w_in, conv_w=conv_w, conv_b=conv_b, lru_wa=lru_wa, lru_ba=lru_ba,
             lru_wi=lru_wi, lru_bi=lru_bi, lru_lambda=lru_lambda, q_norm=q_norm, k_norm=k_norm,
             idx_k_norm=idx_k_norm, mem_norm=mem_norm, w_mem_kv=w_mem_kv, mem_q_norm=mem_q_norm,
             mem_k_norm=mem_k_norm, w_out=w_out, norm_ffn=norm_ffn, peer_wq=peer_wq, peer_subkeys=peer_subkeys,
             peer_u=peer_u, peer_v=peer_v)
    for l in range(norm_mix.shape[0]):
        x = _layer(x, mem, p, l)
    return x
```

```python
import functools

import jax
import jax.numpy as jnp
from jax import lax
from jax.experimental import pallas as pl
from jax.experimental.pallas import tpu as pltpu

_F32 = jnp.float32
_BF = jnp.bfloat16

EPS = 1e-6
ROPE_THETA = 10000.0
LRU_C = 8.0
LRU_BLOCKS = 16
CONV_WIDTH = 4
ATT_HEADS = 8
ATT_KV_HEADS = 2
ATT_GROUP = ATT_HEADS // ATT_KV_HEADS
IDX_HEADS = 8
IDX_DIM = 64
TOPK_MAX = 256
MEM_HEADS = 4
PEER_HEADS = 8
PEER_KEYS = 128
PEER_TOPK = 16
TOP_ROWS = 24

LANES = 128
SUBLANES = 8
ONES_ROWS = 2 * SUBLANES
VMEM_LIMIT_BYTES = 56 * 1024 * 1024
MASK_VALUE = -1e30
FLT_MAX = 3.4028234663852886e38
INT_MIN = -(2 ** 31)


def _cparams(*sem):
    return pltpu.CompilerParams(dimension_semantics=sem, vmem_limit_bytes=VMEM_LIMIT_BYTES)


def _rms(x, g):
    return x * lax.rsqrt(jnp.mean(x * x, axis=-1, keepdims=True) + EPS) * g


GELU_C0 = 0.7978845608028654
GELU_C1 = GELU_C0 * 0.044715


def _gelu(x):
    return 0.5 * x * (1.0 + jnp.tanh(x * (GELU_C0 + GELU_C1 * (x * x))))


def _dot(a, b):
    return jnp.dot(a, b, preferred_element_type=_F32)


def _full_spec(shape):
    n = len(shape)
    return pl.BlockSpec(shape, lambda *_: (0,) * n)


def _lru_body(x_ref, nm_ref, wlx_ref, wlg_ref, wg0_ref, cw_ref, cb_ref, wa_ref, ba_ref, wi_ref,
              bi_ref, lam_ref, out_ref, buf_ref, hc_ref, a_ref, b_ref):
    ts = x_ref.shape[1]
    c_dim = out_ref.shape[2]

    @pl.when(pl.program_id(1) == 0)
    def _():
        buf_ref[0:SUBLANES, :] = jnp.zeros((SUBLANES, c_dim), _F32)
        hc_ref[...] = jnp.zeros_like(hc_ref)

    h = _rms(x_ref[0], nm_ref[...]).astype(_BF)
    lx = _dot(h, wlx_ref[...])
    buf_ref[SUBLANES:SUBLANES + ts, :] = lx
    ext = buf_ref[...]
    cw = cw_ref[...]
    xc = cb_ref[...] + cw[3:4] * lx
    for j in range(CONV_WIDTH - 1):
        d = CONV_WIDTH - 1 - j
        xc = xc + cw[j:j + 1] * pltpu.roll(ext, d, 0)[SUBLANES:, :]
    buf_ref[0:SUBLANES, :] = lx[ts - SUBLANES:, :]

    xcb = xc.astype(_BF)
    r = jax.nn.sigmoid(_dot(xcb, wa_ref[...]) + ba_ref[...])
    ig = jax.nn.sigmoid(_dot(xcb, wi_ref[...]) + bi_ref[...])
    lam = lam_ref[...]
    softplus_neg_lam = jnp.maximum(-lam, 0.0) + jnp.log1p(jnp.exp(-jnp.abs(lam)))
    log_a = (-LRU_C) * r * softplus_neg_lam
    a = jnp.exp(log_a)
    y = jnp.tanh(-log_a) * (a * a + 1.0)
    u = jnp.where(y > 0.0, y * lax.rsqrt(y), 0.0) * (ig * xc)

    a = a.reshape(ts // SUBLANES, SUBLANES, c_dim)
    u = u.reshape(ts // SUBLANES, SUBLANES, c_dim)
    rid = lax.broadcasted_iota(jnp.int32, a.shape, 1)
    for d in (1, 2, 4):
        ok = rid >= d
        a_s = pltpu.roll(a, d, 1)
        u_s = pltpu.roll(u, d, 1)
        u = jnp.where(ok, a * u_s + u, u)
        a = jnp.where(ok, a * a_s, a)
    a_ref[...] = a.reshape(ts, c_dim)
    b_ref[...] = u.reshape(ts, c_dim)

    def step(g, hp):
        r0 = pl.multiple_of(g * SUBLANES, SUBLANES)
        hcur = a_ref[pl.ds(r0, SUBLANES), :] * hp + b_ref[pl.ds(r0, SUBLANES), :]
        b_ref[pl.ds(r0, SUBLANES), :] = hcur
        return jnp.broadcast_to(hcur[SUBLANES - 1:SUBLANES, :], (SUBLANES, c_dim))

    hc_ref[...] = lax.fori_loop(0, ts // SUBLANES, step, hc_ref[...])
    gate = _gelu(_dot(h, wlg_ref[...])) * jax.nn.sigmoid(_dot(h, wg0_ref[...]))
    out_ref[0] = (b_ref[...] * gate).astype(_BF)


def _lru_branch(x, nm, wlx, wlg, wg0, cw, cb, wa, ba, wi, bi, lam, ts):
    b, s, d = x.shape
    c = wlx.shape[1]
    row = lambda v: v.reshape(1, -1)
    args = (x, row(nm), wlx, wlg, wg0, cw, row(cb), wa, row(ba), wi, row(bi), row(lam))
    in_specs = [pl.BlockSpec((1, ts, d), lambda i, j: (i, j, 0))]
    in_specs += [_full_spec(a.shape) for a in args[1:]]
    return pl.pallas_call(
        _lru_body,
        grid=(b, s // ts),
        in_specs=in_specs,
        out_specs=pl.BlockSpec((1, ts, c), lambda i, j: (i, j, 0)),
        out_shape=jax.ShapeDtypeStruct((b, s, c), _BF),
        scratch_shapes=[pltpu.VMEM((ts + SUBLANES, c), _F32), pltpu.VMEM((SUBLANES, c), _F32),
                        pltpu.VMEM((ts, c), _F32), pltpu.VMEM((ts, c), _F32)],
        compiler_params=_cparams("arbitrary", "arbitrary"),
        name="lru",
    )(*args)


def _rope128(x, cos, sin):
    return x * cos + pltpu.roll(x, LANES // 2, 1) * sin


def _rope64(x, cos, sin, first_half):
    rot = jnp.where(first_half, pltpu.roll(x, LANES - IDX_DIM // 2, 1), pltpu.roll(x, IDX_DIM // 2, 1))
    return x * cos + rot * sin


def _dsa_proj_body(x_ref, nm_ref, wq_ref, wk_ref, wv_ref, wqi_ref, wki_ref, wwi_ref, wg1_ref,
                   qn_ref, kn_ref, ikn_ref, c128_ref, s128_ref, c64_ref, s64_ref,
                   qT_ref, k_ref, vT_ref, qiT_ref, ki_ref, wT_ref, g1_ref):
    ts = x_ref.shape[1]
    h = _rms(x_ref[0], nm_ref[...]).astype(_BF)
    c128, s128 = c128_ref[...], s128_ref[...]
    c64, s64 = c64_ref[...], s64_ref[...]
    first_half = (lax.broadcasted_iota(jnp.int32, (ts, LANES), 1) & (IDX_DIM - 1)) < IDX_DIM // 2

    def head_norm_rope(t, g, scale):
        outs = []
        for i in range(t.shape[1] // LANES):
            th = _rms(t[:, i * LANES:(i + 1) * LANES], g)
            outs.append(_rope128(th, c128, s128) * scale)
        return jnp.concatenate(outs, axis=1)

    q = head_norm_rope(_dot(h, wq_ref[...]), qn_ref[...], float(LANES) ** -0.5)
    qT_ref[0] = q.T.astype(_BF)
    k_ref[0] = head_norm_rope(_dot(h, wk_ref[...]), kn_ref[...], 1.0).astype(_BF)
    vT = _dot(h, wv_ref[...]).T.astype(_BF)
    for g in range(ATT_KV_HEADS):
        vT_ref[0, 0, g, :LANES, :] = vT[g * LANES:(g + 1) * LANES, :]
        vT_ref[0, 0, g, LANES:, :] = jnp.ones((ONES_ROWS, ts), _BF)

    qi = _dot(h, wqi_ref[...])
    qi = jnp.concatenate(
        [_rope64(qi[:, i * LANES:(i + 1) * LANES], c64, s64, first_half) for i in range(qi.shape[1] // LANES)],
        axis=1) * (float(IDX_DIM) ** -0.5)
    qiT_ref[0] = qi.T.astype(_BF)
    ki = _rope64(_rms(_dot(h, wki_ref[...]), ikn_ref[...]), c64, s64, first_half)
    ki_ref[0] = ki[:, :IDX_DIM].astype(_BF)
    wi = _dot(h, wwi_ref[...]) * (float(IDX_HEADS) ** -0.5)
    wT_ref[0] = wi.T[:IDX_HEADS, :]
    g1_ref[0] = jax.nn.sigmoid(_dot(h, wg1_ref[...])).astype(_BF)


def _dsa_proj(x, nm, wq, wk, wv, wqi, wki2, wwi, wg1, qn, kn, ikn2, c128, s128, c64, s64, ts):
    b, s, d = x.shape
    row = lambda v: v.reshape(1, -1)
    nq, nk, nv, nqi = wq.shape[1], wk.shape[1], wv.shape[1], wqi.shape[1]
    args = (x, row(nm), wq, wk, wv, wqi, wki2, wwi, wg1, row(qn), row(kn), row(ikn2), c128, s128, c64, s64)
    tile = lambda n: pl.BlockSpec((1, ts, n), lambda i, j: (i, j, 0))
    tile_t = lambda n: pl.BlockSpec((1, n, ts), lambda i, j: (i, 0, j))
    tab = pl.BlockSpec((ts, LANES), lambda i, j: (j, 0))
    in_specs = [tile(d)] + [_full_spec(a.shape) for a in args[1:12]] + [tab] * 4
    out_shape = (
        jax.ShapeDtypeStruct((b, nq, s), _BF),
        jax.ShapeDtypeStruct((b, s, nk), _BF),
        jax.ShapeDtypeStruct((b, s // ts, ATT_KV_HEADS, LANES + ONES_ROWS, ts), _BF),
        jax.ShapeDtypeStruct((b, nqi, s), _BF),
        jax.ShapeDtypeStruct((b, s, IDX_DIM), _BF),
        jax.ShapeDtypeStruct((b, IDX_HEADS, s), _F32),
        jax.ShapeDtypeStruct((b, s, d), _BF),
    )
    out_specs = (tile_t(nq), tile(nk),
                 pl.BlockSpec((1, 1, ATT_KV_HEADS, LANES + ONES_ROWS, ts), lambda i, j: (i, j, 0, 0, 0)), tile_t(nqi),
                 tile(IDX_DIM), tile_t(IDX_HEADS), tile(d))
    return pl.pallas_call(
        _dsa_proj_body, grid=(b, s // ts), in_specs=in_specs, out_specs=out_specs, out_shape=out_shape,
        compiler_params=_cparams("arbitrary", "arbitrary"), name="dsa_proj",
    )(*args)


def _key_to_float(u):
    key = u ^ jnp.int32(INT_MIN)
    bits = key ^ ((key >> 31) & jnp.int32(0x7FFFFFFF))
    return jnp.where((u >> 23) == 0, -jnp.inf, lax.bitcast_convert_type(bits, _F32))


def _high_half(x):
    bits = lax.bitcast_convert_type(x, jnp.int32) & jnp.int32(-65536)
    return lax.bitcast_convert_type(bits, _F32).astype(_BF)


def _dsa_body(qT_ref, qiT_ref, wT_ref, k_ref, vT_ref, ki_ref, g1_ref, out_ref,
              sc_ref, sh_ref, m_ref, l_ref, acc_ref, bias_ref, s_ref, p_ref, *, topk, tk):
    tq = qT_ref.shape[2]
    dh = LANES
    q0 = pl.program_id(1) * tq
    n_chunks = (q0 + tq + tk - 1) // tk

    qicat = jnp.concatenate([qiT_ref[0, h * IDX_DIM:(h + 1) * IDX_DIM, :] for h in range(IDX_HEADS)], axis=1)
    w = wT_ref[0]
    qpos = q0 + lax.broadcasted_iota(jnp.int32, (tk, tq), 1)
    krow = lax.broadcasted_iota(jnp.int32, (tk, tq), 0)

    def score_chunk(c, carry):
        r0 = pl.multiple_of(c * tk, tk)
        lg = _dot(ki_ref[0, pl.ds(r0, tk), :], qicat)
        s = jnp.zeros((tk, tq), _F32)
        for h in range(IDX_HEADS):
            s = s + jnp.maximum(lg[:, h * tq:(h + 1) * tq], 0.0) * w[h:h + 1, :]
        s = jnp.where(krow + r0 <= qpos, s, -jnp.inf)
        sc_ref[pl.ds(r0, tk), :] = s
        sh_ref[pl.ds(r0, tk), :] = _high_half(s)
        return carry

    lax.fori_loop(0, n_chunks, score_chunk, 0)

    def make_count(nc, ref, rows, dtype):
        def count_ge(cand):
            one, zero = jnp.ones((), dtype), jnp.zeros((), dtype)
            parts = [jnp.zeros((rows, tq), dtype) for _ in range(4)]
            n = 0
            for c in range(nc):
                hit = jnp.where(ref[c * tk:(c + 1) * tk, :] >= cand, one, zero)
                for r in range(tk // rows):
                    parts[n % 4] = parts[n % 4] + hit[r * rows:(r + 1) * rows, :]
                    n += 1
            parts = [p.astype(_F32) for p in parts]
            return jnp.sum((parts[0] + parts[1]) + (parts[2] + parts[3]), axis=0, keepdims=True)
        return count_ge

    max_chunks = sc_ref.shape[0] // tk
    assert max_chunks * tk // (2 * SUBLANES) <= 4 * 64
    coarse = [make_count(nc, sh_ref, 2 * SUBLANES, _BF) for nc in range(1, max_chunks + 1)]
    fine = [make_count(nc, sc_ref, SUBLANES, _F32) for nc in range(1, max_chunks + 1)]

    def bisect_high(i, u):
        trial = u | lax.shift_left(jnp.int32(1), 31 - i)
        cnt = lax.switch(n_chunks - 1, coarse, _high_half(_key_to_float(trial)))
        return jnp.where(cnt >= float(topk), trial, u)

    def bisect_low(i, u):
        trial = u | lax.shift_left(jnp.int32(1), 31 - i)
        cnt = lax.switch(n_chunks - 1, fine, _key_to_float(trial))
        return jnp.where(cnt >= float(topk), trial, u)

    u = lax.fori_loop(0, 16, bisect_high, jnp.zeros((1, tq), jnp.int32))
    u = lax.fori_loop(16, 32, bisect_low, u)
    tau = jnp.maximum(_key_to_float(u), -FLT_MAX)

    m_ref[...] = jnp.full(m_ref.shape, MASK_VALUE, _F32)
    l_ref[...] = jnp.zeros_like(l_ref)
    acc_ref[...] = jnp.zeros_like(acc_ref)

    sub = 8 * SUBLANES
    groups = [slice(r * SUBLANES, (r + 1) * SUBLANES) for r in range(sub // SUBLANES)]

    def att_chunk(c, carry):
        r0 = pl.multiple_of(c * tk, tk)
        bias_ref[...] = jnp.where(sc_ref[pl.ds(r0, tk), :] >= tau, 0.0, MASK_VALUE)

        def logits(h):
            g = h // ATT_GROUP
            s_ref[h] = _dot(k_ref[0, pl.ds(r0, tk), g * dh:(g + 1) * dh], qT_ref[0, h * dh:(h + 1) * dh, :])

        logits(0)
        for h in range(ATT_HEADS):
            if h + 1 < ATT_HEADS:
                logits(h + 1)
            m_old = m_ref[h:h + 1, :]
            mx = jnp.broadcast_to(m_old, (SUBLANES, tq))
            for i in range(tk // sub):
                rows = slice(i * sub, (i + 1) * sub)
                sb = s_ref[h, rows, :] + bias_ref[rows, :]
                s_ref[h, rows, :] = sb
                for gr in groups:
                    mx = jnp.maximum(mx, sb[gr, :])
            m_new = jnp.max(mx, axis=0, keepdims=True)
            for i in range(tk // sub):
                rows = slice(i * sub, (i + 1) * sub)
                p_ref[h, rows, :] = jnp.exp((s_ref[h, rows, :] - m_new).astype(_BF))
            alpha = jnp.exp(m_old - m_new)
            pv = _dot(vT_ref[0, c, h // ATT_GROUP], p_ref[h])
            l_ref[h:h + 1, :] = alpha * l_ref[h:h + 1, :] + pv[dh:dh + 1, :]
            acc_ref[h * dh:(h + 1) * dh, :] = alpha * acc_ref[h * dh:(h + 1) * dh, :] + pv[:dh, :]
            m_ref[h:h + 1, :] = m_new
        return carry

    lax.fori_loop(0, n_chunks, att_chunk, 0)
    outs = []
    for h in range(ATT_HEADS):
        o = acc_ref[h * dh:(h + 1) * dh, :] / l_ref[h:h + 1, :]
        outs.append(o.T)
    out_ref[0] = (jnp.concatenate(outs, axis=1) * g1_ref[0].astype(_F32)).astype(_BF)


def _dsa(qT, qiT, wT, k, vT, ki, g1, topk, tq, tk):
    b, nq, s = qT.shape
    d = g1.shape[2]
    in_specs = [
        pl.BlockSpec((1, nq, tq), lambda i, j: (i, 0, j)),
        pl.BlockSpec((1, qiT.shape[1], tq), lambda i, j: (i, 0, j)),
        pl.BlockSpec((1, IDX_HEADS, tq), lambda i, j: (i, 0, j)),
        pl.BlockSpec((1, s, k.shape[2]), lambda i, j: (i, 0, 0)),
        pl.BlockSpec((1,) + vT.shape[1:], lambda i, j: (i, 0, 0, 0, 0)),
        pl.BlockSpec((1, s, IDX_DIM), lambda i, j: (i, 0, 0)),
        pl.BlockSpec((1, tq, d), lambda i, j: (i, j, 0)),
    ]
    return pl.pallas_call(
        functools.partial(_dsa_body, topk=topk, tk=tk),
        grid=(b, s // tq), in_specs=in_specs,
        out_specs=pl.BlockSpec((1, tq, d), lambda i, j: (i, j, 0)),
        out_shape=jax.ShapeDtypeStruct((b, s, d), _BF),
        scratch_shapes=[pltpu.VMEM((s, tq), _F32), pltpu.VMEM((s, tq), _BF), pltpu.VMEM((ATT_HEADS, tq), _F32),
                        pltpu.VMEM((ATT_HEADS, tq), _F32), pltpu.VMEM((nq, tq), _F32),
                        pltpu.VMEM((tk, tq), _F32), pltpu.VMEM((ATT_HEADS, tk, tq), _F32),
                        pltpu.VMEM((ATT_HEADS, tk, tq), _BF)],
        compiler_params=_cparams("arbitrary", "arbitrary"), name="dsa",
    )(qT, qiT, wT, k, vT, ki, g1)


def _mem_kv_body(mem_ref, mn_ref, wkv_ref, mkn_ref, mkT_ref, mv_ref):
    d = mem_ref.shape[2]
    dm = d // MEM_HEADS
    m = _rms(mem_ref[0], mn_ref[...]).astype(_BF)
    kv = _dot(m, wkv_ref[...])
    mk = jnp.concatenate([_rms(kv[:, i * dm:(i + 1) * dm], mkn_ref[...]) for i in range(MEM_HEADS)], axis=1)
    mkT_ref[0] = mk.T.astype(_BF)
    mv_ref[0] = kv[:, d:].astype(_BF)


def _mem_kv(mem, mn, wkv, mkn):
    b, m, d = mem.shape
    row = lambda v: v.reshape(1, -1)
    return pl.pallas_call(
        _mem_kv_body, grid=(b,),
        in_specs=[pl.BlockSpec((1, m, d), lambda i: (i, 0, 0)), _full_spec((1, d)), _full_spec(wkv.shape),
                  _full_spec((1, d // MEM_HEADS))],
        out_specs=(pl.BlockSpec((1, d, m), lambda i: (i, 0, 0)), pl.BlockSpec((1, m, d), lambda i: (i, 0, 0))),
        out_shape=(jax.ShapeDtypeStruct((b, d, m), _BF), jax.ShapeDtypeStruct((b, m, d), _BF)),
        compiler_params=_cparams("arbitrary"), name="mem_kv",
    )(mem, row(mn), wkv, row(mkn))


def _mem_body(x_ref, nm_ref, wmq_ref, wg2_ref, mqn_ref, mkT_ref, mv_ref, out_ref):
    d = x_ref.shape[2]
    dm = d // MEM_HEADS
    h = _rms(x_ref[0], nm_ref[...]).astype(_BF)
    mq = _dot(h, wmq_ref[...])
    outs = []
    for i in range(MEM_HEADS):
        qh = (_rms(mq[:, i * dm:(i + 1) * dm], mqn_ref[...]) * (float(dm) ** -0.5)).astype(_BF)
        s = _dot(qh, mkT_ref[0, i * dm:(i + 1) * dm, :])
        p = jnp.exp(s - jnp.max(s, axis=-1, keepdims=True))
        o = _dot(p.astype(_BF), mv_ref[0, :, i * dm:(i + 1) * dm])
        outs.append(o / jnp.sum(p, axis=-1, keepdims=True))
    out_ref[0] = (jnp.concatenate(outs, axis=1) * jax.nn.sigmoid(_dot(h, wg2_ref[...]))).astype(_BF)


def _mem_branch(x, nm, wmq, wg2, mqn, mkT, mv, ts):
    b, s, d = x.shape
    m = mv.shape[1]
    row = lambda v: v.reshape(1, -1)
    return pl.pallas_call(
        _mem_body, grid=(b, s // ts),
        in_specs=[pl.BlockSpec((1, ts, d), lambda i, j: (i, j, 0)), _full_spec((1, d)), _full_spec(wmq.shape),
                  _full_spec(wg2.shape), _full_spec((1, d // MEM_HEADS)),
                  pl.BlockSpec((1, d, m), lambda i, j: (i, 0, 0)), pl.BlockSpec((1, m, d), lambda i, j: (i, 0, 0))],
        out_specs=pl.BlockSpec((1, ts, d), lambda i, j: (i, j, 0)),
        out_shape=jax.ShapeDtypeStruct((b, s, d), _BF),
        compiler_params=_cparams("arbitrary", "arbitrary"), name="mem",
    )(x, row(nm), wmq, wg2, row(mqn), mkT, mv)


def _mid_body(x_ref, a_ref, b_ref, c_ref, wo_ref, nf_ref, wpqT_ref, sk_ref, x1_ref, h2T_ref, scT_ref):
    merged = (a_ref[...].astype(_F32) + b_ref[...].astype(_F32) + c_ref[...].astype(_F32)).astype(_BF)
    x1 = x_ref[...] + _dot(merged, wo_ref[...])
    x1_ref[...] = x1
    h2T = _rms(x1, nf_ref[...]).T.astype(_BF)
    h2T_ref[...] = h2T
    pqT = _dot(wpqT_ref[...], h2T).astype(_BF)
    for c in range(sk_ref.shape[0]):
        scT_ref[c * PEER_KEYS:(c + 1) * PEER_KEYS, :] = _dot(sk_ref[c], pqT[c * LANES:(c + 1) * LANES, :])


def _mid(x2, ma, mb, mc, wo, nf, wpqT, sk, ts):
    t, d = x2.shape
    nsc = sk.shape[0] * PEER_KEYS
    tile = pl.BlockSpec((ts, d), lambda i: (i, 0))
    return pl.pallas_call(
        _mid_body, grid=(t // ts,),
        in_specs=[tile, tile, tile, tile, _full_spec(wo.shape), _full_spec((1, d)), _full_spec(wpqT.shape),
                  _full_spec(sk.shape)],
        out_specs=(tile, pl.BlockSpec((d, ts), lambda i: (0, i)), pl.BlockSpec((nsc, ts), lambda i: (0, i))),
        out_shape=(jax.ShapeDtypeStruct((t, d), _F32), jax.ShapeDtypeStruct((d, t), _BF),
                   jax.ShapeDtypeStruct((nsc, t), _F32)),
        compiler_params=_cparams("arbitrary"), name="mid",
    )(x2, ma, mb, mc, wo, nf.reshape(1, -1), wpqT, sk)


def _top_values(s, n):
    groups = [s[r * SUBLANES:(r + 1) * SUBLANES, :] for r in range(s.shape[0] // SUBLANES)]
    size = pl.next_power_of_2(len(groups))
    groups += [jnp.full_like(groups[0], -jnp.inf)] * (size - len(groups))
    k = 2
    while k <= size:
        j = k // 2
        while j >= 1:
            for i in range(size):
                l = i ^ j
                if l > i:
                    hi, lo = jnp.maximum(groups[i], groups[l]), jnp.minimum(groups[i], groups[l])
                    groups[i], groups[l] = (hi, lo) if (i & k) == 0 else (lo, hi)
            j //= 2
        k *= 2
    vals = []
    for it in range(n):
        m = jnp.max(groups[0], axis=0, keepdims=True)
        vals.append(m)
        remaining = n - 1 - it
        head = groups[0] == m
        for d in range(min(remaining, size)):
            below = groups[d + 1] if d + 1 < size else jnp.full_like(m, -jnp.inf)
            groups[d] = jnp.where(head, below, groups[d])
    return vals


def _route_body(scT_ref, tau_ref, c_ref, top2_ref):
    def head(h, carry):
        r1 = pl.multiple_of(h * 2 * PEER_KEYS, PEER_KEYS)
        r2 = pl.multiple_of(h * 2 * PEER_KEYS + PEER_KEYS, PEER_KEYS)
        n = PEER_TOPK + 1
        v1 = _top_values(scT_ref[pl.ds(r1, PEER_KEYS), :], n)
        v2 = _top_values(scT_ref[pl.ds(r2, PEER_KEYS), :], n)
        edge = SUBLANES
        assert 2 * (edge + 1) > n
        pad = [jnp.full_like(v1[0], -jnp.inf)] * (-n % SUBLANES)
        col1 = jnp.concatenate(v1 + pad, axis=0)
        col2 = jnp.concatenate(v2 + pad, axis=0)
        cand = jnp.concatenate([v1[a] + col2[:edge] for a in range(edge)]
                               + [v1[0] + col2[edge:], v2[0] + col1[edge:]], axis=0)
        best = _top_values(cand, n)
        tau = best[PEER_TOPK - 1]
        top = v1[0] + v2[0]
        z = jnp.sum(jnp.where(cand >= tau, jnp.exp(cand - top), 0.0), axis=0, keepdims=True)
        tau_ref[pl.ds(h, 1), :] = 0.5 * (tau + best[PEER_TOPK])
        c_ref[pl.ds(h, 1), :] = top + jnp.log(z)
        top2_ref[pl.ds(pl.multiple_of(h * TOP_ROWS, SUBLANES), TOP_ROWS), :] = col2
        return carry

    lax.fori_loop(0, PEER_HEADS, head, 0, unroll=2)


def _route(scT, te):
    nsc, t = scT.shape
    spec = pl.BlockSpec((PEER_HEADS, te), lambda i: (0, i))
    return pl.pallas_call(
        _route_body, grid=(t // te,),
        in_specs=[pl.BlockSpec((nsc, te), lambda i: (0, i))],
        out_specs=(spec, spec, pl.BlockSpec((PEER_HEADS * TOP_ROWS, te), lambda i: (0, i))),
        out_shape=(jax.ShapeDtypeStruct((PEER_HEADS, t), _F32),) * 2
        + (jax.ShapeDtypeStruct((PEER_HEADS * TOP_ROWS, t), _F32),),
        compiler_params=_cparams("arbitrary"), name="route",
    )(scT)


def _experts_body(h2T_ref, scT_ref, tau_ref, c_ref, top2_ref, u_ref, vT_ref, x1_ref, out_ref,
                  acc_ref, e2_ref, rank_ref, cnt_ref, scale_ref, act_ref, g_ref, rows_ref, *, rows_per_step):
    e = pl.program_id(1)
    nk = PEER_KEYS
    ts = act_ref.shape[1]

    @pl.when(e == 0)
    def _():
        acc_ref[...] = jnp.zeros_like(acc_ref)
        for h in range(PEER_HEADS):
            best = [top2_ref[h * TOP_ROWS + b:h * TOP_ROWS + b + 1, :] for b in range(PEER_TOPK + 1)]
            s1 = scT_ref[2 * h * nk:(2 * h + 1) * nk, :]
            s2 = scT_ref[(2 * h + 1) * nk:(2 * h + 2) * nk, :]
            e2_ref[h * nk:(h + 1) * nk, :] = jnp.exp(s2 - best[0]).astype(_BF)
            rank = jnp.zeros((nk, ts), _F32)
            for b in range(PEER_TOPK):
                rank = rank + jnp.where(best[b] > s2, 1.0, 0.0)
            rank_ref[h * nk:(h + 1) * nk, :] = rank.astype(_BF)
            thr = tau_ref[h:h + 1, :] - s1
            cnt = jnp.zeros((nk, ts), _F32)
            for b in range(PEER_TOPK + 1):
                cnt = cnt + jnp.where(best[b] >= thr, 1.0, 0.0)
            cnt_ref[h * nk:(h + 1) * nk, :] = cnt
            scale_ref[h * nk:(h + 1) * nk, :] = 0.5 * jnp.exp(s1 + best[0] - c_ref[h:h + 1, :])

    n_rows = rows_per_step * PEER_HEADS
    for i in range(rows_per_step):
        for h in range(PEER_HEADS):
            src = h * nk + e * rows_per_step + i
            r = i * PEER_HEADS + h
            rows_ref[r:r + 1, :] = cnt_ref[pl.ds(src, 1), :]
            rows_ref[n_rows + r:n_rows + r + 1, :] = scale_ref[pl.ds(src, 1), :]

    def first_matmul(i):
        act_ref[i * nk:(i + 1) * nk, :] = _dot(u_ref[i * nk:(i + 1) * nk, :], h2T_ref[...])

    pair = 2 * nk
    zero = jnp.zeros((), _BF)
    first_matmul(0)
    for i in range(rows_per_step):
        if i + 1 < rows_per_step:
            first_matmul(i + 1)
        wgt = jnp.zeros((nk, ts), _BF)
        for h in range(PEER_HEADS):
            r = i * PEER_HEADS + h
            cnt = rows_ref[r:r + 1, :].astype(_BF)
            scale = rows_ref[n_rows + r:n_rows + r + 1, :].astype(_BF)
            hit = jnp.where(rank_ref[h * nk:(h + 1) * nk, :] < cnt, e2_ref[h * nk:(h + 1) * nk, :], zero)
            wgt = wgt + hit * scale
        x = act_ref[i * nk:(i + 1) * nk, :]
        t = jnp.tanh(x * (GELU_C0 + GELU_C1 * (x * x)))
        g_ref[i * nk:(i + 1) * nk, :] = (x + x * t).astype(_BF) * wgt
        if i % 2 == 1:
            p = i // 2
            acc_ref[...] += _dot(vT_ref[0, :, p * pair:(p + 1) * pair], g_ref[p * pair:(p + 1) * pair, :])

    @pl.when(e == pl.num_programs(1) - 1)
    def _():
        out_ref[...] = x1_ref[...] + acc_ref[...].T


def _experts(h2T, scT, tau, cn, top2, u, vT, x1, ts, rows_per_step):
    d, t = h2T.shape
    nsc = scT.shape[0]
    ne = u.shape[0]
    eb = rows_per_step * PEER_KEYS
    nb = ne // eb
    nhk = PEER_HEADS * PEER_KEYS
    col = lambda n: pl.BlockSpec((n, ts), lambda i, e: (0, i))
    tile = pl.BlockSpec((ts, d), lambda i, e: (i, 0))
    return pl.pallas_call(
        functools.partial(_experts_body, rows_per_step=rows_per_step),
        grid=(t // ts, nb),
        in_specs=[col(d), col(nsc), col(PEER_HEADS), col(PEER_HEADS), col(top2.shape[0]),
                  pl.BlockSpec((eb, d), lambda i, e: (e, 0)), pl.BlockSpec((1, d, eb), lambda i, e: (e, 0, 0)), tile],
        out_specs=tile,
        out_shape=jax.ShapeDtypeStruct((t, d), _F32),
        scratch_shapes=[pltpu.VMEM((d, ts), _F32), pltpu.VMEM((nhk, ts), _BF), pltpu.VMEM((nhk, ts), _BF),
                        pltpu.VMEM((nhk, ts), _F32), pltpu.VMEM((nhk, ts), _F32),
                        pltpu.VMEM((eb, ts), _F32), pltpu.VMEM((eb, ts), _BF),
                        pltpu.VMEM((2 * rows_per_step * PEER_HEADS, ts), _F32)],
        compiler_params=_cparams("arbitrary", "arbitrary"), name="experts",
    )(h2T, scT, tau, cn, top2, u, vT, x1)


def _block_diag(w):
    n, c, d = w.shape
    return jnp.einsum("ncd,nm->ncmd", w, jnp.eye(n, dtype=w.dtype)).reshape(n * c, n * d)


def _rope_tables(s, dim, reps):
    half = dim // 2
    freq = ROPE_THETA ** (-jnp.arange(half, dtype=_F32) / half)
    ang = jnp.arange(s, dtype=_F32)[:, None] * freq[None, :]
    cos = jnp.tile(jnp.concatenate([jnp.cos(ang), jnp.cos(ang)], axis=1), (1, reps))
    sin = jnp.tile(jnp.concatenate([-jnp.sin(ang), jnp.sin(ang)], axis=1), (1, reps))
    return cos, sin


def _tile(n, pref):
    return pref if n % pref == 0 else n


def _layer(x, mem, p, l):
    b, s, d = x.shape
    bf = lambda a: a.astype(_BF)
    w_in = p["w_in"][l]
    n_q, n_kv = ATT_HEADS * LANES, ATT_KV_HEADS * LANES
    n_qi = IDX_HEADS * IDX_DIM
    sizes = (d, d, n_q, n_kv, n_kv, n_qi, IDX_DIM, IDX_HEADS, d, 3 * d)
    offs = [0]
    for n in sizes:
        offs.append(offs[-1] + n)
    wlx, wlg, wq, wk, wv, wqi, wki, wwi, wmq, wgates = (w_in[:, offs[i]:offs[i + 1]] for i in range(len(sizes)))
    wki2 = jnp.concatenate([wki, wki], axis=1)
    wwi_p = jnp.pad(wwi, ((0, 0), (0, LANES - IDX_HEADS)))
    wg0, wg1, wg2 = wgates[:, :d], wgates[:, d:2 * d], wgates[:, 2 * d:]
    nm = p["norm_mix"][l]

    ts_a = _tile(s, 256)
    m_lru = _lru_branch(x, nm, bf(wlx), bf(wlg), bf(wg0), p["conv_w"][l], p["conv_b"][l],
                        bf(_block_diag(p["lru_wa"][l])), p["lru_ba"][l].reshape(-1),
                        bf(_block_diag(p["lru_wi"][l])), p["lru_bi"][l].reshape(-1), p["lru_lambda"][l], ts_a)

    tk = _tile(s, 512)
    c128, s128 = _rope_tables(s, LANES, 1)
    c64, s64 = _rope_tables(s, IDX_DIM, LANES // IDX_DIM)
    ikn2 = jnp.concatenate([p["idx_k_norm"][l]] * 2)
    qT, k, vT, qiT, ki, wT, g1 = _dsa_proj(x, nm, bf(wq), bf(wk), bf(wv), bf(wqi), bf(wki2), bf(wwi_p), bf(wg1),
                                           p["q_norm"][l], p["k_norm"][l], ikn2, c128, s128, c64, s64, tk)
    m_att = _dsa(qT, qiT, wT, k, vT, ki, g1, min(TOPK_MAX, s // 4), _tile(s, 256), tk)

    mkT, mv = _mem_kv(mem, p["mem_norm"][l], bf(p["w_mem_kv"][l]), p["mem_k_norm"][l])
    m_mem = _mem_branch(x, nm, bf(wmq), bf(wg2), p["mem_q_norm"][l], mkT, mv, _tile(s, 512))

    t = b * s
    flat = lambda a: a.reshape(t, d)
    sk = bf(p["peer_subkeys"][l].reshape(2 * PEER_HEADS, PEER_KEYS, -1))
    x1, h2T, scT = _mid(flat(x), flat(m_lru), flat(m_att), flat(m_mem), bf(p["w_out"][l]), p["norm_ffn"][l],
                        bf(p["peer_wq"][l].T), sk, _tile(t, 512))
    tau, cn, top2 = _route(scT, _tile(t, 256))
    rows_per_step = 8
    vT = bf(p["peer_v"][l]).reshape(-1, rows_per_step * PEER_KEYS, d).transpose(0, 2, 1)
    out = _experts(h2T, scT, tau, cn, top2, bf(p["peer_u"][l]), vT, x1, _tile(t, 512), rows_per_step)
    return out.reshape(b, s, d)


def kernel(x, mem, norm_mix, w_in, conv_w, conv_b, lru_wa, lru_ba, lru_wi, lru_bi, lru_lambda, q_norm, k_norm,
           idx_k_norm, mem_norm, w_mem_kv, mem_q_norm, mem_k_norm, w_out, norm_ffn, peer_wq, peer_subkeys,
           peer_u, peer_v):
    p = dict(norm_mix=norm_mix, w_in=w_in, conv_w=conv_w, conv_b=conv_b, lru_wa=lru_wa, lru_ba=lru_ba,
             lru_wi=lru_wi, lru_bi=lru_bi, lru_lambda=lru_lambda, q_norm=q_norm, k_norm=k_norm,
             idx_k_norm=idx_k_norm, mem_norm=mem_norm, w_mem_kv=w_mem_kv, mem_q_norm=mem_q_norm,
             mem_k_norm=mem_k_norm, w_out=w_out, norm_ffn=norm_ffn, peer_wq=peer_wq, peer_subkeys=peer_subkeys,
             peer_u=peer_u, peer_v=peer_v)
    for l in range(norm_mix.shape[0]):
        x = _layer(x, mem, p, l)
    return x
```

```python
import functools

import jax
import jax.numpy as jnp
from jax import lax
from jax.experimental import pallas as pl
from jax.experimental.pallas import tpu as pltpu

_F32 = jnp.float32
_BF = jnp.bfloat16

EPS = 1e-6
ROPE_THETA = 10000.0
LRU_C = 8.0
LRU_BLOCKS = 16
CONV_WIDTH = 4
ATT_HEADS = 8
ATT_KV_HEADS = 2
ATT_GROUP = ATT_HEADS // ATT_KV_HEADS
IDX_HEADS = 8
IDX_DIM = 64
TOPK_MAX = 256
MEM_HEADS = 4
PEER_HEADS = 8
PEER_KEYS = 128
PEER_TOPK = 16
TOP_ROWS = 24

LANES = 128
SUBLANES = 8
ONES_ROWS = 2 * SUBLANES
VMEM_LIMIT_BYTES = 56 * 1024 * 1024
MASK_VALUE = -1e30
FLT_MAX = 3.4028234663852886e38
INT_MIN = -(2 ** 31)


def _cparams(*sem):
    return pltpu.CompilerParams(dimension_semantics=sem, vmem_limit_bytes=VMEM_LIMIT_BYTES)


def _rms(x, g):
    return x * lax.rsqrt(jnp.mean(x * x, axis=-1, keepdims=True) + EPS) * g


GELU_C0 = 0.7978845608028654
GELU_C1 = GELU_C0 * 0.044715


def _gelu(x):
    return 0.5 * x * (1.0 + jnp.tanh(x * (GELU_C0 + GELU_C1 * (x * x))))


def _dot(a, b):
    return jnp.dot(a, b, preferred_element_type=_F32)


def _full_spec(shape):
    n = len(shape)
    return pl.BlockSpec(shape, lambda *_: (0,) * n)


def _lru_body(x_ref, nm_ref, wlx_ref, wlg_ref, wg0_ref, cw_ref, cb_ref, wa_ref, ba_ref, wi_ref,
              bi_ref, lam_ref, out_ref, buf_ref, hc_ref, a_ref, b_ref):
    ts = x_ref.shape[1]
    c_dim = out_ref.shape[2]

    @pl.when(pl.program_id(1) == 0)
    def _():
        buf_ref[0:SUBLANES, :] = jnp.zeros((SUBLANES, c_dim), _F32)
        hc_ref[...] = jnp.zeros_like(hc_ref)

    h = _rms(x_ref[0], nm_ref[...]).astype(_BF)
    lx = _dot(h, wlx_ref[...])
    buf_ref[SUBLANES:SUBLANES + ts, :] = lx
    ext = buf_ref[...]
    cw = cw_ref[...]
    xc = cb_ref[...] + cw[3:4] * lx
    for j in range(CONV_WIDTH - 1):
        d = CONV_WIDTH - 1 - j
        xc = xc + cw[j:j + 1] * pltpu.roll(ext, d, 0)[SUBLANES:, :]
    buf_ref[0:SUBLANES, :] = lx[ts - SUBLANES:, :]

    xcb = xc.astype(_BF)
    r = jax.nn.sigmoid(_dot(xcb, wa_ref[...]) + ba_ref[...])
    ig = jax.nn.sigmoid(_dot(xcb, wi_ref[...]) + bi_ref[...])
    lam = lam_ref[...]
    softplus_neg_lam = jnp.maximum(-lam, 0.0) + jnp.log1p(jnp.exp(-jnp.abs(lam)))
    log_a = (-LRU_C) * r * softplus_neg_lam
    a = jnp.exp(log_a)
    y = jnp.tanh(-log_a) * (a * a + 1.0)
    u = jnp.where(y > 0.0, y * lax.rsqrt(y), 0.0) * (ig * xc)

    a = a.reshape(ts // SUBLANES, SUBLANES, c_dim)
    u = u.reshape(ts // SUBLANES, SUBLANES, c_dim)
    rid = lax.broadcasted_iota(jnp.int32, a.shape, 1)
    for d in (1, 2, 4):
        ok = rid >= d
        a_s = pltpu.roll(a, d, 1)
        u_s = pltpu.roll(u, d, 1)
        u = jnp.where(ok, a * u_s + u, u)
        a = jnp.where(ok, a * a_s, a)
    a_ref[...] = a.reshape(ts, c_dim)
    b_ref[...] = u.reshape(ts, c_dim)

    def step(g, hp):
        r0 = pl.multiple_of(g * SUBLANES, SUBLANES)
        hcur = a_ref[pl.ds(r0, SUBLANES), :] * hp + b_ref[pl.ds(r0, SUBLANES), :]
        b_ref[pl.ds(r0, SUBLANES), :] = hcur
        return jnp.broadcast_to(hcur[SUBLANES - 1:SUBLANES, :], (SUBLANES, c_dim))

    hc_ref[...] = lax.fori_loop(0, ts // SUBLANES, step, hc_ref[...])
    gate = _gelu(_dot(h, wlg_ref[...])) * jax.nn.sigmoid(_dot(h, wg0_ref[...]))
    out_ref[0] = (b_ref[...] * gate).astype(_BF)


def _lru_branch(x, nm, wlx, wlg, wg0, cw, cb, wa, ba, wi, bi, lam, ts):
    b, s, d = x.shape
    c = wlx.shape[1]
    row = lambda v: v.reshape(1, -1)
    args = (x, row(nm), wlx, wlg, wg0, cw, row(cb), wa, row(ba), wi, row(bi), row(lam))
    in_specs = [pl.BlockSpec((1, ts, d), lambda i, j: (i, j, 0))]
    in_specs += [_full_spec(a.shape) for a in args[1:]]
    return pl.pallas_call(
        _lru_body,
        grid=(b, s // ts),
        in_specs=in_specs,
        out_specs=pl.BlockSpec((1, ts, c), lambda i, j: (i, j, 0)),
        out_shape=jax.ShapeDtypeStruct((b, s, c), _BF),
        scratch_shapes=[pltpu.VMEM((ts + SUBLANES, c), _F32), pltpu.VMEM((SUBLANES, c), _F32),
                        pltpu.VMEM((ts, c), _F32), pltpu.VMEM((ts, c), _F32)],
        compiler_params=_cparams("arbitrary", "arbitrary"),
        name="lru",
    )(*args)


def _rope128(x, cos, sin):
    return x * cos + pltpu.roll(x, LANES // 2, 1) * sin


def _rope64(x, cos, sin, first_half):
    rot = jnp.where(first_half, pltpu.roll(x, LANES - IDX_DIM // 2, 1), pltpu.roll(x, IDX_DIM // 2, 1))
    return x * cos + rot * sin


def _dsa_proj_body(x_ref, nm_ref, wq_ref, wk_ref, wv_ref, wqi_ref, wki_ref, wwi_ref, wg1_ref,
                   qn_ref, kn_ref, ikn_ref, c128_ref, s128_ref, c64_ref, s64_ref,
                   qT_ref, k_ref, vT_ref, qiT_ref, ki_ref, wT_ref, g1_ref):
    ts = x_ref.shape[1]
    h = _rms(x_ref[0], nm_ref[...]).astype(_BF)
    c128, s128 = c128_ref[...], s128_ref[...]
    c64, s64 = c64_ref[...], s64_ref[...]
    first_half = (lax.broadcasted_iota(jnp.int32, (ts, LANES), 1) & (IDX_DIM - 1)) < IDX_DIM // 2

    def head_norm_rope(t, g, scale):
        outs = []
        for i in range(t.shape[1] // LANES):
            th = _rms(t[:, i * LANES:(i + 1) * LANES], g)
            outs.append(_rope128(th, c128, s128) * scale)
        return jnp.concatenate(outs, axis=1)

    q = head_norm_rope(_dot(h, wq_ref[...]), qn_ref[...], float(LANES) ** -0.5)
    qT_ref[0] = q.T.astype(_BF)
    k_ref[0] = head_norm_rope(_dot(h, wk_ref[...]), kn_ref[...], 1.0).astype(_BF)
    vT = _dot(h, wv_ref[...]).T.astype(_BF)
    for g in range(ATT_KV_HEADS):
        vT_ref[0, 0, g, :LANES, :] = vT[g * LANES:(g + 1) * LANES, :]
        vT_ref[0, 0, g, LANES:, :] = jnp.ones((ONES_ROWS, ts), _BF)

    qi = _dot(h, wqi_ref[...])
    qi = jnp.concatenate(
        [_rope64(qi[:, i * LANES:(i + 1) * LANES], c64, s64, first_half) for i in range(qi.shape[1] // LANES)],
        axis=1) * (float(IDX_DIM) ** -0.5)
    qiT_ref[0] = qi.T.astype(_BF)
    ki = _rope64(_rms(_dot(h, wki_ref[...]), ikn_ref[...]), c64, s64, first_half)
    ki_ref[0] = ki[:, :IDX_DIM].astype(_BF)
    wi = _dot(h, wwi_ref[...]) * (float(IDX_HEADS) ** -0.5)
    wT_ref[0] = wi.T[:IDX_HEADS, :]
    g1_ref[0] = jax.nn.sigmoid(_dot(h, wg1_ref[...])).astype(_BF)


def _dsa_proj(x, nm, wq, wk, wv, wqi, wki2, wwi, wg1, qn, kn, ikn2, c128, s128, c64, s64, ts):
    b, s, d = x.shape
    row = lambda v: v.reshape(1, -1)
    nq, nk, nv, nqi = wq.shape[1], wk.shape[1], wv.shape[1], wqi.shape[1]
    args = (x, row(nm), wq, wk, wv, wqi, wki2, wwi, wg1, row(qn), row(kn), row(ikn2), c128, s128, c64, s64)
    tile = lambda n: pl.BlockSpec((1, ts, n), lambda i, j: (i, j, 0))
    tile_t = lambda n: pl.BlockSpec((1, n, ts), lambda i, j: (i, 0, j))
    tab = pl.BlockSpec((ts, LANES), lambda i, j: (j, 0))
    in_specs = [tile(d)] + [_full_spec(a.shape) for a in args[1:12]] + [tab] * 4
    out_shape = (
        jax.ShapeDtypeStruct((b, nq, s), _BF),
        jax.ShapeDtypeStruct((b, s, nk), _BF),
        jax.ShapeDtypeStruct((b, s // ts, ATT_KV_HEADS, LANES + ONES_ROWS, ts), _BF),
        jax.ShapeDtypeStruct((b, nqi, s), _BF),
        jax.ShapeDtypeStruct((b, s, IDX_DIM), _BF),
        jax.ShapeDtypeStruct((b, IDX_HEADS, s), _F32),
        jax.ShapeDtypeStruct((b, s, d), _BF),
    )
    out_specs = (tile_t(nq), tile(nk),
                 pl.BlockSpec((1, 1, ATT_KV_HEADS, LANES + ONES_ROWS, ts), lambda i, j: (i, j, 0, 0, 0)), tile_t(nqi),
                 tile(IDX_DIM), tile_t(IDX_HEADS), tile(d))
    return pl.pallas_call(
        _dsa_proj_body, grid=(b, s // ts), in_specs=in_specs, out_specs=out_specs, out_shape=out_shape,
        compiler_params=_cparams("arbitrary", "arbitrary"), name="dsa_proj",
    )(*args)


def _key_to_float(u):
    key = u ^ jnp.int32(INT_MIN)
    bits = key ^ ((key >> 31) & jnp.int32(0x7FFFFFFF))
    return jnp.where((u >> 23) == 0, -jnp.inf, lax.bitcast_convert_type(bits, _F32))


def _high_half(x):
    bits = lax.bitcast_convert_type(x, jnp.int32) & jnp.int32(-65536)
    return lax.bitcast_convert_type(bits, _F32).astype(_BF)


def _dsa_body(qT_ref, qiT_ref, wT_ref, k_ref, vT_ref, ki_ref, g1_ref, out_ref,
              sc_ref, sh_ref, m_ref, l_ref, acc_ref, bias_ref, s_ref, p_ref, *, topk, tk):
    tq = qT_ref.shape[2]
    dh = LANES
    q0 = pl.program_id(1) * tq
    n_chunks = (q0 + tq + tk - 1) // tk

    qicat = jnp.concatenate([qiT_ref[0, h * IDX_DIM:(h + 1) * IDX_DIM, :] for h in range(IDX_HEADS)], axis=1)
    w = wT_ref[0]
    qpos = q0 + lax.broadcasted_iota(jnp.int32, (tk, tq), 1)
    krow = lax.broadcasted_iota(jnp.int32, (tk, tq), 0)

    def score_chunk(c, carry):
        r0 = pl.multiple_of(c * tk, tk)
        lg = _dot(ki_ref[0, pl.ds(r0, tk), :], qicat)
        s = jnp.zeros((tk, tq), _F32)
        for h in range(IDX_HEADS):
            s = s + jnp.maximum(lg[:, h * tq:(h + 1) * tq], 0.0) * w[h:h + 1, :]
        s = jnp.where(krow + r0 <= qpos, s, -jnp.inf)
        sc_ref[pl.ds(r0, tk), :] = s
        sh_ref[pl.ds(r0, tk), :] = _high_half(s)
        return carry

    lax.fori_loop(0, n_chunks, score_chunk, 0)

    def make_count(nc, ref, rows, dtype):
        def count_ge(cand):
            one, zero = jnp.ones((), dtype), jnp.zeros((), dtype)
            parts = [jnp.zeros((rows, tq), dtype) for _ in range(4)]
            n = 0
            for c in range(nc):
                hit = jnp.where(ref[c * tk:(c + 1) * tk, :] >= cand, one, zero)
                for r in range(tk // rows):
                    parts[n % 4] = parts[n % 4] + hit[r * rows:(r + 1) * rows, :]
                    n += 1
            parts = [p.astype(_F32) for p in parts]
            return jnp.sum((parts[0] + parts[1]) + (parts[2] + parts[3]), axis=0, keepdims=True)
        return count_ge

    max_chunks = sc_ref.shape[0] // tk
    assert max_chunks * tk // (2 * SUBLANES) <= 4 * 64
    coarse = [make_count(nc, sh_ref, 2 * SUBLANES, _BF) for nc in range(1, max_chunks + 1)]
    fine = [make_count(nc, sc_ref, SUBLANES, _F32) for nc in range(1, max_chunks + 1)]

    def bisect_high(i, u):
        trial = u | lax.shift_left(jnp.int32(1), 31 - i)
        cnt = lax.switch(n_chunks - 1, coarse, _high_half(_key_to_float(trial)))
        return jnp.where(cnt >= float(topk), trial, u)

    def bisect_low(i, u):
        trial = u | lax.shift_left(jnp.int32(1), 31 - i)
        cnt = lax.switch(n_chunks - 1, fine, _key_to_float(trial))
        return jnp.where(cnt >= float(topk), trial, u)

    u = lax.fori_loop(0, 16, bisect_high, jnp.zeros((1, tq), jnp.int32))
    u = lax.fori_loop(16, 32, bisect_low, u)
    tau = jnp.maximum(_key_to_float(u), -FLT_MAX)

    m_ref[...] = jnp.full(m_ref.shape, MASK_VALUE, _F32)
    l_ref[...] = jnp.zeros_like(l_ref)
    acc_ref[...] = jnp.zeros_like(acc_ref)

    sub = 8 * SUBLANES
    groups = [slice(r * SUBLANES, (r + 1) * SUBLANES) for r in range(sub // SUBLANES)]

    def att_chunk(c, carry):
        r0 = pl.multiple_of(c * tk, tk)
        bias_ref[...] = jnp.where(sc_ref[pl.ds(r0, tk), :] >= tau, 0.0, MASK_VALUE)

        def logits(h):
            g = h // ATT_GROUP
            s_ref[h] = _dot(k_ref[0, pl.ds(r0, tk), g * dh:(g + 1) * dh], qT_ref[0, h * dh:(h + 1) * dh, :])

        logits(0)
        for h in range(ATT_HEADS):
            if h + 1 < ATT_HEADS:
                logits(h + 1)
            m_old = m_ref[h:h + 1, :]
            mx = jnp.broadcast_to(m_old, (SUBLANES, tq))
            for i in range(tk // sub):
                rows = slice(i * sub, (i + 1) * sub)
                sb = s_ref[h, rows, :] + bias_ref[rows, :]
                s_ref[h, rows, :] = sb
                for gr in groups:
                    mx = jnp.maximum(mx, sb[gr, :])
            m_new = jnp.max(mx, axis=0, keepdims=True)
            for i in range(tk // sub):
                rows = slice(i * sub, (i + 1) * sub)
                p_ref[h, rows, :] = jnp.exp((s_ref[h, rows, :] - m_new).astype(_BF))
            alpha = jnp.exp(m_old - m_new)
            pv = _dot(vT_ref[0, c, h // ATT_GROUP], p_ref[h])
            l_ref[h:h + 1, :] = alpha * l_ref[h:h + 1, :] + pv[dh:dh + 1, :]
            acc_ref[h * dh:(h + 1) * dh, :] = alpha * acc_ref[h * dh:(h + 1) * dh, :] + pv[:dh, :]
            m_ref[h:h + 1, :] = m_new
        return carry

    lax.fori_loop(0, n_chunks, att_chunk, 0)
    outs = []
    for h in range(ATT_HEADS):
        o = acc_ref[h * dh:(h + 1) * dh, :] / l_ref[h:h + 1, :]
        outs.append(o.T)
    out_ref[0] = (jnp.concatenate(outs, axis=1) * g1_ref[0].astype(_F32)).astype(_BF)


def _dsa(qT, qiT, wT, k, vT, ki, g1, topk, tq, tk):
    b, nq, s = qT.shape
    d = g1.shape[2]
    in_specs = [
        pl.BlockSpec((1, nq, tq), lambda i, j: (i, 0, j)),
        pl.BlockSpec((1, qiT.shape[1], tq), lambda i, j: (i, 0, j)),
        pl.BlockSpec((1, IDX_HEADS, tq), lambda i, j: (i, 0, j)),
        pl.BlockSpec((1, s, k.shape[2]), lambda i, j: (i, 0, 0)),
        pl.BlockSpec((1,) + vT.shape[1:], lambda i, j: (i, 0, 0, 0, 0)),
        pl.BlockSpec((1, s, IDX_DIM), lambda i, j: (i, 0, 0)),
        pl.BlockSpec((1, tq, d), lambda i, j: (i, j, 0)),
    ]
    return pl.pallas_call(
        functools.partial(_dsa_body, topk=topk, tk=tk),
        grid=(b, s // tq), in_specs=in_specs,
        out_specs=pl.BlockSpec((1, tq, d), lambda i, j: (i, j, 0)),
        out_shape=jax.ShapeDtypeStruct((b, s, d), _BF),
        scratch_shapes=[pltpu.VMEM((s, tq), _F32), pltpu.VMEM((s, tq), _BF), pltpu.VMEM((ATT_HEADS, tq), _F32),
                        pltpu.VMEM((ATT_HEADS, tq), _F32), pltpu.VMEM((nq, tq), _F32),
                        pltpu.VMEM((tk, tq), _F32), pltpu.VMEM((ATT_HEADS, tk, tq), _F32),
                        pltpu.VMEM((ATT_HEADS, tk, tq), _BF)],
        compiler_params=_cparams("arbitrary", "arbitrary"), name="dsa",
    )(qT, qiT, wT, k, vT, ki, g1)


def _mem_kv_body(mem_ref, mn_ref, wkv_ref, mkn_ref, mkT_ref, mv_ref):
    d = mem_ref.shape[2]
    dm = d // MEM_HEADS
    m = _rms(mem_ref[0], mn_ref[...]).astype(_BF)
    kv = _dot(m, wkv_ref[...])
    mk = jnp.concatenate([_rms(kv[:, i * dm:(i + 1) * dm], mkn_ref[...]) for i in range(MEM_HEADS)], axis=1)
    mkT_ref[0] = mk.T.astype(_BF)
    mv_ref[0] = kv[:, d:].astype(_BF)


def _mem_kv(mem, mn, wkv, mkn):
    b, m, d = mem.shape
    row = lambda v: v.reshape(1, -1)
    return pl.pallas_call(
        _mem_kv_body, grid=(b,),
        in_specs=[pl.BlockSpec((1, m, d), lambda i: (i, 0, 0)), _full_spec((1, d)), _full_spec(wkv.shape),
                  _full_spec((1, d // MEM_HEADS))],
        out_specs=(pl.BlockSpec((1, d, m), lambda i: (i, 0, 0)), pl.BlockSpec((1, m, d), lambda i: (i, 0, 0))),
        out_shape=(jax.ShapeDtypeStruct((b, d, m), _BF), jax.ShapeDtypeStruct((b, m, d), _BF)),
        compiler_params=_cparams("arbitrary"), name="mem_kv",
    )(mem, row(mn), wkv, row(mkn))


def _mem_body(x_ref, nm_ref, wmq_ref, wg2_ref, mqn_ref, mkT_ref, mv_ref, out_ref):
    d = x_ref.shape[2]
    dm = d // MEM_HEADS
    h = _rms(x_ref[0], nm_ref[...]).astype(_BF)
    mq = _dot(h, wmq_ref[...])
    outs = []
    for i in range(MEM_HEADS):
        qh = (_rms(mq[:, i * dm:(i + 1) * dm], mqn_ref[...]) * (float(dm) ** -0.5)).astype(_BF)
        s = _dot(qh, mkT_ref[0, i * dm:(i + 1) * dm, :])
        p = jnp.exp(s - jnp.max(s, axis=-1, keepdims=True))
        o = _dot(p.astype(_BF), mv_ref[0, :, i * dm:(i + 1) * dm])
        outs.append(o / jnp.sum(p, axis=-1, keepdims=True))
    out_ref[0] = (jnp.concatenate(outs, axis=1) * jax.nn.sigmoid(_dot(h, wg2_ref[...]))).astype(_BF)


def _mem_branch(x, nm, wmq, wg2, mqn, mkT, mv, ts):
    b, s, d = x.shape
    m = mv.shape[1]
    row = lambda v: v.reshape(1, -1)
    return pl.pallas_call(
        _mem_body, grid=(b, s // ts),
        in_specs=[pl.BlockSpec((1, ts, d), lambda i, j: (i, j, 0)), _full_spec((1, d)), _full_spec(wmq.shape),
                  _full_spec(wg2.shape), _full_spec((1, d // MEM_HEADS)),
                  pl.BlockSpec((1, d, m), lambda i, j: (i, 0, 0)), pl.BlockSpec((1, m, d), lambda i, j: (i, 0, 0))],
        out_specs=pl.BlockSpec((1, ts, d), lambda i, j: (i, j, 0)),
        out_shape=jax.ShapeDtypeStruct((b, s, d), _BF),
        compiler_params=_cparams("arbitrary", "arbitrary"), name="mem",
    )(x, row(nm), wmq, wg2, row(mqn), mkT, mv)


def _mid_body(x_ref, a_ref, b_ref, c_ref, wo_ref, nf_ref, wpqT_ref, sk_ref, x1_ref, h2T_ref, scT_ref):
    merged = (a_ref[...].astype(_F32) + b_ref[...].astype(_F32) + c_ref[...].astype(_F32)).astype(_BF)
    x1 = x_ref[...] + _dot(merged, wo_ref[...])
    x1_ref[...] = x1
    h2T = _rms(x1, nf_ref[...]).T.astype(_BF)
    h2T_ref[...] = h2T
    pqT = _dot(wpqT_ref[...], h2T).astype(_BF)
    for c in range(sk_ref.shape[0]):
        scT_ref[c * PEER_KEYS:(c + 1) * PEER_KEYS, :] = _dot(sk_ref[c], pqT[c * LANES:(c + 1) * LANES, :])


def _mid(x2, ma, mb, mc, wo, nf, wpqT, sk, ts):
    t, d = x2.shape
    nsc = sk.shape[0] * PEER_KEYS
    tile = pl.BlockSpec((ts, d), lambda i: (i, 0))
    return pl.pallas_call(
        _mid_body, grid=(t // ts,),
        in_specs=[tile, tile, tile, tile, _full_spec(wo.shape), _full_spec((1, d)), _full_spec(wpqT.shape),
                  _full_spec(sk.shape)],
        out_specs=(tile, pl.BlockSpec((d, ts), lambda i: (0, i)), pl.BlockSpec((nsc, ts), lambda i: (0, i))),
        out_shape=(jax.ShapeDtypeStruct((t, d), _F32), jax.ShapeDtypeStruct((d, t), _BF),
                   jax.ShapeDtypeStruct((nsc, t), _F32)),
        compiler_params=_cparams("arbitrary"), name="mid",
    )(x2, ma, mb, mc, wo, nf.reshape(1, -1), wpqT, sk)


def _top_values(s, n):
    groups = [s[r * SUBLANES:(r + 1) * SUBLANES, :] for r in range(s.shape[0] // SUBLANES)]
    size = pl.next_power_of_2(len(groups))
    groups += [jnp.full_like(groups[0], -jnp.inf)] * (size - len(groups))
    k = 2
    while k <= size:
        j = k // 2
        while j >= 1:
            for i in range(size):
                l = i ^ j
                if l > i:
                    hi, lo = jnp.maximum(groups[i], groups[l]), jnp.minimum(groups[i], groups[l])
                    groups[i], groups[l] = (hi, lo) if (i & k) == 0 else (lo, hi)
            j //= 2
        k *= 2
    vals = []
    for it in range(n):
        m = jnp.max(groups[0], axis=0, keepdims=True)
        vals.append(m)
        remaining = n - 1 - it
        head = groups[0] == m
        for d in range(min(remaining, size)):
            below = groups[d + 1] if d + 1 < size else jnp.full_like(m, -jnp.inf)
            groups[d] = jnp.where(head, below, groups[d])
    return vals


def _route_body(scT_ref, tau_ref, c_ref, top2_ref):
    def head(h, carry):
        r1 = pl.multiple_of(h * 2 * PEER_KEYS, PEER_KEYS)
        r2 = pl.multiple_of(h * 2 * PEER_KEYS + PEER_KEYS, PEER_KEYS)
        n = PEER_TOPK + 1
        v1 = _top_values(scT_ref[pl.ds(r1, PEER_KEYS), :], n)
        v2 = _top_values(scT_ref[pl.ds(r2, PEER_KEYS), :], n)
        edge = SUBLANES
        assert 2 * (edge + 1) > n
        pad = [jnp.full_like(v1[0], -jnp.inf)] * (-n % SUBLANES)
        col1 = jnp.concatenate(v1 + pad, axis=0)
        col2 = jnp.concatenate(v2 + pad, axis=0)
        cand = jnp.concatenate([v1[a] + col2[:edge] for a in range(edge)]
                               + [v1[0] + col2[edge:], v2[0] + col1[edge:]], axis=0)
        best = _top_values(cand, n)
        tau = best[PEER_TOPK - 1]
        top = v1[0] + v2[0]
        z = jnp.sum(jnp.where(cand >= tau, jnp.exp(cand - top), 0.0), axis=0, keepdims=True)
        tau_ref[pl.ds(h, 1), :] = 0.5 * (tau + best[PEER_TOPK])
        c_ref[pl.ds(h, 1), :] = top + jnp.log(z)
        top2_ref[pl.ds(pl.multiple_of(h * TOP_ROWS, SUBLANES), TOP_ROWS), :] = col2
        return carry

    lax.fori_loop(0, PEER_HEADS, head, 0, unroll=2)


def _route(scT, te):
    nsc, t = scT.shape
    spec = pl.BlockSpec((PEER_HEADS, te), lambda i: (0, i))
    return pl.pallas_call(
        _route_body, grid=(t // te,),
        in_specs=[pl.BlockSpec((nsc, te), lambda i: (0, i))],
        out_specs=(spec, spec, pl.BlockSpec((PEER_HEADS * TOP_ROWS, te), lambda i: (0, i))),
        out_shape=(jax.ShapeDtypeStruct((PEER_HEADS, t), _F32),) * 2
        + (jax.ShapeDtypeStruct((PEER_HEADS * TOP_ROWS, t), _F32),),
        compiler_params=_cparams("arbitrary"), name="route",
    )(scT)


def _experts_body(h2T_ref, scT_ref, tau_ref, c_ref, top2_ref, u_ref, vT_ref, x1_ref, out_ref,
                  acc_ref, e2_ref, rank_ref, cnt_ref, scale_ref, act_ref, g_ref, rows_ref, *, rows_per_step):
    e = pl.program_id(1)
    nk = PEER_KEYS
    ts = act_ref.shape[1]

    @pl.when(e == 0)
    def _():
        acc_ref[...] = jnp.zeros_like(acc_ref)
        for h in range(PEER_HEADS):
            best = [top2_ref[h * TOP_ROWS + b:h * TOP_ROWS + b + 1, :] for b in range(PEER_TOPK + 1)]
            s1 = scT_ref[2 * h * nk:(2 * h + 1) * nk, :]
            s2 = scT_ref[(2 * h + 1) * nk:(2 * h + 2) * nk, :]
            e2_ref[h * nk:(h + 1) * nk, :] = jnp.exp(s2 - best[0]).astype(_BF)
            rank = jnp.zeros((nk, ts), _F32)
            for b in range(PEER_TOPK):
                rank = jnp.where(best[b] > s2, float(b + 1), rank)
            rank_ref[h * nk:(h + 1) * nk, :] = rank.astype(_BF)
            thr = tau_ref[h:h + 1, :] - s1
            cnt = jnp.zeros((nk, ts), _F32)
            for b in range(PEER_TOPK + 1):
                cnt = jnp.where(best[b] >= thr, float(b + 1), cnt)
            cnt_ref[h * nk:(h + 1) * nk, :] = cnt
            scale_ref[h * nk:(h + 1) * nk, :] = 0.5 * jnp.exp(s1 + best[0] - c_ref[h:h + 1, :])

    n_rows = rows_per_step * PEER_HEADS
    for i in range(rows_per_step):
        for h in range(PEER_HEADS):
            src = h * nk + e * rows_per_step + i
            r = i * PEER_HEADS + h
            rows_ref[r:r + 1, :] = cnt_ref[pl.ds(src, 1), :]
            rows_ref[n_rows + r:n_rows + r + 1, :] = scale_ref[pl.ds(src, 1), :]

    def first_matmul(i):
        act_ref[i * nk:(i + 1) * nk, :] = _dot(u_ref[i * nk:(i + 1) * nk, :], h2T_ref[...])

    pair = 2 * nk
    zero = jnp.zeros((), _BF)
    first_matmul(0)
    for i in range(rows_per_step):
        if i + 1 < rows_per_step:
            first_matmul(i + 1)
        wgt = jnp.zeros((nk, ts), _BF)
        for h in range(PEER_HEADS):
            r = i * PEER_HEADS + h
            cnt = rows_ref[r:r + 1, :].astype(_BF)
            scale = rows_ref[n_rows + r:n_rows + r + 1, :].astype(_BF)
            hit = jnp.where(rank_ref[h * nk:(h + 1) * nk, :] < cnt, e2_ref[h * nk:(h + 1) * nk, :], zero)
            wgt = wgt + hit * scale
        x = act_ref[i * nk:(i + 1) * nk, :].astype(_BF)
        t = jnp.tanh(x * (jnp.asarray(GELU_C0, _BF) + jnp.asarray(GELU_C1, _BF) * (x * x)))
        g_ref[i * nk:(i + 1) * nk, :] = (x + x * t) * wgt
        if i % 2 == 1:
            p = i // 2
            acc_ref[...] += _dot(vT_ref[0, :, p * pair:(p + 1) * pair], g_ref[p * pair:(p + 1) * pair, :])

    @pl.when(e == pl.num_programs(1) - 1)
    def _():
        out_ref[...] = x1_ref[...] + acc_ref[...].T


def _experts(h2T, scT, tau, cn, top2, u, vT, x1, ts, rows_per_step):
    d, t = h2T.shape
    nsc = scT.shape[0]
    ne = u.shape[0]
    eb = rows_per_step * PEER_KEYS
    nb = ne // eb
    nhk = PEER_HEADS * PEER_KEYS
    col = lambda n: pl.BlockSpec((n, ts), lambda i, e: (0, i))
    tile = pl.BlockSpec((ts, d), lambda i, e: (i, 0))
    return pl.pallas_call(
        functools.partial(_experts_body, rows_per_step=rows_per_step),
        grid=(t // ts, nb),
        in_specs=[col(d), col(nsc), col(PEER_HEADS), col(PEER_HEADS), col(top2.shape[0]),
                  pl.BlockSpec((eb, d), lambda i, e: (e, 0)), pl.BlockSpec((1, d, eb), lambda i, e: (e, 0, 0)), tile],
        out_specs=tile,
        out_shape=jax.ShapeDtypeStruct((t, d), _F32),
        scratch_shapes=[pltpu.VMEM((d, ts), _F32), pltpu.VMEM((nhk, ts), _BF), pltpu.VMEM((nhk, ts), _BF),
                        pltpu.VMEM((nhk, ts), _F32), pltpu.VMEM((nhk, ts), _F32),
                        pltpu.VMEM((eb, ts), _F32), pltpu.VMEM((eb, ts), _BF),
                        pltpu.VMEM((2 * rows_per_step * PEER_HEADS, ts), _F32)],
        compiler_params=_cparams("arbitrary", "arbitrary"), name="experts",
    )(h2T, scT, tau, cn, top2, u, vT, x1)


def _block_diag(w):
    n, c, d = w.shape
    return jnp.einsum("ncd,nm->ncmd", w, jnp.eye(n, dtype=w.dtype)).reshape(n * c, n * d)


def _rope_tables(s, dim, reps):
    half = dim // 2
    freq = ROPE_THETA ** (-jnp.arange(half, dtype=_F32) / half)
    ang = jnp.arange(s, dtype=_F32)[:, None] * freq[None, :]
    cos = jnp.tile(jnp.concatenate([jnp.cos(ang), jnp.cos(ang)], axis=1), (1, reps))
    sin = jnp.tile(jnp.concatenate([-jnp.sin(ang), jnp.sin(ang)], axis=1), (1, reps))
    return cos, sin


def _tile(n, pref):
    return pref if n % pref == 0 else n


def _layer(x, mem, p, l):
    b, s, d = x.shape
    bf = lambda a: a.astype(_BF)
    w_in = p["w_in"][l]
    n_q, n_kv = ATT_HEADS * LANES, ATT_KV_HEADS * LANES
    n_qi = IDX_HEADS * IDX_DIM
    sizes = (d, d, n_q, n_kv, n_kv, n_qi, IDX_DIM, IDX_HEADS, d, 3 * d)
    offs = [0]
    for n in sizes:
        offs.append(offs[-1] + n)
    wlx, wlg, wq, wk, wv, wqi, wki, wwi, wmq, wgates = (w_in[:, offs[i]:offs[i + 1]] for i in range(len(sizes)))
    wki2 = jnp.concatenate([wki, wki], axis=1)
    wwi_p = jnp.pad(wwi, ((0, 0), (0, LANES - IDX_HEADS)))
    wg0, wg1, wg2 = wgates[:, :d], wgates[:, d:2 * d], wgates[:, 2 * d:]
    nm = p["norm_mix"][l]

    ts_a = _tile(s, 256)
    m_lru = _lru_branch(x, nm, bf(wlx), bf(wlg), bf(wg0), p["conv_w"][l], p["conv_b"][l],
                        bf(_block_diag(p["lru_wa"][l])), p["lru_ba"][l].reshape(-1),
                        bf(_block_diag(p["lru_wi"][l])), p["lru_bi"][l].reshape(-1), p["lru_lambda"][l], ts_a)

    tk = _tile(s, 512)
    c128, s128 = _rope_tables(s, LANES, 1)
    c64, s64 = _rope_tables(s, IDX_DIM, LANES // IDX_DIM)
    ikn2 = jnp.concatenate([p["idx_k_norm"][l]] * 2)
    qT, k, vT, qiT, ki, wT, g1 = _dsa_proj(x, nm, bf(wq), bf(wk), bf(wv), bf(wqi), bf(wki2), bf(wwi_p), bf(wg1),
                                           p["q_norm"][l], p["k_norm"][l], ikn2, c128, s128, c64, s64, tk)
    m_att = _dsa(qT, qiT, wT, k, vT, ki, g1, min(TOPK_MAX, s // 4), _tile(s, 256), tk)

    mkT, mv = _mem_kv(mem, p["mem_norm"][l], bf(p["w_mem_kv"][l]), p["mem_k_norm"][l])
    m_mem = _mem_branch(x, nm, bf(wmq), bf(wg2), p["mem_q_norm"][l], mkT, mv, _tile(s, 512))

    t = b * s
    flat = lambda a: a.reshape(t, d)
    sk = bf(p["peer_subkeys"][l].reshape(2 * PEER_HEADS, PEER_KEYS, -1))
    x1, h2T, scT = _mid(flat(x), flat(m_lru), flat(m_att), flat(m_mem), bf(p["w_out"][l]), p["norm_ffn"][l],
                        bf(p["peer_wq"][l].T), sk, _tile(t, 512))
    tau, cn, top2 = _route(scT, _tile(t, 256))
    rows_per_step = 8
    vT = bf(p["peer_v"][l]).reshape(-1, rows_per_step * PEER_KEYS, d).transpose(0, 2, 1)
    out = _experts(h2T, scT, tau, cn, top2, bf(p["peer_u"][l]), vT, x1, _tile(t, 512), rows_per_step)
    return out.reshape(b, s, d)


def kernel(x, mem, norm_mix, w_in, conv_w, conv_b, lru_wa, lru_ba, lru_wi, lru_bi, lru_lambda, q_norm, k_norm,
           idx_k_norm, mem_norm, w_mem_kv, mem_q_norm, mem_k_norm, w_out, norm_ffn, peer_wq, peer_subkeys,
           peer_u, peer_v):
    p = dict(norm_mix=norm_mix, w_in=w_in, conv_w=conv_w, conv_b=conv_b, lru_wa=lru_wa, lru_ba=lru_ba,
             lru_wi=lru_wi, lru_bi=lru_bi, lru_lambda=lru_lambda, q_norm=q_norm, k_norm=k_norm,
             idx_k_norm=idx_k_norm, mem_norm=mem_norm, w_mem_kv=w_mem_kv, mem_q_norm=mem_q_norm,
             mem_k_norm=mem_k_norm, w_out=w_out, norm_ffn=norm_ffn, peer_wq=peer_wq, peer_subkeys=peer_subkeys,
             peer_u=peer_u, peer_v=peer_v)
    for l in range(norm_mix.shape[0]):
        x = _layer(x, mem, p, l)
    return x
```

```python
import functools

import jax
import jax.numpy as jnp
from jax import lax
from jax.experimental import pallas as pl
from jax.experimental.pallas import tpu as pltpu

_F32 = jnp.float32
_BF = jnp.bfloat16

EPS = 1e-6
ROPE_THETA = 10000.0
LRU_C = 8.0
LRU_BLOCKS = 16
CONV_WIDTH = 4
ATT_HEADS = 8
ATT_KV_HEADS = 2
ATT_GROUP = ATT_HEADS // ATT_KV_HEADS
IDX_HEADS = 8
IDX_DIM = 64
TOPK_MAX = 256
MEM_HEADS = 4
PEER_HEADS = 8
PEER_KEYS = 128
PEER_TOPK = 16
TOP_ROWS = 24

LANES = 128
SUBLANES = 8
ONES_ROWS = 2 * SUBLANES
VMEM_LIMIT_BYTES = 56 * 1024 * 1024
MASK_VALUE = -1e30
FLT_MAX = 3.4028234663852886e38
INT_MIN = -(2 ** 31)


def _cparams(*sem):
    return pltpu.CompilerParams(dimension_semantics=sem, vmem_limit_bytes=VMEM_LIMIT_BYTES)


def _rms(x, g):
    return x * lax.rsqrt(jnp.mean(x * x, axis=-1, keepdims=True) + EPS) * g


GELU_C0 = 0.7978845608028654
GELU_C1 = GELU_C0 * 0.044715


def _gelu(x):
    return 0.5 * x * (1.0 + jnp.tanh(x * (GELU_C0 + GELU_C1 * (x * x))))


def _sigmoid(x):
    return 0.5 * jnp.tanh(0.5 * x) + 0.5


def _dot(a, b):
    return jnp.dot(a, b, preferred_element_type=_F32)


def _full_spec(shape):
    n = len(shape)
    return pl.BlockSpec(shape, lambda *_: (0,) * n)


def _lru_body(x_ref, nm_ref, wlx_ref, wlg_ref, wg0_ref, cw_ref, cb_ref, wa_ref, ba_ref, wi_ref,
              bi_ref, lam_ref, out_ref, buf_ref, hc_ref, a_ref, b_ref):
    ts = x_ref.shape[1]
    c_dim = out_ref.shape[2]

    @pl.when(pl.program_id(1) == 0)
    def _():
        buf_ref[0:SUBLANES, :] = jnp.zeros((SUBLANES, c_dim), _F32)
        hc_ref[...] = jnp.zeros_like(hc_ref)

    h = _rms(x_ref[0], nm_ref[...]).astype(_BF)
    lx = _dot(h, wlx_ref[...])
    buf_ref[SUBLANES:SUBLANES + ts, :] = lx
    ext = buf_ref[...]
    cw = cw_ref[...]
    xc = cb_ref[...] + cw[3:4] * lx
    for j in range(CONV_WIDTH - 1):
        d = CONV_WIDTH - 1 - j
        xc = xc + cw[j:j + 1] * pltpu.roll(ext, d, 0)[SUBLANES:, :]
    buf_ref[0:SUBLANES, :] = lx[ts - SUBLANES:, :]

    xcb = xc.astype(_BF)
    r = _sigmoid(_dot(xcb, wa_ref[...]) + ba_ref[...])
    ig = _sigmoid(_dot(xcb, wi_ref[...]) + bi_ref[...])
    lam = lam_ref[...]
    softplus_neg_lam = jnp.maximum(-lam, 0.0) + jnp.log1p(jnp.exp(-jnp.abs(lam)))
    log_a = (-LRU_C) * r * softplus_neg_lam
    a = jnp.exp(log_a)
    y = jnp.tanh(-log_a) * (a * a + 1.0)
    u = jnp.where(y > 0.0, y * lax.rsqrt(y), 0.0) * (ig * xc)

    a = a.reshape(ts // SUBLANES, SUBLANES, c_dim)
    u = u.reshape(ts // SUBLANES, SUBLANES, c_dim)
    rid = lax.broadcasted_iota(jnp.int32, a.shape, 1)
    for d in (1, 2, 4):
        ok = rid >= d
        a_s = pltpu.roll(a, d, 1)
        u_s = pltpu.roll(u, d, 1)
        u = jnp.where(ok, a * u_s + u, u)
        a = jnp.where(ok, a * a_s, a)
    a_ref[...] = a.reshape(ts, c_dim)
    b_ref[...] = u.reshape(ts, c_dim)

    def step(g, hp):
        r0 = pl.multiple_of(g * SUBLANES, SUBLANES)
        hcur = a_ref[pl.ds(r0, SUBLANES), :] * hp + b_ref[pl.ds(r0, SUBLANES), :]
        b_ref[pl.ds(r0, SUBLANES), :] = hcur
        return jnp.broadcast_to(hcur[SUBLANES - 1:SUBLANES, :], (SUBLANES, c_dim))

    hc_ref[...] = lax.fori_loop(0, ts // SUBLANES, step, hc_ref[...])
    gate = _gelu(_dot(h, wlg_ref[...])) * _sigmoid(_dot(h, wg0_ref[...]))
    out_ref[0] = (b_ref[...] * gate).astype(_BF)


def _lru_branch(x, nm, wlx, wlg, wg0, cw, cb, wa, ba, wi, bi, lam, ts):
    b, s, d = x.shape
    c = wlx.shape[1]
    row = lambda v: v.reshape(1, -1)
    args = (x, row(nm), wlx, wlg, wg0, cw, row(cb), wa, row(ba), wi, row(bi), row(lam))
    in_specs = [pl.BlockSpec((1, ts, d), lambda i, j: (i, j, 0))]
    in_specs += [_full_spec(a.shape) for a in args[1:]]
    return pl.pallas_call(
        _lru_body,
        grid=(b, s // ts),
        in_specs=in_specs,
        out_specs=pl.BlockSpec((1, ts, c), lambda i, j: (i, j, 0)),
        out_shape=jax.ShapeDtypeStruct((b, s, c), _BF),
        scratch_shapes=[pltpu.VMEM((ts + SUBLANES, c), _F32), pltpu.VMEM((SUBLANES, c), _F32),
                        pltpu.VMEM((ts, c), _F32), pltpu.VMEM((ts, c), _F32)],
        compiler_params=_cparams("arbitrary", "arbitrary"),
        name="lru",
    )(*args)


def _rope128(x, cos, sin):
    return x * cos + pltpu.roll(x, LANES // 2, 1) * sin


def _rope64(x, cos, sin, first_half):
    rot = jnp.where(first_half, pltpu.roll(x, LANES - IDX_DIM // 2, 1), pltpu.roll(x, IDX_DIM // 2, 1))
    return x * cos + rot * sin


def _dsa_proj_body(x_ref, nm_ref, wq_ref, wk_ref, wv_ref, wqi_ref, wki_ref, wwi_ref, wg1_ref,
                   qn_ref, kn_ref, ikn_ref, c128_ref, s128_ref, c64_ref, s64_ref,
                   qT_ref, k_ref, vT_ref, qiT_ref, ki_ref, wT_ref, g1_ref):
    ts = x_ref.shape[1]
    h = _rms(x_ref[0], nm_ref[...]).astype(_BF)
    c128, s128 = c128_ref[...], s128_ref[...]
    c64, s64 = c64_ref[...], s64_ref[...]
    first_half = (lax.broadcasted_iota(jnp.int32, (ts, LANES), 1) & (IDX_DIM - 1)) < IDX_DIM // 2

    def head_norm_rope(t, g, scale):
        outs = []
        for i in range(t.shape[1] // LANES):
            th = _rms(t[:, i * LANES:(i + 1) * LANES], g)
            outs.append(_rope128(th, c128, s128) * scale)
        return jnp.concatenate(outs, axis=1)

    q = head_norm_rope(_dot(h, wq_ref[...]), qn_ref[...], float(LANES) ** -0.5)
    qT_ref[0] = q.T.astype(_BF)
    k_ref[0] = head_norm_rope(_dot(h, wk_ref[...]), kn_ref[...], 1.0).astype(_BF)
    vT = _dot(h, wv_ref[...]).T.astype(_BF)
    for g in range(ATT_KV_HEADS):
        vT_ref[0, 0, g, :LANES, :] = vT[g * LANES:(g + 1) * LANES, :]
        vT_ref[0, 0, g, LANES:, :] = jnp.ones((ONES_ROWS, ts), _BF)

    qi = _dot(h, wqi_ref[...])
    qi = jnp.concatenate(
        [_rope64(qi[:, i * LANES:(i + 1) * LANES], c64, s64, first_half) for i in range(qi.shape[1] // LANES)],
        axis=1) * (float(IDX_DIM) ** -0.5)
    qiT_ref[0] = qi.T.astype(_BF)
    ki = _rope64(_rms(_dot(h, wki_ref[...]), ikn_ref[...]), c64, s64, first_half)
    ki_ref[0] = ki[:, :IDX_DIM].astype(_BF)
    wi = _dot(h, wwi_ref[...]) * (float(IDX_HEADS) ** -0.5)
    wT_ref[0] = wi.T[:IDX_HEADS, :]
    g1_ref[0] = _sigmoid(_dot(h, wg1_ref[...])).astype(_BF)


def _dsa_proj(x, nm, wq, wk, wv, wqi, wki2, wwi, wg1, qn, kn, ikn2, c128, s128, c64, s64, ts):
    b, s, d = x.shape
    row = lambda v: v.reshape(1, -1)
    nq, nk, nv, nqi = wq.shape[1], wk.shape[1], wv.shape[1], wqi.shape[1]
    args = (x, row(nm), wq, wk, wv, wqi, wki2, wwi, wg1, row(qn), row(kn), row(ikn2), c128, s128, c64, s64)
    tile = lambda n: pl.BlockSpec((1, ts, n), lambda i, j: (i, j, 0))
    tile_t = lambda n: pl.BlockSpec((1, n, ts), lambda i, j: (i, 0, j))
    tab = pl.BlockSpec((ts, LANES), lambda i, j: (j, 0))
    in_specs = [tile(d)] + [_full_spec(a.shape) for a in args[1:12]] + [tab] * 4
    out_shape = (
        jax.ShapeDtypeStruct((b, nq, s), _BF),
        jax.ShapeDtypeStruct((b, s, nk), _BF),
        jax.ShapeDtypeStruct((b, s // ts, ATT_KV_HEADS, LANES + ONES_ROWS, ts), _BF),
        jax.ShapeDtypeStruct((b, nqi, s), _BF),
        jax.ShapeDtypeStruct((b, s, IDX_DIM), _BF),
        jax.ShapeDtypeStruct((b, IDX_HEADS, s), _F32),
        jax.ShapeDtypeStruct((b, s, d), _BF),
    )
    out_specs = (tile_t(nq), tile(nk),
                 pl.BlockSpec((1, 1, ATT_KV_HEADS, LANES + ONES_ROWS, ts), lambda i, j: (i, j, 0, 0, 0)), tile_t(nqi),
                 tile(IDX_DIM), tile_t(IDX_HEADS), tile(d))
    return pl.pallas_call(
        _dsa_proj_body, grid=(b, s // ts), in_specs=in_specs, out_specs=out_specs, out_shape=out_shape,
        compiler_params=_cparams("arbitrary", "arbitrary"), name="dsa_proj",
    )(*args)


def _key_to_float(u):
    key = u ^ jnp.int32(INT_MIN)
    bits = key ^ ((key >> 31) & jnp.int32(0x7FFFFFFF))
    return jnp.where((u >> 23) == 0, -jnp.inf, lax.bitcast_convert_type(bits, _F32))


def _high_half(x):
    bits = lax.bitcast_convert_type(x, jnp.int32) & jnp.int32(-65536)
    return lax.bitcast_convert_type(bits, _F32).astype(_BF)


def _dsa_body(qT_ref, qiT_ref, wT_ref, k_ref, vT_ref, ki_ref, g1_ref, out_ref,
              sc_ref, sh_ref, m_ref, l_ref, acc_ref, bias_ref, s_ref, p_ref, *, topk, tk):
    tq = qT_ref.shape[2]
    dh = LANES
    q0 = pl.program_id(1) * tq
    n_chunks = (q0 + tq + tk - 1) // tk

    qicat = jnp.concatenate([qiT_ref[0, h * IDX_DIM:(h + 1) * IDX_DIM, :] for h in range(IDX_HEADS)], axis=1)
    w = wT_ref[0]
    qpos = q0 + lax.broadcasted_iota(jnp.int32, (tk, tq), 1)
    krow = lax.broadcasted_iota(jnp.int32, (tk, tq), 0)

    def score_chunk(c, carry):
        r0 = pl.multiple_of(c * tk, tk)
        lg = _dot(ki_ref[0, pl.ds(r0, tk), :], qicat)
        s = jnp.zeros((tk, tq), _F32)
        for h in range(IDX_HEADS):
            s = s + jnp.maximum(lg[:, h * tq:(h + 1) * tq], 0.0) * w[h:h + 1, :]
        s = jnp.where(krow + r0 <= qpos, s, -jnp.inf)
        sc_ref[pl.ds(r0, tk), :] = s
        sh_ref[pl.ds(r0, tk), :] = _high_half(s)
        return carry

    lax.fori_loop(0, n_chunks, score_chunk, 0)

    def make_count(nc, ref, rows, dtype):
        def count_ge(cand):
            one, zero = jnp.ones((), dtype), jnp.zeros((), dtype)
            parts = [jnp.zeros((rows, tq), dtype) for _ in range(4)]
            n = 0
            for c in range(nc):
                hit = jnp.where(ref[c * tk:(c + 1) * tk, :] >= cand, one, zero)
                for r in range(tk // rows):
                    parts[n % 4] = parts[n % 4] + hit[r * rows:(r + 1) * rows, :]
                    n += 1
            parts = [p.astype(_F32) for p in parts]
            return jnp.sum((parts[0] + parts[1]) + (parts[2] + parts[3]), axis=0, keepdims=True)
        return count_ge

    max_chunks = sc_ref.shape[0] // tk
    assert max_chunks * tk // (2 * SUBLANES) <= 4 * 64
    coarse = [make_count(nc, sh_ref, 2 * SUBLANES, _BF) for nc in range(1, max_chunks + 1)]
    fine = [make_count(nc, sc_ref, SUBLANES, _F32) for nc in range(1, max_chunks + 1)]

    def bisect_high(i, u):
        trial = u | lax.shift_left(jnp.int32(1), 31 - i)
        cnt = lax.switch(n_chunks - 1, coarse, _high_half(_key_to_float(trial)))
        return jnp.where(cnt >= float(topk), trial, u)

    def bisect_low(i, u):
        trial = u | lax.shift_left(jnp.int32(1), 31 - i)
        cnt = lax.switch(n_chunks - 1, fine, _key_to_float(trial))
        return jnp.where(cnt >= float(topk), trial, u)

    u = lax.fori_loop(0, 16, bisect_high, jnp.zeros((1, tq), jnp.int32))
    u = lax.fori_loop(16, 32, bisect_low, u)
    tau = jnp.maximum(_key_to_float(u), -FLT_MAX)

    m_ref[...] = jnp.full(m_ref.shape, MASK_VALUE, _F32)
    l_ref[...] = jnp.zeros_like(l_ref)
    acc_ref[...] = jnp.zeros_like(acc_ref)

    sub = 8 * SUBLANES
    groups = [slice(r * SUBLANES, (r + 1) * SUBLANES) for r in range(sub // SUBLANES)]

    def att_chunk(c, carry):
        r0 = pl.multiple_of(c * tk, tk)
        bias_ref[...] = jnp.where(sc_ref[pl.ds(r0, tk), :] >= tau, 0.0, MASK_VALUE)

        def logits(h):
            g = h // ATT_GROUP
            s_ref[h] = _dot(k_ref[0, pl.ds(r0, tk), g * dh:(g + 1) * dh], qT_ref[0, h * dh:(h + 1) * dh, :])

        logits(0)
        logits(1)
        for h in range(ATT_HEADS):
            if h + 2 < ATT_HEADS:
                logits(h + 2)
            m_old = m_ref[h:h + 1, :]
            mx = jnp.broadcast_to(m_old, (SUBLANES, tq))
            for i in range(tk // sub):
                rows = slice(i * sub, (i + 1) * sub)
                sb = s_ref[h, rows, :] + bias_ref[rows, :]
                s_ref[h, rows, :] = sb
                for gr in groups:
                    mx = jnp.maximum(mx, sb[gr, :])
            m_new = jnp.max(mx, axis=0, keepdims=True)
            for i in range(tk // sub):
                rows = slice(i * sub, (i + 1) * sub)
                p_ref[h, rows, :] = jnp.exp((s_ref[h, rows, :] - m_new).astype(_BF))
            alpha = jnp.exp(m_old - m_new)
            pv = _dot(vT_ref[0, c, h // ATT_GROUP], p_ref[h])
            l_ref[h:h + 1, :] = alpha * l_ref[h:h + 1, :] + pv[dh:dh + 1, :]
            acc_ref[h * dh:(h + 1) * dh, :] = alpha * acc_ref[h * dh:(h + 1) * dh, :] + pv[:dh, :]
            m_ref[h:h + 1, :] = m_new
        return carry

    lax.fori_loop(0, n_chunks, att_chunk, 0)
    outs = []
    for h in range(ATT_HEADS):
        o = acc_ref[h * dh:(h + 1) * dh, :] / l_ref[h:h + 1, :]
        outs.append(o.T)
    out_ref[0] = (jnp.concatenate(outs, axis=1) * g1_ref[0].astype(_F32)).astype(_BF)


def _dsa(qT, qiT, wT, k, vT, ki, g1, topk, tq, tk):
    b, nq, s = qT.shape
    d = g1.shape[2]
    in_specs = [
        pl.BlockSpec((1, nq, tq), lambda i, j: (i, 0, j)),
        pl.BlockSpec((1, qiT.shape[1], tq), lambda i, j: (i, 0, j)),
        pl.BlockSpec((1, IDX_HEADS, tq), lambda i, j: (i, 0, j)),
        pl.BlockSpec((1, s, k.shape[2]), lambda i, j: (i, 0, 0)),
        pl.BlockSpec((1,) + vT.shape[1:], lambda i, j: (i, 0, 0, 0, 0)),
        pl.BlockSpec((1, s, IDX_DIM), lambda i, j: (i, 0, 0)),
        pl.BlockSpec((1, tq, d), lambda i, j: (i, j, 0)),
    ]
    return pl.pallas_call(
        functools.partial(_dsa_body, topk=topk, tk=tk),
        grid=(b, s // tq), in_specs=in_specs,
        out_specs=pl.BlockSpec((1, tq, d), lambda i, j: (i, j, 0)),
        out_shape=jax.ShapeDtypeStruct((b, s, d), _BF),
        scratch_shapes=[pltpu.VMEM((s, tq), _F32), pltpu.VMEM((s, tq), _BF), pltpu.VMEM((ATT_HEADS, tq), _F32),
                        pltpu.VMEM((ATT_HEADS, tq), _F32), pltpu.VMEM((nq, tq), _F32),
                        pltpu.VMEM((tk, tq), _F32), pltpu.VMEM((ATT_HEADS, tk, tq), _F32),
                        pltpu.VMEM((ATT_HEADS, tk, tq), _BF)],
        compiler_params=_cparams("arbitrary", "arbitrary"), name="dsa",
    )(qT, qiT, wT, k, vT, ki, g1)


def _mem_kv_body(mem_ref, mn_ref, wkv_ref, mkn_ref, mkT_ref, mv_ref):
    d = mem_ref.shape[2]
    dm = d // MEM_HEADS
    m = _rms(mem_ref[0], mn_ref[...]).astype(_BF)
    kv = _dot(m, wkv_ref[...])
    mk = jnp.concatenate([_rms(kv[:, i * dm:(i + 1) * dm], mkn_ref[...]) for i in range(MEM_HEADS)], axis=1)
    mkT_ref[0] = mk.T.astype(_BF)
    mv_ref[0] = kv[:, d:].astype(_BF)


def _mem_kv(mem, mn, wkv, mkn):
    b, m, d = mem.shape
    row = lambda v: v.reshape(1, -1)
    return pl.pallas_call(
        _mem_kv_body, grid=(b,),
        in_specs=[pl.BlockSpec((1, m, d), lambda i: (i, 0, 0)), _full_spec((1, d)), _full_spec(wkv.shape),
                  _full_spec((1, d // MEM_HEADS))],
        out_specs=(pl.BlockSpec((1, d, m), lambda i: (i, 0, 0)), pl.BlockSpec((1, m, d), lambda i: (i, 0, 0))),
        out_shape=(jax.ShapeDtypeStruct((b, d, m), _BF), jax.ShapeDtypeStruct((b, m, d), _BF)),
        compiler_params=_cparams("arbitrary"), name="mem_kv",
    )(mem, row(mn), wkv, row(mkn))


def _mem_body(x_ref, nm_ref, wmq_ref, wg2_ref, mqn_ref, mkT_ref, mv_ref, out_ref):
    d = x_ref.shape[2]
    dm = d // MEM_HEADS
    h = _rms(x_ref[0], nm_ref[...]).astype(_BF)
    mq = _dot(h, wmq_ref[...])
    outs = []
    for i in range(MEM_HEADS):
        qh = (_rms(mq[:, i * dm:(i + 1) * dm], mqn_ref[...]) * (float(dm) ** -0.5)).astype(_BF)
        s = _dot(qh, mkT_ref[0, i * dm:(i + 1) * dm, :])
        p = jnp.exp(s - jnp.max(s, axis=-1, keepdims=True))
        o = _dot(p.astype(_BF), mv_ref[0, :, i * dm:(i + 1) * dm])
        outs.append(o / jnp.sum(p, axis=-1, keepdims=True))
    out_ref[0] = (jnp.concatenate(outs, axis=1) * _sigmoid(_dot(h, wg2_ref[...]))).astype(_BF)


def _mem_branch(x, nm, wmq, wg2, mqn, mkT, mv, ts):
    b, s, d = x.shape
    m = mv.shape[1]
    row = lambda v: v.reshape(1, -1)
    return pl.pallas_call(
        _mem_body, grid=(b, s // ts),
        in_specs=[pl.BlockSpec((1, ts, d), lambda i, j: (i, j, 0)), _full_spec((1, d)), _full_spec(wmq.shape),
                  _full_spec(wg2.shape), _full_spec((1, d // MEM_HEADS)),
                  pl.BlockSpec((1, d, m), lambda i, j: (i, 0, 0)), pl.BlockSpec((1, m, d), lambda i, j: (i, 0, 0))],
        out_specs=pl.BlockSpec((1, ts, d), lambda i, j: (i, j, 0)),
        out_shape=jax.ShapeDtypeStruct((b, s, d), _BF),
        compiler_params=_cparams("arbitrary", "arbitrary"), name="mem",
    )(x, row(nm), wmq, wg2, row(mqn), mkT, mv)


def _mid_body(x_ref, a_ref, b_ref, c_ref, wo_ref, nf_ref, wpqT_ref, sk_ref, x1_ref, h2T_ref, scT_ref):
    merged = (a_ref[...].astype(_F32) + b_ref[...].astype(_F32) + c_ref[...].astype(_F32)).astype(_BF)
    x1 = x_ref[...] + _dot(merged, wo_ref[...])
    x1_ref[...] = x1
    h2T = _rms(x1, nf_ref[...]).T.astype(_BF)
    h2T_ref[...] = h2T
    pqT = _dot(wpqT_ref[...], h2T).astype(_BF)
    for c in range(sk_ref.shape[0]):
        scT_ref[c * PEER_KEYS:(c + 1) * PEER_KEYS, :] = _dot(sk_ref[c], pqT[c * LANES:(c + 1) * LANES, :])


def _mid(x2, ma, mb, mc, wo, nf, wpqT, sk, ts):
    t, d = x2.shape
    nsc = sk.shape[0] * PEER_KEYS
    tile = pl.BlockSpec((ts, d), lambda i: (i, 0))
    return pl.pallas_call(
        _mid_body, grid=(t // ts,),
        in_specs=[tile, tile, tile, tile, _full_spec(wo.shape), _full_spec((1, d)), _full_spec(wpqT.shape),
                  _full_spec(sk.shape)],
        out_specs=(tile, pl.BlockSpec((d, ts), lambda i: (0, i)), pl.BlockSpec((nsc, ts), lambda i: (0, i))),
        out_shape=(jax.ShapeDtypeStruct((t, d), _F32), jax.ShapeDtypeStruct((d, t), _BF),
                   jax.ShapeDtypeStruct((nsc, t), _F32)),
        compiler_params=_cparams("arbitrary"), name="mid",
    )(x2, ma, mb, mc, wo, nf.reshape(1, -1), wpqT, sk)


def _top_values(s, n):
    groups = [s[r * SUBLANES:(r + 1) * SUBLANES, :] for r in range(s.shape[0] // SUBLANES)]
    size = pl.next_power_of_2(len(groups))
    groups += [jnp.full_like(groups[0], -jnp.inf)] * (size - len(groups))
    k = 2
    while k <= size:
        j = k // 2
        while j >= 1:
            for i in range(size):
                l = i ^ j
                if l > i:
                    hi, lo = jnp.maximum(groups[i], groups[l]), jnp.minimum(groups[i], groups[l])
                    groups[i], groups[l] = (hi, lo) if (i & k) == 0 else (lo, hi)
            j //= 2
        k *= 2
    vals = []
    for it in range(n):
        m = jnp.max(groups[0], axis=0, keepdims=True)
        vals.append(m)
        remaining = n - 1 - it
        head = groups[0] == m
        for d in range(min(remaining, size)):
            below = groups[d + 1] if d + 1 < size else jnp.full_like(m, -jnp.inf)
            groups[d] = jnp.where(head, below, groups[d])
    return vals


def _route_body(scT_ref, tau_ref, c_ref, top2_ref):
    def head(h, carry):
        r1 = pl.multiple_of(h * 2 * PEER_KEYS, PEER_KEYS)
        r2 = pl.multiple_of(h * 2 * PEER_KEYS + PEER_KEYS, PEER_KEYS)
        n = PEER_TOPK + 1
        v1 = _top_values(scT_ref[pl.ds(r1, PEER_KEYS), :], n)
        v2 = _top_values(scT_ref[pl.ds(r2, PEER_KEYS), :], n)
        edge = SUBLANES
        assert 2 * (edge + 1) > n
        pad = [jnp.full_like(v1[0], -jnp.inf)] * (-n % SUBLANES)
        col1 = jnp.concatenate(v1 + pad, axis=0)
        col2 = jnp.concatenate(v2 + pad, axis=0)
        cand = jnp.concatenate([v1[a] + col2[:edge] for a in range(edge)]
                               + [v1[0] + col2[edge:], v2[0] + col1[edge:]], axis=0)
        best = _top_values(cand, n)
        tau = best[PEER_TOPK - 1]
        top = v1[0] + v2[0]
        z = jnp.sum(jnp.where(cand >= tau, jnp.exp(cand - top), 0.0), axis=0, keepdims=True)
        tau_ref[pl.ds(h, 1), :] = 0.5 * (tau + best[PEER_TOPK])
        c_ref[pl.ds(h, 1), :] = top + jnp.log(z)
        top2_ref[pl.ds(pl.multiple_of(h * TOP_ROWS, SUBLANES), TOP_ROWS), :] = col2
        return carry

    lax.fori_loop(0, PEER_HEADS, head, 0, unroll=2)


def _route(scT, te):
    nsc, t = scT.shape
    spec = pl.BlockSpec((PEER_HEADS, te), lambda i: (0, i))
    return pl.pallas_call(
        _route_body, grid=(t // te,),
        in_specs=[pl.BlockSpec((nsc, te), lambda i: (0, i))],
        out_specs=(spec, spec, pl.BlockSpec((PEER_HEADS * TOP_ROWS, te), lambda i: (0, i))),
        out_shape=(jax.ShapeDtypeStruct((PEER_HEADS, t), _F32),) * 2
        + (jax.ShapeDtypeStruct((PEER_HEADS * TOP_ROWS, t), _F32),),
        compiler_params=_cparams("arbitrary"), name="route",
    )(scT)


def _experts_body(h2T_ref, scT_ref, tau_ref, c_ref, top2_ref, u_ref, vT_ref, x1_ref, out_ref,
                  acc_ref, e2_ref, rank_ref, cnt_ref, scale_ref, act_ref, g_ref, rows_ref, *, rows_per_step,
                  n_blocks):
    e = pl.program_id(1)
    nk = PEER_KEYS
    ts = act_ref.shape[1]

    @pl.when(e == 0)
    def _():
        acc_ref[...] = jnp.zeros_like(acc_ref)
        for h in range(PEER_HEADS):
            best = [top2_ref[h * TOP_ROWS + b:h * TOP_ROWS + b + 1, :] for b in range(PEER_TOPK + 1)]
            s1 = scT_ref[2 * h * nk:(2 * h + 1) * nk, :]
            s2 = scT_ref[(2 * h + 1) * nk:(2 * h + 2) * nk, :]
            e2_ref[h * nk:(h + 1) * nk, :] = jnp.exp(s2 - best[0]).astype(_BF)
            rank = jnp.zeros((nk, ts), _F32)
            for b in range(PEER_TOPK):
                rank = jnp.where(best[b] > s2, float(b + 1), rank)
            rank_ref[h * nk:(h + 1) * nk, :] = rank.astype(_BF)
            thr = tau_ref[h:h + 1, :] - s1
            cnt = jnp.zeros((nk, ts), _F32)
            for b in range(PEER_TOPK + 1):
                cnt = jnp.where(best[b] >= thr, float(b + 1), cnt)
            cnt_ref[h * nk:(h + 1) * nk, :] = cnt
            scale_ref[h * nk:(h + 1) * nk, :] = 0.5 * jnp.exp(s1 + best[0] - c_ref[h:h + 1, :])

    n_rows = rows_per_step * PEER_HEADS
    pair = 2 * nk
    zero = jnp.zeros((), _BF)

    def first_matmul(i):
        act_ref[i * pair:(i + 1) * pair, :] = _dot(u_ref[i * pair:(i + 1) * pair, :], h2T_ref[...])

    for i in range(rows_per_step):
        for h in range(PEER_HEADS):
            src = h * nk + e * rows_per_step + i
            r = i * PEER_HEADS + h
            rows_ref[r:r + 1, :] = cnt_ref[pl.ds(src, 1), :]
            rows_ref[n_rows + r:n_rows + r + 1, :] = scale_ref[pl.ds(src, 1), :]

    quad = 4 * nk
    first_matmul(0)
    for i in range(rows_per_step):
        if i % 2 == 0 and i // 2 + 1 < rows_per_step // 2:
            first_matmul(i // 2 + 1)
        wgt = jnp.zeros((nk, ts), _BF)
        for h in range(PEER_HEADS):
            r = i * PEER_HEADS + h
            cnt = rows_ref[r:r + 1, :].astype(_BF)
            scale = rows_ref[n_rows + r:n_rows + r + 1, :].astype(_BF)
            hit = jnp.where(rank_ref[h * nk:(h + 1) * nk, :] < cnt, e2_ref[h * nk:(h + 1) * nk, :], zero)
            wgt = wgt + hit * scale
        x = act_ref[i * nk:(i + 1) * nk, :].astype(_BF)
        t = jnp.tanh(x * (jnp.asarray(GELU_C0, _BF) + jnp.asarray(GELU_C1, _BF) * (x * x)))
        g_ref[i * nk:(i + 1) * nk, :] = (x + x * t) * wgt
        if i % 4 == 3:
            p = i // 4
            acc_ref[...] += _dot(vT_ref[0, :, p * quad:(p + 1) * quad], g_ref[p * quad:(p + 1) * quad, :])

    @pl.when(e == n_blocks - 1)
    def _():
        out_ref[...] = x1_ref[...] + acc_ref[...].T


def _experts(h2T, scT, tau, cn, top2, u, vT, x1, ts, rows_per_step):
    d, t = h2T.shape
    nsc = scT.shape[0]
    ne = u.shape[0]
    eb = rows_per_step * PEER_KEYS
    nb = ne // eb
    nhk = PEER_HEADS * PEER_KEYS
    col = lambda n: pl.BlockSpec((n, ts), lambda i, e: (0, i))
    tile = pl.BlockSpec((ts, d), lambda i, e: (i, 0))
    return pl.pallas_call(
        functools.partial(_experts_body, rows_per_step=rows_per_step, n_blocks=nb),
        grid=(t // ts, nb),
        in_specs=[col(d), col(nsc), col(PEER_HEADS), col(PEER_HEADS), col(top2.shape[0]),
                  pl.BlockSpec((eb, d), lambda i, e: (e, 0)), pl.BlockSpec((1, d, eb), lambda i, e: (e, 0, 0)), tile],
        out_specs=tile,
        out_shape=jax.ShapeDtypeStruct((t, d), _F32),
        scratch_shapes=[pltpu.VMEM((d, ts), _F32), pltpu.VMEM((nhk, ts), _BF), pltpu.VMEM((nhk, ts), _BF),
                        pltpu.VMEM((nhk, ts), _F32), pltpu.VMEM((nhk, ts), _F32),
                        pltpu.VMEM((eb, ts), _F32), pltpu.VMEM((eb, ts), _BF),
                        pltpu.VMEM((2 * rows_per_step * PEER_HEADS, ts), _F32)],
        compiler_params=_cparams("arbitrary", "arbitrary"), name="experts",
    )(h2T, scT, tau, cn, top2, u, vT, x1)


def _block_diag(w):
    n, c, d = w.shape
    return jnp.einsum("ncd,nm->ncmd", w, jnp.eye(n, dtype=w.dtype)).reshape(n * c, n * d)


def _rope_tables(s, dim, reps):
    half = dim // 2
    freq = ROPE_THETA ** (-jnp.arange(half, dtype=_F32) / half)
    ang = jnp.arange(s, dtype=_F32)[:, None] * freq[None, :]
    cos = jnp.tile(jnp.concatenate([jnp.cos(ang), jnp.cos(ang)], axis=1), (1, reps))
    sin = jnp.tile(jnp.concatenate([-jnp.sin(ang), jnp.sin(ang)], axis=1), (1, reps))
    return cos, sin


def _tile(n, pref):
    return pref if n % pref == 0 else n


def _layer(x, mem, p, l):
    b, s, d = x.shape
    bf = lambda a: a.astype(_BF)
    w_in = p["w_in"][l]
    n_q, n_kv = ATT_HEADS * LANES, ATT_KV_HEADS * LANES
    n_qi = IDX_HEADS * IDX_DIM
    sizes = (d, d, n_q, n_kv, n_kv, n_qi, IDX_DIM, IDX_HEADS, d, 3 * d)
    offs = [0]
    for n in sizes:
        offs.append(offs[-1] + n)
    wlx, wlg, wq, wk, wv, wqi, wki, wwi, wmq, wgates = (w_in[:, offs[i]:offs[i + 1]] for i in range(len(sizes)))
    wki2 = jnp.concatenate([wki, wki], axis=1)
    wwi_p = jnp.pad(wwi, ((0, 0), (0, LANES - IDX_HEADS)))
    wg0, wg1, wg2 = wgates[:, :d], wgates[:, d:2 * d], wgates[:, 2 * d:]
    nm = p["norm_mix"][l]

    ts_a = _tile(s, 256)
    m_lru = _lru_branch(x, nm, bf(wlx), bf(wlg), bf(wg0), p["conv_w"][l], p["conv_b"][l],
                        bf(_block_diag(p["lru_wa"][l])), p["lru_ba"][l].reshape(-1),
                        bf(_block_diag(p["lru_wi"][l])), p["lru_bi"][l].reshape(-1), p["lru_lambda"][l], ts_a)

    tk = _tile(s, 512)
    c128, s128 = _rope_tables(s, LANES, 1)
    c64, s64 = _rope_tables(s, IDX_DIM, LANES // IDX_DIM)
    ikn2 = jnp.concatenate([p["idx_k_norm"][l]] * 2)
    qT, k, vT, qiT, ki, wT, g1 = _dsa_proj(x, nm, bf(wq), bf(wk), bf(wv), bf(wqi), bf(wki2), bf(wwi_p), bf(wg1),
                                           p["q_norm"][l], p["k_norm"][l], ikn2, c128, s128, c64, s64, tk)
    m_att = _dsa(qT, qiT, wT, k, vT, ki, g1, min(TOPK_MAX, s // 4), _tile(s, 256), tk)

    mkT, mv = _mem_kv(mem, p["mem_norm"][l], bf(p["w_mem_kv"][l]), p["mem_k_norm"][l])
    m_mem = _mem_branch(x, nm, bf(wmq), bf(wg2), p["mem_q_norm"][l], mkT, mv, _tile(s, 512))

    t = b * s
    flat = lambda a: a.reshape(t, d)
    sk = bf(p["peer_subkeys"][l].reshape(2 * PEER_HEADS, PEER_KEYS, -1))
    x1, h2T, scT = _mid(flat(x), flat(m_lru), flat(m_att), flat(m_mem), bf(p["w_out"][l]), p["norm_ffn"][l],
                        bf(p["peer_wq"][l].T), sk, _tile(t, 512))
    tau, cn, top2 = _route(scT, _tile(t, 256))
    rows_per_step = 8
    vT = bf(p["peer_v"][l]).reshape(-1, rows_per_step * PEER_KEYS, d).transpose(0, 2, 1)
    out = _experts(h2T, scT, tau, cn, top2, bf(p["peer_u"][l]), vT, x1, _tile(t, 512), rows_per_step)
    return out.reshape(b, s, d)


def kernel(x, mem, norm_mix, w_in, conv_w, conv_b, lru_wa, lru_ba, lru_wi, lru_bi, lru_lambda, q_norm, k_norm,
           idx_k_norm, mem_norm, w_mem_kv, mem_q_norm, mem_k_norm, w_out, norm_ffn, peer_wq, peer_subkeys,
           peer_u, peer_v):
    p = dict(norm_mix=norm_mix, w_in=w_in, conv_w=conv_w, conv_b=conv_b, lru_wa=lru_wa, lru_ba=lru_ba,
             lru_wi=lru_wi, lru_bi=lru_bi, lru_lambda=lru_lambda, q_norm=q_norm, k_norm=k_norm,
             idx_k_norm=idx_k_norm, mem_norm=mem_norm, w_mem_kv=w_mem_kv, mem_q_norm=mem_q_norm,
             mem_k_norm=mem_k_norm, w_out=w_out, norm_ffn=norm_ffn, peer_wq=peer_wq, peer_subkeys=peer_subkeys,
             peer_u=peer_u, peer_v=peer_v)
    for l in range(norm_mix.shape[0]):
        x = _layer(x, mem, p, l)
    return x
```

```python
import functools

import jax
import jax.numpy as jnp
from jax import lax
from jax.experimental import pallas as pl
from jax.experimental.pallas import tpu as pltpu

_F32 = jnp.float32
_BF = jnp.bfloat16

EPS = 1e-6
ROPE_THETA = 10000.0
LRU_C = 8.0
LRU_BLOCKS = 16
CONV_WIDTH = 4
ATT_HEADS = 8
ATT_KV_HEADS = 2
ATT_GROUP = ATT_HEADS // ATT_KV_HEADS
IDX_HEADS = 8
IDX_DIM = 64
TOPK_MAX = 256
MEM_HEADS = 4
PEER_HEADS = 8
PEER_KEYS = 128
PEER_TOPK = 16
TOP_ROWS = 24

LANES = 128
SUBLANES = 8
ONES_ROWS = 2 * SUBLANES
VMEM_LIMIT_BYTES = 56 * 1024 * 1024
MASK_VALUE = -1e30
FLT_MAX = 3.4028234663852886e38
INT_MIN = -(2 ** 31)


def _cparams(*sem):
    return pltpu.CompilerParams(dimension_semantics=sem, vmem_limit_bytes=VMEM_LIMIT_BYTES)


def _rms(x, g):
    return x * lax.rsqrt(jnp.mean(x * x, axis=-1, keepdims=True) + EPS) * g


GELU_C0 = 0.7978845608028654
GELU_C1 = GELU_C0 * 0.044715


def _gelu(x):
    return 0.5 * x * (1.0 + jnp.tanh(x * (GELU_C0 + GELU_C1 * (x * x))))


def _sigmoid(x):
    return 0.5 * jnp.tanh(0.5 * x) + 0.5


def _dot(a, b):
    return jnp.dot(a, b, preferred_element_type=_F32)


def _full_spec(shape):
    n = len(shape)
    return pl.BlockSpec(shape, lambda *_: (0,) * n)


def _lru_body(x_ref, nm_ref, wlx_ref, wlg_ref, wg0_ref, cw_ref, cb_ref, wa_ref, ba_ref, wi_ref,
              bi_ref, lam_ref, out_ref, buf_ref, hc_ref, a_ref, b_ref):
    ts = x_ref.shape[1]
    c_dim = out_ref.shape[2]

    @pl.when(pl.program_id(1) == 0)
    def _():
        buf_ref[0:SUBLANES, :] = jnp.zeros((SUBLANES, c_dim), _F32)
        hc_ref[...] = jnp.zeros_like(hc_ref)

    h = _rms(x_ref[0], nm_ref[...]).astype(_BF)
    lx = _dot(h, wlx_ref[...])
    buf_ref[SUBLANES:SUBLANES + ts, :] = lx
    ext = buf_ref[...]
    cw = cw_ref[...]
    xc = cb_ref[...] + cw[3:4] * lx
    for j in range(CONV_WIDTH - 1):
        d = CONV_WIDTH - 1 - j
        xc = xc + cw[j:j + 1] * pltpu.roll(ext, d, 0)[SUBLANES:, :]
    buf_ref[0:SUBLANES, :] = lx[ts - SUBLANES:, :]

    xcb = xc.astype(_BF)
    r = _sigmoid(_dot(xcb, wa_ref[...]) + ba_ref[...])
    ig = _sigmoid(_dot(xcb, wi_ref[...]) + bi_ref[...])
    lam = lam_ref[...]
    softplus_neg_lam = jnp.maximum(-lam, 0.0) + jnp.log1p(jnp.exp(-jnp.abs(lam)))
    log_a = (-LRU_C) * r * softplus_neg_lam
    a = jnp.exp(log_a)
    y = jnp.tanh(-log_a) * (a * a + 1.0)
    u = jnp.where(y > 0.0, y * lax.rsqrt(y), 0.0) * (ig * xc)

    a = a.reshape(ts // SUBLANES, SUBLANES, c_dim)
    u = u.reshape(ts // SUBLANES, SUBLANES, c_dim)
    rid = lax.broadcasted_iota(jnp.int32, a.shape, 1)
    for d in (1, 2, 4):
        ok = rid >= d
        a_s = pltpu.roll(a, d, 1)
        u_s = pltpu.roll(u, d, 1)
        u = jnp.where(ok, a * u_s + u, u)
        a = jnp.where(ok, a * a_s, a)
    a_ref[...] = a.reshape(ts, c_dim)
    b_ref[...] = u.reshape(ts, c_dim)

    def step(g, hp):
        r0 = pl.multiple_of(g * SUBLANES, SUBLANES)
        hcur = a_ref[pl.ds(r0, SUBLANES), :] * hp + b_ref[pl.ds(r0, SUBLANES), :]
        b_ref[pl.ds(r0, SUBLANES), :] = hcur
        return jnp.broadcast_to(hcur[SUBLANES - 1:SUBLANES, :], (SUBLANES, c_dim))

    hc_ref[...] = lax.fori_loop(0, ts // SUBLANES, step, hc_ref[...])
    gate = _gelu(_dot(h, wlg_ref[...])) * _sigmoid(_dot(h, wg0_ref[...]))
    out_ref[0] = (b_ref[...] * gate).astype(_BF)


def _lru_branch(x, nm, wlx, wlg, wg0, cw, cb, wa, ba, wi, bi, lam, ts):
    b, s, d = x.shape
    c = wlx.shape[1]
    row = lambda v: v.reshape(1, -1)
    args = (x, row(nm), wlx, wlg, wg0, cw, row(cb), wa, row(ba), wi, row(bi), row(lam))
    in_specs = [pl.BlockSpec((1, ts, d), lambda i, j: (i, j, 0))]
    in_specs += [_full_spec(a.shape) for a in args[1:]]
    return pl.pallas_call(
        _lru_body,
        grid=(b, s // ts),
        in_specs=in_specs,
        out_specs=pl.BlockSpec((1, ts, c), lambda i, j: (i, j, 0)),
        out_shape=jax.ShapeDtypeStruct((b, s, c), _BF),
        scratch_shapes=[pltpu.VMEM((ts + SUBLANES, c), _F32), pltpu.VMEM((SUBLANES, c), _F32),
                        pltpu.VMEM((ts, c), _F32), pltpu.VMEM((ts, c), _F32)],
        compiler_params=_cparams("arbitrary", "arbitrary"),
        name="lru",
    )(*args)


def _rope128(x, cos, sin):
    return x * cos + pltpu.roll(x, LANES // 2, 1) * sin


def _rope64(x, cos, sin, first_half):
    rot = jnp.where(first_half, pltpu.roll(x, LANES - IDX_DIM // 2, 1), pltpu.roll(x, IDX_DIM // 2, 1))
    return x * cos + rot * sin


def _dsa_proj_body(x_ref, nm_ref, wq_ref, wk_ref, wv_ref, wqi_ref, wki_ref, wwi_ref, wg1_ref,
                   qn_ref, kn_ref, ikn_ref, c128_ref, s128_ref, c64_ref, s64_ref,
                   qT_ref, k_ref, vT_ref, qiT_ref, ki_ref, wT_ref, g1_ref):
    ts = x_ref.shape[1]
    h = _rms(x_ref[0], nm_ref[...]).astype(_BF)
    c128, s128 = c128_ref[...], s128_ref[...]
    c64, s64 = c64_ref[...], s64_ref[...]
    first_half = (lax.broadcasted_iota(jnp.int32, (ts, LANES), 1) & (IDX_DIM - 1)) < IDX_DIM // 2

    def head_norm_rope(t, g, scale):
        outs = []
        for i in range(t.shape[1] // LANES):
            th = _rms(t[:, i * LANES:(i + 1) * LANES], g)
            outs.append(_rope128(th, c128, s128) * scale)
        return jnp.concatenate(outs, axis=1)

    q = head_norm_rope(_dot(h, wq_ref[...]), qn_ref[...], float(LANES) ** -0.5)
    qT_ref[0] = q.T.astype(_BF)
    k_ref[0] = head_norm_rope(_dot(h, wk_ref[...]), kn_ref[...], 1.0).astype(_BF)
    vT = _dot(h, wv_ref[...]).T.astype(_BF)
    for g in range(ATT_KV_HEADS):
        vT_ref[0, 0, g, :LANES, :] = vT[g * LANES:(g + 1) * LANES, :]
        vT_ref[0, 0, g, LANES:, :] = jnp.ones((ONES_ROWS, ts), _BF)

    qi = _dot(h, wqi_ref[...])
    qi = jnp.concatenate(
        [_rope64(qi[:, i * LANES:(i + 1) * LANES], c64, s64, first_half) for i in range(qi.shape[1] // LANES)],
        axis=1) * (float(IDX_DIM) ** -0.5)
    qiT_ref[0] = qi.T.astype(_BF)
    ki = _rope64(_rms(_dot(h, wki_ref[...]), ikn_ref[...]), c64, s64, first_half)
    ki_ref[0] = ki[:, :IDX_DIM].astype(_BF)
    wi = _dot(h, wwi_ref[...]) * (float(IDX_HEADS) ** -0.5)
    wT_ref[0] = wi.T[:IDX_HEADS, :]
    g1_ref[0] = _sigmoid(_dot(h, wg1_ref[...])).astype(_BF)


def _dsa_proj(x, nm, wq, wk, wv, wqi, wki2, wwi, wg1, qn, kn, ikn2, c128, s128, c64, s64, ts):
    b, s, d = x.shape
    row = lambda v: v.reshape(1, -1)
    nq, nk, nv, nqi = wq.shape[1], wk.shape[1], wv.shape[1], wqi.shape[1]
    args = (x, row(nm), wq, wk, wv, wqi, wki2, wwi, wg1, row(qn), row(kn), row(ikn2), c128, s128, c64, s64)
    tile = lambda n: pl.BlockSpec((1, ts, n), lambda i, j: (i, j, 0))
    tile_t = lambda n: pl.BlockSpec((1, n, ts), lambda i, j: (i, 0, j))
    tab = pl.BlockSpec((ts, LANES), lambda i, j: (j, 0))
    in_specs = [tile(d)] + [_full_spec(a.shape) for a in args[1:12]] + [tab] * 4
    out_shape = (
        jax.ShapeDtypeStruct((b, nq, s), _BF),
        jax.ShapeDtypeStruct((b, s, nk), _BF),
        jax.ShapeDtypeStruct((b, s // ts, ATT_KV_HEADS, LANES + ONES_ROWS, ts), _BF),
        jax.ShapeDtypeStruct((b, nqi, s), _BF),
        jax.ShapeDtypeStruct((b, s, IDX_DIM), _BF),
        jax.ShapeDtypeStruct((b, IDX_HEADS, s), _F32),
        jax.ShapeDtypeStruct((b, s, d), _BF),
    )
    out_specs = (tile_t(nq), tile(nk),
                 pl.BlockSpec((1, 1, ATT_KV_HEADS, LANES + ONES_ROWS, ts), lambda i, j: (i, j, 0, 0, 0)), tile_t(nqi),
                 tile(IDX_DIM), tile_t(IDX_HEADS), tile(d))
    return pl.pallas_call(
        _dsa_proj_body, grid=(b, s // ts), in_specs=in_specs, out_specs=out_specs, out_shape=out_shape,
        compiler_params=_cparams("arbitrary", "arbitrary"), name="dsa_proj",
    )(*args)


def _key_to_float(u):
    key = u ^ jnp.int32(INT_MIN)
    bits = key ^ ((key >> 31) & jnp.int32(0x7FFFFFFF))
    return jnp.where((u >> 23) == 0, -jnp.inf, lax.bitcast_convert_type(bits, _F32))


def _high_half(x):
    bits = lax.bitcast_convert_type(x, jnp.int32) & jnp.int32(-65536)
    return lax.bitcast_convert_type(bits, _F32).astype(_BF)


def _dsa_body(qT_ref, qiT_ref, wT_ref, k_ref, vT_ref, ki_ref, g1_ref, out_ref,
              sc_ref, sh_ref, m_ref, l_ref, acc_ref, bias_ref, s_ref, p_ref, *, topk, tk):
    tq = qT_ref.shape[2]
    dh = LANES
    q0 = pl.program_id(1) * tq
    n_chunks = (q0 + tq + tk - 1) // tk

    qicat = jnp.concatenate([qiT_ref[0, h * IDX_DIM:(h + 1) * IDX_DIM, :] for h in range(IDX_HEADS)], axis=1)
    w = wT_ref[0]
    qpos = q0 + lax.broadcasted_iota(jnp.int32, (tk, tq), 1)
    krow = lax.broadcasted_iota(jnp.int32, (tk, tq), 0)

    def score_chunk(c, carry):
        r0 = pl.multiple_of(c * tk, tk)
        lg = _dot(ki_ref[0, pl.ds(r0, tk), :], qicat)
        s = jnp.zeros((tk, tq), _F32)
        for h in range(IDX_HEADS):
            s = s + jnp.maximum(lg[:, h * tq:(h + 1) * tq], 0.0) * w[h:h + 1, :]
        s = jnp.where(krow + r0 <= qpos, s, -jnp.inf)
        sc_ref[pl.ds(r0, tk), :] = s
        sh_ref[pl.ds(r0, tk), :] = _high_half(s)
        return carry

    lax.fori_loop(0, n_chunks, score_chunk, 0)

    def make_count(nc, ref, rows, dtype):
        def count_ge(cand):
            one, zero = jnp.ones((), dtype), jnp.zeros((), dtype)
            parts = [jnp.zeros((rows, tq), dtype) for _ in range(4)]
            n = 0
            for c in range(nc):
                hit = jnp.where(ref[c * tk:(c + 1) * tk, :] >= cand, one, zero)
                for r in range(tk // rows):
                    parts[n % 4] = parts[n % 4] + hit[r * rows:(r + 1) * rows, :]
                    n += 1
            parts = [p.astype(_F32) for p in parts]
            return jnp.sum((parts[0] + parts[1]) + (parts[2] + parts[3]), axis=0, keepdims=True)
        return count_ge

    max_chunks = sc_ref.shape[0] // tk
    assert max_chunks * tk // (2 * SUBLANES) <= 4 * 64
    coarse = [make_count(nc, sh_ref, 2 * SUBLANES, _BF) for nc in range(1, max_chunks + 1)]
    fine = [make_count(nc, sc_ref, SUBLANES, _F32) for nc in range(1, max_chunks + 1)]

    def bisect_high(i, u):
        trial = u | lax.shift_left(jnp.int32(1), 31 - i)
        cnt = lax.switch(n_chunks - 1, coarse, _high_half(_key_to_float(trial)))
        return jnp.where(cnt >= float(topk), trial, u)

    def bisect_low(i, u):
        trial = u | lax.shift_left(jnp.int32(1), 31 - i)
        cnt = lax.switch(n_chunks - 1, fine, _key_to_float(trial))
        return jnp.where(cnt >= float(topk), trial, u)

    u = lax.fori_loop(0, 16, bisect_high, jnp.zeros((1, tq), jnp.int32))
    u = lax.fori_loop(16, 32, bisect_low, u)
    tau = jnp.maximum(_key_to_float(u), -FLT_MAX)

    m_ref[...] = jnp.full(m_ref.shape, MASK_VALUE, _F32)
    l_ref[...] = jnp.zeros_like(l_ref)
    acc_ref[...] = jnp.zeros_like(acc_ref)

    sub = 8 * SUBLANES
    groups = [slice(r * SUBLANES, (r + 1) * SUBLANES) for r in range(sub // SUBLANES)]

    def att_chunk(c, carry):
        r0 = pl.multiple_of(c * tk, tk)
        bias_ref[...] = jnp.where(sc_ref[pl.ds(r0, tk), :] >= tau, 0.0, MASK_VALUE)

        def logits(h):
            g = h // ATT_GROUP
            s_ref[h] = _dot(k_ref[0, pl.ds(r0, tk), g * dh:(g + 1) * dh], qT_ref[0, h * dh:(h + 1) * dh, :])

        logits(0)
        logits(1)
        for h in range(ATT_HEADS):
            if h + 2 < ATT_HEADS:
                logits(h + 2)
            m_old = m_ref[h:h + 1, :]
            mx = jnp.broadcast_to(m_old, (SUBLANES, tq))
            for i in range(tk // sub):
                rows = slice(i * sub, (i + 1) * sub)
                sb = s_ref[h, rows, :] + bias_ref[rows, :]
                s_ref[h, rows, :] = sb
                for gr in groups:
                    mx = jnp.maximum(mx, sb[gr, :])
            m_new = jnp.max(mx, axis=0, keepdims=True)
            for i in range(tk // sub):
                rows = slice(i * sub, (i + 1) * sub)
                p_ref[h, rows, :] = jnp.exp((s_ref[h, rows, :] - m_new).astype(_BF))
            alpha = jnp.exp(m_old - m_new)
            pv = _dot(vT_ref[0, c, h // ATT_GROUP], p_ref[h])
            l_ref[h:h + 1, :] = alpha * l_ref[h:h + 1, :] + pv[dh:dh + 1, :]
            acc_ref[h * dh:(h + 1) * dh, :] = alpha * acc_ref[h * dh:(h + 1) * dh, :] + pv[:dh, :]
            m_ref[h:h + 1, :] = m_new
        return carry

    lax.fori_loop(0, n_chunks, att_chunk, 0)
    outs = []
    for h in range(ATT_HEADS):
        o = acc_ref[h * dh:(h + 1) * dh, :] / l_ref[h:h + 1, :]
        outs.append(o.T)
    out_ref[0] = (jnp.concatenate(outs, axis=1) * g1_ref[0].astype(_F32)).astype(_BF)


def _dsa(qT, qiT, wT, k, vT, ki, g1, topk, tq, tk):
    b, nq, s = qT.shape
    d = g1.shape[2]
    in_specs = [
        pl.BlockSpec((1, nq, tq), lambda i, j: (i, 0, j)),
        pl.BlockSpec((1, qiT.shape[1], tq), lambda i, j: (i, 0, j)),
        pl.BlockSpec((1, IDX_HEADS, tq), lambda i, j: (i, 0, j)),
        pl.BlockSpec((1, s, k.shape[2]), lambda i, j: (i, 0, 0)),
        pl.BlockSpec((1,) + vT.shape[1:], lambda i, j: (i, 0, 0, 0, 0)),
        pl.BlockSpec((1, s, IDX_DIM), lambda i, j: (i, 0, 0)),
        pl.BlockSpec((1, tq, d), lambda i, j: (i, j, 0)),
    ]
    return pl.pallas_call(
        functools.partial(_dsa_body, topk=topk, tk=tk),
        grid=(b, s // tq), in_specs=in_specs,
        out_specs=pl.BlockSpec((1, tq, d), lambda i, j: (i, j, 0)),
        out_shape=jax.ShapeDtypeStruct((b, s, d), _BF),
        scratch_shapes=[pltpu.VMEM((s, tq), _F32), pltpu.VMEM((s, tq), _BF), pltpu.VMEM((ATT_HEADS, tq), _F32),
                        pltpu.VMEM((ATT_HEADS, tq), _F32), pltpu.VMEM((nq, tq), _F32),
                        pltpu.VMEM((tk, tq), _F32), pltpu.VMEM((ATT_HEADS, tk, tq), _F32),
                        pltpu.VMEM((ATT_HEADS, tk, tq), _BF)],
        compiler_params=_cparams("arbitrary", "arbitrary"), name="dsa",
    )(qT, qiT, wT, k, vT, ki, g1)


def _mem_kv_body(mem_ref, mn_ref, wkv_ref, mkn_ref, mkT_ref, mv_ref):
    d = mem_ref.shape[2]
    dm = d // MEM_HEADS
    m = _rms(mem_ref[0], mn_ref[...]).astype(_BF)
    kv = _dot(m, wkv_ref[...])
    mk = jnp.concatenate([_rms(kv[:, i * dm:(i + 1) * dm], mkn_ref[...]) for i in range(MEM_HEADS)], axis=1)
    mkT_ref[0] = mk.T.astype(_BF)
    mv_ref[0] = kv[:, d:].astype(_BF)


def _mem_kv(mem, mn, wkv, mkn):
    b, m, d = mem.shape
    row = lambda v: v.reshape(1, -1)
    return pl.pallas_call(
        _mem_kv_body, grid=(b,),
        in_specs=[pl.BlockSpec((1, m, d), lambda i: (i, 0, 0)), _full_spec((1, d)), _full_spec(wkv.shape),
                  _full_spec((1, d // MEM_HEADS))],
        out_specs=(pl.BlockSpec((1, d, m), lambda i: (i, 0, 0)), pl.BlockSpec((1, m, d), lambda i: (i, 0, 0))),
        out_shape=(jax.ShapeDtypeStruct((b, d, m), _BF), jax.ShapeDtypeStruct((b, m, d), _BF)),
        compiler_params=_cparams("arbitrary"), name="mem_kv",
    )(mem, row(mn), wkv, row(mkn))


def _mem_body(x_ref, nm_ref, wmq_ref, wg2_ref, mqn_ref, mkT_ref, mv_ref, out_ref):
    d = x_ref.shape[2]
    dm = d // MEM_HEADS
    h = _rms(x_ref[0], nm_ref[...]).astype(_BF)
    mq = _dot(h, wmq_ref[...])
    outs = []
    for i in range(MEM_HEADS):
        qh = (_rms(mq[:, i * dm:(i + 1) * dm], mqn_ref[...]) * (float(dm) ** -0.5)).astype(_BF)
        s = _dot(qh, mkT_ref[0, i * dm:(i + 1) * dm, :])
        p = jnp.exp(s - jnp.max(s, axis=-1, keepdims=True))
        o = _dot(p.astype(_BF), mv_ref[0, :, i * dm:(i + 1) * dm])
        outs.append(o / jnp.sum(p, axis=-1, keepdims=True))
    out_ref[0] = (jnp.concatenate(outs, axis=1) * _sigmoid(_dot(h, wg2_ref[...]))).astype(_BF)


def _mem_branch(x, nm, wmq, wg2, mqn, mkT, mv, ts):
    b, s, d = x.shape
    m = mv.shape[1]
    row = lambda v: v.reshape(1, -1)
    return pl.pallas_call(
        _mem_body, grid=(b, s // ts),
        in_specs=[pl.BlockSpec((1, ts, d), lambda i, j: (i, j, 0)), _full_spec((1, d)), _full_spec(wmq.shape),
                  _full_spec(wg2.shape), _full_spec((1, d // MEM_HEADS)),
                  pl.BlockSpec((1, d, m), lambda i, j: (i, 0, 0)), pl.BlockSpec((1, m, d), lambda i, j: (i, 0, 0))],
        out_specs=pl.BlockSpec((1, ts, d), lambda i, j: (i, j, 0)),
        out_shape=jax.ShapeDtypeStruct((b, s, d), _BF),
        compiler_params=_cparams("arbitrary", "arbitrary"), name="mem",
    )(x, row(nm), wmq, wg2, row(mqn), mkT, mv)


def _mid_body(x_ref, a_ref, b_ref, c_ref, wo_ref, nf_ref, wpqT_ref, sk_ref, x1_ref, h2T_ref, scT_ref):
    merged = (a_ref[...].astype(_F32) + b_ref[...].astype(_F32) + c_ref[...].astype(_F32)).astype(_BF)
    x1 = x_ref[...] + _dot(merged, wo_ref[...])
    x1_ref[...] = x1
    h2T = _rms(x1, nf_ref[...]).T.astype(_BF)
    h2T_ref[...] = h2T
    pqT = _dot(wpqT_ref[...], h2T).astype(_BF)
    for c in range(sk_ref.shape[0]):
        scT_ref[c * PEER_KEYS:(c + 1) * PEER_KEYS, :] = _dot(sk_ref[c], pqT[c * LANES:(c + 1) * LANES, :])


def _mid(x2, ma, mb, mc, wo, nf, wpqT, sk, ts):
    t, d = x2.shape
    nsc = sk.shape[0] * PEER_KEYS
    tile = pl.BlockSpec((ts, d), lambda i: (i, 0))
    return pl.pallas_call(
        _mid_body, grid=(t // ts,),
        in_specs=[tile, tile, tile, tile, _full_spec(wo.shape), _full_spec((1, d)), _full_spec(wpqT.shape),
                  _full_spec(sk.shape)],
        out_specs=(tile, pl.BlockSpec((d, ts), lambda i: (0, i)), pl.BlockSpec((nsc, ts), lambda i: (0, i))),
        out_shape=(jax.ShapeDtypeStruct((t, d), _F32), jax.ShapeDtypeStruct((d, t), _BF),
                   jax.ShapeDtypeStruct((nsc, t), _F32)),
        compiler_params=_cparams("arbitrary"), name="mid",
    )(x2, ma, mb, mc, wo, nf.reshape(1, -1), wpqT, sk)


def _top_values(s, n):
    groups = [s[r * SUBLANES:(r + 1) * SUBLANES, :] for r in range(s.shape[0] // SUBLANES)]
    size = pl.next_power_of_2(len(groups))
    groups += [jnp.full_like(groups[0], -jnp.inf)] * (size - len(groups))
    k = 2
    while k <= size:
        j = k // 2
        while j >= 1:
            for i in range(size):
                l = i ^ j
                if l > i:
                    hi, lo = jnp.maximum(groups[i], groups[l]), jnp.minimum(groups[i], groups[l])
                    groups[i], groups[l] = (hi, lo) if (i & k) == 0 else (lo, hi)
            j //= 2
        k *= 2
    vals = []
    for it in range(n):
        m = jnp.max(groups[0], axis=0, keepdims=True)
        vals.append(m)
        remaining = n - 1 - it
        head = groups[0] == m
        for d in range(min(remaining, size)):
            below = groups[d + 1] if d + 1 < size else jnp.full_like(m, -jnp.inf)
            groups[d] = jnp.where(head, below, groups[d])
    return vals


def _route_body(scT_ref, tau_ref, c_ref, top2_ref):
    def head(h, carry):
        r1 = pl.multiple_of(h * 2 * PEER_KEYS, PEER_KEYS)
        r2 = pl.multiple_of(h * 2 * PEER_KEYS + PEER_KEYS, PEER_KEYS)
        n = PEER_TOPK + 1
        v1 = _top_values(scT_ref[pl.ds(r1, PEER_KEYS), :], n)
        v2 = _top_values(scT_ref[pl.ds(r2, PEER_KEYS), :], n)
        edge = SUBLANES
        assert 2 * (edge + 1) > n
        pad = [jnp.full_like(v1[0], -jnp.inf)] * (-n % SUBLANES)
        col1 = jnp.concatenate(v1 + pad, axis=0)
        col2 = jnp.concatenate(v2 + pad, axis=0)
        cand = jnp.concatenate([v1[a] + col2[:edge] for a in range(edge)]
                               + [v1[0] + col2[edge:], v2[0] + col1[edge:]], axis=0)
        best = _top_values(cand, n)
        tau = best[PEER_TOPK - 1]
        top = v1[0] + v2[0]
        z = jnp.sum(jnp.where(cand >= tau, jnp.exp(cand - top), 0.0), axis=0, keepdims=True)
        tau_ref[pl.ds(h, 1), :] = 0.5 * (tau + best[PEER_TOPK])
        c_ref[pl.ds(h, 1), :] = top + jnp.log(z)
        top2_ref[pl.ds(pl.multiple_of(h * TOP_ROWS, SUBLANES), TOP_ROWS), :] = col2
        return carry

    lax.fori_loop(0, PEER_HEADS, head, 0, unroll=2)


def _route(scT, te):
    nsc, t = scT.shape
    spec = pl.BlockSpec((PEER_HEADS, te), lambda i: (0, i))
    return pl.pallas_call(
        _route_body, grid=(t // te,),
        in_specs=[pl.BlockSpec((nsc, te), lambda i: (0, i))],
        out_specs=(spec, spec, pl.BlockSpec((PEER_HEADS * TOP_ROWS, te), lambda i: (0, i))),
        out_shape=(jax.ShapeDtypeStruct((PEER_HEADS, t), _F32),) * 2
        + (jax.ShapeDtypeStruct((PEER_HEADS * TOP_ROWS, t), _F32),),
        compiler_params=_cparams("arbitrary"), name="route",
    )(scT)


def _experts_body(h2T_ref, scT_ref, tau_ref, c_ref, top2_ref, u_ref, vT_ref, x1_ref, out_ref,
                  acc_ref, e2_ref, rank_ref, cnt_ref, scale_ref, act_ref, g_ref, rows_ref, *, rows_per_step,
                  n_blocks):
    e = pl.program_id(1)
    nk = PEER_KEYS
    ts = act_ref.shape[1]

    @pl.when(e == 0)
    def _():
        acc_ref[...] = jnp.zeros_like(acc_ref)
        for h in range(PEER_HEADS):
            best = [top2_ref[h * TOP_ROWS + b:h * TOP_ROWS + b + 1, :] for b in range(PEER_TOPK + 1)]
            s1 = scT_ref[2 * h * nk:(2 * h + 1) * nk, :]
            s2 = scT_ref[(2 * h + 1) * nk:(2 * h + 2) * nk, :]
            e2_ref[h * nk:(h + 1) * nk, :] = jnp.exp(s2 - best[0]).astype(_BF)
            rank = jnp.zeros((nk, ts), _F32)
            for b in range(PEER_TOPK):
                rank = jnp.where(best[b] > s2, float(b + 1), rank)
            rank_ref[h * nk:(h + 1) * nk, :] = rank.astype(_BF)
            thr = tau_ref[h:h + 1, :] - s1
            cnt = jnp.zeros((nk, ts), _F32)
            for b in range(PEER_TOPK + 1):
                cnt = jnp.where(best[b] >= thr, float(b + 1), cnt)
            cnt_ref[h * nk:(h + 1) * nk, :] = cnt
            scale_ref[h * nk:(h + 1) * nk, :] = 0.5 * jnp.exp(s1 + best[0] - c_ref[h:h + 1, :])

    n_rows = rows_per_step * PEER_HEADS
    pair = 2 * nk
    zero = jnp.zeros((), _BF)

    def first_matmul(i):
        act_ref[i * pair:(i + 1) * pair, :] = _dot(u_ref[i * pair:(i + 1) * pair, :], h2T_ref[...])

    for i in range(rows_per_step):
        for h in range(PEER_HEADS):
            src = h * nk + e * rows_per_step + i
            r = i * PEER_HEADS + h
            rows_ref[r:r + 1, :] = cnt_ref[pl.ds(src, 1), :]
            rows_ref[n_rows + r:n_rows + r + 1, :] = scale_ref[pl.ds(src, 1), :]

    quad = 4 * nk
    first_matmul(0)
    for i in range(rows_per_step):
        if i % 2 == 0 and i // 2 + 1 < rows_per_step // 2:
            first_matmul(i // 2 + 1)
        wgt = jnp.zeros((nk, ts), _BF)
        for h in range(PEER_HEADS):
            r = i * PEER_HEADS + h
            cnt = rows_ref[r:r + 1, :].astype(_BF)
            scale = rows_ref[n_rows + r:n_rows + r + 1, :].astype(_BF)
            hit = jnp.where(rank_ref[h * nk:(h + 1) * nk, :] < cnt, e2_ref[h * nk:(h + 1) * nk, :], zero)
            wgt = wgt + hit * scale
        x = act_ref[i * nk:(i + 1) * nk, :].astype(_BF)
        t = jnp.tanh(x * (jnp.asarray(GELU_C0, _BF) + jnp.asarray(GELU_C1, _BF) * (x * x)))
        g_ref[i * nk:(i + 1) * nk, :] = (x + x * t) * wgt
        if i % 4 == 3:
            p = i // 4
            acc_ref[...] += _dot(vT_ref[0, :, p * quad:(p + 1) * quad], g_ref[p * quad:(p + 1) * quad, :])

    @pl.when(e == n_blocks - 1)
    def _():
        out_ref[...] = x1_ref[...] + acc_ref[...].T


def _experts(h2T, scT, tau, cn, top2, u, vT, x1, ts, rows_per_step):
    d, t = h2T.shape
    nsc = scT.shape[0]
    ne = u.shape[0]
    eb = rows_per_step * PEER_KEYS
    nb = ne // eb
    nhk = PEER_HEADS * PEER_KEYS
    col = lambda n: pl.BlockSpec((n, ts), lambda i, e: (0, i))
    tile = pl.BlockSpec((ts, d), lambda i, e: (i, 0))
    return pl.pallas_call(
        functools.partial(_experts_body, rows_per_step=rows_per_step, n_blocks=nb),
        grid=(t // ts, nb),
        in_specs=[col(d), col(nsc), col(PEER_HEADS), col(PEER_HEADS), col(top2.shape[0]),
                  pl.BlockSpec((eb, d), lambda i, e: (e, 0)), pl.BlockSpec((1, d, eb), lambda i, e: (e, 0, 0)), tile],
        out_specs=tile,
        out_shape=jax.ShapeDtypeStruct((t, d), _F32),
        scratch_shapes=[pltpu.VMEM((d, ts), _F32), pltpu.VMEM((nhk, ts), _BF), pltpu.VMEM((nhk, ts), _BF),
                        pltpu.VMEM((nhk, ts), _F32), pltpu.VMEM((nhk, ts), _F32),
                        pltpu.VMEM((eb, ts), _F32), pltpu.VMEM((eb, ts), _BF),
                        pltpu.VMEM((2 * rows_per_step * PEER_HEADS, ts), _F32)],
        compiler_params=_cparams("arbitrary", "arbitrary"), name="experts",
    )(h2T, scT, tau, cn, top2, u, vT, x1)


def _block_diag(w):
    n, c, d = w.shape
    return jnp.einsum("ncd,nm->ncmd", w, jnp.eye(n, dtype=w.dtype)).reshape(n * c, n * d)


def _rope_tables(s, dim, reps):
    half = dim // 2
    freq = ROPE_THETA ** (-jnp.arange(half, dtype=_F32) / half)
    ang = jnp.arange(s, dtype=_F32)[:, None] * freq[None, :]
    cos = jnp.tile(jnp.concatenate([jnp.cos(ang), jnp.cos(ang)], axis=1), (1, reps))
    sin = jnp.tile(jnp.concatenate([-jnp.sin(ang), jnp.sin(ang)], axis=1), (1, reps))
    return cos, sin


def _tile(n, pref):
    return pref if n % pref == 0 else n


def _layer(x, mem, p, l):
    b, s, d = x.shape
    bf = lambda a: a.astype(_BF)
    w_in = p["w_in"][l]
    n_q, n_kv = ATT_HEADS * LANES, ATT_KV_HEADS * LANES
    n_qi = IDX_HEADS * IDX_DIM
    sizes = (d, d, n_q, n_kv, n_kv, n_qi, IDX_DIM, IDX_HEADS, d, 3 * d)
    offs = [0]
    for n in sizes:
        offs.append(offs[-1] + n)
    wlx, wlg, wq, wk, wv, wqi, wki, wwi, wmq, wgates = (w_in[:, offs[i]:offs[i + 1]] for i in range(len(sizes)))
    wki2 = jnp.concatenate([wki, wki], axis=1)
    wwi_p = jnp.pad(wwi, ((0, 0), (0, LANES - IDX_HEADS)))
    wg0, wg1, wg2 = wgates[:, :d], wgates[:, d:2 * d], wgates[:, 2 * d:]
    nm = p["norm_mix"][l]

    ts_a = _tile(s, 256)
    m_lru = _lru_branch(x, nm, bf(wlx), bf(wlg), bf(wg0), p["conv_w"][l], p["conv_b"][l],
                        bf(_block_diag(p["lru_wa"][l])), p["lru_ba"][l].reshape(-1),
                        bf(_block_diag(p["lru_wi"][l])), p["lru_bi"][l].reshape(-1), p["lru_lambda"][l], ts_a)

    tk = _tile(s, 512)
    c128, s128 = _rope_tables(s, LANES, 1)
    c64, s64 = _rope_tables(s, IDX_DIM, LANES // IDX_DIM)
    ikn2 = jnp.concatenate([p["idx_k_norm"][l]] * 2)
    qT, k, vT, qiT, ki, wT, g1 = _dsa_proj(x, nm, bf(wq), bf(wk), bf(wv), bf(wqi), bf(wki2), bf(wwi_p), bf(wg1),
                                           p["q_norm"][l], p["k_norm"][l], ikn2, c128, s128, c64, s64, tk)
    m_att = _dsa(qT, qiT, wT, k, vT, ki, g1, min(TOPK_MAX, s // 4), _tile(s, 256), tk)

    mkT, mv = _mem_kv(mem, p["mem_norm"][l], bf(p["w_mem_kv"][l]), p["mem_k_norm"][l])
    m_mem = _mem_branch(x, nm, bf(wmq), bf(wg2), p["mem_q_norm"][l], mkT, mv, _tile(s, 512))

    t = b * s
    flat = lambda a: a.reshape(t, d)
    sk = bf(p["peer_subkeys"][l].reshape(2 * PEER_HEADS, PEER_KEYS, -1))
    x1, h2T, scT = _mid(flat(x), flat(m_lru), flat(m_att), flat(m_mem), bf(p["w_out"][l]), p["norm_ffn"][l],
                        bf(p["peer_wq"][l].T), sk, _tile(t, 512))
    tau, cn, top2 = _route(scT, _tile(t, 256))
    rows_per_step = 16
    vT = bf(p["peer_v"][l]).reshape(-1, rows_per_step * PEER_KEYS, d).transpose(0, 2, 1)
    out = _experts(h2T, scT, tau, cn, top2, bf(p["peer_u"][l]), vT, x1, _tile(t, 512), rows_per_step)
    return out.reshape(b, s, d)


def kernel(x, mem, norm_mix, w_in, conv_w, conv_b, lru_wa, lru_ba, lru_wi, lru_bi, lru_lambda, q_norm, k_norm,
           idx_k_norm, mem_norm, w_mem_kv, mem_q_norm, mem_k_norm, w_out, norm_ffn, peer_wq, peer_subkeys,
           peer_u, peer_v):
    p = dict(norm_mix=norm_mix, w_in=w_in, conv_w=conv_w, conv_b=conv_b, lru_wa=lru_wa, lru_ba=lru_ba,
             lru_wi=lru_wi, lru_bi=lru_bi, lru_lambda=lru_lambda, q_norm=q_norm, k_norm=k_norm,
             idx_k_norm=idx_k_norm, mem_norm=mem_norm, w_mem_kv=w_mem_kv, mem_q_norm=mem_q_norm,
             mem_k_norm=mem_k_norm, w_out=w_out, norm_ffn=norm_ffn, peer_wq=peer_wq, peer_subkeys=peer_subkeys,
             peer_u=peer_u, peer_v=peer_v)
    for l in range(norm_mix.shape[0]):
        x = _layer(x, mem, p, l)
    return x
```

```python
import functools

import jax
import jax.numpy as jnp
from jax import lax
from jax.experimental import pallas as pl
from jax.experimental.pallas import tpu as pltpu

_F32 = jnp.float32
_BF = jnp.bfloat16

EPS = 1e-6
ROPE_THETA = 10000.0
LRU_C = 8.0
LRU_BLOCKS = 16
CONV_WIDTH = 4
ATT_HEADS = 8
ATT_KV_HEADS = 2
ATT_GROUP = ATT_HEADS // ATT_KV_HEADS
IDX_HEADS = 8
IDX_DIM = 64
TOPK_MAX = 256
MEM_HEADS = 4
PEER_HEADS = 8
PEER_KEYS = 128
PEER_TOPK = 16
TOP_ROWS = 24

LANES = 128
SUBLANES = 8
ONES_ROWS = 2 * SUBLANES
VMEM_LIMIT_BYTES = 56 * 1024 * 1024
MASK_VALUE = -1e30
FLT_MAX = 3.4028234663852886e38
INT_MIN = -(2 ** 31)


def _cparams(*sem):
    return pltpu.CompilerParams(dimension_semantics=sem, vmem_limit_bytes=VMEM_LIMIT_BYTES)


def _rms(x, g):
    return x * lax.rsqrt(jnp.mean(x * x, axis=-1, keepdims=True) + EPS) * g


GELU_C0 = 0.7978845608028654
GELU_C1 = GELU_C0 * 0.044715


def _gelu(x):
    return 0.5 * x * (1.0 + jnp.tanh(x * (GELU_C0 + GELU_C1 * (x * x))))


def _sigmoid(x):
    return 0.5 * jnp.tanh(0.5 * x) + 0.5


def _dot(a, b):
    return jnp.dot(a, b, preferred_element_type=_F32)


def _full_spec(shape):
    n = len(shape)
    return pl.BlockSpec(shape, lambda *_: (0,) * n)


def _lru_body(x_ref, nm_ref, wlx_ref, wlg_ref, wg0_ref, cw_ref, cb_ref, wa_ref, ba_ref, wi_ref,
              bi_ref, lam_ref, out_ref, buf_ref, hc_ref, a_ref, b_ref):
    ts = x_ref.shape[1]
    c_dim = out_ref.shape[2]

    @pl.when(pl.program_id(1) == 0)
    def _():
        buf_ref[0:SUBLANES, :] = jnp.zeros((SUBLANES, c_dim), _F32)
        hc_ref[...] = jnp.zeros_like(hc_ref)

    h = _rms(x_ref[0], nm_ref[...]).astype(_BF)
    lx = _dot(h, wlx_ref[...])
    buf_ref[SUBLANES:SUBLANES + ts, :] = lx
    ext = buf_ref[...]
    cw = cw_ref[...]
    xc = cb_ref[...] + cw[3:4] * lx
    for j in range(CONV_WIDTH - 1):
        d = CONV_WIDTH - 1 - j
        xc = xc + cw[j:j + 1] * pltpu.roll(ext, d, 0)[SUBLANES:, :]
    buf_ref[0:SUBLANES, :] = lx[ts - SUBLANES:, :]

    xcb = xc.astype(_BF)
    r = _sigmoid(_dot(xcb, wa_ref[...]) + ba_ref[...])
    ig = _sigmoid(_dot(xcb, wi_ref[...]) + bi_ref[...])
    lam = lam_ref[...]
    softplus_neg_lam = jnp.maximum(-lam, 0.0) + jnp.log1p(jnp.exp(-jnp.abs(lam)))
    log_a = (-LRU_C) * r * softplus_neg_lam
    a = jnp.exp(log_a)
    y = jnp.tanh(-log_a) * (a * a + 1.0)
    u = jnp.where(y > 0.0, y * lax.rsqrt(y), 0.0) * (ig * xc)

    a = a.reshape(ts // SUBLANES, SUBLANES, c_dim)
    u = u.reshape(ts // SUBLANES, SUBLANES, c_dim)
    rid = lax.broadcasted_iota(jnp.int32, a.shape, 1)
    for d in (1, 2, 4):
        ok = rid >= d
        a_s = pltpu.roll(a, d, 1)
        u_s = pltpu.roll(u, d, 1)
        u = jnp.where(ok, a * u_s + u, u)
        a = jnp.where(ok, a * a_s, a)
    a_ref[...] = a.reshape(ts, c_dim)
    b_ref[...] = u.reshape(ts, c_dim)

    def step(g, hp):
        r0 = pl.multiple_of(g * SUBLANES, SUBLANES)
        hcur = a_ref[pl.ds(r0, SUBLANES), :] * hp + b_ref[pl.ds(r0, SUBLANES), :]
        b_ref[pl.ds(r0, SUBLANES), :] = hcur
        return jnp.broadcast_to(hcur[SUBLANES - 1:SUBLANES, :], (SUBLANES, c_dim))

    hc_ref[...] = lax.fori_loop(0, ts // SUBLANES, step, hc_ref[...])
    gate = _gelu(_dot(h, wlg_ref[...])) * _sigmoid(_dot(h, wg0_ref[...]))
    out_ref[0] = (b_ref[...] * gate).astype(_BF)


def _lru_branch(x, nm, wlx, wlg, wg0, cw, cb, wa, ba, wi, bi, lam, ts):
    b, s, d = x.shape
    c = wlx.shape[1]
    row = lambda v: v.reshape(1, -1)
    args = (x, row(nm), wlx, wlg, wg0, cw, row(cb), wa, row(ba), wi, row(bi), row(lam))
    in_specs = [pl.BlockSpec((1, ts, d), lambda i, j: (i, j, 0))]
    in_specs += [_full_spec(a.shape) for a in args[1:]]
    return pl.pallas_call(
        _lru_body,
        grid=(b, s // ts),
        in_specs=in_specs,
        out_specs=pl.BlockSpec((1, ts, c), lambda i, j: (i, j, 0)),
        out_shape=jax.ShapeDtypeStruct((b, s, c), _BF),
        scratch_shapes=[pltpu.VMEM((ts + SUBLANES, c), _F32), pltpu.VMEM((SUBLANES, c), _F32),
                        pltpu.VMEM((ts, c), _F32), pltpu.VMEM((ts, c), _F32)],
        compiler_params=_cparams("arbitrary", "arbitrary"),
        name="lru",
    )(*args)


def _rope128(x, cos, sin):
    return x * cos + pltpu.roll(x, LANES // 2, 1) * sin


def _rope64(x, cos, sin, first_half):
    rot = jnp.where(first_half, pltpu.roll(x, LANES - IDX_DIM // 2, 1), pltpu.roll(x, IDX_DIM // 2, 1))
    return x * cos + rot * sin


def _dsa_proj_body(x_ref, nm_ref, wq_ref, wk_ref, wv_ref, wqi_ref, wki_ref, wwi_ref, wg1_ref,
                   qn_ref, kn_ref, ikn_ref, c128_ref, s128_ref, c64_ref, s64_ref,
                   qT_ref, k_ref, vT_ref, qiT_ref, ki_ref, wT_ref, g1_ref):
    ts = x_ref.shape[1]
    h = _rms(x_ref[0], nm_ref[...]).astype(_BF)
    c128, s128 = c128_ref[...], s128_ref[...]
    c64, s64 = c64_ref[...], s64_ref[...]
    first_half = (lax.broadcasted_iota(jnp.int32, (ts, LANES), 1) & (IDX_DIM - 1)) < IDX_DIM // 2

    def head_norm_rope(t, g, scale):
        outs = []
        for i in range(t.shape[1] // LANES):
            th = _rms(t[:, i * LANES:(i + 1) * LANES], g)
            outs.append(_rope128(th, c128, s128) * scale)
        return jnp.concatenate(outs, axis=1)

    q = head_norm_rope(_dot(h, wq_ref[...]), qn_ref[...], float(LANES) ** -0.5)
    qT_ref[0] = q.T.astype(_BF)
    k_ref[0] = head_norm_rope(_dot(h, wk_ref[...]), kn_ref[...], 1.0).astype(_BF)
    vT = _dot(h, wv_ref[...]).T.astype(_BF)
    for g in range(ATT_KV_HEADS):
        vT_ref[0, 0, g, :LANES, :] = vT[g * LANES:(g + 1) * LANES, :]
        vT_ref[0, 0, g, LANES:, :] = jnp.ones((ONES_ROWS, ts), _BF)

    qi = _dot(h, wqi_ref[...])
    qi = jnp.concatenate(
        [_rope64(qi[:, i * LANES:(i + 1) * LANES], c64, s64, first_half) for i in range(qi.shape[1] // LANES)],
        axis=1) * (float(IDX_DIM) ** -0.5)
    qiT_ref[0] = qi.T.astype(_BF)
    ki = _rope64(_rms(_dot(h, wki_ref[...]), ikn_ref[...]), c64, s64, first_half)
    ki_ref[0] = ki[:, :IDX_DIM].astype(_BF)
    wi = _dot(h, wwi_ref[...]) * (float(IDX_HEADS) ** -0.5)
    wT_ref[0] = wi.T[:IDX_HEADS, :]
    g1_ref[0] = _sigmoid(_dot(h, wg1_ref[...])).astype(_BF)


def _dsa_proj(x, nm, wq, wk, wv, wqi, wki2, wwi, wg1, qn, kn, ikn2, c128, s128, c64, s64, ts):
    b, s, d = x.shape
    row = lambda v: v.reshape(1, -1)
    nq, nk, nv, nqi = wq.shape[1], wk.shape[1], wv.shape[1], wqi.shape[1]
    args = (x, row(nm), wq, wk, wv, wqi, wki2, wwi, wg1, row(qn), row(kn), row(ikn2), c128, s128, c64, s64)
    tile = lambda n: pl.BlockSpec((1, ts, n), lambda i, j: (i, j, 0))
    tile_t = lambda n: pl.BlockSpec((1, n, ts), lambda i, j: (i, 0, j))
    tab = pl.BlockSpec((ts, LANES), lambda i, j: (j, 0))
    in_specs = [tile(d)] + [_full_spec(a.shape) for a in args[1:12]] + [tab] * 4
    out_shape = (
        jax.ShapeDtypeStruct((b, nq, s), _BF),
        jax.ShapeDtypeStruct((b, s, nk), _BF),
        jax.ShapeDtypeStruct((b, s // ts, ATT_KV_HEADS, LANES + ONES_ROWS, ts), _BF),
        jax.ShapeDtypeStruct((b, nqi, s), _BF),
        jax.ShapeDtypeStruct((b, s, IDX_DIM), _BF),
        jax.ShapeDtypeStruct((b, IDX_HEADS, s), _F32),
        jax.ShapeDtypeStruct((b, s, d), _BF),
    )
    out_specs = (tile_t(nq), tile(nk),
                 pl.BlockSpec((1, 1, ATT_KV_HEADS, LANES + ONES_ROWS, ts), lambda i, j: (i, j, 0, 0, 0)), tile_t(nqi),
                 tile(IDX_DIM), tile_t(IDX_HEADS), tile(d))
    return pl.pallas_call(
        _dsa_proj_body, grid=(b, s // ts), in_specs=in_specs, out_specs=out_specs, out_shape=out_shape,
        compiler_params=_cparams("arbitrary", "arbitrary"), name="dsa_proj",
    )(*args)


def _key_to_float(u):
    key = u ^ jnp.int32(INT_MIN)
    bits = key ^ ((key >> 31) & jnp.int32(0x7FFFFFFF))
    return jnp.where((u >> 23) == 0, -jnp.inf, lax.bitcast_convert_type(bits, _F32))


def _high_half(x):
    bits = lax.bitcast_convert_type(x, jnp.int32) & jnp.int32(-65536)
    return lax.bitcast_convert_type(bits, _F32).astype(_BF)


def _dsa_body(qT_ref, qiT_ref, wT_ref, k_ref, vT_ref, ki_ref, g1_ref, out_ref,
              sc_ref, sh_ref, m_ref, l_ref, acc_ref, bias_ref, s_ref, p_ref, *, topk, tk):
    tq = qT_ref.shape[2]
    dh = LANES
    q0 = pl.program_id(1) * tq
    n_chunks = (q0 + tq + tk - 1) // tk

    qicat = jnp.concatenate([qiT_ref[0, h * IDX_DIM:(h + 1) * IDX_DIM, :] for h in range(IDX_HEADS)], axis=1)
    w = wT_ref[0]
    qpos = q0 + lax.broadcasted_iota(jnp.int32, (tk, tq), 1)
    krow = lax.broadcasted_iota(jnp.int32, (tk, tq), 0)

    def score_chunk(c, carry):
        r0 = pl.multiple_of(c * tk, tk)
        lg = _dot(ki_ref[0, pl.ds(r0, tk), :], qicat)
        s = jnp.zeros((tk, tq), _F32)
        for h in range(IDX_HEADS):
            s = s + jnp.maximum(lg[:, h * tq:(h + 1) * tq], 0.0) * w[h:h + 1, :]
        s = jnp.where(krow + r0 <= qpos, s, -jnp.inf)
        sc_ref[pl.ds(r0, tk), :] = s
        sh_ref[pl.ds(r0, tk), :] = _high_half(s)
        return carry

    lax.fori_loop(0, n_chunks, score_chunk, 0)

    def make_count(nc, ref, rows, dtype):
        def count_ge(cand):
            one, zero = jnp.ones((), dtype), jnp.zeros((), dtype)
            parts = [jnp.zeros((rows, tq), dtype) for _ in range(4)]
            n = 0
            for c in range(nc):
                hit = jnp.where(ref[c * tk:(c + 1) * tk, :] >= cand, one, zero)
                for r in range(tk // rows):
                    parts[n % 4] = parts[n % 4] + hit[r * rows:(r + 1) * rows, :]
                    n += 1
            parts = [p.astype(_F32) for p in parts]
            return jnp.sum((parts[0] + parts[1]) + (parts[2] + parts[3]), axis=0, keepdims=True)
        return count_ge

    max_chunks = sc_ref.shape[0] // tk
    assert max_chunks * tk // (2 * SUBLANES) <= 4 * 64
    coarse = [make_count(nc, sh_ref, 2 * SUBLANES, _BF) for nc in range(1, max_chunks + 1)]
    fine = [make_count(nc, sc_ref, SUBLANES, _F32) for nc in range(1, max_chunks + 1)]

    def bisect_high(i, u):
        trial = u | lax.shift_left(jnp.int32(1), 31 - i)
        cnt = lax.switch(n_chunks - 1, coarse, _high_half(_key_to_float(trial)))
        return jnp.where(cnt >= float(topk), trial, u)

    def bisect_low(i, u):
        trial = u | lax.shift_left(jnp.int32(1), 31 - i)
        cnt = lax.switch(n_chunks - 1, fine, _key_to_float(trial))
        return jnp.where(cnt >= float(topk), trial, u)

    u = lax.fori_loop(0, 16, bisect_high, jnp.zeros((1, tq), jnp.int32))
    u = lax.fori_loop(16, 32, bisect_low, u)
    tau = jnp.maximum(_key_to_float(u), -FLT_MAX)

    n_ge = lax.switch(n_chunks - 1, fine, tau)

    @pl.when(jnp.max(n_ge) > float(topk))
    def _():
        def count_gt(c, cnt):
            r0 = pl.multiple_of(c * tk, tk)
            return cnt + jnp.sum(jnp.where(sc_ref[pl.ds(r0, tk), :] > tau, 1.0, 0.0), axis=0, keepdims=True)

        need = float(topk) - lax.fori_loop(0, n_chunks, count_gt, jnp.zeros((1, tq), _F32))
        before = (lax.broadcasted_iota(jnp.int32, (tk, tk), 1)
                  < lax.broadcasted_iota(jnp.int32, (tk, tk), 0)).astype(_BF)

        def strike(c, seen):
            r0 = pl.multiple_of(c * tk, tk)
            blk = sc_ref[pl.ds(r0, tk), :]
            tied = blk == tau
            tied_b = jnp.where(tied, 1.0, 0.0).astype(_BF)
            earlier = _dot(before, tied_b) + seen
            sc_ref[pl.ds(r0, tk), :] = jnp.where(tied & (earlier >= need), -jnp.inf, blk)
            return seen + jnp.sum(tied_b.astype(_F32), axis=0, keepdims=True)

        lax.fori_loop(0, n_chunks, strike, jnp.zeros((1, tq), _F32))

    m_ref[...] = jnp.full(m_ref.shape, MASK_VALUE, _F32)
    l_ref[...] = jnp.zeros_like(l_ref)
    acc_ref[...] = jnp.zeros_like(acc_ref)

    sub = 8 * SUBLANES
    groups = [slice(r * SUBLANES, (r + 1) * SUBLANES) for r in range(sub // SUBLANES)]

    def att_chunk(c, carry):
        r0 = pl.multiple_of(c * tk, tk)
        bias_ref[...] = jnp.where(sc_ref[pl.ds(r0, tk), :] >= tau, 0.0, MASK_VALUE)

        def logits(h):
            g = h // ATT_GROUP
            s_ref[h] = _dot(k_ref[0, pl.ds(r0, tk), g * dh:(g + 1) * dh], qT_ref[0, h * dh:(h + 1) * dh, :])

        logits(0)
        logits(1)
        for h in range(ATT_HEADS):
            if h + 2 < ATT_HEADS:
                logits(h + 2)
            m_old = m_ref[h:h + 1, :]
            mx = jnp.broadcast_to(m_old, (SUBLANES, tq))
            for i in range(tk // sub):
                rows = slice(i * sub, (i + 1) * sub)
                sb = s_ref[h, rows, :] + bias_ref[rows, :]
                s_ref[h, rows, :] = sb
                for gr in groups:
                    mx = jnp.maximum(mx, sb[gr, :])
            m_new = jnp.max(mx, axis=0, keepdims=True)
            for i in range(tk // sub):
                rows = slice(i * sub, (i + 1) * sub)
                p_ref[h, rows, :] = jnp.exp((s_ref[h, rows, :] - m_new).astype(_BF))
            alpha = jnp.exp(m_old - m_new)
            pv = _dot(vT_ref[0, c, h // ATT_GROUP], p_ref[h])
            l_ref[h:h + 1, :] = alpha * l_ref[h:h + 1, :] + pv[dh:dh + 1, :]
            acc_ref[h * dh:(h + 1) * dh, :] = alpha * acc_ref[h * dh:(h + 1) * dh, :] + pv[:dh, :]
            m_ref[h:h + 1, :] = m_new
        return carry

    lax.fori_loop(0, n_chunks, att_chunk, 0)
    outs = []
    for h in range(ATT_HEADS):
        o = acc_ref[h * dh:(h + 1) * dh, :] / l_ref[h:h + 1, :]
        outs.append(o.T)
    out_ref[0] = (jnp.concatenate(outs, axis=1) * g1_ref[0].astype(_F32)).astype(_BF)


def _dsa(qT, qiT, wT, k, vT, ki, g1, topk, tq, tk):
    b, nq, s = qT.shape
    d = g1.shape[2]
    in_specs = [
        pl.BlockSpec((1, nq, tq), lambda i, j: (i, 0, j)),
        pl.BlockSpec((1, qiT.shape[1], tq), lambda i, j: (i, 0, j)),
        pl.BlockSpec((1, IDX_HEADS, tq), lambda i, j: (i, 0, j)),
        pl.BlockSpec((1, s, k.shape[2]), lambda i, j: (i, 0, 0)),
        pl.BlockSpec((1,) + vT.shape[1:], lambda i, j: (i, 0, 0, 0, 0)),
        pl.BlockSpec((1, s, IDX_DIM), lambda i, j: (i, 0, 0)),
        pl.BlockSpec((1, tq, d), lambda i, j: (i, j, 0)),
    ]
    return pl.pallas_call(
        functools.partial(_dsa_body, topk=topk, tk=tk),
        grid=(b, s // tq), in_specs=in_specs,
        out_specs=pl.BlockSpec((1, tq, d), lambda i, j: (i, j, 0)),
        out_shape=jax.ShapeDtypeStruct((b, s, d), _BF),
        scratch_shapes=[pltpu.VMEM((s, tq), _F32), pltpu.VMEM((s, tq), _BF), pltpu.VMEM((ATT_HEADS, tq), _F32),
                        pltpu.VMEM((ATT_HEADS, tq), _F32), pltpu.VMEM((nq, tq), _F32),
                        pltpu.VMEM((tk, tq), _F32), pltpu.VMEM((ATT_HEADS, tk, tq), _F32),
                        pltpu.VMEM((ATT_HEADS, tk, tq), _BF)],
        compiler_params=_cparams("arbitrary", "arbitrary"), name="dsa",
    )(qT, qiT, wT, k, vT, ki, g1)


def _mem_kv_body(mem_ref, mn_ref, wkv_ref, mkn_ref, mkT_ref, mv_ref):
    d = mem_ref.shape[2]
    dm = d // MEM_HEADS
    m = _rms(mem_ref[0], mn_ref[...]).astype(_BF)
    kv = _dot(m, wkv_ref[...])
    mk = jnp.concatenate([_rms(kv[:, i * dm:(i + 1) * dm], mkn_ref[...]) for i in range(MEM_HEADS)], axis=1)
    mkT_ref[0] = mk.T.astype(_BF)
    mv_ref[0] = kv[:, d:].astype(_BF)


def _mem_kv(mem, mn, wkv, mkn):
    b, m, d = mem.shape
    row = lambda v: v.reshape(1, -1)
    return pl.pallas_call(
        _mem_kv_body, grid=(b,),
        in_specs=[pl.BlockSpec((1, m, d), lambda i: (i, 0, 0)), _full_spec((1, d)), _full_spec(wkv.shape),
                  _full_spec((1, d // MEM_HEADS))],
        out_specs=(pl.BlockSpec((1, d, m), lambda i: (i, 0, 0)), pl.BlockSpec((1, m, d), lambda i: (i, 0, 0))),
        out_shape=(jax.ShapeDtypeStruct((b, d, m), _BF), jax.ShapeDtypeStruct((b, m, d), _BF)),
        compiler_params=_cparams("arbitrary"), name="mem_kv",
    )(mem, row(mn), wkv, row(mkn))


def _mem_body(x_ref, nm_ref, wmq_ref, wg2_ref, mqn_ref, mkT_ref, mv_ref, out_ref):
    d = x_ref.shape[2]
    dm = d // MEM_HEADS
    h = _rms(x_ref[0], nm_ref[...]).astype(_BF)
    mq = _dot(h, wmq_ref[...])
    outs = []
    for i in range(MEM_HEADS):
        qh = (_rms(mq[:, i * dm:(i + 1) * dm], mqn_ref[...]) * (float(dm) ** -0.5)).astype(_BF)
        s = _dot(qh, mkT_ref[0, i * dm:(i + 1) * dm, :])
        p = jnp.exp(s - jnp.max(s, axis=-1, keepdims=True))
        o = _dot(p.astype(_BF), mv_ref[0, :, i * dm:(i + 1) * dm])
        outs.append(o / jnp.sum(p, axis=-1, keepdims=True))
    out_ref[0] = (jnp.concatenate(outs, axis=1) * _sigmoid(_dot(h, wg2_ref[...]))).astype(_BF)


def _mem_branch(x, nm, wmq, wg2, mqn, mkT, mv, ts):
    b, s, d = x.shape
    m = mv.shape[1]
    row = lambda v: v.reshape(1, -1)
    return pl.pallas_call(
        _mem_body, grid=(b, s // ts),
        in_specs=[pl.BlockSpec((1, ts, d), lambda i, j: (i, j, 0)), _full_spec((1, d)), _full_spec(wmq.shape),
                  _full_spec(wg2.shape), _full_spec((1, d // MEM_HEADS)),
                  pl.BlockSpec((1, d, m), lambda i, j: (i, 0, 0)), pl.BlockSpec((1, m, d), lambda i, j: (i, 0, 0))],
        out_specs=pl.BlockSpec((1, ts, d), lambda i, j: (i, j, 0)),
        out_shape=jax.ShapeDtypeStruct((b, s, d), _BF),
        compiler_params=_cparams("arbitrary", "arbitrary"), name="mem",
    )(x, row(nm), wmq, wg2, row(mqn), mkT, mv)


def _mid_body(x_ref, a_ref, b_ref, c_ref, wo_ref, nf_ref, wpqT_ref, sk_ref, x1_ref, h2T_ref, scT_ref):
    merged = (a_ref[...].astype(_F32) + b_ref[...].astype(_F32) + c_ref[...].astype(_F32)).astype(_BF)
    x1 = x_ref[...] + _dot(merged, wo_ref[...])
    x1_ref[...] = x1
    h2T = _rms(x1, nf_ref[...]).T.astype(_BF)
    h2T_ref[...] = h2T
    pqT = _dot(wpqT_ref[...], h2T).astype(_BF)
    for c in range(sk_ref.shape[0]):
        scT_ref[c * PEER_KEYS:(c + 1) * PEER_KEYS, :] = _dot(sk_ref[c], pqT[c * LANES:(c + 1) * LANES, :])


def _mid(x2, ma, mb, mc, wo, nf, wpqT, sk, ts):
    t, d = x2.shape
    nsc = sk.shape[0] * PEER_KEYS
    tile = pl.BlockSpec((ts, d), lambda i: (i, 0))
    return pl.pallas_call(
        _mid_body, grid=(t // ts,),
        in_specs=[tile, tile, tile, tile, _full_spec(wo.shape), _full_spec((1, d)), _full_spec(wpqT.shape),
                  _full_spec(sk.shape)],
        out_specs=(tile, pl.BlockSpec((d, ts), lambda i: (0, i)), pl.BlockSpec((nsc, ts), lambda i: (0, i))),
        out_shape=(jax.ShapeDtypeStruct((t, d), _F32), jax.ShapeDtypeStruct((d, t), _BF),
                   jax.ShapeDtypeStruct((nsc, t), _F32)),
        compiler_params=_cparams("arbitrary"), name="mid",
    )(x2, ma, mb, mc, wo, nf.reshape(1, -1), wpqT, sk)


def _top_values(s, n):
    groups = [s[r * SUBLANES:(r + 1) * SUBLANES, :] for r in range(s.shape[0] // SUBLANES)]
    size = pl.next_power_of_2(len(groups))
    groups += [jnp.full_like(groups[0], -jnp.inf)] * (size - len(groups))
    k = 2
    while k <= size:
        j = k // 2
        while j >= 1:
            for i in range(size):
                l = i ^ j
                if l > i:
                    hi, lo = jnp.maximum(groups[i], groups[l]), jnp.minimum(groups[i], groups[l])
                    groups[i], groups[l] = (hi, lo) if (i & k) == 0 else (lo, hi)
            j //= 2
        k *= 2
    vals = []
    for it in range(n):
        m = jnp.max(groups[0], axis=0, keepdims=True)
        vals.append(m)
        remaining = n - 1 - it
        head = groups[0] == m
        for d in range(min(remaining, size)):
            below = groups[d + 1] if d + 1 < size else jnp.full_like(m, -jnp.inf)
            groups[d] = jnp.where(head, below, groups[d])
    return vals


def _route_body(scT_ref, tau_ref, c_ref, top2_ref):
    def head(h, carry):
        r1 = pl.multiple_of(h * 2 * PEER_KEYS, PEER_KEYS)
        r2 = pl.multiple_of(h * 2 * PEER_KEYS + PEER_KEYS, PEER_KEYS)
        n = PEER_TOPK + 1
        v1 = _top_values(scT_ref[pl.ds(r1, PEER_KEYS), :], n)
        v2 = _top_values(scT_ref[pl.ds(r2, PEER_KEYS), :], n)
        edge = SUBLANES
        assert 2 * (edge + 1) > n
        pad = [jnp.full_like(v1[0], -jnp.inf)] * (-n % SUBLANES)
        col1 = jnp.concatenate(v1 + pad, axis=0)
        col2 = jnp.concatenate(v2 + pad, axis=0)
        cand = jnp.concatenate([v1[a] + col2[:edge] for a in range(edge)]
                               + [v1[0] + col2[edge:], v2[0] + col1[edge:]], axis=0)
        best = _top_values(cand, n)
        tau = best[PEER_TOPK - 1]
        top = v1[0] + v2[0]
        z = jnp.sum(jnp.where(cand >= tau, jnp.exp(cand - top), 0.0), axis=0, keepdims=True)
        tau_ref[pl.ds(h, 1), :] = 0.5 * (tau + best[PEER_TOPK])
        c_ref[pl.ds(h, 1), :] = top + jnp.log(z)
        top2_ref[pl.ds(pl.multiple_of(h * TOP_ROWS, SUBLANES), TOP_ROWS), :] = col2
        return carry

    lax.fori_loop(0, PEER_HEADS, head, 0, unroll=2)


def _route(scT, te):
    nsc, t = scT.shape
    spec = pl.BlockSpec((PEER_HEADS, te), lambda i: (0, i))
    return pl.pallas_call(
        _route_body, grid=(t // te,),
        in_specs=[pl.BlockSpec((nsc, te), lambda i: (0, i))],
        out_specs=(spec, spec, pl.BlockSpec((PEER_HEADS * TOP_ROWS, te), lambda i: (0, i))),
        out_shape=(jax.ShapeDtypeStruct((PEER_HEADS, t), _F32),) * 2
        + (jax.ShapeDtypeStruct((PEER_HEADS * TOP_ROWS, t), _F32),),
        compiler_params=_cparams("arbitrary"), name="route",
    )(scT)


def _experts_body(h2T_ref, scT_ref, tau_ref, c_ref, top2_ref, u_ref, vT_ref, x1_ref, out_ref,
                  acc_ref, e2_ref, rank_ref, cnt_ref, scale_ref, act_ref, g_ref, rows_ref, *, rows_per_step,
                  n_blocks):
    e = pl.program_id(1)
    nk = PEER_KEYS
    ts = act_ref.shape[1]

    @pl.when(e == 0)
    def _():
        acc_ref[...] = jnp.zeros_like(acc_ref)
        for h in range(PEER_HEADS):
            best = [top2_ref[h * TOP_ROWS + b:h * TOP_ROWS + b + 1, :] for b in range(PEER_TOPK + 1)]
            s1 = scT_ref[2 * h * nk:(2 * h + 1) * nk, :]
            s2 = scT_ref[(2 * h + 1) * nk:(2 * h + 2) * nk, :]
            e2_ref[h * nk:(h + 1) * nk, :] = jnp.exp(s2 - best[0]).astype(_BF)
            rank = jnp.zeros((nk, ts), _F32)
            for b in range(PEER_TOPK):
                rank = jnp.where(best[b] > s2, float(b + 1), rank)
            rank_ref[h * nk:(h + 1) * nk, :] = rank.astype(_BF)
            thr = tau_ref[h:h + 1, :] - s1
            cnt = jnp.zeros((nk, ts), _F32)
            for b in range(PEER_TOPK + 1):
                cnt = jnp.where(best[b] >= thr, float(b + 1), cnt)
            cnt_ref[h * nk:(h + 1) * nk, :] = cnt
            scale_ref[h * nk:(h + 1) * nk, :] = 0.5 * jnp.exp(s1 + best[0] - c_ref[h:h + 1, :])

    n_rows = rows_per_step * PEER_HEADS
    first_group = 2
    zero = jnp.zeros((), _BF)

    def first_matmul(i):
        rows = slice(i * first_group * nk, (i + 1) * first_group * nk)
        act_ref[rows, :] = _dot(u_ref[rows, :], h2T_ref[...])

    for i in range(rows_per_step):
        for h in range(PEER_HEADS):
            src = h * nk + e * rows_per_step + i
            r = i * PEER_HEADS + h
            rows_ref[r:r + 1, :] = cnt_ref[pl.ds(src, 1), :]
            rows_ref[n_rows + r:n_rows + r + 1, :] = scale_ref[pl.ds(src, 1), :]

    second_group = 4
    first_matmul(0)
    for i in range(rows_per_step):
        if i % first_group == 0 and i // first_group + 1 < rows_per_step // first_group:
            first_matmul(i // first_group + 1)
        wgt = jnp.zeros((nk, ts), _BF)
        for h in range(PEER_HEADS):
            r = i * PEER_HEADS + h
            cnt = rows_ref[r:r + 1, :].astype(_BF)
            scale = rows_ref[n_rows + r:n_rows + r + 1, :].astype(_BF)
            hit = jnp.where(rank_ref[h * nk:(h + 1) * nk, :] < cnt, e2_ref[h * nk:(h + 1) * nk, :], zero)
            wgt = wgt + hit * scale
        x = act_ref[i * nk:(i + 1) * nk, :].astype(_BF)
        t = jnp.tanh(x * (jnp.asarray(GELU_C0, _BF) + jnp.asarray(GELU_C1, _BF) * (x * x)))
        g_ref[i * nk:(i + 1) * nk, :] = (x + x * t) * wgt
        if i % second_group == second_group - 1:
            cols = slice((i + 1 - second_group) * nk, (i + 1) * nk)
            acc_ref[...] += _dot(vT_ref[0, :, cols], g_ref[cols, :])

    @pl.when(e == n_blocks - 1)
    def _():
        out_ref[...] = x1_ref[...] + acc_ref[...].T


def _experts(h2T, scT, tau, cn, top2, u, vT, x1, ts, rows_per_step):
    d, t = h2T.shape
    nsc = scT.shape[0]
    ne = u.shape[0]
    eb = rows_per_step * PEER_KEYS
    nb = ne // eb
    nhk = PEER_HEADS * PEER_KEYS
    col = lambda n: pl.BlockSpec((n, ts), lambda i, e: (0, i))
    tile = pl.BlockSpec((ts, d), lambda i, e: (i, 0))
    return pl.pallas_call(
        functools.partial(_experts_body, rows_per_step=rows_per_step, n_blocks=nb),
        grid=(t // ts, nb),
        in_specs=[col(d), col(nsc), col(PEER_HEADS), col(PEER_HEADS), col(top2.shape[0]),
                  pl.BlockSpec((eb, d), lambda i, e: (e, 0)), pl.BlockSpec((1, d, eb), lambda i, e: (e, 0, 0)), tile],
        out_specs=tile,
        out_shape=jax.ShapeDtypeStruct((t, d), _F32),
        scratch_shapes=[pltpu.VMEM((d, ts), _F32), pltpu.VMEM((nhk, ts), _BF), pltpu.VMEM((nhk, ts), _BF),
                        pltpu.VMEM((nhk, ts), _F32), pltpu.VMEM((nhk, ts), _F32),
                        pltpu.VMEM((eb, ts), _F32), pltpu.VMEM((eb, ts), _BF),
                        pltpu.VMEM((2 * rows_per_step * PEER_HEADS, ts), _F32)],
        compiler_params=_cparams("arbitrary", "arbitrary"), name="experts",
    )(h2T, scT, tau, cn, top2, u, vT, x1)


def _block_diag(w):
    n, c, d = w.shape
    return jnp.einsum("ncd,nm->ncmd", w, jnp.eye(n, dtype=w.dtype)).reshape(n * c, n * d)


def _rope_tables(s, dim, reps):
    half = dim // 2
    freq = ROPE_THETA ** (-jnp.arange(half, dtype=_F32) / half)
    ang = jnp.arange(s, dtype=_F32)[:, None] * freq[None, :]
    cos = jnp.tile(jnp.concatenate([jnp.cos(ang), jnp.cos(ang)], axis=1), (1, reps))
    sin = jnp.tile(jnp.concatenate([-jnp.sin(ang), jnp.sin(ang)], axis=1), (1, reps))
    return cos, sin


def _tile(n, pref):
    return pref if n % pref == 0 else n


def _layer(x, mem, p, l):
    b, s, d = x.shape
    bf = lambda a: a.astype(_BF)
    w_in = p["w_in"][l]
    n_q, n_kv = ATT_HEADS * LANES, ATT_KV_HEADS * LANES
    n_qi = IDX_HEADS * IDX_DIM
    sizes = (d, d, n_q, n_kv, n_kv, n_qi, IDX_DIM, IDX_HEADS, d, 3 * d)
    offs = [0]
    for n in sizes:
        offs.append(offs[-1] + n)
    wlx, wlg, wq, wk, wv, wqi, wki, wwi, wmq, wgates = (w_in[:, offs[i]:offs[i + 1]] for i in range(len(sizes)))
    wki2 = jnp.concatenate([wki, wki], axis=1)
    wwi_p = jnp.pad(wwi, ((0, 0), (0, LANES - IDX_HEADS)))
    wg0, wg1, wg2 = wgates[:, :d], wgates[:, d:2 * d], wgates[:, 2 * d:]
    nm = p["norm_mix"][l]

    ts_a = _tile(s, 512)
    m_lru = _lru_branch(x, nm, bf(wlx), bf(wlg), bf(wg0), p["conv_w"][l], p["conv_b"][l],
                        bf(_block_diag(p["lru_wa"][l])), p["lru_ba"][l].reshape(-1),
                        bf(_block_diag(p["lru_wi"][l])), p["lru_bi"][l].reshape(-1), p["lru_lambda"][l], ts_a)

    tk = _tile(s, 512)
    c128, s128 = _rope_tables(s, LANES, 1)
    c64, s64 = _rope_tables(s, IDX_DIM, LANES // IDX_DIM)
    ikn2 = jnp.concatenate([p["idx_k_norm"][l]] * 2)
    qT, k, vT, qiT, ki, wT, g1 = _dsa_proj(x, nm, bf(wq), bf(wk), bf(wv), bf(wqi), bf(wki2), bf(wwi_p), bf(wg1),
                                           p["q_norm"][l], p["k_norm"][l], ikn2, c128, s128, c64, s64, tk)
    m_att = _dsa(qT, qiT, wT, k, vT, ki, g1, min(TOPK_MAX, s // 4), _tile(s, 256), tk)

    mkT, mv = _mem_kv(mem, p["mem_norm"][l], bf(p["w_mem_kv"][l]), p["mem_k_norm"][l])
    m_mem = _mem_branch(x, nm, bf(wmq), bf(wg2), p["mem_q_norm"][l], mkT, mv, _tile(s, 512))

    t = b * s
    flat = lambda a: a.reshape(t, d)
    sk = bf(p["peer_subkeys"][l].reshape(2 * PEER_HEADS, PEER_KEYS, -1))
    x1, h2T, scT = _mid(flat(x), flat(m_lru), flat(m_att), flat(m_mem), bf(p["w_out"][l]), p["norm_ffn"][l],
                        bf(p["peer_wq"][l].T), sk, _tile(t, 512))
    tau, cn, top2 = _route(scT, _tile(t, 256))
    rows_per_step = 16
    vT = bf(p["peer_v"][l]).reshape(-1, rows_per_step * PEER_KEYS, d).transpose(0, 2, 1)
    out = _experts(h2T, scT, tau, cn, top2, bf(p["peer_u"][l]), vT, x1, _tile(t, 512), rows_per_step)
    return out.reshape(b, s, d)


def kernel(x, mem, norm_mix, w_in, conv_w, conv_b, lru_wa, lru_ba, lru_wi, lru_bi, lru_lambda, q_norm, k_norm,
           idx_k_norm, mem_norm, w_mem_kv, mem_q_norm, mem_k_norm, w_out, norm_ffn, peer_wq, peer_subkeys,
           peer_u, peer_v):
    p = dict(norm_mix=norm_mix, w_in=w_in, conv_w=conv_w, conv_b=conv_b, lru_wa=lru_wa, lru_ba=lru_ba,
             lru_wi=lru_wi, lru_bi=lru_bi, lru_lambda=lru_lambda, q_norm=q_norm, k_norm=k_norm,
             idx_k_norm=idx_k_norm, mem_norm=mem_norm, w_mem_kv=w_mem_kv, mem_q_norm=mem_q_norm,
             mem_k_norm=mem_k_norm, w_out=w_out, norm_ffn=norm_ffn, peer_wq=peer_wq, peer_subkeys=peer_subkeys,
             peer_u=peer_u, peer_v=peer_v)
    for l in range(norm_mix.shape[0]):
        x = _layer(x, mem, p, l)
    return x
```

```python
import functools

import jax
import jax.numpy as jnp
from jax import lax
from jax.experimental import pallas as pl
from jax.experimental.pallas import tpu as pltpu

_F32 = jnp.float32
_BF = jnp.bfloat16

EPS = 1e-6
ROPE_THETA = 10000.0
LRU_C = 8.0
LRU_BLOCKS = 16
CONV_WIDTH = 4
ATT_HEADS = 8
ATT_KV_HEADS = 2
ATT_GROUP = ATT_HEADS // ATT_KV_HEADS
IDX_HEADS = 8
IDX_DIM = 64
TOPK_MAX = 256
MEM_HEADS = 4
PEER_HEADS = 8
PEER_KEYS = 128
PEER_TOPK = 16
TOP_ROWS = 24

LANES = 128
SUBLANES = 8
ONES_ROWS = 2 * SUBLANES
VMEM_LIMIT_BYTES = 56 * 1024 * 1024
MASK_VALUE = -1e30
FLT_MAX = 3.4028234663852886e38
INT_MIN = -(2 ** 31)


def _cparams(*sem):
    return pltpu.CompilerParams(dimension_semantics=sem, vmem_limit_bytes=VMEM_LIMIT_BYTES)


def _rms(x, g):
    return x * lax.rsqrt(jnp.mean(x * x, axis=-1, keepdims=True) + EPS) * g


GELU_C0 = 0.7978845608028654
GELU_C1 = GELU_C0 * 0.044715


def _gelu(x):
    return 0.5 * x * (1.0 + jnp.tanh(x * (GELU_C0 + GELU_C1 * (x * x))))


def _sigmoid(x):
    return 0.5 * jnp.tanh(0.5 * x) + 0.5


def _dot(a, b):
    return jnp.dot(a, b, preferred_element_type=_F32)


def _full_spec(shape):
    n = len(shape)
    return pl.BlockSpec(shape, lambda *_: (0,) * n)


def _lru_body(x_ref, nm_ref, wlx_ref, wlg_ref, wg0_ref, cw_ref, cb_ref, wa_ref, ba_ref, wi_ref,
              bi_ref, lam_ref, out_ref, buf_ref, hc_ref, a_ref, b_ref):
    ts = x_ref.shape[1]
    c_dim = out_ref.shape[2]

    @pl.when(pl.program_id(1) == 0)
    def _():
        buf_ref[0:SUBLANES, :] = jnp.zeros((SUBLANES, c_dim), _F32)
        hc_ref[...] = jnp.zeros_like(hc_ref)

    h = _rms(x_ref[0], nm_ref[...]).astype(_BF)
    lx = _dot(h, wlx_ref[...])
    buf_ref[SUBLANES:SUBLANES + ts, :] = lx
    ext = buf_ref[...]
    cw = cw_ref[...]
    xc = cb_ref[...] + cw[3:4] * lx
    for j in range(CONV_WIDTH - 1):
        d = CONV_WIDTH - 1 - j
        xc = xc + cw[j:j + 1] * pltpu.roll(ext, d, 0)[SUBLANES:, :]
    buf_ref[0:SUBLANES, :] = lx[ts - SUBLANES:, :]

    xcb = xc.astype(_BF)
    r = _sigmoid(_dot(xcb, wa_ref[...]) + ba_ref[...])
    ig = _sigmoid(_dot(xcb, wi_ref[...]) + bi_ref[...])
    lam = lam_ref[...]
    softplus_neg_lam = jnp.maximum(-lam, 0.0) + jnp.log1p(jnp.exp(-jnp.abs(lam)))
    log_a = (-LRU_C) * r * softplus_neg_lam
    a = jnp.exp(log_a)
    y = jnp.tanh(-log_a) * (a * a + 1.0)
    u = jnp.where(y > 0.0, y * lax.rsqrt(y), 0.0) * (ig * xc)

    a = a.reshape(ts // SUBLANES, SUBLANES, c_dim)
    u = u.reshape(ts // SUBLANES, SUBLANES, c_dim)
    rid = lax.broadcasted_iota(jnp.int32, a.shape, 1)
    for d in (1, 2, 4):
        ok = rid >= d
        a_s = pltpu.roll(a, d, 1)
        u_s = pltpu.roll(u, d, 1)
        u = jnp.where(ok, a * u_s + u, u)
        a = jnp.where(ok, a * a_s, a)
    a_ref[...] = a.reshape(ts, c_dim)
    b_ref[...] = u.reshape(ts, c_dim)

    def step(g, hp):
        r0 = pl.multiple_of(g * SUBLANES, SUBLANES)
        hcur = a_ref[pl.ds(r0, SUBLANES), :] * hp + b_ref[pl.ds(r0, SUBLANES), :]
        b_ref[pl.ds(r0, SUBLANES), :] = hcur
        return jnp.broadcast_to(hcur[SUBLANES - 1:SUBLANES, :], (SUBLANES, c_dim))

    hc_ref[...] = lax.fori_loop(0, ts // SUBLANES, step, hc_ref[...])
    gate = _gelu(_dot(h, wlg_ref[...])) * _sigmoid(_dot(h, wg0_ref[...]))
    out_ref[0] = (b_ref[...] * gate).astype(_BF)


def _lru_branch(x, nm, wlx, wlg, wg0, cw, cb, wa, ba, wi, bi, lam, ts):
    b, s, d = x.shape
    c = wlx.shape[1]
    row = lambda v: v.reshape(1, -1)
    args = (x, row(nm), wlx, wlg, wg0, cw, row(cb), wa, row(ba), wi, row(bi), row(lam))
    in_specs = [pl.BlockSpec((1, ts, d), lambda i, j: (i, j, 0))]
    in_specs += [_full_spec(a.shape) for a in args[1:]]
    return pl.pallas_call(
        _lru_body,
        grid=(b, s // ts),
        in_specs=in_specs,
        out_specs=pl.BlockSpec((1, ts, c), lambda i, j: (i, j, 0)),
        out_shape=jax.ShapeDtypeStruct((b, s, c), _BF),
        scratch_shapes=[pltpu.VMEM((ts + SUBLANES, c), _F32), pltpu.VMEM((SUBLANES, c), _F32),
                        pltpu.VMEM((ts, c), _F32), pltpu.VMEM((ts, c), _F32)],
        compiler_params=_cparams("arbitrary", "arbitrary"),
        name="lru",
    )(*args)


def _rope128(x, cos, sin):
    return x * cos + pltpu.roll(x, LANES // 2, 1) * sin


def _rope64(x, cos, sin, first_half):
    rot = jnp.where(first_half, pltpu.roll(x, LANES - IDX_DIM // 2, 1), pltpu.roll(x, IDX_DIM // 2, 1))
    return x * cos + rot * sin


def _dsa_proj_body(x_ref, nm_ref, wq_ref, wk_ref, wv_ref, wqi_ref, wki_ref, wwi_ref, wg1_ref,
                   qn_ref, kn_ref, ikn_ref, c128_ref, s128_ref, c64_ref, s64_ref,
                   qT_ref, k_ref, vT_ref, qiT_ref, ki_ref, wT_ref, g1_ref):
    ts = x_ref.shape[1]
    h = _rms(x_ref[0], nm_ref[...]).astype(_BF)
    c128, s128 = c128_ref[...], s128_ref[...]
    c64, s64 = c64_ref[...], s64_ref[...]
    first_half = (lax.broadcasted_iota(jnp.int32, (ts, LANES), 1) & (IDX_DIM - 1)) < IDX_DIM // 2

    def head_norm_rope(t, g, scale):
        outs = []
        for i in range(t.shape[1] // LANES):
            th = _rms(t[:, i * LANES:(i + 1) * LANES], g)
            outs.append(_rope128(th, c128, s128) * scale)
        return jnp.concatenate(outs, axis=1)

    q = head_norm_rope(_dot(h, wq_ref[...]), qn_ref[...], float(LANES) ** -0.5)
    qT_ref[0] = q.T.astype(_BF)
    k_ref[0] = head_norm_rope(_dot(h, wk_ref[...]), kn_ref[...], 1.0).astype(_BF)
    vT = _dot(h, wv_ref[...]).T.astype(_BF)
    for g in range(ATT_KV_HEADS):
        vT_ref[0, 0, g, :LANES, :] = vT[g * LANES:(g + 1) * LANES, :]
        vT_ref[0, 0, g, LANES:, :] = jnp.ones((ONES_ROWS, ts), _BF)

    qi = _dot(h, wqi_ref[...])
    qi = jnp.concatenate(
        [_rope64(qi[:, i * LANES:(i + 1) * LANES], c64, s64, first_half) for i in range(qi.shape[1] // LANES)],
        axis=1) * (float(IDX_DIM) ** -0.5)
    qiT_ref[0] = qi.T.astype(_BF)
    ki = _rope64(_rms(_dot(h, wki_ref[...]), ikn_ref[...]), c64, s64, first_half)
    ki_ref[0] = ki[:, :IDX_DIM].astype(_BF)
    wi = _dot(h, wwi_ref[...]) * (float(IDX_HEADS) ** -0.5)
    wT_ref[0] = wi.T[:IDX_HEADS, :]
    g1_ref[0] = _sigmoid(_dot(h, wg1_ref[...])).astype(_BF)


def _dsa_proj(x, nm, wq, wk, wv, wqi, wki2, wwi, wg1, qn, kn, ikn2, c128, s128, c64, s64, ts):
    b, s, d = x.shape
    row = lambda v: v.reshape(1, -1)
    nq, nk, nv, nqi = wq.shape[1], wk.shape[1], wv.shape[1], wqi.shape[1]
    args = (x, row(nm), wq, wk, wv, wqi, wki2, wwi, wg1, row(qn), row(kn), row(ikn2), c128, s128, c64, s64)
    tile = lambda n: pl.BlockSpec((1, ts, n), lambda i, j: (i, j, 0))
    tile_t = lambda n: pl.BlockSpec((1, n, ts), lambda i, j: (i, 0, j))
    tab = pl.BlockSpec((ts, LANES), lambda i, j: (j, 0))
    in_specs = [tile(d)] + [_full_spec(a.shape) for a in args[1:12]] + [tab] * 4
    out_shape = (
        jax.ShapeDtypeStruct((b, nq, s), _BF),
        jax.ShapeDtypeStruct((b, s, nk), _BF),
        jax.ShapeDtypeStruct((b, s // ts, ATT_KV_HEADS, LANES + ONES_ROWS, ts), _BF),
        jax.ShapeDtypeStruct((b, nqi, s), _BF),
        jax.ShapeDtypeStruct((b, s, IDX_DIM), _BF),
        jax.ShapeDtypeStruct((b, IDX_HEADS, s), _F32),
        jax.ShapeDtypeStruct((b, s, d), _BF),
    )
    out_specs = (tile_t(nq), tile(nk),
                 pl.BlockSpec((1, 1, ATT_KV_HEADS, LANES + ONES_ROWS, ts), lambda i, j: (i, j, 0, 0, 0)), tile_t(nqi),
                 tile(IDX_DIM), tile_t(IDX_HEADS), tile(d))
    return pl.pallas_call(
        _dsa_proj_body, grid=(b, s // ts), in_specs=in_specs, out_specs=out_specs, out_shape=out_shape,
        compiler_params=_cparams("arbitrary", "arbitrary"), name="dsa_proj",
    )(*args)


def _key_to_float(u):
    key = u ^ jnp.int32(INT_MIN)
    bits = key ^ ((key >> 31) & jnp.int32(0x7FFFFFFF))
    return jnp.where((u >> 23) == 0, -jnp.inf, lax.bitcast_convert_type(bits, _F32))


def _high_half(x):
    bits = lax.bitcast_convert_type(x, jnp.int32) & jnp.int32(-65536)
    return lax.bitcast_convert_type(bits, _F32).astype(_BF)


def _dsa_body(qT_ref, qiT_ref, wT_ref, k_ref, vT_ref, ki_ref, g1_ref, out_ref,
              sc_ref, sh_ref, m_ref, l_ref, acc_ref, bias_ref, s_ref, p_ref, *, topk, tk):
    tq = qT_ref.shape[2]
    dh = LANES
    q0 = pl.program_id(1) * tq
    n_chunks = (q0 + tq + tk - 1) // tk

    qicat = jnp.concatenate([qiT_ref[0, h * IDX_DIM:(h + 1) * IDX_DIM, :] for h in range(IDX_HEADS)], axis=1)
    w = wT_ref[0]
    qpos = q0 + lax.broadcasted_iota(jnp.int32, (tk, tq), 1)
    krow = lax.broadcasted_iota(jnp.int32, (tk, tq), 0)

    def score_chunk(c, carry):
        r0 = pl.multiple_of(c * tk, tk)
        lg = _dot(ki_ref[0, pl.ds(r0, tk), :], qicat)
        s = jnp.zeros((tk, tq), _F32)
        for h in range(IDX_HEADS):
            s = s + jnp.maximum(lg[:, h * tq:(h + 1) * tq], 0.0) * w[h:h + 1, :]
        s = jnp.where(krow + r0 <= qpos, s, -jnp.inf)
        sc_ref[pl.ds(r0, tk), :] = s
        sh_ref[pl.ds(r0, tk), :] = _high_half(s)
        return carry

    lax.fori_loop(0, n_chunks, score_chunk, 0)

    def make_count(nc, ref, rows, dtype):
        def count_ge(cand):
            one, zero = jnp.ones((), dtype), jnp.zeros((), dtype)
            parts = [jnp.zeros((rows, tq), dtype) for _ in range(4)]
            n = 0
            for c in range(nc):
                hit = jnp.where(ref[c * tk:(c + 1) * tk, :] >= cand, one, zero)
                for r in range(tk // rows):
                    parts[n % 4] = parts[n % 4] + hit[r * rows:(r + 1) * rows, :]
                    n += 1
            parts = [p.astype(_F32) for p in parts]
            return jnp.sum((parts[0] + parts[1]) + (parts[2] + parts[3]), axis=0, keepdims=True)
        return count_ge

    max_chunks = sc_ref.shape[0] // tk
    assert max_chunks * tk // (2 * SUBLANES) <= 4 * 64
    coarse = [make_count(nc, sh_ref, 2 * SUBLANES, _BF) for nc in range(1, max_chunks + 1)]
    fine = [make_count(nc, sc_ref, SUBLANES, _F32) for nc in range(1, max_chunks + 1)]

    def bisect_high(i, carry):
        u, n_ge = carry
        trial = u | lax.shift_left(jnp.int32(1), 31 - i)
        cnt = lax.switch(n_chunks - 1, coarse, _high_half(_key_to_float(trial)))
        keep = cnt >= float(topk)
        return jnp.where(keep, trial, u), jnp.where(keep, cnt, n_ge)

    def bisect_low(i, carry):
        u, n_ge = carry
        trial = u | lax.shift_left(jnp.int32(1), 31 - i)
        cnt = lax.switch(n_chunks - 1, fine, _key_to_float(trial))
        keep = cnt >= float(topk)
        return jnp.where(keep, trial, u), jnp.where(keep, cnt, n_ge)

    carry = lax.fori_loop(0, 16, bisect_high, (jnp.zeros((1, tq), jnp.int32), jnp.zeros((1, tq), _F32)))
    u, n_ge = lax.fori_loop(16, 32, bisect_low, carry)
    tau = jnp.maximum(_key_to_float(u), -FLT_MAX)

    @pl.when(jnp.max(n_ge) > float(topk))
    def _():
        def count_gt(c, cnt):
            r0 = pl.multiple_of(c * tk, tk)
            return cnt + jnp.sum(jnp.where(sc_ref[pl.ds(r0, tk), :] > tau, 1.0, 0.0), axis=0, keepdims=True)

        need = float(topk) - lax.fori_loop(0, n_chunks, count_gt, jnp.zeros((1, tq), _F32))
        before = (lax.broadcasted_iota(jnp.int32, (tk, tk), 1)
                  < lax.broadcasted_iota(jnp.int32, (tk, tk), 0)).astype(_BF)

        def strike(c, seen):
            r0 = pl.multiple_of(c * tk, tk)
            blk = sc_ref[pl.ds(r0, tk), :]
            tied = blk == tau
            tied_b = jnp.where(tied, 1.0, 0.0).astype(_BF)
            earlier = _dot(before, tied_b) + seen
            sc_ref[pl.ds(r0, tk), :] = jnp.where(tied & (earlier >= need), -jnp.inf, blk)
            return seen + jnp.sum(tied_b.astype(_F32), axis=0, keepdims=True)

        lax.fori_loop(0, n_chunks, strike, jnp.zeros((1, tq), _F32))

    m_ref[...] = jnp.full(m_ref.shape, MASK_VALUE, _F32)
    l_ref[...] = jnp.zeros_like(l_ref)
    acc_ref[...] = jnp.zeros_like(acc_ref)

    sub = 8 * SUBLANES
    groups = [slice(r * SUBLANES, (r + 1) * SUBLANES) for r in range(sub // SUBLANES)]

    def att_chunk(c, carry):
        r0 = pl.multiple_of(c * tk, tk)
        bias_ref[...] = jnp.where(sc_ref[pl.ds(r0, tk), :] >= tau, 0.0, MASK_VALUE)

        def logits(h):
            g = h // ATT_GROUP
            s_ref[h] = _dot(k_ref[0, pl.ds(r0, tk), g * dh:(g + 1) * dh], qT_ref[0, h * dh:(h + 1) * dh, :])

        logits(0)
        logits(1)
        for h in range(ATT_HEADS):
            if h + 2 < ATT_HEADS:
                logits(h + 2)
            m_old = m_ref[h:h + 1, :]
            mx = jnp.broadcast_to(m_old, (SUBLANES, tq))
            for i in range(tk // sub):
                rows = slice(i * sub, (i + 1) * sub)
                sb = s_ref[h, rows, :] + bias_ref[rows, :]
                s_ref[h, rows, :] = sb
                for gr in groups:
                    mx = jnp.maximum(mx, sb[gr, :])
            m_new = jnp.max(mx, axis=0, keepdims=True)
            for i in range(tk // sub):
                rows = slice(i * sub, (i + 1) * sub)
                p_ref[h, rows, :] = jnp.exp((s_ref[h, rows, :] - m_new).astype(_BF))
            alpha = jnp.exp(m_old - m_new)
            pv = _dot(vT_ref[0, c, h // ATT_GROUP], p_ref[h])
            l_ref[h:h + 1, :] = alpha * l_ref[h:h + 1, :] + pv[dh:dh + 1, :]
            acc_ref[h * dh:(h + 1) * dh, :] = alpha * acc_ref[h * dh:(h + 1) * dh, :] + pv[:dh, :]
            m_ref[h:h + 1, :] = m_new
        return carry

    lax.fori_loop(0, n_chunks, att_chunk, 0)
    outs = []
    for h in range(ATT_HEADS):
        o = acc_ref[h * dh:(h + 1) * dh, :] / l_ref[h:h + 1, :]
        outs.append(o.T)
    out_ref[0] = (jnp.concatenate(outs, axis=1) * g1_ref[0].astype(_F32)).astype(_BF)


def _dsa(qT, qiT, wT, k, vT, ki, g1, topk, tq, tk):
    b, nq, s = qT.shape
    d = g1.shape[2]
    in_specs = [
        pl.BlockSpec((1, nq, tq), lambda i, j: (i, 0, j)),
        pl.BlockSpec((1, qiT.shape[1], tq), lambda i, j: (i, 0, j)),
        pl.BlockSpec((1, IDX_HEADS, tq), lambda i, j: (i, 0, j)),
        pl.BlockSpec((1, s, k.shape[2]), lambda i, j: (i, 0, 0)),
        pl.BlockSpec((1,) + vT.shape[1:], lambda i, j: (i, 0, 0, 0, 0)),
        pl.BlockSpec((1, s, IDX_DIM), lambda i, j: (i, 0, 0)),
        pl.BlockSpec((1, tq, d), lambda i, j: (i, j, 0)),
    ]
    return pl.pallas_call(
        functools.partial(_dsa_body, topk=topk, tk=tk),
        grid=(b, s // tq), in_specs=in_specs,
        out_specs=pl.BlockSpec((1, tq, d), lambda i, j: (i, j, 0)),
        out_shape=jax.ShapeDtypeStruct((b, s, d), _BF),
        scratch_shapes=[pltpu.VMEM((s, tq), _F32), pltpu.VMEM((s, tq), _BF), pltpu.VMEM((ATT_HEADS, tq), _F32),
                        pltpu.VMEM((ATT_HEADS, tq), _F32), pltpu.VMEM((nq, tq), _F32),
                        pltpu.VMEM((tk, tq), _F32), pltpu.VMEM((ATT_HEADS, tk, tq), _F32),
                        pltpu.VMEM((ATT_HEADS, tk, tq), _BF)],
        compiler_params=_cparams("arbitrary", "arbitrary"), name="dsa",
    )(qT, qiT, wT, k, vT, ki, g1)


def _mem_kv_body(mem_ref, mn_ref, wkv_ref, mkn_ref, mkT_ref, mv_ref):
    d = mem_ref.shape[2]
    dm = d // MEM_HEADS
    m = _rms(mem_ref[0], mn_ref[...]).astype(_BF)
    kv = _dot(m, wkv_ref[...])
    mk = jnp.concatenate([_rms(kv[:, i * dm:(i + 1) * dm], mkn_ref[...]) for i in range(MEM_HEADS)], axis=1)
    mkT_ref[0] = mk.T.astype(_BF)
    mv_ref[0] = kv[:, d:].astype(_BF)


def _mem_kv(mem, mn, wkv, mkn):
    b, m, d = mem.shape
    row = lambda v: v.reshape(1, -1)
    return pl.pallas_call(
        _mem_kv_body, grid=(b,),
        in_specs=[pl.BlockSpec((1, m, d), lambda i: (i, 0, 0)), _full_spec((1, d)), _full_spec(wkv.shape),
                  _full_spec((1, d // MEM_HEADS))],
        out_specs=(pl.BlockSpec((1, d, m), lambda i: (i, 0, 0)), pl.BlockSpec((1, m, d), lambda i: (i, 0, 0))),
        out_shape=(jax.ShapeDtypeStruct((b, d, m), _BF), jax.ShapeDtypeStruct((b, m, d), _BF)),
        compiler_params=_cparams("arbitrary"), name="mem_kv",
    )(mem, row(mn), wkv, row(mkn))


def _mem_body(x_ref, nm_ref, wmq_ref, wg2_ref, mqn_ref, mkT_ref, mv_ref, out_ref):
    d = x_ref.shape[2]
    dm = d // MEM_HEADS
    h = _rms(x_ref[0], nm_ref[...]).astype(_BF)
    mq = _dot(h, wmq_ref[...])
    outs = []
    for i in range(MEM_HEADS):
        qh = (_rms(mq[:, i * dm:(i + 1) * dm], mqn_ref[...]) * (float(dm) ** -0.5)).astype(_BF)
        s = _dot(qh, mkT_ref[0, i * dm:(i + 1) * dm, :])
        p = jnp.exp(s - jnp.max(s, axis=-1, keepdims=True))
        o = _dot(p.astype(_BF), mv_ref[0, :, i * dm:(i + 1) * dm])
        outs.append(o / jnp.sum(p, axis=-1, keepdims=True))
    out_ref[0] = (jnp.concatenate(outs, axis=1) * _sigmoid(_dot(h, wg2_ref[...]))).astype(_BF)


def _mem_branch(x, nm, wmq, wg2, mqn, mkT, mv, ts):
    b, s, d = x.shape
    m = mv.shape[1]
    row = lambda v: v.reshape(1, -1)
    return pl.pallas_call(
        _mem_body, grid=(b, s // ts),
        in_specs=[pl.BlockSpec((1, ts, d), lambda i, j: (i, j, 0)), _full_spec((1, d)), _full_spec(wmq.shape),
                  _full_spec(wg2.shape), _full_spec((1, d // MEM_HEADS)),
                  pl.BlockSpec((1, d, m), lambda i, j: (i, 0, 0)), pl.BlockSpec((1, m, d), lambda i, j: (i, 0, 0))],
        out_specs=pl.BlockSpec((1, ts, d), lambda i, j: (i, j, 0)),
        out_shape=jax.ShapeDtypeStruct((b, s, d), _BF),
        compiler_params=_cparams("arbitrary", "arbitrary"), name="mem",
    )(x, row(nm), wmq, wg2, row(mqn), mkT, mv)


def _mid_body(x_ref, a_ref, b_ref, c_ref, wo_ref, nf_ref, wpqT_ref, sk_ref, x1_ref, h2T_ref, scT_ref):
    merged = (a_ref[...].astype(_F32) + b_ref[...].astype(_F32) + c_ref[...].astype(_F32)).astype(_BF)
    x1 = x_ref[...] + _dot(merged, wo_ref[...])
    x1_ref[...] = x1
    h2T = _rms(x1, nf_ref[...]).T.astype(_BF)
    h2T_ref[...] = h2T
    pqT = _dot(wpqT_ref[...], h2T).astype(_BF)
    for c in range(sk_ref.shape[0]):
        scT_ref[c * PEER_KEYS:(c + 1) * PEER_KEYS, :] = _dot(sk_ref[c], pqT[c * LANES:(c + 1) * LANES, :])


def _mid(x2, ma, mb, mc, wo, nf, wpqT, sk, ts):
    t, d = x2.shape
    nsc = sk.shape[0] * PEER_KEYS
    tile = pl.BlockSpec((ts, d), lambda i: (i, 0))
    return pl.pallas_call(
        _mid_body, grid=(t // ts,),
        in_specs=[tile, tile, tile, tile, _full_spec(wo.shape), _full_spec((1, d)), _full_spec(wpqT.shape),
                  _full_spec(sk.shape)],
        out_specs=(tile, pl.BlockSpec((d, ts), lambda i: (0, i)), pl.BlockSpec((nsc, ts), lambda i: (0, i))),
        out_shape=(jax.ShapeDtypeStruct((t, d), _F32), jax.ShapeDtypeStruct((d, t), _BF),
                   jax.ShapeDtypeStruct((nsc, t), _F32)),
        compiler_params=_cparams("arbitrary"), name="mid",
    )(x2, ma, mb, mc, wo, nf.reshape(1, -1), wpqT, sk)


def _top_values(s, n):
    groups = [s[r * SUBLANES:(r + 1) * SUBLANES, :] for r in range(s.shape[0] // SUBLANES)]
    size = pl.next_power_of_2(len(groups))
    groups += [jnp.full_like(groups[0], -jnp.inf)] * (size - len(groups))
    k = 2
    while k <= size:
        j = k // 2
        while j >= 1:
            for i in range(size):
                l = i ^ j
                if l > i:
                    hi, lo = jnp.maximum(groups[i], groups[l]), jnp.minimum(groups[i], groups[l])
                    groups[i], groups[l] = (hi, lo) if (i & k) == 0 else (lo, hi)
            j //= 2
        k *= 2
    vals = []
    for it in range(n):
        m = jnp.max(groups[0], axis=0, keepdims=True)
        vals.append(m)
        remaining = n - 1 - it
        head = groups[0] == m
        for d in range(min(remaining, size)):
            below = groups[d + 1] if d + 1 < size else jnp.full_like(m, -jnp.inf)
            groups[d] = jnp.where(head, below, groups[d])
    return vals


def _route_body(scT_ref, tau_ref, c_ref, top2_ref):
    def head(h, carry):
        r1 = pl.multiple_of(h * 2 * PEER_KEYS, PEER_KEYS)
        r2 = pl.multiple_of(h * 2 * PEER_KEYS + PEER_KEYS, PEER_KEYS)
        n = PEER_TOPK + 1
        v1 = _top_values(scT_ref[pl.ds(r1, PEER_KEYS), :], n)
        v2 = _top_values(scT_ref[pl.ds(r2, PEER_KEYS), :], n)
        edge = SUBLANES
        assert 2 * (edge + 1) > n
        pad = [jnp.full_like(v1[0], -jnp.inf)] * (-n % SUBLANES)
        col1 = jnp.concatenate(v1 + pad, axis=0)
        col2 = jnp.concatenate(v2 + pad, axis=0)
        cand = jnp.concatenate([v1[a] + col2[:edge] for a in range(edge)]
                               + [v1[0] + col2[edge:], v2[0] + col1[edge:]], axis=0)
        best = _top_values(cand, n)
        tau = best[PEER_TOPK - 1]
        top = v1[0] + v2[0]
        z = jnp.sum(jnp.where(cand >= tau, jnp.exp(cand - top), 0.0), axis=0, keepdims=True)
        tau_ref[pl.ds(h, 1), :] = 0.5 * (tau + best[PEER_TOPK])
        c_ref[pl.ds(h, 1), :] = top + jnp.log(z)
        top2_ref[pl.ds(pl.multiple_of(h * TOP_ROWS, SUBLANES), TOP_ROWS), :] = col2
        return carry

    lax.fori_loop(0, PEER_HEADS, head, 0, unroll=2)


def _route(scT, te):
    nsc, t = scT.shape
    spec = pl.BlockSpec((PEER_HEADS, te), lambda i: (0, i))
    return pl.pallas_call(
        _route_body, grid=(t // te,),
        in_specs=[pl.BlockSpec((nsc, te), lambda i: (0, i))],
        out_specs=(spec, spec, pl.BlockSpec((PEER_HEADS * TOP_ROWS, te), lambda i: (0, i))),
        out_shape=(jax.ShapeDtypeStruct((PEER_HEADS, t), _F32),) * 2
        + (jax.ShapeDtypeStruct((PEER_HEADS * TOP_ROWS, t), _F32),),
        compiler_params=_cparams("arbitrary"), name="route",
    )(scT)


def _experts_body(h2T_ref, scT_ref, tau_ref, c_ref, top2_ref, u_ref, vT_ref, x1_ref, out_ref,
                  acc_ref, e2_ref, rank_ref, cnt_ref, scale_ref, act_ref, g_ref, rows_ref, *, rows_per_step,
                  n_blocks):
    e = pl.program_id(1)
    nk = PEER_KEYS
    ts = act_ref.shape[1]

    @pl.when(e == 0)
    def _():
        acc_ref[...] = jnp.zeros_like(acc_ref)
        for h in range(PEER_HEADS):
            best = [top2_ref[h * TOP_ROWS + b:h * TOP_ROWS + b + 1, :] for b in range(PEER_TOPK + 1)]
            s1 = scT_ref[2 * h * nk:(2 * h + 1) * nk, :]
            s2 = scT_ref[(2 * h + 1) * nk:(2 * h + 2) * nk, :]
            e2_ref[h * nk:(h + 1) * nk, :] = jnp.exp(s2 - best[0]).astype(_BF)
            rank = jnp.zeros((nk, ts), _F32)
            for b in range(PEER_TOPK):
                rank = jnp.where(best[b] > s2, float(b + 1), rank)
            rank_ref[h * nk:(h + 1) * nk, :] = rank.astype(_BF)
            thr = tau_ref[h:h + 1, :] - s1
            cnt = jnp.zeros((nk, ts), _F32)
            for b in range(PEER_TOPK + 1):
                cnt = jnp.where(best[b] >= thr, float(b + 1), cnt)
            cnt_ref[h * nk:(h + 1) * nk, :] = cnt
            scale_ref[h * nk:(h + 1) * nk, :] = 0.5 * jnp.exp(s1 + best[0] - c_ref[h:h + 1, :])

    n_rows = rows_per_step * PEER_HEADS
    first_group = 2
    zero = jnp.zeros((), _BF)

    def first_matmul(i):
        rows = slice(i * first_group * nk, (i + 1) * first_group * nk)
        act_ref[rows, :] = _dot(u_ref[rows, :], h2T_ref[...])

    for i in range(rows_per_step):
        for h in range(PEER_HEADS):
            src = h * nk + e * rows_per_step + i
            r = i * PEER_HEADS + h
            rows_ref[r:r + 1, :] = cnt_ref[pl.ds(src, 1), :]
            rows_ref[n_rows + r:n_rows + r + 1, :] = scale_ref[pl.ds(src, 1), :]

    second_group = 4
    first_matmul(0)
    for i in range(rows_per_step):
        if i % first_group == 0 and i // first_group + 1 < rows_per_step // first_group:
            first_matmul(i // first_group + 1)
        wgt = jnp.zeros((nk, ts), _BF)
        for h in range(PEER_HEADS):
            r = i * PEER_HEADS + h
            cnt = rows_ref[r:r + 1, :].astype(_BF)
            scale = rows_ref[n_rows + r:n_rows + r + 1, :].astype(_BF)
            hit = jnp.where(rank_ref[h * nk:(h + 1) * nk, :] < cnt, e2_ref[h * nk:(h + 1) * nk, :], zero)
            wgt = wgt + hit * scale
        x = act_ref[i * nk:(i + 1) * nk, :].astype(_BF)
        t = jnp.tanh(x * (jnp.asarray(GELU_C0, _BF) + jnp.asarray(GELU_C1, _BF) * (x * x)))
        g_ref[i * nk:(i + 1) * nk, :] = (x + x * t) * wgt
        if i % second_group == second_group - 1:
            cols = slice((i + 1 - second_group) * nk, (i + 1) * nk)
            acc_ref[...] += _dot(vT_ref[0, :, cols], g_ref[cols, :])

    @pl.when(e == n_blocks - 1)
    def _():
        out_ref[...] = x1_ref[...] + acc_ref[...].T


def _experts(h2T, scT, tau, cn, top2, u, vT, x1, ts, rows_per_step):
    d, t = h2T.shape
    nsc = scT.shape[0]
    ne = u.shape[0]
    eb = rows_per_step * PEER_KEYS
    nb = ne // eb
    nhk = PEER_HEADS * PEER_KEYS
    col = lambda n: pl.BlockSpec((n, ts), lambda i, e: (0, i))
    tile = pl.BlockSpec((ts, d), lambda i, e: (i, 0))
    return pl.pallas_call(
        functools.partial(_experts_body, rows_per_step=rows_per_step, n_blocks=nb),
        grid=(t // ts, nb),
        in_specs=[col(d), col(nsc), col(PEER_HEADS), col(PEER_HEADS), col(top2.shape[0]),
                  pl.BlockSpec((eb, d), lambda i, e: (e, 0)), pl.BlockSpec((1, d, eb), lambda i, e: (e, 0, 0)), tile],
        out_specs=tile,
        out_shape=jax.ShapeDtypeStruct((t, d), _F32),
        scratch_shapes=[pltpu.VMEM((d, ts), _F32), pltpu.VMEM((nhk, ts), _BF), pltpu.VMEM((nhk, ts), _BF),
                        pltpu.VMEM((nhk, ts), _F32), pltpu.VMEM((nhk, ts), _F32),
                        pltpu.VMEM((eb, ts), _F32), pltpu.VMEM((eb, ts), _BF),
                        pltpu.VMEM((2 * rows_per_step * PEER_HEADS, ts), _F32)],
        compiler_params=_cparams("arbitrary", "arbitrary"), name="experts",
    )(h2T, scT, tau, cn, top2, u, vT, x1)


def _block_diag(w):
    n, c, d = w.shape
    return jnp.einsum("ncd,nm->ncmd", w, jnp.eye(n, dtype=w.dtype)).reshape(n * c, n * d)


def _rope_tables(s, dim, reps):
    half = dim // 2
    freq = ROPE_THETA ** (-jnp.arange(half, dtype=_F32) / half)
    ang = jnp.arange(s, dtype=_F32)[:, None] * freq[None, :]
    cos = jnp.tile(jnp.concatenate([jnp.cos(ang), jnp.cos(ang)], axis=1), (1, reps))
    sin = jnp.tile(jnp.concatenate([-jnp.sin(ang), jnp.sin(ang)], axis=1), (1, reps))
    return cos, sin


def _tile(n, pref):
    return pref if n % pref == 0 else n


def _layer(x, mem, p, l):
    b, s, d = x.shape
    bf = lambda a: a.astype(_BF)
    w_in = p["w_in"][l]
    n_q, n_kv = ATT_HEADS * LANES, ATT_KV_HEADS * LANES
    n_qi = IDX_HEADS * IDX_DIM
    sizes = (d, d, n_q, n_kv, n_kv, n_qi, IDX_DIM, IDX_HEADS, d, 3 * d)
    offs = [0]
    for n in sizes:
        offs.append(offs[-1] + n)
    wlx, wlg, wq, wk, wv, wqi, wki, wwi, wmq, wgates = (w_in[:, offs[i]:offs[i + 1]] for i in range(len(sizes)))
    wki2 = jnp.concatenate([wki, wki], axis=1)
    wwi_p = jnp.pad(wwi, ((0, 0), (0, LANES - IDX_HEADS)))
    wg0, wg1, wg2 = wgates[:, :d], wgates[:, d:2 * d], wgates[:, 2 * d:]
    nm = p["norm_mix"][l]

    ts_a = _tile(s, 512)
    m_lru = _lru_branch(x, nm, bf(wlx), bf(wlg), bf(wg0), p["conv_w"][l], p["conv_b"][l],
                        bf(_block_diag(p["lru_wa"][l])), p["lru_ba"][l].reshape(-1),
                        bf(_block_diag(p["lru_wi"][l])), p["lru_bi"][l].reshape(-1), p["lru_lambda"][l], ts_a)

    tk = _tile(s, 512)
    c128, s128 = _rope_tables(s, LANES, 1)
    c64, s64 = _rope_tables(s, IDX_DIM, LANES // IDX_DIM)
    ikn2 = jnp.concatenate([p["idx_k_norm"][l]] * 2)
    qT, k, vT, qiT, ki, wT, g1 = _dsa_proj(x, nm, bf(wq), bf(wk), bf(wv), bf(wqi), bf(wki2), bf(wwi_p), bf(wg1),
                                           p["q_norm"][l], p["k_norm"][l], ikn2, c128, s128, c64, s64, tk)
    m_att = _dsa(qT, qiT, wT, k, vT, ki, g1, min(TOPK_MAX, s // 4), _tile(s, 256), tk)

    mkT, mv = _mem_kv(mem, p["mem_norm"][l], bf(p["w_mem_kv"][l]), p["mem_k_norm"][l])
    m_mem = _mem_branch(x, nm, bf(wmq), bf(wg2), p["mem_q_norm"][l], mkT, mv, _tile(s, 512))

    t = b * s
    flat = lambda a: a.reshape(t, d)
    sk = bf(p["peer_subkeys"][l].reshape(2 * PEER_HEADS, PEER_KEYS, -1))
    x1, h2T, scT = _mid(flat(x), flat(m_lru), flat(m_att), flat(m_mem), bf(p["w_out"][l]), p["norm_ffn"][l],
                        bf(p["peer_wq"][l].T), sk, _tile(t, 512))
    tau, cn, top2 = _route(scT, _tile(t, 256))
    rows_per_step = 16
    vT = bf(p["peer_v"][l]).reshape(-1, rows_per_step * PEER_KEYS, d).transpose(0, 2, 1)
    out = _experts(h2T, scT, tau, cn, top2, bf(p["peer_u"][l]), vT, x1, _tile(t, 512), rows_per_step)
    return out.reshape(b, s, d)


def kernel(x, mem, norm_mix, w_in, conv_w, conv_b, lru_wa, lru_ba, lru_wi, lru_bi, lru_lambda, q_norm, k_norm,
           idx_k_norm, mem_norm, w_mem_kv, mem_q_norm, mem_k_norm, w_out, norm_ffn, peer_wq, peer_subkeys,
           peer_u, peer_v):
    p = dict(norm_mix=norm_mix, w_in=w_in, conv_w=conv_w, conv_b=conv_b, lru_wa=lru_wa, lru_ba=lru_ba,
             lru_wi=lru_wi, lru_bi=lru_bi, lru_lambda=lru_lambda, q_norm=q_norm, k_norm=k_norm,
             idx_k_norm=idx_k_norm, mem_norm=mem_norm, w_mem_kv=w_mem_kv, mem_q_norm=mem_q_norm,
             mem_k_norm=mem_k_norm, w_out=w_out, norm_ffn=norm_ffn, peer_wq=peer_wq, peer_subkeys=peer_subkeys,
             peer_u=peer_u, peer_v=peer_v)
    for l in range(norm_mix.shape[0]):
        x = _layer(x, mem, p, l)
    return x
```

```python
import functools

import jax
import jax.numpy as jnp
from jax import lax
from jax.experimental import pallas as pl
from jax.experimental.pallas import tpu as pltpu

_F32 = jnp.float32
_BF = jnp.bfloat16

EPS = 1e-6
ROPE_THETA = 10000.0
LRU_C = 8.0
CONV_WIDTH = 4
ATT_HEADS = 8
ATT_KV_HEADS = 2
ATT_GROUP = ATT_HEADS // ATT_KV_HEADS
IDX_HEADS = 8
IDX_DIM = 64
TOPK_MAX = 256
MEM_HEADS = 4
PEER_HEADS = 8
PEER_KEYS = 128
PEER_TOPK = 16
TOP_ROWS = 24

LANES = 128
SUBLANES = 8
ONES_ROWS = 2 * SUBLANES
VMEM_LIMIT_BYTES = 56 * 1024 * 1024
MASK_VALUE = -1e30
FLT_MAX = 3.4028234663852886e38
INT_MIN = -(2 ** 31)


def _cparams(*sem):
    return pltpu.CompilerParams(dimension_semantics=sem, vmem_limit_bytes=VMEM_LIMIT_BYTES)


def _rms(x, g):
    return x * lax.rsqrt(jnp.mean(x * x, axis=-1, keepdims=True) + EPS) * g


GELU_C0 = 0.7978845608028654
GELU_C1 = GELU_C0 * 0.044715


def _gelu(x):
    return 0.5 * x * (1.0 + jnp.tanh(x * (GELU_C0 + GELU_C1 * (x * x))))


def _sigmoid(x):
    return 0.5 * jnp.tanh(0.5 * x) + 0.5


def _dot(a, b):
    return jnp.dot(a, b, preferred_element_type=_F32)


def _full_spec(shape):
    n = len(shape)
    return pl.BlockSpec(shape, lambda *_: (0,) * n)


def _lru_body(x_ref, nm_ref, wlx_ref, wlg_ref, wg0_ref, cw_ref, cb_ref, wa_ref, ba_ref, wi_ref,
              bi_ref, lam_ref, out_ref, buf_ref, hc_ref, a_ref, b_ref):
    ts = x_ref.shape[1]
    c_dim = out_ref.shape[2]

    @pl.when(pl.program_id(1) == 0)
    def _():
        buf_ref[0:SUBLANES, :] = jnp.zeros((SUBLANES, c_dim), _F32)
        hc_ref[...] = jnp.zeros_like(hc_ref)

    h = _rms(x_ref[0], nm_ref[...]).astype(_BF)
    lx = _dot(h, wlx_ref[...])
    buf_ref[SUBLANES:SUBLANES + ts, :] = lx
    ext = buf_ref[...]
    cw = cw_ref[...]
    xc = cb_ref[...] + cw[3:4] * lx
    for j in range(CONV_WIDTH - 1):
        d = CONV_WIDTH - 1 - j
        xc = xc + cw[j:j + 1] * pltpu.roll(ext, d, 0)[SUBLANES:, :]
    buf_ref[0:SUBLANES, :] = lx[ts - SUBLANES:, :]

    xcb = xc.astype(_BF)
    r = _sigmoid(_dot(xcb, wa_ref[...]) + ba_ref[...])
    ig = _sigmoid(_dot(xcb, wi_ref[...]) + bi_ref[...])
    lam = lam_ref[...]
    softplus_neg_lam = jnp.maximum(-lam, 0.0) + jnp.log1p(jnp.exp(-jnp.abs(lam)))
    log_a = (-LRU_C) * r * softplus_neg_lam
    a = jnp.exp(log_a)
    y = jnp.tanh(-log_a) * (a * a + 1.0)
    u = jnp.where(y > 0.0, y * lax.rsqrt(y), 0.0) * (ig * xc)

    a = a.reshape(ts // SUBLANES, SUBLANES, c_dim)
    u = u.reshape(ts // SUBLANES, SUBLANES, c_dim)
    rid = lax.broadcasted_iota(jnp.int32, a.shape, 1)
    for d in (1, 2, 4):
        ok = rid >= d
        a_s = pltpu.roll(a, d, 1)
        u_s = pltpu.roll(u, d, 1)
        u = jnp.where(ok, a * u_s + u, u)
        a = jnp.where(ok, a * a_s, a)
    a_ref[...] = a.reshape(ts, c_dim)
    b_ref[...] = u.reshape(ts, c_dim)

    def step(g, hp):
        r0 = pl.multiple_of(g * SUBLANES, SUBLANES)
        hcur = a_ref[pl.ds(r0, SUBLANES), :] * hp + b_ref[pl.ds(r0, SUBLANES), :]
        b_ref[pl.ds(r0, SUBLANES), :] = hcur
        return jnp.broadcast_to(hcur[SUBLANES - 1:SUBLANES, :], (SUBLANES, c_dim))

    hc_ref[...] = lax.fori_loop(0, ts // SUBLANES, step, hc_ref[...])
    gate = _gelu(_dot(h, wlg_ref[...])) * _sigmoid(_dot(h, wg0_ref[...]))
    out_ref[0] = (b_ref[...] * gate).astype(_BF)


def _lru_branch(x, nm, wlx, wlg, wg0, cw, cb, wa, ba, wi, bi, lam, ts):
    b, s, d = x.shape
    c = wlx.shape[1]
    row = lambda v: v.reshape(1, -1)
    args = (x, row(nm), wlx, wlg, wg0, cw, row(cb), wa, row(ba), wi, row(bi), row(lam))
    in_specs = [pl.BlockSpec((1, ts, d), lambda i, j: (i, j, 0))]
    in_specs += [_full_spec(a.shape) for a in args[1:]]
    return pl.pallas_call(
        _lru_body,
        grid=(b, s // ts),
        in_specs=in_specs,
        out_specs=pl.BlockSpec((1, ts, c), lambda i, j: (i, j, 0)),
        out_shape=jax.ShapeDtypeStruct((b, s, c), _BF),
        scratch_shapes=[pltpu.VMEM((ts + SUBLANES, c), _F32), pltpu.VMEM((SUBLANES, c), _F32),
                        pltpu.VMEM((ts, c), _F32), pltpu.VMEM((ts, c), _F32)],
        compiler_params=_cparams("arbitrary", "arbitrary"),
        name="lru",
    )(*args)


def _rope128(x, cos, sin):
    return x * cos + pltpu.roll(x, LANES // 2, 1) * sin


def _rope64(x, cos, sin, first_half):
    rot = jnp.where(first_half, pltpu.roll(x, LANES - IDX_DIM // 2, 1), pltpu.roll(x, IDX_DIM // 2, 1))
    return x * cos + rot * sin


def _dsa_proj_body(x_ref, nm_ref, wq_ref, wk_ref, wv_ref, wqi_ref, wki_ref, wwi_ref, wg1_ref,
                   qn_ref, kn_ref, ikn_ref, c128_ref, s128_ref, c64_ref, s64_ref,
                   qT_ref, k_ref, vT_ref, qiT_ref, ki_ref, wT_ref, g1_ref):
    ts = x_ref.shape[1]
    h = _rms(x_ref[0], nm_ref[...]).astype(_BF)
    c128, s128 = c128_ref[...], s128_ref[...]
    c64, s64 = c64_ref[...], s64_ref[...]
    first_half = (lax.broadcasted_iota(jnp.int32, (ts, LANES), 1) & (IDX_DIM - 1)) < IDX_DIM // 2

    def head_norm_rope(t, g, scale):
        outs = []
        for i in range(t.shape[1] // LANES):
            th = _rms(t[:, i * LANES:(i + 1) * LANES], g)
            outs.append(_rope128(th, c128, s128) * scale)
        return jnp.concatenate(outs, axis=1)

    q = head_norm_rope(_dot(h, wq_ref[...]), qn_ref[...], float(LANES) ** -0.5)
    qT_ref[0] = q.T.astype(_BF)
    k_ref[0] = head_norm_rope(_dot(h, wk_ref[...]), kn_ref[...], 1.0).astype(_BF)
    vT = _dot(h, wv_ref[...]).T.astype(_BF)
    for g in range(ATT_KV_HEADS):
        vT_ref[0, 0, g, :LANES, :] = vT[g * LANES:(g + 1) * LANES, :]
        vT_ref[0, 0, g, LANES:, :] = jnp.ones((ONES_ROWS, ts), _BF)

    qi = _dot(h, wqi_ref[...])
    qi = jnp.concatenate(
        [_rope64(qi[:, i * LANES:(i + 1) * LANES], c64, s64, first_half) for i in range(qi.shape[1] // LANES)],
        axis=1) * (float(IDX_DIM) ** -0.5)
    qiT_ref[0] = qi.T.astype(_BF)
    ki = _rope64(_rms(_dot(h, wki_ref[...]), ikn_ref[...]), c64, s64, first_half)
    ki_ref[0] = ki[:, :IDX_DIM].astype(_BF)
    wi = _dot(h, wwi_ref[...]) * (float(IDX_HEADS) ** -0.5)
    wT_ref[0] = wi.T[:IDX_HEADS, :]
    g1_ref[0] = _sigmoid(_dot(h, wg1_ref[...])).astype(_BF)


def _dsa_proj(x, nm, wq, wk, wv, wqi, wki2, wwi, wg1, qn, kn, ikn2, c128, s128, c64, s64, ts):
    b, s, d = x.shape
    row = lambda v: v.reshape(1, -1)
    nq, nk, nqi = wq.shape[1], wk.shape[1], wqi.shape[1]
    args = (x, row(nm), wq, wk, wv, wqi, wki2, wwi, wg1, row(qn), row(kn), row(ikn2), c128, s128, c64, s64)
    tile = lambda n: pl.BlockSpec((1, ts, n), lambda i, j: (i, j, 0))
    tile_t = lambda n: pl.BlockSpec((1, n, ts), lambda i, j: (i, 0, j))
    tab = pl.BlockSpec((ts, LANES), lambda i, j: (j, 0))
    in_specs = [tile(d)] + [_full_spec(a.shape) for a in args[1:12]] + [tab] * 4
    out_shape = (
        jax.ShapeDtypeStruct((b, nq, s), _BF),
        jax.ShapeDtypeStruct((b, s, nk), _BF),
        jax.ShapeDtypeStruct((b, s // ts, ATT_KV_HEADS, LANES + ONES_ROWS, ts), _BF),
        jax.ShapeDtypeStruct((b, nqi, s), _BF),
        jax.ShapeDtypeStruct((b, s, IDX_DIM), _BF),
        jax.ShapeDtypeStruct((b, IDX_HEADS, s), _F32),
        jax.ShapeDtypeStruct((b, s, d), _BF),
    )
    out_specs = (tile_t(nq), tile(nk),
                 pl.BlockSpec((1, 1, ATT_KV_HEADS, LANES + ONES_ROWS, ts), lambda i, j: (i, j, 0, 0, 0)), tile_t(nqi),
                 tile(IDX_DIM), tile_t(IDX_HEADS), tile(d))
    return pl.pallas_call(
        _dsa_proj_body, grid=(b, s // ts), in_specs=in_specs, out_specs=out_specs, out_shape=out_shape,
        compiler_params=_cparams("arbitrary", "arbitrary"), name="dsa_proj",
    )(*args)


def _key_to_float(u):
    key = u ^ jnp.int32(INT_MIN)
    bits = key ^ ((key >> 31) & jnp.int32(0x7FFFFFFF))
    return jnp.where((u >> 23) == 0, -jnp.inf, lax.bitcast_convert_type(bits, _F32))


def _high_half(x):
    bits = lax.bitcast_convert_type(x, jnp.int32) & jnp.int32(-65536)
    return lax.bitcast_convert_type(bits, _F32).astype(_BF)


def _dsa_body(qT_ref, qiT_ref, wT_ref, k_ref, vT_ref, ki_ref, g1_ref, out_ref,
              sc_ref, sh_ref, m_ref, l_ref, acc_ref, bias_ref, s_ref, p_ref, *, topk, tk):
    tq = qT_ref.shape[2]
    dh = LANES
    q0 = pl.program_id(1) * tq
    n_chunks = (q0 + tq + tk - 1) // tk

    qicat = jnp.concatenate([qiT_ref[0, h * IDX_DIM:(h + 1) * IDX_DIM, :] for h in range(IDX_HEADS)], axis=1)
    w = wT_ref[0]
    qpos = q0 + lax.broadcasted_iota(jnp.int32, (tk, tq), 1)
    krow = lax.broadcasted_iota(jnp.int32, (tk, tq), 0)

    def score_chunk(c, carry):
        r0 = pl.multiple_of(c * tk, tk)
        lg = _dot(ki_ref[0, pl.ds(r0, tk), :], qicat)
        s = jnp.zeros((tk, tq), _F32)
        for h in range(IDX_HEADS):
            s = s + jnp.maximum(lg[:, h * tq:(h + 1) * tq], 0.0) * w[h:h + 1, :]
        s = jnp.where(krow + r0 <= qpos, s, -jnp.inf)
        sc_ref[pl.ds(r0, tk), :] = s
        sh_ref[pl.ds(r0, tk), :] = _high_half(s)
        return carry

    lax.fori_loop(0, n_chunks, score_chunk, 0)

    def make_count(nc, ref, rows, dtype):
        def count_ge(cand):
            one, zero = jnp.ones((), dtype), jnp.zeros((), dtype)
            parts = [jnp.zeros((rows, tq), dtype) for _ in range(4)]
            n = 0
            for c in range(nc):
                hit = jnp.where(ref[c * tk:(c + 1) * tk, :] >= cand, one, zero)
                for r in range(tk // rows):
                    parts[n % 4] = parts[n % 4] + hit[r * rows:(r + 1) * rows, :]
                    n += 1
            parts = [p.astype(_F32) for p in parts]
            return jnp.sum((parts[0] + parts[1]) + (parts[2] + parts[3]), axis=0, keepdims=True)
        return count_ge

    max_chunks = sc_ref.shape[0] // tk
    assert max_chunks * tk // (2 * SUBLANES) <= 4 * 64
    coarse = [make_count(nc, sh_ref, 2 * SUBLANES, _BF) for nc in range(1, max_chunks + 1)]
    fine = [make_count(nc, sc_ref, SUBLANES, _F32) for nc in range(1, max_chunks + 1)]

    def bisect_high(i, carry):
        u, n_ge = carry
        trial = u | lax.shift_left(jnp.int32(1), 31 - i)
        cnt = lax.switch(n_chunks - 1, coarse, _high_half(_key_to_float(trial)))
        keep = cnt >= float(topk)
        return jnp.where(keep, trial, u), jnp.where(keep, cnt, n_ge)

    def bisect_low(i, carry):
        u, n_ge = carry
        trial = u | lax.shift_left(jnp.int32(1), 31 - i)
        cnt = lax.switch(n_chunks - 1, fine, _key_to_float(trial))
        keep = cnt >= float(topk)
        return jnp.where(keep, trial, u), jnp.where(keep, cnt, n_ge)

    carry = lax.fori_loop(0, 16, bisect_high, (jnp.zeros((1, tq), jnp.int32), jnp.zeros((1, tq), _F32)))
    u, n_ge = lax.fori_loop(16, 32, bisect_low, carry)
    tau = jnp.maximum(_key_to_float(u), -FLT_MAX)

    @pl.when(jnp.max(jnp.where(tau > -FLT_MAX, n_ge, 0.0)) > float(topk))
    def _():
        def count_gt(c, cnt):
            r0 = pl.multiple_of(c * tk, tk)
            return cnt + jnp.sum(jnp.where(sc_ref[pl.ds(r0, tk), :] > tau, 1.0, 0.0), axis=0, keepdims=True)

        need = float(topk) - lax.fori_loop(0, n_chunks, count_gt, jnp.zeros((1, tq), _F32))
        before = (lax.broadcasted_iota(jnp.int32, (tk, tk), 1)
                  < lax.broadcasted_iota(jnp.int32, (tk, tk), 0)).astype(_BF)

        def strike(c, seen):
            r0 = pl.multiple_of(c * tk, tk)
            blk = sc_ref[pl.ds(r0, tk), :]
            tied = blk == tau
            tied_b = jnp.where(tied, 1.0, 0.0).astype(_BF)
            earlier = _dot(before, tied_b) + seen
            sc_ref[pl.ds(r0, tk), :] = jnp.where(tied & (earlier >= need), -jnp.inf, blk)
            return seen + jnp.sum(tied_b.astype(_F32), axis=0, keepdims=True)

        lax.fori_loop(0, n_chunks, strike, jnp.zeros((1, tq), _F32))

    m_ref[...] = jnp.full(m_ref.shape, MASK_VALUE, _F32)
    l_ref[...] = jnp.zeros_like(l_ref)
    acc_ref[...] = jnp.zeros_like(acc_ref)

    sub = 8 * SUBLANES
    groups = [slice(r * SUBLANES, (r + 1) * SUBLANES) for r in range(sub // SUBLANES)]

    def att_chunk(c, carry):
        r0 = pl.multiple_of(c * tk, tk)
        bias_ref[...] = jnp.where(sc_ref[pl.ds(r0, tk), :] >= tau, 0.0, MASK_VALUE)

        def logits(h):
            g = h // ATT_GROUP
            s_ref[h] = _dot(k_ref[0, pl.ds(r0, tk), g * dh:(g + 1) * dh], qT_ref[0, h * dh:(h + 1) * dh, :])

        logits(0)
        logits(1)
        for h in range(ATT_HEADS):
            if h + 2 < ATT_HEADS:
                logits(h + 2)
            m_old = m_ref[h:h + 1, :]
            mx = jnp.broadcast_to(m_old, (SUBLANES, tq))
            for i in range(tk // sub):
                rows = slice(i * sub, (i + 1) * sub)
                sb = s_ref[h, rows, :] + bias_ref[rows, :]
                s_ref[h, rows, :] = sb
                for gr in groups:
                    mx = jnp.maximum(mx, sb[gr, :])
            m_new = jnp.max(mx, axis=0, keepdims=True)
            for i in range(tk // sub):
                rows = slice(i * sub, (i + 1) * sub)
                p_ref[h, rows, :] = jnp.exp((s_ref[h, rows, :] - m_new).astype(_BF))
            alpha = jnp.exp(m_old - m_new)
            pv = _dot(vT_ref[0, c, h // ATT_GROUP], p_ref[h])
            l_ref[h:h + 1, :] = alpha * l_ref[h:h + 1, :] + pv[dh:dh + 1, :]
            acc_ref[h * dh:(h + 1) * dh, :] = alpha * acc_ref[h * dh:(h + 1) * dh, :] + pv[:dh, :]
            m_ref[h:h + 1, :] = m_new
        return carry

    lax.fori_loop(0, n_chunks, att_chunk, 0)
    outs = []
    for h in range(ATT_HEADS):
        o = acc_ref[h * dh:(h + 1) * dh, :] / l_ref[h:h + 1, :]
        outs.append(o.T)
    out_ref[0] = (jnp.concatenate(outs, axis=1) * g1_ref[0].astype(_F32)).astype(_BF)


def _dsa(qT, qiT, wT, k, vT, ki, g1, topk, tq, tk):
    b, nq, s = qT.shape
    d = g1.shape[2]
    in_specs = [
        pl.BlockSpec((1, nq, tq), lambda i, j: (i, 0, j)),
        pl.BlockSpec((1, qiT.shape[1], tq), lambda i, j: (i, 0, j)),
        pl.BlockSpec((1, IDX_HEADS, tq), lambda i, j: (i, 0, j)),
        pl.BlockSpec((1, s, k.shape[2]), lambda i, j: (i, 0, 0)),
        pl.BlockSpec((1,) + vT.shape[1:], lambda i, j: (i, 0, 0, 0, 0)),
        pl.BlockSpec((1, s, IDX_DIM), lambda i, j: (i, 0, 0)),
        pl.BlockSpec((1, tq, d), lambda i, j: (i, j, 0)),
    ]
    return pl.pallas_call(
        functools.partial(_dsa_body, topk=topk, tk=tk),
        grid=(b, s // tq), in_specs=in_specs,
        out_specs=pl.BlockSpec((1, tq, d), lambda i, j: (i, j, 0)),
        out_shape=jax.ShapeDtypeStruct((b, s, d), _BF),
        scratch_shapes=[pltpu.VMEM((s, tq), _F32), pltpu.VMEM((s, tq), _BF), pltpu.VMEM((ATT_HEADS, tq), _F32),
                        pltpu.VMEM((ATT_HEADS, tq), _F32), pltpu.VMEM((nq, tq), _F32),
                        pltpu.VMEM((tk, tq), _F32), pltpu.VMEM((ATT_HEADS, tk, tq), _F32),
                        pltpu.VMEM((ATT_HEADS, tk, tq), _BF)],
        compiler_params=_cparams("arbitrary", "arbitrary"), name="dsa",
    )(qT, qiT, wT, k, vT, ki, g1)


def _mem_kv_body(mem_ref, mn_ref, wkv_ref, mkn_ref, mkT_ref, mv_ref):
    d = mem_ref.shape[2]
    dm = d // MEM_HEADS
    m = _rms(mem_ref[0], mn_ref[...]).astype(_BF)
    kv = _dot(m, wkv_ref[...])
    mk = jnp.concatenate([_rms(kv[:, i * dm:(i + 1) * dm], mkn_ref[...]) for i in range(MEM_HEADS)], axis=1)
    mkT_ref[0] = mk.T.astype(_BF)
    mv_ref[0] = kv[:, d:].astype(_BF)


def _mem_kv(mem, mn, wkv, mkn):
    b, m, d = mem.shape
    row = lambda v: v.reshape(1, -1)
    return pl.pallas_call(
        _mem_kv_body, grid=(b,),
        in_specs=[pl.BlockSpec((1, m, d), lambda i: (i, 0, 0)), _full_spec((1, d)), _full_spec(wkv.shape),
                  _full_spec((1, d // MEM_HEADS))],
        out_specs=(pl.BlockSpec((1, d, m), lambda i: (i, 0, 0)), pl.BlockSpec((1, m, d), lambda i: (i, 0, 0))),
        out_shape=(jax.ShapeDtypeStruct((b, d, m), _BF), jax.ShapeDtypeStruct((b, m, d), _BF)),
        compiler_params=_cparams("arbitrary"), name="mem_kv",
    )(mem, row(mn), wkv, row(mkn))


def _mem_body(x_ref, nm_ref, wmq_ref, wg2_ref, mqn_ref, mkT_ref, mv_ref, out_ref):
    d = x_ref.shape[2]
    dm = d // MEM_HEADS
    h = _rms(x_ref[0], nm_ref[...]).astype(_BF)
    mq = _dot(h, wmq_ref[...])
    outs = []
    for i in range(MEM_HEADS):
        qh = (_rms(mq[:, i * dm:(i + 1) * dm], mqn_ref[...]) * (float(dm) ** -0.5)).astype(_BF)
        s = _dot(qh, mkT_ref[0, i * dm:(i + 1) * dm, :])
        p = jnp.exp(s - jnp.max(s, axis=-1, keepdims=True))
        o = _dot(p.astype(_BF), mv_ref[0, :, i * dm:(i + 1) * dm])
        outs.append(o / jnp.sum(p, axis=-1, keepdims=True))
    out_ref[0] = (jnp.concatenate(outs, axis=1) * _sigmoid(_dot(h, wg2_ref[...]))).astype(_BF)


def _mem_branch(x, nm, wmq, wg2, mqn, mkT, mv, ts):
    b, s, d = x.shape
    m = mv.shape[1]
    row = lambda v: v.reshape(1, -1)
    return pl.pallas_call(
        _mem_body, grid=(b, s // ts),
        in_specs=[pl.BlockSpec((1, ts, d), lambda i, j: (i, j, 0)), _full_spec((1, d)), _full_spec(wmq.shape),
                  _full_spec(wg2.shape), _full_spec((1, d // MEM_HEADS)),
                  pl.BlockSpec((1, d, m), lambda i, j: (i, 0, 0)), pl.BlockSpec((1, m, d), lambda i, j: (i, 0, 0))],
        out_specs=pl.BlockSpec((1, ts, d), lambda i, j: (i, j, 0)),
        out_shape=jax.ShapeDtypeStruct((b, s, d), _BF),
        compiler_params=_cparams("arbitrary", "arbitrary"), name="mem",
    )(x, row(nm), wmq, wg2, row(mqn), mkT, mv)


def _mid_body(x_ref, a_ref, b_ref, c_ref, wo_ref, nf_ref, wpqT_ref, sk_ref, x1_ref, h2T_ref, scT_ref):
    merged = (a_ref[...].astype(_F32) + b_ref[...].astype(_F32) + c_ref[...].astype(_F32)).astype(_BF)
    x1 = x_ref[...] + _dot(merged, wo_ref[...])
    x1_ref[...] = x1
    h2T = _rms(x1, nf_ref[...]).T.astype(_BF)
    h2T_ref[...] = h2T
    pqT = _dot(wpqT_ref[...], h2T).astype(_BF)
    for c in range(sk_ref.shape[0]):
        scT_ref[c * PEER_KEYS:(c + 1) * PEER_KEYS, :] = _dot(sk_ref[c], pqT[c * LANES:(c + 1) * LANES, :])


def _mid(x2, ma, mb, mc, wo, nf, wpqT, sk, ts):
    t, d = x2.shape
    nsc = sk.shape[0] * PEER_KEYS
    tile = pl.BlockSpec((ts, d), lambda i: (i, 0))
    return pl.pallas_call(
        _mid_body, grid=(t // ts,),
        in_specs=[tile, tile, tile, tile, _full_spec(wo.shape), _full_spec((1, d)), _full_spec(wpqT.shape),
                  _full_spec(sk.shape)],
        out_specs=(tile, pl.BlockSpec((d, ts), lambda i: (0, i)), pl.BlockSpec((nsc, ts), lambda i: (0, i))),
        out_shape=(jax.ShapeDtypeStruct((t, d), _F32), jax.ShapeDtypeStruct((d, t), _BF),
                   jax.ShapeDtypeStruct((nsc, t), _F32)),
        compiler_params=_cparams("arbitrary"), name="mid",
    )(x2, ma, mb, mc, wo, nf.reshape(1, -1), wpqT, sk)


def _top_values(s, n):
    groups = [s[r * SUBLANES:(r + 1) * SUBLANES, :] for r in range(s.shape[0] // SUBLANES)]
    size = pl.next_power_of_2(len(groups))
    groups += [jnp.full_like(groups[0], -jnp.inf)] * (size - len(groups))
    k = 2
    while k <= size:
        j = k // 2
        while j >= 1:
            for i in range(size):
                l = i ^ j
                if l > i:
                    hi, lo = jnp.maximum(groups[i], groups[l]), jnp.minimum(groups[i], groups[l])
                    groups[i], groups[l] = (hi, lo) if (i & k) == 0 else (lo, hi)
            j //= 2
        k *= 2
    vals = []
    for it in range(n):
        m = jnp.max(groups[0], axis=0, keepdims=True)
        vals.append(m)
        remaining = n - 1 - it
        head = groups[0] == m
        for d in range(min(remaining, size)):
            below = groups[d + 1] if d + 1 < size else jnp.full_like(m, -jnp.inf)
            groups[d] = jnp.where(head, below, groups[d])
    return vals


def _route_body(scT_ref, tau_ref, c_ref, top2_ref):
    def head(h, carry):
        r1 = pl.multiple_of(h * 2 * PEER_KEYS, PEER_KEYS)
        r2 = pl.multiple_of(h * 2 * PEER_KEYS + PEER_KEYS, PEER_KEYS)
        n = PEER_TOPK + 1
        v1 = _top_values(scT_ref[pl.ds(r1, PEER_KEYS), :], n)
        v2 = _top_values(scT_ref[pl.ds(r2, PEER_KEYS), :], n)
        edge = SUBLANES
        assert 2 * (edge + 1) > n
        pad = [jnp.full_like(v1[0], -jnp.inf)] * (-n % SUBLANES)
        col1 = jnp.concatenate(v1 + pad, axis=0)
        col2 = jnp.concatenate(v2 + pad, axis=0)
        cand = jnp.concatenate([v1[a] + col2[:edge] for a in range(edge)]
                               + [v1[0] + col2[edge:], v2[0] + col1[edge:]], axis=0)
        best = _top_values(cand, n)
        tau = best[PEER_TOPK - 1]
        top = v1[0] + v2[0]
        z = jnp.sum(jnp.where(cand >= tau, jnp.exp(cand - top), 0.0), axis=0, keepdims=True)
        tau_ref[pl.ds(h, 1), :] = 0.5 * (tau + best[PEER_TOPK])
        c_ref[pl.ds(h, 1), :] = top + jnp.log(z)
        top2_ref[pl.ds(pl.multiple_of(h * TOP_ROWS, SUBLANES), TOP_ROWS), :] = col2
        return carry

    lax.fori_loop(0, PEER_HEADS, head, 0, unroll=2)


def _route(scT, te):
    nsc, t = scT.shape
    spec = pl.BlockSpec((PEER_HEADS, te), lambda i: (0, i))
    return pl.pallas_call(
        _route_body, grid=(t // te,),
        in_specs=[pl.BlockSpec((nsc, te), lambda i: (0, i))],
        out_specs=(spec, spec, pl.BlockSpec((PEER_HEADS * TOP_ROWS, te), lambda i: (0, i))),
        out_shape=(jax.ShapeDtypeStruct((PEER_HEADS, t), _F32),) * 2
        + (jax.ShapeDtypeStruct((PEER_HEADS * TOP_ROWS, t), _F32),),
        compiler_params=_cparams("arbitrary"), name="route",
    )(scT)


def _experts_body(h2T_ref, scT_ref, tau_ref, c_ref, top2_ref, u_ref, vT_ref, x1_ref, out_ref,
                  acc_ref, e2_ref, rank_ref, cnt_ref, scale_ref, act_ref, g_ref, rows_ref, *, rows_per_step,
                  n_blocks):
    e = pl.program_id(1)
    nk = PEER_KEYS
    ts = act_ref.shape[1]

    @pl.when(e == 0)
    def _():
        acc_ref[...] = jnp.zeros_like(acc_ref)
        for h in range(PEER_HEADS):
            best = [top2_ref[h * TOP_ROWS + b:h * TOP_ROWS + b + 1, :] for b in range(PEER_TOPK + 1)]
            s1 = scT_ref[2 * h * nk:(2 * h + 1) * nk, :]
            s2 = scT_ref[(2 * h + 1) * nk:(2 * h + 2) * nk, :]
            e2_ref[h * nk:(h + 1) * nk, :] = jnp.exp(s2 - best[0]).astype(_BF)
            rank = jnp.zeros((nk, ts), _F32)
            for b in range(PEER_TOPK):
                rank = jnp.where(best[b] > s2, float(b + 1), rank)
            rank_ref[h * nk:(h + 1) * nk, :] = rank.astype(_BF)
            thr = tau_ref[h:h + 1, :] - s1
            cnt = jnp.zeros((nk, ts), _F32)
            for b in range(PEER_TOPK + 1):
                cnt = jnp.where(best[b] >= thr, float(b + 1), cnt)
            cnt_ref[h * nk:(h + 1) * nk, :] = cnt
            scale_ref[h * nk:(h + 1) * nk, :] = 0.5 * jnp.exp(s1 + best[0] - c_ref[h:h + 1, :])

    n_rows = rows_per_step * PEER_HEADS
    first_group = 2
    zero = jnp.zeros((), _BF)

    def first_matmul(i):
        rows = slice(i * first_group * nk, (i + 1) * first_group * nk)
        act_ref[rows, :] = _dot(u_ref[rows, :], h2T_ref[...])

    for i in range(rows_per_step):
        for h in range(PEER_HEADS):
            src = h * nk + e * rows_per_step + i
            r = i * PEER_HEADS + h
            rows_ref[r:r + 1, :] = cnt_ref[pl.ds(src, 1), :]
            rows_ref[n_rows + r:n_rows + r + 1, :] = scale_ref[pl.ds(src, 1), :]

    second_group = 4
    first_matmul(0)
    for i in range(rows_per_step):
        if i % first_group == 0 and i // first_group + 1 < rows_per_step // first_group:
            first_matmul(i // first_group + 1)
        wgt = jnp.zeros((nk, ts), _BF)
        for h in range(PEER_HEADS):
            r = i * PEER_HEADS + h
            cnt = rows_ref[r:r + 1, :].astype(_BF)
            scale = rows_ref[n_rows + r:n_rows + r + 1, :].astype(_BF)
            hit = jnp.where(rank_ref[h * nk:(h + 1) * nk, :] < cnt, e2_ref[h * nk:(h + 1) * nk, :], zero)
            wgt = wgt + hit * scale
        x = act_ref[i * nk:(i + 1) * nk, :].astype(_BF)
        t = jnp.tanh(x * (jnp.asarray(GELU_C0, _BF) + jnp.asarray(GELU_C1, _BF) * (x * x)))
        g_ref[i * nk:(i + 1) * nk, :] = (x + x * t) * wgt
        if i % second_group == second_group - 1:
            cols = slice((i + 1 - second_group) * nk, (i + 1) * nk)
            acc_ref[...] += _dot(vT_ref[0, :, cols], g_ref[cols, :])

    @pl.when(e == n_blocks - 1)
    def _():
        out_ref[...] = x1_ref[...] + acc_ref[...].T


def _experts(h2T, scT, tau, cn, top2, u, vT, x1, ts, rows_per_step):
    d, t = h2T.shape
    nsc = scT.shape[0]
    ne = u.shape[0]
    eb = rows_per_step * PEER_KEYS
    nb = ne // eb
    nhk = PEER_HEADS * PEER_KEYS
    col = lambda n: pl.BlockSpec((n, ts), lambda i, e: (0, i))
    tile = pl.BlockSpec((ts, d), lambda i, e: (i, 0))
    return pl.pallas_call(
        functools.partial(_experts_body, rows_per_step=rows_per_step, n_blocks=nb),
        grid=(t // ts, nb),
        in_specs=[col(d), col(nsc), col(PEER_HEADS), col(PEER_HEADS), col(top2.shape[0]),
                  pl.BlockSpec((eb, d), lambda i, e: (e, 0)), pl.BlockSpec((1, d, eb), lambda i, e: (e, 0, 0)), tile],
        out_specs=tile,
        out_shape=jax.ShapeDtypeStruct((t, d), _F32),
        scratch_shapes=[pltpu.VMEM((d, ts), _F32), pltpu.VMEM((nhk, ts), _BF), pltpu.VMEM((nhk, ts), _BF),
                        pltpu.VMEM((nhk, ts), _F32), pltpu.VMEM((nhk, ts), _F32),
                        pltpu.VMEM((eb, ts), _F32), pltpu.VMEM((eb, ts), _BF),
                        pltpu.VMEM((2 * rows_per_step * PEER_HEADS, ts), _F32)],
        compiler_params=_cparams("arbitrary", "arbitrary"), name="experts",
    )(h2T, scT, tau, cn, top2, u, vT, x1)


def _block_diag(w):
    n, c, d = w.shape
    return jnp.einsum("ncd,nm->ncmd", w, jnp.eye(n, dtype=w.dtype)).reshape(n * c, n * d)


def _rope_tables(s, dim, reps):
    half = dim // 2
    freq = ROPE_THETA ** (-jnp.arange(half, dtype=_F32) / half)
    ang = jnp.arange(s, dtype=_F32)[:, None] * freq[None, :]
    cos = jnp.tile(jnp.concatenate([jnp.cos(ang), jnp.cos(ang)], axis=1), (1, reps))
    sin = jnp.tile(jnp.concatenate([-jnp.sin(ang), jnp.sin(ang)], axis=1), (1, reps))
    return cos, sin


def _tile(n, pref):
    return pref if n % pref == 0 else n


def _layer(x, mem, p, l):
    b, s, d = x.shape
    bf = lambda a: a.astype(_BF)
    w_in = p["w_in"][l]
    n_q, n_kv = ATT_HEADS * LANES, ATT_KV_HEADS * LANES
    n_qi = IDX_HEADS * IDX_DIM
    sizes = (d, d, n_q, n_kv, n_kv, n_qi, IDX_DIM, IDX_HEADS, d, 3 * d)
    offs = [0]
    for n in sizes:
        offs.append(offs[-1] + n)
    wlx, wlg, wq, wk, wv, wqi, wki, wwi, wmq, wgates = (w_in[:, offs[i]:offs[i + 1]] for i in range(len(sizes)))
    wki2 = jnp.concatenate([wki, wki], axis=1)
    wwi_p = jnp.pad(wwi, ((0, 0), (0, LANES - IDX_HEADS)))
    wg0, wg1, wg2 = wgates[:, :d], wgates[:, d:2 * d], wgates[:, 2 * d:]
    nm = p["norm_mix"][l]

    ts_a = _tile(s, 512)
    m_lru = _lru_branch(x, nm, bf(wlx), bf(wlg), bf(wg0), p["conv_w"][l], p["conv_b"][l],
                        bf(_block_diag(p["lru_wa"][l])), p["lru_ba"][l].reshape(-1),
                        bf(_block_diag(p["lru_wi"][l])), p["lru_bi"][l].reshape(-1), p["lru_lambda"][l], ts_a)

    tk = _tile(s, 512)
    c128, s128 = _rope_tables(s, LANES, 1)
    c64, s64 = _rope_tables(s, IDX_DIM, LANES // IDX_DIM)
    ikn2 = jnp.concatenate([p["idx_k_norm"][l]] * 2)
    qT, k, vT, qiT, ki, wT, g1 = _dsa_proj(x, nm, bf(wq), bf(wk), bf(wv), bf(wqi), bf(wki2), bf(wwi_p), bf(wg1),
                                           p["q_norm"][l], p["k_norm"][l], ikn2, c128, s128, c64, s64, tk)
    m_att = _dsa(qT, qiT, wT, k, vT, ki, g1, min(TOPK_MAX, s // 4), _tile(s, 256), tk)

    mkT, mv = _mem_kv(mem, p["mem_norm"][l], bf(p["w_mem_kv"][l]), p["mem_k_norm"][l])
    m_mem = _mem_branch(x, nm, bf(wmq), bf(wg2), p["mem_q_norm"][l], mkT, mv, _tile(s, 512))

    t = b * s
    flat = lambda a: a.reshape(t, d)
    sk = bf(p["peer_subkeys"][l].reshape(2 * PEER_HEADS, PEER_KEYS, -1))
    x1, h2T, scT = _mid(flat(x), flat(m_lru), flat(m_att), flat(m_mem), bf(p["w_out"][l]), p["norm_ffn"][l],
                        bf(p["peer_wq"][l].T), sk, _tile(t, 512))
    tau, cn, top2 = _route(scT, _tile(t, 256))
    rows_per_step = 16
    vT = bf(p["peer_v"][l]).reshape(-1, rows_per_step * PEER_KEYS, d).transpose(0, 2, 1)
    out = _experts(h2T, scT, tau, cn, top2, bf(p["peer_u"][l]), vT, x1, _tile(t, 512), rows_per_step)
    return out.reshape(b, s, d)


def kernel(x, mem, norm_mix, w_in, conv_w, conv_b, lru_wa, lru_ba, lru_wi, lru_bi, lru_lambda, q_norm, k_norm,
           idx_k_norm, mem_norm, w_mem_kv, mem_q_norm, mem_k_norm, w_out, norm_ffn, peer_wq, peer_subkeys,
           peer_u, peer_v):
    p = dict(norm_mix=norm_mix, w_in=w_in, conv_w=conv_w, conv_b=conv_b, lru_wa=lru_wa, lru_ba=lru_ba,
             lru_wi=lru_wi, lru_bi=lru_bi, lru_lambda=lru_lambda, q_norm=q_norm, k_norm=k_norm,
             idx_k_norm=idx_k_norm, mem_norm=mem_norm, w_mem_kv=w_mem_kv, mem_q_norm=mem_q_norm,
             mem_k_norm=mem_k_norm, w_out=w_out, norm_ffn=norm_ffn, peer_wq=peer_wq, peer_subkeys=peer_subkeys,
             peer_u=peer_u, peer_v=peer_v)
    for l in range(norm_mix.shape[0]):
        x = _layer(x, mem, p, l)
    return x
```

```python
import functools

import jax
import jax.numpy as jnp
from jax import lax
from jax.experimental import pallas as pl
from jax.experimental.pallas import tpu as pltpu

_F32 = jnp.float32
_BF = jnp.bfloat16

EPS = 1e-6
ROPE_THETA = 10000.0
LRU_C = 8.0
CONV_WIDTH = 4
ATT_HEADS = 8
ATT_KV_HEADS = 2
ATT_GROUP = ATT_HEADS // ATT_KV_HEADS
IDX_HEADS = 8
IDX_DIM = 64
TOPK_MAX = 256
MEM_HEADS = 4
PEER_HEADS = 8
PEER_KEYS = 128
PEER_TOPK = 16
TOP_ROWS = 24

LANES = 128
SUBLANES = 8
ONES_ROWS = 2 * SUBLANES
VMEM_LIMIT_BYTES = 56 * 1024 * 1024
MASK_VALUE = -1e30
FLT_MAX = 3.4028234663852886e38
INT_MIN = -(2 ** 31)


def _cparams(*sem):
    return pltpu.CompilerParams(dimension_semantics=sem, vmem_limit_bytes=VMEM_LIMIT_BYTES)


def _rms(x, g):
    return x * lax.rsqrt(jnp.mean(x * x, axis=-1, keepdims=True) + EPS) * g


GELU_C0 = 0.7978845608028654
GELU_C1 = GELU_C0 * 0.044715


def _gelu(x):
    return 0.5 * x * (1.0 + jnp.tanh(x * (GELU_C0 + GELU_C1 * (x * x))))


def _sigmoid(x):
    return 0.5 * jnp.tanh(0.5 * x) + 0.5


def _dot(a, b):
    return jnp.dot(a, b, preferred_element_type=_F32)


def _full_spec(shape):
    n = len(shape)
    return pl.BlockSpec(shape, lambda *_: (0,) * n)


def _lru_body(x_ref, nm_ref, wlx_ref, wlg_ref, wg0_ref, cw_ref, cb_ref, wa_ref, ba_ref, wi_ref,
              bi_ref, lam_ref, out_ref, buf_ref, hc_ref, a_ref, b_ref):
    ts = x_ref.shape[1]
    c_dim = out_ref.shape[2]

    @pl.when(pl.program_id(1) == 0)
    def _():
        buf_ref[0:SUBLANES, :] = jnp.zeros((SUBLANES, c_dim), _F32)
        hc_ref[...] = jnp.zeros_like(hc_ref)

    h = _rms(x_ref[0], nm_ref[...]).astype(_BF)
    lx = _dot(h, wlx_ref[...])
    buf_ref[SUBLANES:SUBLANES + ts, :] = lx
    ext = buf_ref[...]
    cw = cw_ref[...]
    xc = cb_ref[...] + cw[3:4] * lx
    for j in range(CONV_WIDTH - 1):
        d = CONV_WIDTH - 1 - j
        xc = xc + cw[j:j + 1] * pltpu.roll(ext, d, 0)[SUBLANES:, :]
    buf_ref[0:SUBLANES, :] = lx[ts - SUBLANES:, :]

    xcb = xc.astype(_BF)
    r = _sigmoid(_dot(xcb, wa_ref[...]) + ba_ref[...])
    ig = _sigmoid(_dot(xcb, wi_ref[...]) + bi_ref[...])
    lam = lam_ref[...]
    softplus_neg_lam = jnp.maximum(-lam, 0.0) + jnp.log1p(jnp.exp(-jnp.abs(lam)))
    log_a = (-LRU_C) * r * softplus_neg_lam
    a = jnp.exp(log_a)
    y = jnp.tanh(-log_a) * (a * a + 1.0)
    u = jnp.where(y > 0.0, y * lax.rsqrt(y), 0.0) * (ig * xc)

    a = a.reshape(ts // SUBLANES, SUBLANES, c_dim)
    u = u.reshape(ts // SUBLANES, SUBLANES, c_dim)
    rid = lax.broadcasted_iota(jnp.int32, a.shape, 1)
    for d in (1, 2, 4):
        ok = rid >= d
        a_s = pltpu.roll(a, d, 1)
        u_s = pltpu.roll(u, d, 1)
        u = jnp.where(ok, a * u_s + u, u)
        a = jnp.where(ok, a * a_s, a)
    a_ref[...] = a.reshape(ts, c_dim)
    b_ref[...] = u.reshape(ts, c_dim)

    def step(g, hp):
        r0 = pl.multiple_of(g * SUBLANES, SUBLANES)
        hcur = a_ref[pl.ds(r0, SUBLANES), :] * hp + b_ref[pl.ds(r0, SUBLANES), :]
        b_ref[pl.ds(r0, SUBLANES), :] = hcur
        return jnp.broadcast_to(hcur[SUBLANES - 1:SUBLANES, :], (SUBLANES, c_dim))

    hc_ref[...] = lax.fori_loop(0, ts // SUBLANES, step, hc_ref[...])
    gate = _gelu(_dot(h, wlg_ref[...])) * _sigmoid(_dot(h, wg0_ref[...]))
    out_ref[0] = (b_ref[...] * gate).astype(_BF)


def _lru_branch(x, nm, wlx, wlg, wg0, cw, cb, wa, ba, wi, bi, lam, ts):
    b, s, d = x.shape
    c = wlx.shape[1]
    row = lambda v: v.reshape(1, -1)
    args = (x, row(nm), wlx, wlg, wg0, cw, row(cb), wa, row(ba), wi, row(bi), row(lam))
    in_specs = [pl.BlockSpec((1, ts, d), lambda i, j: (i, j, 0))]
    in_specs += [_full_spec(a.shape) for a in args[1:]]
    return pl.pallas_call(
        _lru_body,
        grid=(b, s // ts),
        in_specs=in_specs,
        out_specs=pl.BlockSpec((1, ts, c), lambda i, j: (i, j, 0)),
        out_shape=jax.ShapeDtypeStruct((b, s, c), _BF),
        scratch_shapes=[pltpu.VMEM((ts + SUBLANES, c), _F32), pltpu.VMEM((SUBLANES, c), _F32),
                        pltpu.VMEM((ts, c), _F32), pltpu.VMEM((ts, c), _F32)],
        compiler_params=_cparams("arbitrary", "arbitrary"),
        name="lru",
    )(*args)


def _rope128(x, cos, sin):
    return x * cos + pltpu.roll(x, LANES // 2, 1) * sin


def _rope64(x, cos, sin, first_half):
    rot = jnp.where(first_half, pltpu.roll(x, LANES - IDX_DIM // 2, 1), pltpu.roll(x, IDX_DIM // 2, 1))
    return x * cos + rot * sin


def _dsa_proj_body(x_ref, nm_ref, wq_ref, wk_ref, wv_ref, wqi_ref, wki_ref, wwi_ref, wg1_ref,
                   qn_ref, kn_ref, ikn_ref, c128_ref, s128_ref, c64_ref, s64_ref,
                   qT_ref, k_ref, vT_ref, qiT_ref, ki_ref, wT_ref, g1_ref):
    ts = x_ref.shape[1]
    h = _rms(x_ref[0], nm_ref[...]).astype(_BF)
    c128, s128 = c128_ref[...], s128_ref[...]
    c64, s64 = c64_ref[...], s64_ref[...]
    first_half = (lax.broadcasted_iota(jnp.int32, (ts, LANES), 1) & (IDX_DIM - 1)) < IDX_DIM // 2

    def head_norm_rope(t, g, scale):
        outs = []
        for i in range(t.shape[1] // LANES):
            th = _rms(t[:, i * LANES:(i + 1) * LANES], g)
            outs.append(_rope128(th, c128, s128) * scale)
        return jnp.concatenate(outs, axis=1)

    q = head_norm_rope(_dot(h, wq_ref[...]), qn_ref[...], float(LANES) ** -0.5)
    qT_ref[0] = q.T.astype(_BF)
    k_ref[0] = head_norm_rope(_dot(h, wk_ref[...]), kn_ref[...], 1.0).astype(_BF)
    vT = _dot(h, wv_ref[...]).T.astype(_BF)
    for g in range(ATT_KV_HEADS):
        vT_ref[0, 0, g, :LANES, :] = vT[g * LANES:(g + 1) * LANES, :]
        vT_ref[0, 0, g, LANES:, :] = jnp.ones((ONES_ROWS, ts), _BF)

    qi = _dot(h, wqi_ref[...])
    qi = jnp.concatenate(
        [_rope64(qi[:, i * LANES:(i + 1) * LANES], c64, s64, first_half) for i in range(qi.shape[1] // LANES)],
        axis=1) * (float(IDX_DIM) ** -0.5)
    qiT_ref[0] = qi.T.astype(_BF)
    ki = _rope64(_rms(_dot(h, wki_ref[...]), ikn_ref[...]), c64, s64, first_half)
    ki_ref[0] = ki[:, :IDX_DIM].astype(_BF)
    wi = _dot(h, wwi_ref[...]) * (float(IDX_HEADS) ** -0.5)
    wT_ref[0] = wi.T[:IDX_HEADS, :]
    g1_ref[0] = _sigmoid(_dot(h, wg1_ref[...])).astype(_BF)


def _dsa_proj(x, nm, wq, wk, wv, wqi, wki2, wwi, wg1, qn, kn, ikn2, c128, s128, c64, s64, ts):
    b, s, d = x.shape
    row = lambda v: v.reshape(1, -1)
    nq, nk, nqi = wq.shape[1], wk.shape[1], wqi.shape[1]
    args = (x, row(nm), wq, wk, wv, wqi, wki2, wwi, wg1, row(qn), row(kn), row(ikn2), c128, s128, c64, s64)
    tile = lambda n: pl.BlockSpec((1, ts, n), lambda i, j: (i, j, 0))
    tile_t = lambda n: pl.BlockSpec((1, n, ts), lambda i, j: (i, 0, j))
    tab = pl.BlockSpec((ts, LANES), lambda i, j: (j, 0))
    in_specs = [tile(d)] + [_full_spec(a.shape) for a in args[1:12]] + [tab] * 4
    out_shape = (
        jax.ShapeDtypeStruct((b, nq, s), _BF),
        jax.ShapeDtypeStruct((b, s, nk), _BF),
        jax.ShapeDtypeStruct((b, s // ts, ATT_KV_HEADS, LANES + ONES_ROWS, ts), _BF),
        jax.ShapeDtypeStruct((b, nqi, s), _BF),
        jax.ShapeDtypeStruct((b, s, IDX_DIM), _BF),
        jax.ShapeDtypeStruct((b, IDX_HEADS, s), _F32),
        jax.ShapeDtypeStruct((b, s, d), _BF),
    )
    out_specs = (tile_t(nq), tile(nk),
                 pl.BlockSpec((1, 1, ATT_KV_HEADS, LANES + ONES_ROWS, ts), lambda i, j: (i, j, 0, 0, 0)), tile_t(nqi),
                 tile(IDX_DIM), tile_t(IDX_HEADS), tile(d))
    return pl.pallas_call(
        _dsa_proj_body, grid=(b, s // ts), in_specs=in_specs, out_specs=out_specs, out_shape=out_shape,
        compiler_params=_cparams("arbitrary", "arbitrary"), name="dsa_proj",
    )(*args)


def _key_to_float(u):
    key = u ^ jnp.int32(INT_MIN)
    bits = key ^ ((key >> 31) & jnp.int32(0x7FFFFFFF))
    return jnp.where((u >> 23) == 0, -jnp.inf, lax.bitcast_convert_type(bits, _F32))


def _high_half(x):
    bits = lax.bitcast_convert_type(x, jnp.int32) & jnp.int32(-65536)
    return lax.bitcast_convert_type(bits, _F32).astype(_BF)


def _dsa_body(qT_ref, qiT_ref, wT_ref, k_ref, vT_ref, ki_ref, g1_ref, out_ref,
              sc_ref, sh_ref, m_ref, l_ref, acc_ref, bias_ref, s_ref, p_ref, *, topk, tk):
    tq = qT_ref.shape[2]
    dh = LANES
    q0 = pl.program_id(1) * tq
    n_chunks = (q0 + tq + tk - 1) // tk

    qicat = jnp.concatenate([qiT_ref[0, h * IDX_DIM:(h + 1) * IDX_DIM, :] for h in range(IDX_HEADS)], axis=1)
    w = wT_ref[0]
    qpos = q0 + lax.broadcasted_iota(jnp.int32, (tk, tq), 1)
    krow = lax.broadcasted_iota(jnp.int32, (tk, tq), 0)

    def score_chunk(c, carry):
        r0 = pl.multiple_of(c * tk, tk)
        lg = _dot(ki_ref[0, pl.ds(r0, tk), :], qicat)
        s = jnp.zeros((tk, tq), _F32)
        for h in range(IDX_HEADS):
            s = s + jnp.maximum(lg[:, h * tq:(h + 1) * tq], 0.0) * w[h:h + 1, :]
        s = jnp.where(krow + r0 <= qpos, s, -jnp.inf)
        sc_ref[pl.ds(r0, tk), :] = s
        sh_ref[pl.ds(r0, tk), :] = _high_half(s)
        return carry

    lax.fori_loop(0, n_chunks, score_chunk, 0)

    def make_count(nc, ref, rows, dtype):
        def count_ge(cand):
            one, zero = jnp.ones((), dtype), jnp.zeros((), dtype)
            parts = [jnp.zeros((rows, tq), dtype) for _ in range(4)]
            n = 0
            for c in range(nc):
                hit = jnp.where(ref[c * tk:(c + 1) * tk, :] >= cand, one, zero)
                for r in range(tk // rows):
                    parts[n % 4] = parts[n % 4] + hit[r * rows:(r + 1) * rows, :]
                    n += 1
            parts = [p.astype(_F32) for p in parts]
            return jnp.sum((parts[0] + parts[1]) + (parts[2] + parts[3]), axis=0, keepdims=True)
        return count_ge

    max_chunks = sc_ref.shape[0] // tk
    assert max_chunks * tk // (2 * SUBLANES) <= 4 * 64
    coarse = [make_count(nc, sh_ref, 2 * SUBLANES, _BF) for nc in range(1, max_chunks + 1)]
    fine = [make_count(nc, sc_ref, SUBLANES, _F32) for nc in range(1, max_chunks + 1)]

    def bisect_high(i, carry):
        u, n_ge = carry
        trial = u | lax.shift_left(jnp.int32(1), 31 - i)
        cnt = lax.switch(n_chunks - 1, coarse, _high_half(_key_to_float(trial)))
        keep = cnt >= float(topk)
        return jnp.where(keep, trial, u), jnp.where(keep, cnt, n_ge)

    def bisect_low(i, carry):
        u, n_ge = carry
        trial = u | lax.shift_left(jnp.int32(1), 31 - i)
        cnt = lax.switch(n_chunks - 1, fine, _key_to_float(trial))
        keep = cnt >= float(topk)
        return jnp.where(keep, trial, u), jnp.where(keep, cnt, n_ge)

    carry = lax.fori_loop(0, 16, bisect_high, (jnp.zeros((1, tq), jnp.int32), jnp.zeros((1, tq), _F32)))
    u, n_ge = lax.fori_loop(16, 32, bisect_low, carry)
    tau = jnp.maximum(_key_to_float(u), -FLT_MAX)

    @pl.when(jnp.max(jnp.where(tau > -FLT_MAX, n_ge, 0.0)) > float(topk))
    def _():
        def count_gt(c, cnt):
            r0 = pl.multiple_of(c * tk, tk)
            return cnt + jnp.sum(jnp.where(sc_ref[pl.ds(r0, tk), :] > tau, 1.0, 0.0), axis=0, keepdims=True)

        need = float(topk) - lax.fori_loop(0, n_chunks, count_gt, jnp.zeros((1, tq), _F32))
        before = (lax.broadcasted_iota(jnp.int32, (tk, tk), 1)
                  < lax.broadcasted_iota(jnp.int32, (tk, tk), 0)).astype(_BF)

        def strike(c, seen):
            r0 = pl.multiple_of(c * tk, tk)
            blk = sc_ref[pl.ds(r0, tk), :]
            tied = blk == tau
            tied_b = jnp.where(tied, 1.0, 0.0).astype(_BF)
            earlier = _dot(before, tied_b) + seen
            sc_ref[pl.ds(r0, tk), :] = jnp.where(tied & (earlier >= need), -jnp.inf, blk)
            return seen + jnp.sum(tied_b.astype(_F32), axis=0, keepdims=True)

        lax.fori_loop(0, n_chunks, strike, jnp.zeros((1, tq), _F32))

    m_ref[...] = jnp.full(m_ref.shape, MASK_VALUE, _F32)
    l_ref[...] = jnp.zeros_like(l_ref)
    acc_ref[...] = jnp.zeros_like(acc_ref)

    sub = 8 * SUBLANES
    groups = [slice(r * SUBLANES, (r + 1) * SUBLANES) for r in range(sub // SUBLANES)]

    last_chunk = sc_ref.shape[0] // tk - 1
    ahead = 2

    def logits(c, h):
        r0 = pl.multiple_of(c * tk, tk)
        g = h // ATT_GROUP
        s_ref[h] = _dot(k_ref[0, pl.ds(r0, tk), g * dh:(g + 1) * dh], qT_ref[0, h * dh:(h + 1) * dh, :])

    for h in range(ahead):
        logits(0, h)

    def att_chunk(c, carry):
        r0 = pl.multiple_of(c * tk, tk)
        bias_ref[...] = jnp.where(sc_ref[pl.ds(r0, tk), :] >= tau, 0.0, MASK_VALUE)
        for h in range(ATT_HEADS):
            if h + ahead < ATT_HEADS:
                logits(c, h + ahead)
            else:
                logits(jnp.minimum(c + 1, last_chunk), h + ahead - ATT_HEADS)
            m_old = m_ref[h:h + 1, :]
            mx = jnp.broadcast_to(m_old, (SUBLANES, tq))
            for i in range(tk // sub):
                rows = slice(i * sub, (i + 1) * sub)
                sb = s_ref[h, rows, :] + bias_ref[rows, :]
                s_ref[h, rows, :] = sb
                for gr in groups:
                    mx = jnp.maximum(mx, sb[gr, :])
            m_new = jnp.max(mx, axis=0, keepdims=True)
            for i in range(tk // sub):
                rows = slice(i * sub, (i + 1) * sub)
                p_ref[h, rows, :] = jnp.exp((s_ref[h, rows, :] - m_new).astype(_BF))
            alpha = jnp.exp(m_old - m_new)
            pv = _dot(vT_ref[0, c, h // ATT_GROUP], p_ref[h])
            l_ref[h:h + 1, :] = alpha * l_ref[h:h + 1, :] + pv[dh:dh + 1, :]
            acc_ref[h * dh:(h + 1) * dh, :] = alpha * acc_ref[h * dh:(h + 1) * dh, :] + pv[:dh, :]
            m_ref[h:h + 1, :] = m_new
        return carry

    lax.fori_loop(0, n_chunks, att_chunk, 0)
    outs = []
    for h in range(ATT_HEADS):
        o = acc_ref[h * dh:(h + 1) * dh, :] / l_ref[h:h + 1, :]
        outs.append(o.T)
    out_ref[0] = (jnp.concatenate(outs, axis=1) * g1_ref[0].astype(_F32)).astype(_BF)


def _dsa(qT, qiT, wT, k, vT, ki, g1, topk, tq, tk):
    b, nq, s = qT.shape
    d = g1.shape[2]
    in_specs = [
        pl.BlockSpec((1, nq, tq), lambda i, j: (i, 0, j)),
        pl.BlockSpec((1, qiT.shape[1], tq), lambda i, j: (i, 0, j)),
        pl.BlockSpec((1, IDX_HEADS, tq), lambda i, j: (i, 0, j)),
        pl.BlockSpec((1, s, k.shape[2]), lambda i, j: (i, 0, 0)),
        pl.BlockSpec((1,) + vT.shape[1:], lambda i, j: (i, 0, 0, 0, 0)),
        pl.BlockSpec((1, s, IDX_DIM), lambda i, j: (i, 0, 0)),
        pl.BlockSpec((1, tq, d), lambda i, j: (i, j, 0)),
    ]
    return pl.pallas_call(
        functools.partial(_dsa_body, topk=topk, tk=tk),
        grid=(b, s // tq), in_specs=in_specs,
        out_specs=pl.BlockSpec((1, tq, d), lambda i, j: (i, j, 0)),
        out_shape=jax.ShapeDtypeStruct((b, s, d), _BF),
        scratch_shapes=[pltpu.VMEM((s, tq), _F32), pltpu.VMEM((s, tq), _BF), pltpu.VMEM((ATT_HEADS, tq), _F32),
                        pltpu.VMEM((ATT_HEADS, tq), _F32), pltpu.VMEM((nq, tq), _F32),
                        pltpu.VMEM((tk, tq), _F32), pltpu.VMEM((ATT_HEADS, tk, tq), _F32),
                        pltpu.VMEM((ATT_HEADS, tk, tq), _BF)],
        compiler_params=_cparams("arbitrary", "arbitrary"), name="dsa",
    )(qT, qiT, wT, k, vT, ki, g1)


def _mem_kv_body(mem_ref, mn_ref, wkv_ref, mkn_ref, mkT_ref, mv_ref):
    d = mem_ref.shape[2]
    dm = d // MEM_HEADS
    m = _rms(mem_ref[0], mn_ref[...]).astype(_BF)
    kv = _dot(m, wkv_ref[...])
    mk = jnp.concatenate([_rms(kv[:, i * dm:(i + 1) * dm], mkn_ref[...]) for i in range(MEM_HEADS)], axis=1)
    mkT_ref[0] = mk.T.astype(_BF)
    mv_ref[0] = kv[:, d:].astype(_BF)


def _mem_kv(mem, mn, wkv, mkn):
    b, m, d = mem.shape
    row = lambda v: v.reshape(1, -1)
    return pl.pallas_call(
        _mem_kv_body, grid=(b,),
        in_specs=[pl.BlockSpec((1, m, d), lambda i: (i, 0, 0)), _full_spec((1, d)), _full_spec(wkv.shape),
                  _full_spec((1, d // MEM_HEADS))],
        out_specs=(pl.BlockSpec((1, d, m), lambda i: (i, 0, 0)), pl.BlockSpec((1, m, d), lambda i: (i, 0, 0))),
        out_shape=(jax.ShapeDtypeStruct((b, d, m), _BF), jax.ShapeDtypeStruct((b, m, d), _BF)),
        compiler_params=_cparams("arbitrary"), name="mem_kv",
    )(mem, row(mn), wkv, row(mkn))


def _mem_body(x_ref, nm_ref, wmq_ref, wg2_ref, mqn_ref, mkT_ref, mv_ref, out_ref):
    d = x_ref.shape[2]
    dm = d // MEM_HEADS
    h = _rms(x_ref[0], nm_ref[...]).astype(_BF)
    mq = _dot(h, wmq_ref[...])
    outs = []
    for i in range(MEM_HEADS):
        qh = (_rms(mq[:, i * dm:(i + 1) * dm], mqn_ref[...]) * (float(dm) ** -0.5)).astype(_BF)
        s = _dot(qh, mkT_ref[0, i * dm:(i + 1) * dm, :])
        p = jnp.exp(s - jnp.max(s, axis=-1, keepdims=True))
        o = _dot(p.astype(_BF), mv_ref[0, :, i * dm:(i + 1) * dm])
        outs.append(o / jnp.sum(p, axis=-1, keepdims=True))
    out_ref[0] = (jnp.concatenate(outs, axis=1) * _sigmoid(_dot(h, wg2_ref[...]))).astype(_BF)


def _mem_branch(x, nm, wmq, wg2, mqn, mkT, mv, ts):
    b, s, d = x.shape
    m = mv.shape[1]
    row = lambda v: v.reshape(1, -1)
    return pl.pallas_call(
        _mem_body, grid=(b, s // ts),
        in_specs=[pl.BlockSpec((1, ts, d), lambda i, j: (i, j, 0)), _full_spec((1, d)), _full_spec(wmq.shape),
                  _full_spec(wg2.shape), _full_spec((1, d // MEM_HEADS)),
                  pl.BlockSpec((1, d, m), lambda i, j: (i, 0, 0)), pl.BlockSpec((1, m, d), lambda i, j: (i, 0, 0))],
        out_specs=pl.BlockSpec((1, ts, d), lambda i, j: (i, j, 0)),
        out_shape=jax.ShapeDtypeStruct((b, s, d), _BF),
        compiler_params=_cparams("arbitrary", "arbitrary"), name="mem",
    )(x, row(nm), wmq, wg2, row(mqn), mkT, mv)


def _mid_body(x_ref, a_ref, b_ref, c_ref, wo_ref, nf_ref, wpqT_ref, sk_ref, x1_ref, h2T_ref, scT_ref):
    merged = (a_ref[...].astype(_F32) + b_ref[...].astype(_F32) + c_ref[...].astype(_F32)).astype(_BF)
    x1 = x_ref[...] + _dot(merged, wo_ref[...])
    x1_ref[...] = x1
    h2T = _rms(x1, nf_ref[...]).T.astype(_BF)
    h2T_ref[...] = h2T
    pqT = _dot(wpqT_ref[...], h2T).astype(_BF)
    for c in range(sk_ref.shape[0]):
        scT_ref[c * PEER_KEYS:(c + 1) * PEER_KEYS, :] = _dot(sk_ref[c], pqT[c * LANES:(c + 1) * LANES, :])


def _mid(x2, ma, mb, mc, wo, nf, wpqT, sk, ts):
    t, d = x2.shape
    nsc = sk.shape[0] * PEER_KEYS
    tile = pl.BlockSpec((ts, d), lambda i: (i, 0))
    return pl.pallas_call(
        _mid_body, grid=(t // ts,),
        in_specs=[tile, tile, tile, tile, _full_spec(wo.shape), _full_spec((1, d)), _full_spec(wpqT.shape),
                  _full_spec(sk.shape)],
        out_specs=(tile, pl.BlockSpec((d, ts), lambda i: (0, i)), pl.BlockSpec((nsc, ts), lambda i: (0, i))),
        out_shape=(jax.ShapeDtypeStruct((t, d), _F32), jax.ShapeDtypeStruct((d, t), _BF),
                   jax.ShapeDtypeStruct((nsc, t), _F32)),
        compiler_params=_cparams("arbitrary"), name="mid",
    )(x2, ma, mb, mc, wo, nf.reshape(1, -1), wpqT, sk)


def _top_values(s, n):
    groups = [s[r * SUBLANES:(r + 1) * SUBLANES, :] for r in range(s.shape[0] // SUBLANES)]
    size = pl.next_power_of_2(len(groups))
    groups += [jnp.full_like(groups[0], -jnp.inf)] * (size - len(groups))
    k = 2
    while k <= size:
        j = k // 2
        while j >= 1:
            for i in range(size):
                l = i ^ j
                if l > i:
                    hi, lo = jnp.maximum(groups[i], groups[l]), jnp.minimum(groups[i], groups[l])
                    groups[i], groups[l] = (hi, lo) if (i & k) == 0 else (lo, hi)
            j //= 2
        k *= 2
    vals = []
    for it in range(n):
        m = jnp.max(groups[0], axis=0, keepdims=True)
        vals.append(m)
        remaining = n - 1 - it
        head = groups[0] == m
        for d in range(min(remaining, size)):
            below = groups[d + 1] if d + 1 < size else jnp.full_like(m, -jnp.inf)
            groups[d] = jnp.where(head, below, groups[d])
    return vals


def _route_body(scT_ref, tau_ref, c_ref, top2_ref):
    def head(h, carry):
        r1 = pl.multiple_of(h * 2 * PEER_KEYS, PEER_KEYS)
        r2 = pl.multiple_of(h * 2 * PEER_KEYS + PEER_KEYS, PEER_KEYS)
        n = PEER_TOPK + 1
        v1 = _top_values(scT_ref[pl.ds(r1, PEER_KEYS), :], n)
        v2 = _top_values(scT_ref[pl.ds(r2, PEER_KEYS), :], n)
        edge = SUBLANES
        assert 2 * (edge + 1) > n
        pad = [jnp.full_like(v1[0], -jnp.inf)] * (-n % SUBLANES)
        col1 = jnp.concatenate(v1 + pad, axis=0)
        col2 = jnp.concatenate(v2 + pad, axis=0)
        cand = jnp.concatenate([v1[a] + col2[:edge] for a in range(edge)]
                               + [v1[0] + col2[edge:], v2[0] + col1[edge:]], axis=0)
        best = _top_values(cand, n)
        tau = best[PEER_TOPK - 1]
        top = v1[0] + v2[0]
        z = jnp.sum(jnp.where(cand >= tau, jnp.exp(cand - top), 0.0), axis=0, keepdims=True)
        tau_ref[pl.ds(h, 1), :] = 0.5 * (tau + best[PEER_TOPK])
        c_ref[pl.ds(h, 1), :] = top + jnp.log(z)
        top2_ref[pl.ds(pl.multiple_of(h * TOP_ROWS, SUBLANES), TOP_ROWS), :] = col2
        return carry

    lax.fori_loop(0, PEER_HEADS, head, 0, unroll=2)


def _route(scT, te):
    nsc, t = scT.shape
    spec = pl.BlockSpec((PEER_HEADS, te), lambda i: (0, i))
    return pl.pallas_call(
        _route_body, grid=(t // te,),
        in_specs=[pl.BlockSpec((nsc, te), lambda i: (0, i))],
        out_specs=(spec, spec, pl.BlockSpec((PEER_HEADS * TOP_ROWS, te), lambda i: (0, i))),
        out_shape=(jax.ShapeDtypeStruct((PEER_HEADS, t), _F32),) * 2
        + (jax.ShapeDtypeStruct((PEER_HEADS * TOP_ROWS, t), _F32),),
        compiler_params=_cparams("arbitrary"), name="route",
    )(scT)


def _experts_body(h2T_ref, scT_ref, tau_ref, c_ref, top2_ref, u_ref, vT_ref, x1_ref, out_ref,
                  acc_ref, e2_ref, rank_ref, cnt_ref, scale_ref, act_ref, g_ref, rows_ref, *, rows_per_step,
                  n_blocks):
    e = pl.program_id(1)
    nk = PEER_KEYS
    ts = act_ref.shape[1]

    @pl.when(e == 0)
    def _():
        acc_ref[...] = jnp.zeros_like(acc_ref)
        for h in range(PEER_HEADS):
            best = [top2_ref[h * TOP_ROWS + b:h * TOP_ROWS + b + 1, :] for b in range(PEER_TOPK + 1)]
            s1 = scT_ref[2 * h * nk:(2 * h + 1) * nk, :]
            s2 = scT_ref[(2 * h + 1) * nk:(2 * h + 2) * nk, :]
            e2_ref[h * nk:(h + 1) * nk, :] = jnp.exp(s2 - best[0]).astype(_BF)
            rank = jnp.zeros((nk, ts), _F32)
            for b in range(PEER_TOPK):
                rank = jnp.where(best[b] > s2, float(b + 1), rank)
            rank_ref[h * nk:(h + 1) * nk, :] = rank.astype(_BF)
            thr = tau_ref[h:h + 1, :] - s1
            cnt = jnp.zeros((nk, ts), _F32)
            for b in range(PEER_TOPK + 1):
                cnt = jnp.where(best[b] >= thr, float(b + 1), cnt)
            cnt_ref[h * nk:(h + 1) * nk, :] = cnt
            scale_ref[h * nk:(h + 1) * nk, :] = 0.5 * jnp.exp(s1 + best[0] - c_ref[h:h + 1, :])

    n_rows = rows_per_step * PEER_HEADS
    first_group = 2
    zero = jnp.zeros((), _BF)

    def first_matmul(i):
        rows = slice(i * first_group * nk, (i + 1) * first_group * nk)
        act_ref[rows, :] = _dot(u_ref[rows, :], h2T_ref[...])

    for i in range(rows_per_step):
        for h in range(PEER_HEADS):
            src = h * nk + e * rows_per_step + i
            r = i * PEER_HEADS + h
            rows_ref[r:r + 1, :] = cnt_ref[pl.ds(src, 1), :]
            rows_ref[n_rows + r:n_rows + r + 1, :] = scale_ref[pl.ds(src, 1), :]

    second_group = 4
    first_matmul(0)
    for i in range(rows_per_step):
        if i % first_group == 0 and i // first_group + 1 < rows_per_step // first_group:
            first_matmul(i // first_group + 1)
        wgt = jnp.zeros((nk, ts), _BF)
        for h in range(PEER_HEADS):
            r = i * PEER_HEADS + h
            cnt = rows_ref[r:r + 1, :].astype(_BF)
            scale = rows_ref[n_rows + r:n_rows + r + 1, :].astype(_BF)
            hit = jnp.where(rank_ref[h * nk:(h + 1) * nk, :] < cnt, e2_ref[h * nk:(h + 1) * nk, :], zero)
            wgt = wgt + hit * scale
        x = act_ref[i * nk:(i + 1) * nk, :].astype(_BF)
        t = jnp.tanh(x * (jnp.asarray(GELU_C0, _BF) + jnp.asarray(GELU_C1, _BF) * (x * x)))
        g_ref[i * nk:(i + 1) * nk, :] = (x + x * t) * wgt
        if i % second_group == second_group - 1:
            cols = slice((i + 1 - second_group) * nk, (i + 1) * nk)
            acc_ref[...] += _dot(vT_ref[0, :, cols], g_ref[cols, :])

    @pl.when(e == n_blocks - 1)
    def _():
        out_ref[...] = x1_ref[...] + acc_ref[...].T


def _experts(h2T, scT, tau, cn, top2, u, vT, x1, ts, rows_per_step):
    d, t = h2T.shape
    nsc = scT.shape[0]
    ne = u.shape[0]
    eb = rows_per_step * PEER_KEYS
    nb = ne // eb
    nhk = PEER_HEADS * PEER_KEYS
    col = lambda n: pl.BlockSpec((n, ts), lambda i, e: (0, i))
    tile = pl.BlockSpec((ts, d), lambda i, e: (i, 0))
    return pl.pallas_call(
        functools.partial(_experts_body, rows_per_step=rows_per_step, n_blocks=nb),
        grid=(t // ts, nb),
        in_specs=[col(d), col(nsc), col(PEER_HEADS), col(PEER_HEADS), col(top2.shape[0]),
                  pl.BlockSpec((eb, d), lambda i, e: (e, 0)), pl.BlockSpec((1, d, eb), lambda i, e: (e, 0, 0)), tile],
        out_specs=tile,
        out_shape=jax.ShapeDtypeStruct((t, d), _F32),
        scratch_shapes=[pltpu.VMEM((d, ts), _F32), pltpu.VMEM((nhk, ts), _BF), pltpu.VMEM((nhk, ts), _BF),
                        pltpu.VMEM((nhk, ts), _F32), pltpu.VMEM((nhk, ts), _F32),
                        pltpu.VMEM((eb, ts), _F32), pltpu.VMEM((eb, ts), _BF),
                        pltpu.VMEM((2 * rows_per_step * PEER_HEADS, ts), _F32)],
        compiler_params=_cparams("arbitrary", "arbitrary"), name="experts",
    )(h2T, scT, tau, cn, top2, u, vT, x1)


def _block_diag(w):
    n, c, d = w.shape
    return jnp.einsum("ncd,nm->ncmd", w, jnp.eye(n, dtype=w.dtype)).reshape(n * c, n * d)


def _rope_tables(s, dim, reps):
    half = dim // 2
    freq = ROPE_THETA ** (-jnp.arange(half, dtype=_F32) / half)
    ang = jnp.arange(s, dtype=_F32)[:, None] * freq[None, :]
    cos = jnp.tile(jnp.concatenate([jnp.cos(ang), jnp.cos(ang)], axis=1), (1, reps))
    sin = jnp.tile(jnp.concatenate([-jnp.sin(ang), jnp.sin(ang)], axis=1), (1, reps))
    return cos, sin


def _tile(n, pref):
    return pref if n % pref == 0 else n


def _layer(x, mem, p, l):
    b, s, d = x.shape
    bf = lambda a: a.astype(_BF)
    w_in = p["w_in"][l]
    n_q, n_kv = ATT_HEADS * LANES, ATT_KV_HEADS * LANES
    n_qi = IDX_HEADS * IDX_DIM
    sizes = (d, d, n_q, n_kv, n_kv, n_qi, IDX_DIM, IDX_HEADS, d, 3 * d)
    offs = [0]
    for n in sizes:
        offs.append(offs[-1] + n)
    wlx, wlg, wq, wk, wv, wqi, wki, wwi, wmq, wgates = (w_in[:, offs[i]:offs[i + 1]] for i in range(len(sizes)))
    wki2 = jnp.concatenate([wki, wki], axis=1)
    wwi_p = jnp.pad(wwi, ((0, 0), (0, LANES - IDX_HEADS)))
    wg0, wg1, wg2 = wgates[:, :d], wgates[:, d:2 * d], wgates[:, 2 * d:]
    nm = p["norm_mix"][l]

    ts_a = _tile(s, 512)
    m_lru = _lru_branch(x, nm, bf(wlx), bf(wlg), bf(wg0), p["conv_w"][l], p["conv_b"][l],
                        bf(_block_diag(p["lru_wa"][l])), p["lru_ba"][l].reshape(-1),
                        bf(_block_diag(p["lru_wi"][l])), p["lru_bi"][l].reshape(-1), p["lru_lambda"][l], ts_a)

    tk = _tile(s, 512)
    c128, s128 = _rope_tables(s, LANES, 1)
    c64, s64 = _rope_tables(s, IDX_DIM, LANES // IDX_DIM)
    ikn2 = jnp.concatenate([p["idx_k_norm"][l]] * 2)
    qT, k, vT, qiT, ki, wT, g1 = _dsa_proj(x, nm, bf(wq), bf(wk), bf(wv), bf(wqi), bf(wki2), bf(wwi_p), bf(wg1),
                                           p["q_norm"][l], p["k_norm"][l], ikn2, c128, s128, c64, s64, tk)
    m_att = _dsa(qT, qiT, wT, k, vT, ki, g1, min(TOPK_MAX, s // 4), _tile(s, 256), tk)

    mkT, mv = _mem_kv(mem, p["mem_norm"][l], bf(p["w_mem_kv"][l]), p["mem_k_norm"][l])
    m_mem = _mem_branch(x, nm, bf(wmq), bf(wg2), p["mem_q_norm"][l], mkT, mv, _tile(s, 512))

    t = b * s
    flat = lambda a: a.reshape(t, d)
    sk = bf(p["peer_subkeys"][l].reshape(2 * PEER_HEADS, PEER_KEYS, -1))
    x1, h2T, scT = _mid(flat(x), flat(m_lru), flat(m_att), flat(m_mem), bf(p["w_out"][l]), p["norm_ffn"][l],
                        bf(p["peer_wq"][l].T), sk, _tile(t, 512))
    tau, cn, top2 = _route(scT, _tile(t, 256))
    rows_per_step = 16
    vT = bf(p["peer_v"][l]).reshape(-1, rows_per_step * PEER_KEYS, d).transpose(0, 2, 1)
    out = _experts(h2T, scT, tau, cn, top2, bf(p["peer_u"][l]), vT, x1, _tile(t, 512), rows_per_step)
    return out.reshape(b, s, d)


def kernel(x, mem, norm_mix, w_in, conv_w, conv_b, lru_wa, lru_ba, lru_wi, lru_bi, lru_lambda, q_norm, k_norm,
           idx_k_norm, mem_norm, w_mem_kv, mem_q_norm, mem_k_norm, w_out, norm_ffn, peer_wq, peer_subkeys,
           peer_u, peer_v):
    p = dict(norm_mix=norm_mix, w_in=w_in, conv_w=conv_w, conv_b=conv_b, lru_wa=lru_wa, lru_ba=lru_ba,
             lru_wi=lru_wi, lru_bi=lru_bi, lru_lambda=lru_lambda, q_norm=q_norm, k_norm=k_norm,
             idx_k_norm=idx_k_norm, mem_norm=mem_norm, w_mem_kv=w_mem_kv, mem_q_norm=mem_q_norm,
             mem_k_norm=mem_k_norm, w_out=w_out, norm_ffn=norm_ffn, peer_wq=peer_wq, peer_subkeys=peer_subkeys,
             peer_u=peer_u, peer_v=peer_v)
    for l in range(norm_mix.shape[0]):
        x = _layer(x, mem, p, l)
    return x
```

```python
import functools

import jax
import jax.numpy as jnp
from jax import lax
from jax.experimental import pallas as pl
from jax.experimental.pallas import tpu as pltpu

_F32 = jnp.float32
_BF = jnp.bfloat16

EPS = 1e-6
ROPE_THETA = 10000.0
LRU_C = 8.0
CONV_WIDTH = 4
ATT_HEADS = 8
ATT_KV_HEADS = 2
ATT_GROUP = ATT_HEADS // ATT_KV_HEADS
IDX_HEADS = 8
IDX_DIM = 64
TOPK_MAX = 256
MEM_HEADS = 4
PEER_HEADS = 8
PEER_KEYS = 128
PEER_TOPK = 16
TOP_ROWS = 24

LANES = 128
SUBLANES = 8
ONES_ROWS = 2 * SUBLANES
VMEM_LIMIT_BYTES = 56 * 1024 * 1024
MASK_VALUE = -1e30
FLT_MAX = 3.4028234663852886e38
INT_MIN = -(2 ** 31)


def _cparams(*sem):
    return pltpu.CompilerParams(dimension_semantics=sem, vmem_limit_bytes=VMEM_LIMIT_BYTES)


def _rms(x, g):
    return x * lax.rsqrt(jnp.mean(x * x, axis=-1, keepdims=True) + EPS) * g


GELU_C0 = 0.7978845608028654
GELU_C1 = GELU_C0 * 0.044715


def _gelu(x):
    return 0.5 * x * (1.0 + jnp.tanh(x * (GELU_C0 + GELU_C1 * (x * x))))


def _sigmoid(x):
    return 0.5 * jnp.tanh(0.5 * x) + 0.5


def _dot(a, b):
    return jnp.dot(a, b, preferred_element_type=_F32)


def _full_spec(shape):
    n = len(shape)
    return pl.BlockSpec(shape, lambda *_: (0,) * n)


def _lru_body(x_ref, nm_ref, wlx_ref, wlg_ref, wg0_ref, cw_ref, cb_ref, wa_ref, ba_ref, wi_ref,
              bi_ref, lam_ref, out_ref, buf_ref, hc_ref, a_ref, b_ref):
    ts = x_ref.shape[1]
    c_dim = out_ref.shape[2]

    @pl.when(pl.program_id(1) == 0)
    def _():
        buf_ref[0:SUBLANES, :] = jnp.zeros((SUBLANES, c_dim), _F32)
        hc_ref[...] = jnp.zeros_like(hc_ref)

    h = _rms(x_ref[0], nm_ref[...]).astype(_BF)
    lx = _dot(h, wlx_ref[...])
    buf_ref[SUBLANES:SUBLANES + ts, :] = lx
    ext = buf_ref[...]
    cw = cw_ref[...]
    xc = cb_ref[...] + cw[3:4] * lx
    for j in range(CONV_WIDTH - 1):
        d = CONV_WIDTH - 1 - j
        xc = xc + cw[j:j + 1] * pltpu.roll(ext, d, 0)[SUBLANES:, :]
    buf_ref[0:SUBLANES, :] = lx[ts - SUBLANES:, :]

    xcb = xc.astype(_BF)
    r = _sigmoid(_dot(xcb, wa_ref[...]) + ba_ref[...])
    ig = _sigmoid(_dot(xcb, wi_ref[...]) + bi_ref[...])
    lam = lam_ref[...]
    softplus_neg_lam = jnp.maximum(-lam, 0.0) + jnp.log1p(jnp.exp(-jnp.abs(lam)))
    log_a = (-LRU_C) * r * softplus_neg_lam
    a = jnp.exp(log_a)
    y = jnp.tanh(-log_a) * (a * a + 1.0)
    u = jnp.where(y > 0.0, y * lax.rsqrt(y), 0.0) * (ig * xc)

    a = a.reshape(ts // SUBLANES, SUBLANES, c_dim)
    u = u.reshape(ts // SUBLANES, SUBLANES, c_dim)
    rid = lax.broadcasted_iota(jnp.int32, a.shape, 1)
    for d in (1, 2, 4):
        ok = rid >= d
        a_s = pltpu.roll(a, d, 1)
        u_s = pltpu.roll(u, d, 1)
        u = jnp.where(ok, a * u_s + u, u)
        a = jnp.where(ok, a * a_s, a)
    a_ref[...] = a.reshape(ts, c_dim)
    b_ref[...] = u.reshape(ts, c_dim)

    def step(g, hp):
        r0 = pl.multiple_of(g * SUBLANES, SUBLANES)
        hcur = a_ref[pl.ds(r0, SUBLANES), :] * hp + b_ref[pl.ds(r0, SUBLANES), :]
        b_ref[pl.ds(r0, SUBLANES), :] = hcur
        return jnp.broadcast_to(hcur[SUBLANES - 1:SUBLANES, :], (SUBLANES, c_dim))

    hc_ref[...] = lax.fori_loop(0, ts // SUBLANES, step, hc_ref[...])
    gate = _gelu(_dot(h, wlg_ref[...])) * _sigmoid(_dot(h, wg0_ref[...]))
    out_ref[0] = (b_ref[...] * gate).astype(_BF)


def _lru_branch(x, nm, wlx, wlg, wg0, cw, cb, wa, ba, wi, bi, lam, ts):
    b, s, d = x.shape
    c = wlx.shape[1]
    row = lambda v: v.reshape(1, -1)
    args = (x, row(nm), wlx, wlg, wg0, cw, row(cb), wa, row(ba), wi, row(bi), row(lam))
    in_specs = [pl.BlockSpec((1, ts, d), lambda i, j: (i, j, 0))]
    in_specs += [_full_spec(a.shape) for a in args[1:]]
    return pl.pallas_call(
        _lru_body,
        grid=(b, s // ts),
        in_specs=in_specs,
        out_specs=pl.BlockSpec((1, ts, c), lambda i, j: (i, j, 0)),
        out_shape=jax.ShapeDtypeStruct((b, s, c), _BF),
        scratch_shapes=[pltpu.VMEM((ts + SUBLANES, c), _F32), pltpu.VMEM((SUBLANES, c), _F32),
                        pltpu.VMEM((ts, c), _F32), pltpu.VMEM((ts, c), _F32)],
        compiler_params=_cparams("arbitrary", "arbitrary"),
        name="lru",
    )(*args)


def _rope128(x, cos, sin):
    return x * cos + pltpu.roll(x, LANES // 2, 1) * sin


def _rope64(x, cos, sin, first_half):
    rot = jnp.where(first_half, pltpu.roll(x, LANES - IDX_DIM // 2, 1), pltpu.roll(x, IDX_DIM // 2, 1))
    return x * cos + rot * sin


def _dsa_proj_body(x_ref, nm_ref, wq_ref, wk_ref, wv_ref, wqi_ref, wki_ref, wwi_ref, wg1_ref,
                   qn_ref, kn_ref, ikn_ref, c128_ref, s128_ref, c64_ref, s64_ref,
                   qT_ref, k_ref, vT_ref, qiT_ref, ki_ref, wT_ref, g1_ref):
    ts = x_ref.shape[1]
    h = _rms(x_ref[0], nm_ref[...]).astype(_BF)
    c128, s128 = c128_ref[...], s128_ref[...]
    c64, s64 = c64_ref[...], s64_ref[...]
    first_half = (lax.broadcasted_iota(jnp.int32, (ts, LANES), 1) & (IDX_DIM - 1)) < IDX_DIM // 2

    def head_norm_rope(t, g, scale):
        outs = []
        for i in range(t.shape[1] // LANES):
            th = _rms(t[:, i * LANES:(i + 1) * LANES], g)
            outs.append(_rope128(th, c128, s128) * scale)
        return jnp.concatenate(outs, axis=1)

    q = head_norm_rope(_dot(h, wq_ref[...]), qn_ref[...], float(LANES) ** -0.5)
    qT_ref[0] = q.T.astype(_BF)
    k_ref[0] = head_norm_rope(_dot(h, wk_ref[...]), kn_ref[...], 1.0).astype(_BF)
    vT = _dot(h, wv_ref[...]).T.astype(_BF)
    for g in range(ATT_KV_HEADS):
        vT_ref[0, 0, g, :LANES, :] = vT[g * LANES:(g + 1) * LANES, :]
        vT_ref[0, 0, g, LANES:, :] = jnp.ones((ONES_ROWS, ts), _BF)

    qi = _dot(h, wqi_ref[...])
    qi = jnp.concatenate(
        [_rope64(qi[:, i * LANES:(i + 1) * LANES], c64, s64, first_half) for i in range(qi.shape[1] // LANES)],
        axis=1) * (float(IDX_DIM) ** -0.5)
    qiT_ref[0] = qi.T.astype(_BF)
    ki = _rope64(_rms(_dot(h, wki_ref[...]), ikn_ref[...]), c64, s64, first_half)
    ki_ref[0] = ki[:, :IDX_DIM].astype(_BF)
    wi = _dot(h, wwi_ref[...]) * (float(IDX_HEADS) ** -0.5)
    wT_ref[0] = wi.T[:IDX_HEADS, :]
    g1_ref[0] = _sigmoid(_dot(h, wg1_ref[...])).astype(_BF)


def _dsa_proj(x, nm, wq, wk, wv, wqi, wki2, wwi, wg1, qn, kn, ikn2, c128, s128, c64, s64, ts):
    b, s, d = x.shape
    row = lambda v: v.reshape(1, -1)
    nq, nk, nqi = wq.shape[1], wk.shape[1], wqi.shape[1]
    args = (x, row(nm), wq, wk, wv, wqi, wki2, wwi, wg1, row(qn), row(kn), row(ikn2), c128, s128, c64, s64)
    tile = lambda n: pl.BlockSpec((1, ts, n), lambda i, j: (i, j, 0))
    tile_t = lambda n: pl.BlockSpec((1, n, ts), lambda i, j: (i, 0, j))
    tab = pl.BlockSpec((ts, LANES), lambda i, j: (j, 0))
    in_specs = [tile(d)] + [_full_spec(a.shape) for a in args[1:12]] + [tab] * 4
    out_shape = (
        jax.ShapeDtypeStruct((b, nq, s), _BF),
        jax.ShapeDtypeStruct((b, s, nk), _BF),
        jax.ShapeDtypeStruct((b, s // ts, ATT_KV_HEADS, LANES + ONES_ROWS, ts), _BF),
        jax.ShapeDtypeStruct((b, nqi, s), _BF),
        jax.ShapeDtypeStruct((b, s, IDX_DIM), _BF),
        jax.ShapeDtypeStruct((b, IDX_HEADS, s), _F32),
        jax.ShapeDtypeStruct((b, s, d), _BF),
    )
    out_specs = (tile_t(nq), tile(nk),
                 pl.BlockSpec((1, 1, ATT_KV_HEADS, LANES + ONES_ROWS, ts), lambda i, j: (i, j, 0, 0, 0)), tile_t(nqi),
                 tile(IDX_DIM), tile_t(IDX_HEADS), tile(d))
    return pl.pallas_call(
        _dsa_proj_body, grid=(b, s // ts), in_specs=in_specs, out_specs=out_specs, out_shape=out_shape,
        compiler_params=_cparams("arbitrary", "arbitrary"), name="dsa_proj",
    )(*args)


def _key_to_float(u):
    key = u ^ jnp.int32(INT_MIN)
    bits = key ^ ((key >> 31) & jnp.int32(0x7FFFFFFF))
    return jnp.where((u >> 23) == 0, -jnp.inf, lax.bitcast_convert_type(bits, _F32))


def _high_half(x):
    bits = lax.bitcast_convert_type(x, jnp.int32) & jnp.int32(-65536)
    return lax.bitcast_convert_type(bits, _F32).astype(_BF)


def _dsa_body(qT_ref, qiT_ref, wT_ref, k_ref, vT_ref, ki_ref, g1_ref, out_ref,
              sc_ref, sh_ref, m_ref, l_ref, acc_ref, bias_ref, s_ref, p_ref, *, topk, tk):
    tq = qT_ref.shape[2]
    dh = LANES
    q0 = pl.program_id(1) * tq
    n_chunks = (q0 + tq + tk - 1) // tk

    qicat = jnp.concatenate([qiT_ref[0, h * IDX_DIM:(h + 1) * IDX_DIM, :] for h in range(IDX_HEADS)], axis=1)
    w = wT_ref[0]
    qpos = q0 + lax.broadcasted_iota(jnp.int32, (tk, tq), 1)
    krow = lax.broadcasted_iota(jnp.int32, (tk, tq), 0)

    def score_chunk(c, carry):
        r0 = pl.multiple_of(c * tk, tk)
        lg = _dot(ki_ref[0, pl.ds(r0, tk), :], qicat)
        s = jnp.zeros((tk, tq), _F32)
        for h in range(IDX_HEADS):
            s = s + jnp.maximum(lg[:, h * tq:(h + 1) * tq], 0.0) * w[h:h + 1, :]
        s = jnp.where(krow + r0 <= qpos, s, -jnp.inf)
        sc_ref[pl.ds(r0, tk), :] = s
        sh_ref[pl.ds(r0, tk), :] = _high_half(s)
        return carry

    lax.fori_loop(0, n_chunks, score_chunk, 0)

    def make_count(nc, ref, rows, dtype):
        def count_ge(cand):
            one, zero = jnp.ones((), dtype), jnp.zeros((), dtype)
            parts = [jnp.zeros((rows, tq), dtype) for _ in range(4)]
            n = 0
            for c in range(nc):
                hit = jnp.where(ref[c * tk:(c + 1) * tk, :] >= cand, one, zero)
                for r in range(tk // rows):
                    parts[n % 4] = parts[n % 4] + hit[r * rows:(r + 1) * rows, :]
                    n += 1
            parts = [p.astype(_F32) for p in parts]
            return jnp.sum((parts[0] + parts[1]) + (parts[2] + parts[3]), axis=0, keepdims=True)
        return count_ge

    max_chunks = sc_ref.shape[0] // tk
    assert max_chunks * tk // (2 * SUBLANES) <= 4 * 64
    coarse = [make_count(nc, sh_ref, 2 * SUBLANES, _BF) for nc in range(1, max_chunks + 1)]
    fine = [make_count(nc, sc_ref, SUBLANES, _F32) for nc in range(1, max_chunks + 1)]

    def bisect_high(i, carry):
        u, n_ge = carry
        trial = u | lax.shift_left(jnp.int32(1), 31 - i)
        cnt = lax.switch(n_chunks - 1, coarse, _high_half(_key_to_float(trial)))
        keep = cnt >= float(topk)
        return jnp.where(keep, trial, u), jnp.where(keep, cnt, n_ge)

    def bisect_low(i, carry):
        u, n_ge = carry
        trial = u | lax.shift_left(jnp.int32(1), 31 - i)
        cnt = lax.switch(n_chunks - 1, fine, _key_to_float(trial))
        keep = cnt >= float(topk)
        return jnp.where(keep, trial, u), jnp.where(keep, cnt, n_ge)

    carry = lax.fori_loop(0, 16, bisect_high, (jnp.zeros((1, tq), jnp.int32), jnp.zeros((1, tq), _F32)))
    u, n_ge = lax.fori_loop(16, 32, bisect_low, carry)
    tau = jnp.maximum(_key_to_float(u), -FLT_MAX)

    @pl.when(jnp.max(jnp.where(tau > -FLT_MAX, n_ge, 0.0)) > float(topk))
    def _():
        def count_gt(c, cnt):
            r0 = pl.multiple_of(c * tk, tk)
            return cnt + jnp.sum(jnp.where(sc_ref[pl.ds(r0, tk), :] > tau, 1.0, 0.0), axis=0, keepdims=True)

        need = float(topk) - lax.fori_loop(0, n_chunks, count_gt, jnp.zeros((1, tq), _F32))
        before = (lax.broadcasted_iota(jnp.int32, (tk, tk), 1)
                  < lax.broadcasted_iota(jnp.int32, (tk, tk), 0)).astype(_BF)

        def strike(c, seen):
            r0 = pl.multiple_of(c * tk, tk)
            blk = sc_ref[pl.ds(r0, tk), :]
            tied = blk == tau
            tied_b = jnp.where(tied, 1.0, 0.0).astype(_BF)
            earlier = _dot(before, tied_b) + seen
            sc_ref[pl.ds(r0, tk), :] = jnp.where(tied & (earlier >= need), -jnp.inf, blk)
            return seen + jnp.sum(tied_b.astype(_F32), axis=0, keepdims=True)

        lax.fori_loop(0, n_chunks, strike, jnp.zeros((1, tq), _F32))

    m_ref[...] = jnp.full(m_ref.shape, MASK_VALUE, _F32)
    l_ref[...] = jnp.zeros_like(l_ref)
    acc_ref[...] = jnp.zeros_like(acc_ref)

    sub = 8 * SUBLANES
    groups = [slice(r * SUBLANES, (r + 1) * SUBLANES) for r in range(sub // SUBLANES)]

    last_chunk = sc_ref.shape[0] // tk - 1
    ahead = 3

    def logits(c, h):
        r0 = pl.multiple_of(c * tk, tk)
        g = h // ATT_GROUP
        s_ref[h] = _dot(k_ref[0, pl.ds(r0, tk), g * dh:(g + 1) * dh], qT_ref[0, h * dh:(h + 1) * dh, :])

    for h in range(ahead):
        logits(0, h)

    def att_chunk(c, carry):
        r0 = pl.multiple_of(c * tk, tk)
        bias_ref[...] = jnp.where(sc_ref[pl.ds(r0, tk), :] >= tau, 0.0, MASK_VALUE)
        for h in range(ATT_HEADS):
            if h + ahead < ATT_HEADS:
                logits(c, h + ahead)
            else:
                logits(jnp.minimum(c + 1, last_chunk), h + ahead - ATT_HEADS)
            m_old = m_ref[h:h + 1, :]
            mx = jnp.broadcast_to(m_old, (SUBLANES, tq))
            for i in range(tk // sub):
                rows = slice(i * sub, (i + 1) * sub)
                sb = s_ref[h, rows, :] + bias_ref[rows, :]
                s_ref[h, rows, :] = sb
                for gr in groups:
                    mx = jnp.maximum(mx, sb[gr, :])
            m_new = jnp.max(mx, axis=0, keepdims=True)
            for i in range(tk // sub):
                rows = slice(i * sub, (i + 1) * sub)
                p_ref[h, rows, :] = jnp.exp((s_ref[h, rows, :] - m_new).astype(_BF))
            alpha = jnp.exp(m_old - m_new)
            pv = _dot(vT_ref[0, c, h // ATT_GROUP], p_ref[h])
            l_ref[h:h + 1, :] = alpha * l_ref[h:h + 1, :] + pv[dh:dh + 1, :]
            acc_ref[h * dh:(h + 1) * dh, :] = alpha * acc_ref[h * dh:(h + 1) * dh, :] + pv[:dh, :]
            m_ref[h:h + 1, :] = m_new
        return carry

    lax.fori_loop(0, n_chunks, att_chunk, 0)
    outs = []
    for h in range(ATT_HEADS):
        o = acc_ref[h * dh:(h + 1) * dh, :] / l_ref[h:h + 1, :]
        outs.append(o.T)
    out_ref[0] = (jnp.concatenate(outs, axis=1) * g1_ref[0].astype(_F32)).astype(_BF)


def _dsa(qT, qiT, wT, k, vT, ki, g1, topk, tq, tk):
    b, nq, s = qT.shape
    d = g1.shape[2]
    in_specs = [
        pl.BlockSpec((1, nq, tq), lambda i, j: (i, 0, j)),
        pl.BlockSpec((1, qiT.shape[1], tq), lambda i, j: (i, 0, j)),
        pl.BlockSpec((1, IDX_HEADS, tq), lambda i, j: (i, 0, j)),
        pl.BlockSpec((1, s, k.shape[2]), lambda i, j: (i, 0, 0)),
        pl.BlockSpec((1,) + vT.shape[1:], lambda i, j: (i, 0, 0, 0, 0)),
        pl.BlockSpec((1, s, IDX_DIM), lambda i, j: (i, 0, 0)),
        pl.BlockSpec((1, tq, d), lambda i, j: (i, j, 0)),
    ]
    return pl.pallas_call(
        functools.partial(_dsa_body, topk=topk, tk=tk),
        grid=(b, s // tq), in_specs=in_specs,
        out_specs=pl.BlockSpec((1, tq, d), lambda i, j: (i, j, 0)),
        out_shape=jax.ShapeDtypeStruct((b, s, d), _BF),
        scratch_shapes=[pltpu.VMEM((s, tq), _F32), pltpu.VMEM((s, tq), _BF), pltpu.VMEM((ATT_HEADS, tq), _F32),
                        pltpu.VMEM((ATT_HEADS, tq), _F32), pltpu.VMEM((nq, tq), _F32),
                        pltpu.VMEM((tk, tq), _F32), pltpu.VMEM((ATT_HEADS, tk, tq), _F32),
                        pltpu.VMEM((ATT_HEADS, tk, tq), _BF)],
        compiler_params=_cparams("arbitrary", "arbitrary"), name="dsa",
    )(qT, qiT, wT, k, vT, ki, g1)


def _mem_kv_body(mem_ref, mn_ref, wkv_ref, mkn_ref, mkT_ref, mv_ref):
    d = mem_ref.shape[2]
    dm = d // MEM_HEADS
    m = _rms(mem_ref[0], mn_ref[...]).astype(_BF)
    kv = _dot(m, wkv_ref[...])
    mk = jnp.concatenate([_rms(kv[:, i * dm:(i + 1) * dm], mkn_ref[...]) for i in range(MEM_HEADS)], axis=1)
    mkT_ref[0] = mk.T.astype(_BF)
    mv_ref[0] = kv[:, d:].astype(_BF)


def _mem_kv(mem, mn, wkv, mkn):
    b, m, d = mem.shape
    row = lambda v: v.reshape(1, -1)
    return pl.pallas_call(
        _mem_kv_body, grid=(b,),
        in_specs=[pl.BlockSpec((1, m, d), lambda i: (i, 0, 0)), _full_spec((1, d)), _full_spec(wkv.shape),
                  _full_spec((1, d // MEM_HEADS))],
        out_specs=(pl.BlockSpec((1, d, m), lambda i: (i, 0, 0)), pl.BlockSpec((1, m, d), lambda i: (i, 0, 0))),
        out_shape=(jax.ShapeDtypeStruct((b, d, m), _BF), jax.ShapeDtypeStruct((b, m, d), _BF)),
        compiler_params=_cparams("arbitrary"), name="mem_kv",
    )(mem, row(mn), wkv, row(mkn))


def _mem_body(x_ref, nm_ref, wmq_ref, wg2_ref, mqn_ref, mkT_ref, mv_ref, out_ref):
    d = x_ref.shape[2]
    dm = d // MEM_HEADS
    h = _rms(x_ref[0], nm_ref[...]).astype(_BF)
    mq = _dot(h, wmq_ref[...])
    outs = []
    for i in range(MEM_HEADS):
        qh = (_rms(mq[:, i * dm:(i + 1) * dm], mqn_ref[...]) * (float(dm) ** -0.5)).astype(_BF)
        s = _dot(qh, mkT_ref[0, i * dm:(i + 1) * dm, :])
        p = jnp.exp(s - jnp.max(s, axis=-1, keepdims=True))
        o = _dot(p.astype(_BF), mv_ref[0, :, i * dm:(i + 1) * dm])
        outs.append(o / jnp.sum(p, axis=-1, keepdims=True))
    out_ref[0] = (jnp.concatenate(outs, axis=1) * _sigmoid(_dot(h, wg2_ref[...]))).astype(_BF)


def _mem_branch(x, nm, wmq, wg2, mqn, mkT, mv, ts):
    b, s, d = x.shape
    m = mv.shape[1]
    row = lambda v: v.reshape(1, -1)
    return pl.pallas_call(
        _mem_body, grid=(b, s // ts),
        in_specs=[pl.BlockSpec((1, ts, d), lambda i, j: (i, j, 0)), _full_spec((1, d)), _full_spec(wmq.shape),
                  _full_spec(wg2.shape), _full_spec((1, d // MEM_HEADS)),
                  pl.BlockSpec((1, d, m), lambda i, j: (i, 0, 0)), pl.BlockSpec((1, m, d), lambda i, j: (i, 0, 0))],
        out_specs=pl.BlockSpec((1, ts, d), lambda i, j: (i, j, 0)),
        out_shape=jax.ShapeDtypeStruct((b, s, d), _BF),
        compiler_params=_cparams("arbitrary", "arbitrary"), name="mem",
    )(x, row(nm), wmq, wg2, row(mqn), mkT, mv)


def _mid_body(x_ref, a_ref, b_ref, c_ref, wo_ref, nf_ref, wpqT_ref, sk_ref, x1_ref, h2T_ref, scT_ref):
    merged = (a_ref[...].astype(_F32) + b_ref[...].astype(_F32) + c_ref[...].astype(_F32)).astype(_BF)
    x1 = x_ref[...] + _dot(merged, wo_ref[...])
    x1_ref[...] = x1
    h2T = _rms(x1, nf_ref[...]).T.astype(_BF)
    h2T_ref[...] = h2T
    pqT = _dot(wpqT_ref[...], h2T).astype(_BF)
    for c in range(sk_ref.shape[0]):
        scT_ref[c * PEER_KEYS:(c + 1) * PEER_KEYS, :] = _dot(sk_ref[c], pqT[c * LANES:(c + 1) * LANES, :])


def _mid(x2, ma, mb, mc, wo, nf, wpqT, sk, ts):
    t, d = x2.shape
    nsc = sk.shape[0] * PEER_KEYS
    tile = pl.BlockSpec((ts, d), lambda i: (i, 0))
    return pl.pallas_call(
        _mid_body, grid=(t // ts,),
        in_specs=[tile, tile, tile, tile, _full_spec(wo.shape), _full_spec((1, d)), _full_spec(wpqT.shape),
                  _full_spec(sk.shape)],
        out_specs=(tile, pl.BlockSpec((d, ts), lambda i: (0, i)), pl.BlockSpec((nsc, ts), lambda i: (0, i))),
        out_shape=(jax.ShapeDtypeStruct((t, d), _F32), jax.ShapeDtypeStruct((d, t), _BF),
                   jax.ShapeDtypeStruct((nsc, t), _F32)),
        compiler_params=_cparams("arbitrary"), name="mid",
    )(x2, ma, mb, mc, wo, nf.reshape(1, -1), wpqT, sk)


def _top_values(s, n):
    groups = [s[r * SUBLANES:(r + 1) * SUBLANES, :] for r in range(s.shape[0] // SUBLANES)]
    size = pl.next_power_of_2(len(groups))
    groups += [jnp.full_like(groups[0], -jnp.inf)] * (size - len(groups))
    k = 2
    while k <= size:
        j = k // 2
        while j >= 1:
            for i in range(size):
                l = i ^ j
                if l > i:
                    hi, lo = jnp.maximum(groups[i], groups[l]), jnp.minimum(groups[i], groups[l])
                    groups[i], groups[l] = (hi, lo) if (i & k) == 0 else (lo, hi)
            j //= 2
        k *= 2
    vals = []
    for it in range(n):
        m = jnp.max(groups[0], axis=0, keepdims=True)
        vals.append(m)
        remaining = n - 1 - it
        head = groups[0] == m
        for d in range(min(remaining, size)):
            below = groups[d + 1] if d + 1 < size else jnp.full_like(m, -jnp.inf)
            groups[d] = jnp.where(head, below, groups[d])
    return vals


def _route_body(scT_ref, tau_ref, c_ref, top2_ref):
    def head(h, carry):
        r1 = pl.multiple_of(h * 2 * PEER_KEYS, PEER_KEYS)
        r2 = pl.multiple_of(h * 2 * PEER_KEYS + PEER_KEYS, PEER_KEYS)
        n = PEER_TOPK + 1
        v1 = _top_values(scT_ref[pl.ds(r1, PEER_KEYS), :], n)
        v2 = _top_values(scT_ref[pl.ds(r2, PEER_KEYS), :], n)
        edge = SUBLANES
        assert 2 * (edge + 1) > n
        pad = [jnp.full_like(v1[0], -jnp.inf)] * (-n % SUBLANES)
        col1 = jnp.concatenate(v1 + pad, axis=0)
        col2 = jnp.concatenate(v2 + pad, axis=0)
        cand = jnp.concatenate([v1[a] + col2[:edge] for a in range(edge)]
                               + [v1[0] + col2[edge:], v2[0] + col1[edge:]], axis=0)
        best = _top_values(cand, n)
        tau = best[PEER_TOPK - 1]
        top = v1[0] + v2[0]
        z = jnp.sum(jnp.where(cand >= tau, jnp.exp(cand - top), 0.0), axis=0, keepdims=True)
        tau_ref[pl.ds(h, 1), :] = 0.5 * (tau + best[PEER_TOPK])
        c_ref[pl.ds(h, 1), :] = top + jnp.log(z)
        top2_ref[pl.ds(pl.multiple_of(h * TOP_ROWS, SUBLANES), TOP_ROWS), :] = col2
        return carry

    lax.fori_loop(0, PEER_HEADS, head, 0, unroll=2)


def _route(scT, te):
    nsc, t = scT.shape
    spec = pl.BlockSpec((PEER_HEADS, te), lambda i: (0, i))
    return pl.pallas_call(
        _route_body, grid=(t // te,),
        in_specs=[pl.BlockSpec((nsc, te), lambda i: (0, i))],
        out_specs=(spec, spec, pl.BlockSpec((PEER_HEADS * TOP_ROWS, te), lambda i: (0, i))),
        out_shape=(jax.ShapeDtypeStruct((PEER_HEADS, t), _F32),) * 2
        + (jax.ShapeDtypeStruct((PEER_HEADS * TOP_ROWS, t), _F32),),
        compiler_params=_cparams("arbitrary"), name="route",
    )(scT)


def _experts_body(h2T_ref, scT_ref, tau_ref, c_ref, top2_ref, u_ref, vT_ref, x1_ref, out_ref,
                  acc_ref, e2_ref, rank_ref, cnt_ref, scale_ref, act_ref, g_ref, rows_ref, *, rows_per_step,
                  n_blocks):
    e = pl.program_id(1)
    nk = PEER_KEYS
    ts = act_ref.shape[1]

    @pl.when(e == 0)
    def _():
        acc_ref[...] = jnp.zeros_like(acc_ref)
        for h in range(PEER_HEADS):
            best = [top2_ref[h * TOP_ROWS + b:h * TOP_ROWS + b + 1, :] for b in range(PEER_TOPK + 1)]
            s1 = scT_ref[2 * h * nk:(2 * h + 1) * nk, :]
            s2 = scT_ref[(2 * h + 1) * nk:(2 * h + 2) * nk, :]
            e2_ref[h * nk:(h + 1) * nk, :] = jnp.exp(s2 - best[0]).astype(_BF)
            rank = jnp.zeros((nk, ts), _F32)
            for b in range(PEER_TOPK):
                rank = jnp.where(best[b] > s2, float(b + 1), rank)
            rank_ref[h * nk:(h + 1) * nk, :] = rank.astype(_BF)
            thr = tau_ref[h:h + 1, :] - s1
            cnt = jnp.zeros((nk, ts), _F32)
            for b in range(PEER_TOPK + 1):
                cnt = jnp.where(best[b] >= thr, float(b + 1), cnt)
            cnt_ref[h * nk:(h + 1) * nk, :] = cnt
            scale_ref[h * nk:(h + 1) * nk, :] = 0.5 * jnp.exp(s1 + best[0] - c_ref[h:h + 1, :])

    n_rows = rows_per_step * PEER_HEADS
    first_group = 2
    zero = jnp.zeros((), _BF)

    def first_matmul(i):
        rows = slice(i * first_group * nk, (i + 1) * first_group * nk)
        act_ref[rows, :] = _dot(u_ref[rows, :], h2T_ref[...])

    for i in range(rows_per_step):
        for h in range(PEER_HEADS):
            src = h * nk + e * rows_per_step + i
            r = i * PEER_HEADS + h
            rows_ref[r:r + 1, :] = cnt_ref[pl.ds(src, 1), :]
            rows_ref[n_rows + r:n_rows + r + 1, :] = scale_ref[pl.ds(src, 1), :]

    second_group = 4
    ahead = 1
    n_groups = rows_per_step // first_group
    for g in range(min(ahead, n_groups)):
        first_matmul(g)
    for i in range(rows_per_step):
        if i % first_group == 0 and i // first_group + ahead < n_groups:
            first_matmul(i // first_group + ahead)
        wgt = jnp.zeros((nk, ts), _BF)
        for h in range(PEER_HEADS):
            r = i * PEER_HEADS + h
            cnt = rows_ref[r:r + 1, :].astype(_BF)
            scale = rows_ref[n_rows + r:n_rows + r + 1, :].astype(_BF)
            hit = jnp.where(rank_ref[h * nk:(h + 1) * nk, :] < cnt, e2_ref[h * nk:(h + 1) * nk, :], zero)
            wgt = wgt + hit * scale
        x = act_ref[i * nk:(i + 1) * nk, :].astype(_BF)
        t = jnp.tanh(x * (jnp.asarray(GELU_C0, _BF) + jnp.asarray(GELU_C1, _BF) * (x * x)))
        g_ref[i * nk:(i + 1) * nk, :] = (x + x * t) * wgt
        if i % second_group == second_group - 1:
            cols = slice((i + 1 - second_group) * nk, (i + 1) * nk)
            acc_ref[...] += _dot(vT_ref[0, :, cols], g_ref[cols, :])

    @pl.when(e == n_blocks - 1)
    def _():
        out_ref[...] = x1_ref[...] + acc_ref[...].T


def _experts(h2T, scT, tau, cn, top2, u, vT, x1, ts, rows_per_step):
    d, t = h2T.shape
    nsc = scT.shape[0]
    ne = u.shape[0]
    eb = rows_per_step * PEER_KEYS
    nb = ne // eb
    nhk = PEER_HEADS * PEER_KEYS
    col = lambda n: pl.BlockSpec((n, ts), lambda i, e: (0, i))
    tile = pl.BlockSpec((ts, d), lambda i, e: (i, 0))
    return pl.pallas_call(
        functools.partial(_experts_body, rows_per_step=rows_per_step, n_blocks=nb),
        grid=(t // ts, nb),
        in_specs=[col(d), col(nsc), col(PEER_HEADS), col(PEER_HEADS), col(top2.shape[0]),
                  pl.BlockSpec((eb, d), lambda i, e: (e, 0)), pl.BlockSpec((1, d, eb), lambda i, e: (e, 0, 0)), tile],
        out_specs=tile,
        out_shape=jax.ShapeDtypeStruct((t, d), _F32),
        scratch_shapes=[pltpu.VMEM((d, ts), _F32), pltpu.VMEM((nhk, ts), _BF), pltpu.VMEM((nhk, ts), _BF),
                        pltpu.VMEM((nhk, ts), _F32), pltpu.VMEM((nhk, ts), _F32),
                        pltpu.VMEM((eb, ts), _F32), pltpu.VMEM((eb, ts), _BF),
                        pltpu.VMEM((2 * rows_per_step * PEER_HEADS, ts), _F32)],
        compiler_params=_cparams("arbitrary", "arbitrary"), name="experts",
    )(h2T, scT, tau, cn, top2, u, vT, x1)


def _block_diag(w):
    n, c, d = w.shape
    return jnp.einsum("ncd,nm->ncmd", w, jnp.eye(n, dtype=w.dtype)).reshape(n * c, n * d)


def _rope_tables(s, dim, reps):
    half = dim // 2
    freq = ROPE_THETA ** (-jnp.arange(half, dtype=_F32) / half)
    ang = jnp.arange(s, dtype=_F32)[:, None] * freq[None, :]
    cos = jnp.tile(jnp.concatenate([jnp.cos(ang), jnp.cos(ang)], axis=1), (1, reps))
    sin = jnp.tile(jnp.concatenate([-jnp.sin(ang), jnp.sin(ang)], axis=1), (1, reps))
    return cos, sin


def _tile(n, pref):
    return pref if n % pref == 0 else n


def _layer(x, mem, p, l):
    b, s, d = x.shape
    bf = lambda a: a.astype(_BF)
    w_in = p["w_in"][l]
    n_q, n_kv = ATT_HEADS * LANES, ATT_KV_HEADS * LANES
    n_qi = IDX_HEADS * IDX_DIM
    sizes = (d, d, n_q, n_kv, n_kv, n_qi, IDX_DIM, IDX_HEADS, d, 3 * d)
    offs = [0]
    for n in sizes:
        offs.append(offs[-1] + n)
    wlx, wlg, wq, wk, wv, wqi, wki, wwi, wmq, wgates = (w_in[:, offs[i]:offs[i + 1]] for i in range(len(sizes)))
    wki2 = jnp.concatenate([wki, wki], axis=1)
    wwi_p = jnp.pad(wwi, ((0, 0), (0, LANES - IDX_HEADS)))
    wg0, wg1, wg2 = wgates[:, :d], wgates[:, d:2 * d], wgates[:, 2 * d:]
    nm = p["norm_mix"][l]

    ts_a = _tile(s, 512)
    m_lru = _lru_branch(x, nm, bf(wlx), bf(wlg), bf(wg0), p["conv_w"][l], p["conv_b"][l],
                        bf(_block_diag(p["lru_wa"][l])), p["lru_ba"][l].reshape(-1),
                        bf(_block_diag(p["lru_wi"][l])), p["lru_bi"][l].reshape(-1), p["lru_lambda"][l], ts_a)

    tk = _tile(s, 512)
    c128, s128 = _rope_tables(s, LANES, 1)
    c64, s64 = _rope_tables(s, IDX_DIM, LANES // IDX_DIM)
    ikn2 = jnp.concatenate([p["idx_k_norm"][l]] * 2)
    qT, k, vT, qiT, ki, wT, g1 = _dsa_proj(x, nm, bf(wq), bf(wk), bf(wv), bf(wqi), bf(wki2), bf(wwi_p), bf(wg1),
                                           p["q_norm"][l], p["k_norm"][l], ikn2, c128, s128, c64, s64, tk)
    m_att = _dsa(qT, qiT, wT, k, vT, ki, g1, min(TOPK_MAX, s // 4), _tile(s, 256), tk)

    mkT, mv = _mem_kv(mem, p["mem_norm"][l], bf(p["w_mem_kv"][l]), p["mem_k_norm"][l])
    m_mem = _mem_branch(x, nm, bf(wmq), bf(wg2), p["mem_q_norm"][l], mkT, mv, _tile(s, 512))

    t = b * s
    flat = lambda a: a.reshape(t, d)
    sk = bf(p["peer_subkeys"][l].reshape(2 * PEER_HEADS, PEER_KEYS, -1))
    x1, h2T, scT = _mid(flat(x), flat(m_lru), flat(m_att), flat(m_mem), bf(p["w_out"][l]), p["norm_ffn"][l],
                        bf(p["peer_wq"][l].T), sk, _tile(t, 512))
    tau, cn, top2 = _route(scT, _tile(t, 256))
    rows_per_step = 16
    vT = bf(p["peer_v"][l]).reshape(-1, rows_per_step * PEER_KEYS, d).transpose(0, 2, 1)
    out = _experts(h2T, scT, tau, cn, top2, bf(p["peer_u"][l]), vT, x1, _tile(t, 512), rows_per_step)
    return out.reshape(b, s, d)


def kernel(x, mem, norm_mix, w_in, conv_w, conv_b, lru_wa, lru_ba, lru_wi, lru_bi, lru_lambda, q_norm, k_norm,
           idx_k_norm, mem_norm, w_mem_kv, mem_q_norm, mem_k_norm, w_out, norm_ffn, peer_wq, peer_subkeys,
           peer_u, peer_v):
    p = dict(norm_mix=norm_mix, w_in=w_in, conv_w=conv_w, conv_b=conv_b, lru_wa=lru_wa, lru_ba=lru_ba,
             lru_wi=lru_wi, lru_bi=lru_bi, lru_lambda=lru_lambda, q_norm=q_norm, k_norm=k_norm,
             idx_k_norm=idx_k_norm, mem_norm=mem_norm, w_mem_kv=w_mem_kv, mem_q_norm=mem_q_norm,
             mem_k_norm=mem_k_norm, w_out=w_out, norm_ffn=norm_ffn, peer_wq=peer_wq, peer_subkeys=peer_subkeys,
             peer_u=peer_u, peer_v=peer_v)
    for l in range(norm_mix.shape[0]):
        x = _layer(x, mem, p, l)
    return x
```

```python
import functools

import jax
import jax.numpy as jnp
from jax import lax
from jax.experimental import pallas as pl
from jax.experimental.pallas import tpu as pltpu

_F32 = jnp.float32
_BF = jnp.bfloat16

EPS = 1e-6
ROPE_THETA = 10000.0
LRU_C = 8.0
CONV_WIDTH = 4
ATT_HEADS = 8
ATT_KV_HEADS = 2
ATT_GROUP = ATT_HEADS // ATT_KV_HEADS
IDX_HEADS = 8
IDX_DIM = 64
TOPK_MAX = 256
MEM_HEADS = 4
PEER_HEADS = 8
PEER_KEYS = 128
PEER_TOPK = 16
TOP_ROWS = 24

LANES = 128
SUBLANES = 8
ONES_ROWS = 2 * SUBLANES
VMEM_LIMIT_BYTES = 56 * 1024 * 1024
MASK_VALUE = -1e30
FLT_MAX = 3.4028234663852886e38
INT_MIN = -(2 ** 31)


def _cparams(*sem):
    return pltpu.CompilerParams(dimension_semantics=sem, vmem_limit_bytes=VMEM_LIMIT_BYTES)


def _rms(x, g):
    return x * lax.rsqrt(jnp.mean(x * x, axis=-1, keepdims=True) + EPS) * g


GELU_C0 = 0.7978845608028654
GELU_C1 = GELU_C0 * 0.044715


def _gelu(x):
    return 0.5 * x * (1.0 + jnp.tanh(x * (GELU_C0 + GELU_C1 * (x * x))))


def _sigmoid(x):
    return 0.5 * jnp.tanh(0.5 * x) + 0.5


def _dot(a, b):
    return jnp.dot(a, b, preferred_element_type=_F32)


def _full_spec(shape):
    n = len(shape)
    return pl.BlockSpec(shape, lambda *_: (0,) * n)


def _lru_body(x_ref, nm_ref, wlx_ref, wlg_ref, wg0_ref, cw_ref, cb_ref, wa_ref, ba_ref, wi_ref,
              bi_ref, lam_ref, out_ref, buf_ref, hc_ref, a_ref, b_ref):
    ts = x_ref.shape[1]
    c_dim = out_ref.shape[2]

    @pl.when(pl.program_id(1) == 0)
    def _():
        buf_ref[0:SUBLANES, :] = jnp.zeros((SUBLANES, c_dim), _F32)
        hc_ref[...] = jnp.zeros_like(hc_ref)

    h = _rms(x_ref[0], nm_ref[...]).astype(_BF)
    lx = _dot(h, wlx_ref[...])
    buf_ref[SUBLANES:SUBLANES + ts, :] = lx
    ext = buf_ref[...]
    cw = cw_ref[...]
    xc = cb_ref[...] + cw[3:4] * lx
    for j in range(CONV_WIDTH - 1):
        d = CONV_WIDTH - 1 - j
        xc = xc + cw[j:j + 1] * pltpu.roll(ext, d, 0)[SUBLANES:, :]
    buf_ref[0:SUBLANES, :] = lx[ts - SUBLANES:, :]

    xcb = xc.astype(_BF)
    r = _sigmoid(_dot(xcb, wa_ref[...]) + ba_ref[...])
    ig = _sigmoid(_dot(xcb, wi_ref[...]) + bi_ref[...])
    lam = lam_ref[...]
    softplus_neg_lam = jnp.maximum(-lam, 0.0) + jnp.log1p(jnp.exp(-jnp.abs(lam)))
    log_a = (-LRU_C) * r * softplus_neg_lam
    a = jnp.exp(log_a)
    y = jnp.tanh(-log_a) * (a * a + 1.0)
    u = jnp.where(y > 0.0, y * lax.rsqrt(y), 0.0) * (ig * xc)

    a = a.reshape(ts // SUBLANES, SUBLANES, c_dim)
    u = u.reshape(ts // SUBLANES, SUBLANES, c_dim)
    rid = lax.broadcasted_iota(jnp.int32, a.shape, 1)
    for d in (1, 2, 4):
        ok = rid >= d
        a_s = pltpu.roll(a, d, 1)
        u_s = pltpu.roll(u, d, 1)
        u = jnp.where(ok, a * u_s + u, u)
        a = jnp.where(ok, a * a_s, a)
    a_ref[...] = a.reshape(ts, c_dim)
    b_ref[...] = u.reshape(ts, c_dim)

    def step(g, hp):
        r0 = pl.multiple_of(g * SUBLANES, SUBLANES)
        hcur = a_ref[pl.ds(r0, SUBLANES), :] * hp + b_ref[pl.ds(r0, SUBLANES), :]
        b_ref[pl.ds(r0, SUBLANES), :] = hcur
        return jnp.broadcast_to(hcur[SUBLANES - 1:SUBLANES, :], (SUBLANES, c_dim))

    hc_ref[...] = lax.fori_loop(0, ts // SUBLANES, step, hc_ref[...])
    gate = _gelu(_dot(h, wlg_ref[...])) * _sigmoid(_dot(h, wg0_ref[...]))
    out_ref[0] = (b_ref[...] * gate).astype(_BF)


def _lru_branch(x, nm, wlx, wlg, wg0, cw, cb, wa, ba, wi, bi, lam, ts):
    b, s, d = x.shape
    c = wlx.shape[1]
    row = lambda v: v.reshape(1, -1)
    args = (x, row(nm), wlx, wlg, wg0, cw, row(cb), wa, row(ba), wi, row(bi), row(lam))
    in_specs = [pl.BlockSpec((1, ts, d), lambda i, j: (i, j, 0))]
    in_specs += [_full_spec(a.shape) for a in args[1:]]
    return pl.pallas_call(
        _lru_body,
        grid=(b, s // ts),
        in_specs=in_specs,
        out_specs=pl.BlockSpec((1, ts, c), lambda i, j: (i, j, 0)),
        out_shape=jax.ShapeDtypeStruct((b, s, c), _BF),
        scratch_shapes=[pltpu.VMEM((ts + SUBLANES, c), _F32), pltpu.VMEM((SUBLANES, c), _F32),
                        pltpu.VMEM((ts, c), _F32), pltpu.VMEM((ts, c), _F32)],
        compiler_params=_cparams("arbitrary", "arbitrary"),
        name="lru",
    )(*args)


def _rope128(x, cos, sin):
    return x * cos + pltpu.roll(x, LANES // 2, 1) * sin


def _rope64(x, cos, sin, first_half):
    rot = jnp.where(first_half, pltpu.roll(x, LANES - IDX_DIM // 2, 1), pltpu.roll(x, IDX_DIM // 2, 1))
    return x * cos + rot * sin


def _dsa_proj_body(x_ref, nm_ref, wq_ref, wk_ref, wv_ref, wqi_ref, wki_ref, wwi_ref, wg1_ref,
                   qn_ref, kn_ref, ikn_ref, c128_ref, s128_ref, c64_ref, s64_ref,
                   qT_ref, k_ref, vT_ref, qiT_ref, ki_ref, wT_ref, g1_ref):
    ts = x_ref.shape[1]
    h = _rms(x_ref[0], nm_ref[...]).astype(_BF)
    c128, s128 = c128_ref[...], s128_ref[...]
    c64, s64 = c64_ref[...], s64_ref[...]
    first_half = (lax.broadcasted_iota(jnp.int32, (ts, LANES), 1) & (IDX_DIM - 1)) < IDX_DIM // 2

    def head_norm_rope(t, g, scale):
        outs = []
        for i in range(t.shape[1] // LANES):
            th = _rms(t[:, i * LANES:(i + 1) * LANES], g)
            outs.append(_rope128(th, c128, s128) * scale)
        return jnp.concatenate(outs, axis=1)

    q = head_norm_rope(_dot(h, wq_ref[...]), qn_ref[...], float(LANES) ** -0.5)
    qT_ref[0] = q.T.astype(_BF)
    k_ref[0] = head_norm_rope(_dot(h, wk_ref[...]), kn_ref[...], 1.0).astype(_BF)
    vT = _dot(h, wv_ref[...]).T.astype(_BF)
    for g in range(ATT_KV_HEADS):
        vT_ref[0, 0, g, :LANES, :] = vT[g * LANES:(g + 1) * LANES, :]
        vT_ref[0, 0, g, LANES:, :] = jnp.ones((ONES_ROWS, ts), _BF)

    qi = _dot(h, wqi_ref[...])
    qi = jnp.concatenate(
        [_rope64(qi[:, i * LANES:(i + 1) * LANES], c64, s64, first_half) for i in range(qi.shape[1] // LANES)],
        axis=1) * (float(IDX_DIM) ** -0.5)
    qiT_ref[0] = qi.T.astype(_BF)
    ki = _rope64(_rms(_dot(h, wki_ref[...]), ikn_ref[...]), c64, s64, first_half)
    ki_ref[0] = ki[:, :IDX_DIM].astype(_BF)
    wi = _dot(h, wwi_ref[...]) * (float(IDX_HEADS) ** -0.5)
    wT_ref[0] = wi.T[:IDX_HEADS, :]
    g1_ref[0] = _sigmoid(_dot(h, wg1_ref[...])).astype(_BF)


def _dsa_proj(x, nm, wq, wk, wv, wqi, wki2, wwi, wg1, qn, kn, ikn2, c128, s128, c64, s64, ts):
    b, s, d = x.shape
    row = lambda v: v.reshape(1, -1)
    nq, nk, nqi = wq.shape[1], wk.shape[1], wqi.shape[1]
    args = (x, row(nm), wq, wk, wv, wqi, wki2, wwi, wg1, row(qn), row(kn), row(ikn2), c128, s128, c64, s64)
    tile = lambda n: pl.BlockSpec((1, ts, n), lambda i, j: (i, j, 0))
    tile_t = lambda n: pl.BlockSpec((1, n, ts), lambda i, j: (i, 0, j))
    tab = pl.BlockSpec((ts, LANES), lambda i, j: (j, 0))
    in_specs = [tile(d)] + [_full_spec(a.shape) for a in args[1:12]] + [tab] * 4
    out_shape = (
        jax.ShapeDtypeStruct((b, nq, s), _BF),
        jax.ShapeDtypeStruct((b, s, nk), _BF),
        jax.ShapeDtypeStruct((b, s // ts, ATT_KV_HEADS, LANES + ONES_ROWS, ts), _BF),
        jax.ShapeDtypeStruct((b, nqi, s), _BF),
        jax.ShapeDtypeStruct((b, s, IDX_DIM), _BF),
        jax.ShapeDtypeStruct((b, IDX_HEADS, s), _F32),
        jax.ShapeDtypeStruct((b, s, d), _BF),
    )
    out_specs = (tile_t(nq), tile(nk),
                 pl.BlockSpec((1, 1, ATT_KV_HEADS, LANES + ONES_ROWS, ts), lambda i, j: (i, j, 0, 0, 0)), tile_t(nqi),
                 tile(IDX_DIM), tile_t(IDX_HEADS), tile(d))
    return pl.pallas_call(
        _dsa_proj_body, grid=(b, s // ts), in_specs=in_specs, out_specs=out_specs, out_shape=out_shape,
        compiler_params=_cparams("arbitrary", "arbitrary"), name="dsa_proj",
    )(*args)


def _key_to_float(u):
    key = u ^ jnp.int32(INT_MIN)
    bits = key ^ ((key >> 31) & jnp.int32(0x7FFFFFFF))
    return jnp.where((u >> 23) == 0, -jnp.inf, lax.bitcast_convert_type(bits, _F32))


def _high_half(x):
    bits = lax.bitcast_convert_type(x, jnp.int32) & jnp.int32(-65536)
    return lax.bitcast_convert_type(bits, _F32).astype(_BF)


def _dsa_body(qT_ref, qiT_ref, wT_ref, k_ref, vT_ref, ki_ref, g1_ref, out_ref,
              sc_ref, sh_ref, m_ref, l_ref, acc_ref, bias_ref, s_ref, p_ref, *, topk, tk):
    tq = qT_ref.shape[2]
    dh = LANES
    q0 = pl.program_id(1) * tq
    n_chunks = (q0 + tq + tk - 1) // tk

    qicat = jnp.concatenate([qiT_ref[0, h * IDX_DIM:(h + 1) * IDX_DIM, :] for h in range(IDX_HEADS)], axis=1)
    w = wT_ref[0]
    qpos = q0 + lax.broadcasted_iota(jnp.int32, (tk, tq), 1)
    krow = lax.broadcasted_iota(jnp.int32, (tk, tq), 0)

    def score_chunk(c, carry):
        r0 = pl.multiple_of(c * tk, tk)
        lg = _dot(ki_ref[0, pl.ds(r0, tk), :], qicat)
        s = jnp.zeros((tk, tq), _F32)
        for h in range(IDX_HEADS):
            s = s + jnp.maximum(lg[:, h * tq:(h + 1) * tq], 0.0) * w[h:h + 1, :]
        s = jnp.where(krow + r0 <= qpos, s, -jnp.inf)
        sc_ref[pl.ds(r0, tk), :] = s
        sh_ref[pl.ds(r0, tk), :] = _high_half(s)
        return carry

    lax.fori_loop(0, n_chunks, score_chunk, 0)

    def make_count(nc, ref, rows, dtype):
        def count_ge(cand):
            one, zero = jnp.ones((), dtype), jnp.zeros((), dtype)
            parts = [jnp.zeros((rows, tq), dtype) for _ in range(4)]
            n = 0
            for c in range(nc):
                hit = jnp.where(ref[c * tk:(c + 1) * tk, :] >= cand, one, zero)
                for r in range(tk // rows):
                    parts[n % 4] = parts[n % 4] + hit[r * rows:(r + 1) * rows, :]
                    n += 1
            parts = [p.astype(_F32) for p in parts]
            return jnp.sum((parts[0] + parts[1]) + (parts[2] + parts[3]), axis=0, keepdims=True)
        return count_ge

    max_chunks = sc_ref.shape[0] // tk
    assert max_chunks * tk // (2 * SUBLANES) <= 4 * 64
    coarse = [make_count(nc, sh_ref, 2 * SUBLANES, _BF) for nc in range(1, max_chunks + 1)]
    fine = [make_count(nc, sc_ref, SUBLANES, _F32) for nc in range(1, max_chunks + 1)]

    def bisect_high(i, carry):
        u, n_ge = carry
        trial = u | lax.shift_left(jnp.int32(1), 31 - i)
        cnt = lax.switch(n_chunks - 1, coarse, _high_half(_key_to_float(trial)))
        keep = cnt >= float(topk)
        return jnp.where(keep, trial, u), jnp.where(keep, cnt, n_ge)

    def bisect_low(i, carry):
        u, n_ge = carry
        trial = u | lax.shift_left(jnp.int32(1), 31 - i)
        cnt = lax.switch(n_chunks - 1, fine, _key_to_float(trial))
        keep = cnt >= float(topk)
        return jnp.where(keep, trial, u), jnp.where(keep, cnt, n_ge)

    carry = lax.fori_loop(0, 16, bisect_high, (jnp.zeros((1, tq), jnp.int32), jnp.zeros((1, tq), _F32)))
    u, n_ge = lax.fori_loop(16, 32, bisect_low, carry)
    tau = jnp.maximum(_key_to_float(u), -FLT_MAX)

    @pl.when(jnp.max(jnp.where(tau > -FLT_MAX, n_ge, 0.0)) > float(topk))
    def _():
        def count_gt(c, cnt):
            r0 = pl.multiple_of(c * tk, tk)
            return cnt + jnp.sum(jnp.where(sc_ref[pl.ds(r0, tk), :] > tau, 1.0, 0.0), axis=0, keepdims=True)

        need = float(topk) - lax.fori_loop(0, n_chunks, count_gt, jnp.zeros((1, tq), _F32))
        before = (lax.broadcasted_iota(jnp.int32, (tk, tk), 1)
                  < lax.broadcasted_iota(jnp.int32, (tk, tk), 0)).astype(_BF)

        def strike(c, seen):
            r0 = pl.multiple_of(c * tk, tk)
            blk = sc_ref[pl.ds(r0, tk), :]
            tied = blk == tau
            tied_b = jnp.where(tied, 1.0, 0.0).astype(_BF)
            earlier = _dot(before, tied_b) + seen
            sc_ref[pl.ds(r0, tk), :] = jnp.where(tied & (earlier >= need), -jnp.inf, blk)
            return seen + jnp.sum(tied_b.astype(_F32), axis=0, keepdims=True)

        lax.fori_loop(0, n_chunks, strike, jnp.zeros((1, tq), _F32))

    m_ref[...] = jnp.full(m_ref.shape, MASK_VALUE, _F32)
    l_ref[...] = jnp.zeros_like(l_ref)
    acc_ref[...] = jnp.zeros_like(acc_ref)

    sub = 8 * SUBLANES
    groups = [slice(r * SUBLANES, (r + 1) * SUBLANES) for r in range(sub // SUBLANES)]

    last_chunk = sc_ref.shape[0] // tk - 1
    ahead = 3

    def logits(c, h):
        r0 = pl.multiple_of(c * tk, tk)
        g = h // ATT_GROUP
        s_ref[h] = _dot(k_ref[0, pl.ds(r0, tk), g * dh:(g + 1) * dh], qT_ref[0, h * dh:(h + 1) * dh, :])

    for h in range(ahead):
        logits(0, h)

    def att_chunk(c, carry):
        r0 = pl.multiple_of(c * tk, tk)
        bias_ref[...] = jnp.where(sc_ref[pl.ds(r0, tk), :] >= tau, 0.0, MASK_VALUE)
        for h in range(ATT_HEADS):
            if h + ahead < ATT_HEADS:
                logits(c, h + ahead)
            else:
                logits(jnp.minimum(c + 1, last_chunk), h + ahead - ATT_HEADS)
            m_old = m_ref[h:h + 1, :]
            mx = jnp.broadcast_to(m_old, (SUBLANES, tq))
            for i in range(tk // sub):
                rows = slice(i * sub, (i + 1) * sub)
                sb = s_ref[h, rows, :] + bias_ref[rows, :]
                s_ref[h, rows, :] = sb
                for gr in groups:
                    mx = jnp.maximum(mx, sb[gr, :])
            m_new = jnp.max(mx, axis=0, keepdims=True)
            for i in range(tk // sub):
                rows = slice(i * sub, (i + 1) * sub)
                p_ref[h, rows, :] = jnp.exp((s_ref[h, rows, :] - m_new).astype(_BF))
            alpha = jnp.exp(m_old - m_new)
            pv = _dot(vT_ref[0, c, h // ATT_GROUP], p_ref[h])
            l_ref[h:h + 1, :] = alpha * l_ref[h:h + 1, :] + pv[dh:dh + 1, :]
            acc_ref[h * dh:(h + 1) * dh, :] = alpha * acc_ref[h * dh:(h + 1) * dh, :] + pv[:dh, :]
            m_ref[h:h + 1, :] = m_new
        return carry

    lax.fori_loop(0, n_chunks, att_chunk, 0)
    outs = []
    for h in range(ATT_HEADS):
        o = acc_ref[h * dh:(h + 1) * dh, :] / l_ref[h:h + 1, :]
        outs.append(o.T)
    out_ref[0] = (jnp.concatenate(outs, axis=1) * g1_ref[0].astype(_F32)).astype(_BF)


def _dsa(qT, qiT, wT, k, vT, ki, g1, topk, tq, tk):
    b, nq, s = qT.shape
    d = g1.shape[2]
    in_specs = [
        pl.BlockSpec((1, nq, tq), lambda i, j: (i, 0, j)),
        pl.BlockSpec((1, qiT.shape[1], tq), lambda i, j: (i, 0, j)),
        pl.BlockSpec((1, IDX_HEADS, tq), lambda i, j: (i, 0, j)),
        pl.BlockSpec((1, s, k.shape[2]), lambda i, j: (i, 0, 0)),
        pl.BlockSpec((1,) + vT.shape[1:], lambda i, j: (i, 0, 0, 0, 0)),
        pl.BlockSpec((1, s, IDX_DIM), lambda i, j: (i, 0, 0)),
        pl.BlockSpec((1, tq, d), lambda i, j: (i, j, 0)),
    ]
    return pl.pallas_call(
        functools.partial(_dsa_body, topk=topk, tk=tk),
        grid=(b, s // tq), in_specs=in_specs,
        out_specs=pl.BlockSpec((1, tq, d), lambda i, j: (i, j, 0)),
        out_shape=jax.ShapeDtypeStruct((b, s, d), _BF),
        scratch_shapes=[pltpu.VMEM((s, tq), _F32), pltpu.VMEM((s, tq), _BF), pltpu.VMEM((ATT_HEADS, tq), _F32),
                        pltpu.VMEM((ATT_HEADS, tq), _F32), pltpu.VMEM((nq, tq), _F32),
                        pltpu.VMEM((tk, tq), _F32), pltpu.VMEM((ATT_HEADS, tk, tq), _F32),
                        pltpu.VMEM((ATT_HEADS, tk, tq), _BF)],
        compiler_params=_cparams("arbitrary", "arbitrary"), name="dsa",
    )(qT, qiT, wT, k, vT, ki, g1)


def _mem_kv_body(mem_ref, mn_ref, wkv_ref, mkn_ref, mkT_ref, mv_ref):
    d = mem_ref.shape[2]
    dm = d // MEM_HEADS
    m = _rms(mem_ref[0], mn_ref[...]).astype(_BF)
    kv = _dot(m, wkv_ref[...])
    mk = jnp.concatenate([_rms(kv[:, i * dm:(i + 1) * dm], mkn_ref[...]) for i in range(MEM_HEADS)], axis=1)
    mkT_ref[0] = mk.T.astype(_BF)
    mv_ref[0] = kv[:, d:].astype(_BF)


def _mem_kv(mem, mn, wkv, mkn):
    b, m, d = mem.shape
    row = lambda v: v.reshape(1, -1)
    return pl.pallas_call(
        _mem_kv_body, grid=(b,),
        in_specs=[pl.BlockSpec((1, m, d), lambda i: (i, 0, 0)), _full_spec((1, d)), _full_spec(wkv.shape),
                  _full_spec((1, d // MEM_HEADS))],
        out_specs=(pl.BlockSpec((1, d, m), lambda i: (i, 0, 0)), pl.BlockSpec((1, m, d), lambda i: (i, 0, 0))),
        out_shape=(jax.ShapeDtypeStruct((b, d, m), _BF), jax.ShapeDtypeStruct((b, m, d), _BF)),
        compiler_params=_cparams("arbitrary"), name="mem_kv",
    )(mem, row(mn), wkv, row(mkn))


def _mem_body(x_ref, nm_ref, wmq_ref, wg2_ref, mqn_ref, mkT_ref, mv_ref, out_ref):
    d = x_ref.shape[2]
    dm = d // MEM_HEADS
    h = _rms(x_ref[0], nm_ref[...]).astype(_BF)
    mq = _dot(h, wmq_ref[...])
    outs = []
    for i in range(MEM_HEADS):
        qh = (_rms(mq[:, i * dm:(i + 1) * dm], mqn_ref[...]) * (float(dm) ** -0.5)).astype(_BF)
        s = _dot(qh, mkT_ref[0, i * dm:(i + 1) * dm, :])
        p = jnp.exp(s - jnp.max(s, axis=-1, keepdims=True))
        o = _dot(p.astype(_BF), mv_ref[0, :, i * dm:(i + 1) * dm])
        outs.append(o / jnp.sum(p, axis=-1, keepdims=True))
    out_ref[0] = (jnp.concatenate(outs, axis=1) * _sigmoid(_dot(h, wg2_ref[...]))).astype(_BF)


def _mem_branch(x, nm, wmq, wg2, mqn, mkT, mv, ts):
    b, s, d = x.shape
    m = mv.shape[1]
    row = lambda v: v.reshape(1, -1)
    return pl.pallas_call(
        _mem_body, grid=(b, s // ts),
        in_specs=[pl.BlockSpec((1, ts, d), lambda i, j: (i, j, 0)), _full_spec((1, d)), _full_spec(wmq.shape),
                  _full_spec(wg2.shape), _full_spec((1, d // MEM_HEADS)),
                  pl.BlockSpec((1, d, m), lambda i, j: (i, 0, 0)), pl.BlockSpec((1, m, d), lambda i, j: (i, 0, 0))],
        out_specs=pl.BlockSpec((1, ts, d), lambda i, j: (i, j, 0)),
        out_shape=jax.ShapeDtypeStruct((b, s, d), _BF),
        compiler_params=_cparams("arbitrary", "arbitrary"), name="mem",
    )(x, row(nm), wmq, wg2, row(mqn), mkT, mv)


def _mid_body(x_ref, a_ref, b_ref, c_ref, wo_ref, nf_ref, wpqT_ref, sk_ref, x1_ref, h2T_ref, scT_ref):
    merged = (a_ref[...].astype(_F32) + b_ref[...].astype(_F32) + c_ref[...].astype(_F32)).astype(_BF)
    x1 = x_ref[...] + _dot(merged, wo_ref[...])
    x1_ref[...] = x1
    h2T = _rms(x1, nf_ref[...]).T.astype(_BF)
    h2T_ref[...] = h2T
    pqT = _dot(wpqT_ref[...], h2T).astype(_BF)
    for c in range(sk_ref.shape[0]):
        scT_ref[c * PEER_KEYS:(c + 1) * PEER_KEYS, :] = _dot(sk_ref[c], pqT[c * LANES:(c + 1) * LANES, :])


def _mid(x2, ma, mb, mc, wo, nf, wpqT, sk, ts):
    t, d = x2.shape
    nsc = sk.shape[0] * PEER_KEYS
    tile = pl.BlockSpec((ts, d), lambda i: (i, 0))
    return pl.pallas_call(
        _mid_body, grid=(t // ts,),
        in_specs=[tile, tile, tile, tile, _full_spec(wo.shape), _full_spec((1, d)), _full_spec(wpqT.shape),
                  _full_spec(sk.shape)],
        out_specs=(tile, pl.BlockSpec((d, ts), lambda i: (0, i)), pl.BlockSpec((nsc, ts), lambda i: (0, i))),
        out_shape=(jax.ShapeDtypeStruct((t, d), _F32), jax.ShapeDtypeStruct((d, t), _BF),
                   jax.ShapeDtypeStruct((nsc, t), _F32)),
        compiler_params=_cparams("arbitrary"), name="mid",
    )(x2, ma, mb, mc, wo, nf.reshape(1, -1), wpqT, sk)


def _top_values(s, n):
    groups = [s[r * SUBLANES:(r + 1) * SUBLANES, :] for r in range(s.shape[0] // SUBLANES)]
    size = pl.next_power_of_2(len(groups))
    groups += [jnp.full_like(groups[0], -jnp.inf)] * (size - len(groups))
    k = 2
    while k <= size:
        j = k // 2
        while j >= 1:
            for i in range(size):
                l = i ^ j
                if l > i:
                    hi, lo = jnp.maximum(groups[i], groups[l]), jnp.minimum(groups[i], groups[l])
                    groups[i], groups[l] = (hi, lo) if (i & k) == 0 else (lo, hi)
            j //= 2
        k *= 2
    vals = []
    for it in range(n):
        m = jnp.max(groups[0], axis=0, keepdims=True)
        vals.append(m)
        remaining = n - 1 - it
        head = groups[0] == m
        for d in range(min(remaining, size)):
            below = groups[d + 1] if d + 1 < size else jnp.full_like(m, -jnp.inf)
            groups[d] = jnp.where(head, below, groups[d])
    return vals


def _route_body(scT_ref, tau_ref, c_ref, top2_ref):
    def head(h, carry):
        r1 = pl.multiple_of(h * 2 * PEER_KEYS, PEER_KEYS)
        r2 = pl.multiple_of(h * 2 * PEER_KEYS + PEER_KEYS, PEER_KEYS)
        n = PEER_TOPK + 1
        v1 = _top_values(scT_ref[pl.ds(r1, PEER_KEYS), :], n)
        v2 = _top_values(scT_ref[pl.ds(r2, PEER_KEYS), :], n)
        edge = SUBLANES
        assert 2 * (edge + 1) > n
        pad = [jnp.full_like(v1[0], -jnp.inf)] * (-n % SUBLANES)
        col1 = jnp.concatenate(v1 + pad, axis=0)
        col2 = jnp.concatenate(v2 + pad, axis=0)
        cand = jnp.concatenate([v1[a] + col2[:edge] for a in range(edge)]
                               + [v1[0] + col2[edge:], v2[0] + col1[edge:]], axis=0)
        best = _top_values(cand, n)
        tau = best[PEER_TOPK - 1]
        top = v1[0] + v2[0]
        z = jnp.sum(jnp.where(cand >= tau, jnp.exp(cand - top), 0.0), axis=0, keepdims=True)
        tau_ref[pl.ds(h, 1), :] = 0.5 * (tau + best[PEER_TOPK])
        c_ref[pl.ds(h, 1), :] = top + jnp.log(z)
        top2_ref[pl.ds(pl.multiple_of(h * TOP_ROWS, SUBLANES), TOP_ROWS), :] = col2
        return carry

    lax.fori_loop(0, PEER_HEADS, head, 0, unroll=2)


def _route(scT, te):
    nsc, t = scT.shape
    spec = pl.BlockSpec((PEER_HEADS, te), lambda i: (0, i))
    return pl.pallas_call(
        _route_body, grid=(t // te,),
        in_specs=[pl.BlockSpec((nsc, te), lambda i: (0, i))],
        out_specs=(spec, spec, pl.BlockSpec((PEER_HEADS * TOP_ROWS, te), lambda i: (0, i))),
        out_shape=(jax.ShapeDtypeStruct((PEER_HEADS, t), _F32),) * 2
        + (jax.ShapeDtypeStruct((PEER_HEADS * TOP_ROWS, t), _F32),),
        compiler_params=_cparams("arbitrary"), name="route",
    )(scT)


def _experts_body(h2T_ref, scT_ref, tau_ref, c_ref, top2_ref, u_ref, vT_ref, x1_ref, out_ref,
                  acc_ref, e2_ref, rank_ref, cnt_ref, scale_ref, act_ref, g_ref, rows_ref, *, rows_per_step,
                  n_blocks):
    e = pl.program_id(1)
    nk = PEER_KEYS
    ts = act_ref.shape[1]

    @pl.when(e == 0)
    def _():
        acc_ref[...] = jnp.zeros_like(acc_ref)
        for h in range(PEER_HEADS):
            best = [top2_ref[h * TOP_ROWS + b:h * TOP_ROWS + b + 1, :] for b in range(PEER_TOPK + 1)]
            s1 = scT_ref[2 * h * nk:(2 * h + 1) * nk, :]
            s2 = scT_ref[(2 * h + 1) * nk:(2 * h + 2) * nk, :]
            e2_ref[h * nk:(h + 1) * nk, :] = jnp.exp(s2 - best[0]).astype(_BF)
            rank = jnp.zeros((nk, ts), _F32)
            for b in range(PEER_TOPK):
                rank = jnp.where(best[b] > s2, float(b + 1), rank)
            rank_ref[h * nk:(h + 1) * nk, :] = rank.astype(_BF)
            thr = tau_ref[h:h + 1, :] - s1
            cnt = jnp.zeros((nk, ts), _F32)
            for b in range(PEER_TOPK + 1):
                cnt = jnp.where(best[b] >= thr, float(b + 1), cnt)
            cnt_ref[h * nk:(h + 1) * nk, :] = cnt
            scale_ref[h * nk:(h + 1) * nk, :] = 0.5 * jnp.exp(s1 + best[0] - c_ref[h:h + 1, :])

    n_rows = rows_per_step * PEER_HEADS
    first_group = 2
    zero = jnp.zeros((), _BF)

    def first_matmul(i):
        rows = slice(i * first_group * nk, (i + 1) * first_group * nk)
        act_ref[rows, :] = _dot(u_ref[rows, :], h2T_ref[...])

    for i in range(rows_per_step):
        for h in range(PEER_HEADS):
            src = h * nk + e * rows_per_step + i
            r = i * PEER_HEADS + h
            rows_ref[r:r + 1, :] = cnt_ref[pl.ds(src, 1), :]
            rows_ref[n_rows + r:n_rows + r + 1, :] = scale_ref[pl.ds(src, 1), :]

    second_group = 4
    ahead = 1
    n_groups = rows_per_step // first_group
    for g in range(n_groups):
        first_matmul(g)
    for i in range(rows_per_step):
        wgt = jnp.zeros((nk, ts), _BF)
        for h in range(PEER_HEADS):
            r = i * PEER_HEADS + h
            cnt = rows_ref[r:r + 1, :].astype(_BF)
            scale = rows_ref[n_rows + r:n_rows + r + 1, :].astype(_BF)
            hit = jnp.where(rank_ref[h * nk:(h + 1) * nk, :] < cnt, e2_ref[h * nk:(h + 1) * nk, :], zero)
            wgt = wgt + hit * scale
        g_ref[i * nk:(i + 1) * nk, :] = wgt
    for i in range(rows_per_step):
        x = act_ref[i * nk:(i + 1) * nk, :].astype(_BF)
        t = jnp.tanh(x * (jnp.asarray(GELU_C0, _BF) + jnp.asarray(GELU_C1, _BF) * (x * x)))
        g_ref[i * nk:(i + 1) * nk, :] = (x + x * t) * g_ref[i * nk:(i + 1) * nk, :]
        if i % second_group == second_group - 1:
            cols = slice((i + 1 - second_group) * nk, (i + 1) * nk)
            acc_ref[...] += _dot(vT_ref[0, :, cols], g_ref[cols, :])

    @pl.when(e == n_blocks - 1)
    def _():
        out_ref[...] = x1_ref[...] + acc_ref[...].T


def _experts(h2T, scT, tau, cn, top2, u, vT, x1, ts, rows_per_step):
    d, t = h2T.shape
    nsc = scT.shape[0]
    ne = u.shape[0]
    eb = rows_per_step * PEER_KEYS
    nb = ne // eb
    nhk = PEER_HEADS * PEER_KEYS
    col = lambda n: pl.BlockSpec((n, ts), lambda i, e: (0, i))
    tile = pl.BlockSpec((ts, d), lambda i, e: (i, 0))
    return pl.pallas_call(
        functools.partial(_experts_body, rows_per_step=rows_per_step, n_blocks=nb),
        grid=(t // ts, nb),
        in_specs=[col(d), col(nsc), col(PEER_HEADS), col(PEER_HEADS), col(top2.shape[0]),
                  pl.BlockSpec((eb, d), lambda i, e: (e, 0)), pl.BlockSpec((1, d, eb), lambda i, e: (e, 0, 0)), tile],
        out_specs=tile,
        out_shape=jax.ShapeDtypeStruct((t, d), _F32),
        scratch_shapes=[pltpu.VMEM((d, ts), _F32), pltpu.VMEM((nhk, ts), _BF), pltpu.VMEM((nhk, ts), _BF),
                        pltpu.VMEM((nhk, ts), _F32), pltpu.VMEM((nhk, ts), _F32),
                        pltpu.VMEM((eb, ts), _F32), pltpu.VMEM((eb, ts), _BF),
                        pltpu.VMEM((2 * rows_per_step * PEER_HEADS, ts), _F32)],
        compiler_params=_cparams("arbitrary", "arbitrary"), name="experts",
    )(h2T, scT, tau, cn, top2, u, vT, x1)


def _block_diag(w):
    n, c, d = w.shape
    return jnp.einsum("ncd,nm->ncmd", w, jnp.eye(n, dtype=w.dtype)).reshape(n * c, n * d)


def _rope_tables(s, dim, reps):
    half = dim // 2
    freq = ROPE_THETA ** (-jnp.arange(half, dtype=_F32) / half)
    ang = jnp.arange(s, dtype=_F32)[:, None] * freq[None, :]
    cos = jnp.tile(jnp.concatenate([jnp.cos(ang), jnp.cos(ang)], axis=1), (1, reps))
    sin = jnp.tile(jnp.concatenate([-jnp.sin(ang), jnp.sin(ang)], axis=1), (1, reps))
    return cos, sin


def _tile(n, pref):
    return pref if n % pref == 0 else n


def _layer(x, mem, p, l):
    b, s, d = x.shape
    bf = lambda a: a.astype(_BF)
    w_in = p["w_in"][l]
    n_q, n_kv = ATT_HEADS * LANES, ATT_KV_HEADS * LANES
    n_qi = IDX_HEADS * IDX_DIM
    sizes = (d, d, n_q, n_kv, n_kv, n_qi, IDX_DIM, IDX_HEADS, d, 3 * d)
    offs = [0]
    for n in sizes:
        offs.append(offs[-1] + n)
    wlx, wlg, wq, wk, wv, wqi, wki, wwi, wmq, wgates = (w_in[:, offs[i]:offs[i + 1]] for i in range(len(sizes)))
    wki2 = jnp.concatenate([wki, wki], axis=1)
    wwi_p = jnp.pad(wwi, ((0, 0), (0, LANES - IDX_HEADS)))
    wg0, wg1, wg2 = wgates[:, :d], wgates[:, d:2 * d], wgates[:, 2 * d:]
    nm = p["norm_mix"][l]

    ts_a = _tile(s, 512)
    m_lru = _lru_branch(x, nm, bf(wlx), bf(wlg), bf(wg0), p["conv_w"][l], p["conv_b"][l],
                        bf(_block_diag(p["lru_wa"][l])), p["lru_ba"][l].reshape(-1),
                        bf(_block_diag(p["lru_wi"][l])), p["lru_bi"][l].reshape(-1), p["lru_lambda"][l], ts_a)

    tk = _tile(s, 512)
    c128, s128 = _rope_tables(s, LANES, 1)
    c64, s64 = _rope_tables(s, IDX_DIM, LANES // IDX_DIM)
    ikn2 = jnp.concatenate([p["idx_k_norm"][l]] * 2)
    qT, k, vT, qiT, ki, wT, g1 = _dsa_proj(x, nm, bf(wq), bf(wk), bf(wv), bf(wqi), bf(wki2), bf(wwi_p), bf(wg1),
                                           p["q_norm"][l], p["k_norm"][l], ikn2, c128, s128, c64, s64, tk)
    m_att = _dsa(qT, qiT, wT, k, vT, ki, g1, min(TOPK_MAX, s // 4), _tile(s, 256), tk)

    mkT, mv = _mem_kv(mem, p["mem_norm"][l], bf(p["w_mem_kv"][l]), p["mem_k_norm"][l])
    m_mem = _mem_branch(x, nm, bf(wmq), bf(wg2), p["mem_q_norm"][l], mkT, mv, _tile(s, 512))

    t = b * s
    flat = lambda a: a.reshape(t, d)
    sk = bf(p["peer_subkeys"][l].reshape(2 * PEER_HEADS, PEER_KEYS, -1))
    x1, h2T, scT = _mid(flat(x), flat(m_lru), flat(m_att), flat(m_mem), bf(p["w_out"][l]), p["norm_ffn"][l],
                        bf(p["peer_wq"][l].T), sk, _tile(t, 512))
    tau, cn, top2 = _route(scT, _tile(t, 256))
    rows_per_step = 16
    vT = bf(p["peer_v"][l]).reshape(-1, rows_per_step * PEER_KEYS, d).transpose(0, 2, 1)
    out = _experts(h2T, scT, tau, cn, top2, bf(p["peer_u"][l]), vT, x1, _tile(t, 512), rows_per_step)
    return out.reshape(b, s, d)


def kernel(x, mem, norm_mix, w_in, conv_w, conv_b, lru_wa, lru_ba, lru_wi, lru_bi, lru_lambda, q_norm, k_norm,
           idx_k_norm, mem_norm, w_mem_kv, mem_q_norm, mem_k_norm, w_out, norm_ffn, peer_wq, peer_subkeys,
           peer_u, peer_v):
    p = dict(norm_mix=norm_mix, w_in=w_in, conv_w=conv_w, conv_b=conv_b, lru_wa=lru_wa, lru_ba=lru_ba,
             lru_wi=lru_wi, lru_bi=lru_bi, lru_lambda=lru_lambda, q_norm=q_norm, k_norm=k_norm,
             idx_k_norm=idx_k_norm, mem_norm=mem_norm, w_mem_kv=w_mem_kv, mem_q_norm=mem_q_norm,
             mem_k_norm=mem_k_norm, w_out=w_out, norm_ffn=norm_ffn, peer_wq=peer_wq, peer_subkeys=peer_subkeys,
             peer_u=peer_u, peer_v=peer_v)
    for l in range(norm_mix.shape[0]):
        x = _layer(x, mem, p, l)
    return x
```

```python
import functools

import jax
import jax.numpy as jnp
from jax import lax
from jax.experimental import pallas as pl
from jax.experimental.pallas import tpu as pltpu

_F32 = jnp.float32
_BF = jnp.bfloat16

EPS = 1e-6
ROPE_THETA = 10000.0
LRU_C = 8.0
CONV_WIDTH = 4
ATT_HEADS = 8
ATT_KV_HEADS = 2
ATT_GROUP = ATT_HEADS // ATT_KV_HEADS
IDX_HEADS = 8
IDX_DIM = 64
TOPK_MAX = 256
MEM_HEADS = 4
PEER_HEADS = 8
PEER_KEYS = 128
PEER_TOPK = 16
TOP_ROWS = 24

LANES = 128
SUBLANES = 8
ONES_ROWS = 2 * SUBLANES
VMEM_LIMIT_BYTES = 56 * 1024 * 1024
MASK_VALUE = -1e30
FLT_MAX = 3.4028234663852886e38
INT_MIN = -(2 ** 31)


def _cparams(*sem):
    return pltpu.CompilerParams(dimension_semantics=sem, vmem_limit_bytes=VMEM_LIMIT_BYTES)


def _rms(x, g):
    return x * lax.rsqrt(jnp.mean(x * x, axis=-1, keepdims=True) + EPS) * g


GELU_C0 = 0.7978845608028654
GELU_C1 = GELU_C0 * 0.044715


def _gelu(x):
    return 0.5 * x * (1.0 + jnp.tanh(x * (GELU_C0 + GELU_C1 * (x * x))))


def _sigmoid(x):
    return 0.5 * jnp.tanh(0.5 * x) + 0.5


def _dot(a, b):
    return jnp.dot(a, b, preferred_element_type=_F32)


def _full_spec(shape):
    n = len(shape)
    return pl.BlockSpec(shape, lambda *_: (0,) * n)


def _lru_body(x_ref, nm_ref, wlx_ref, wlg_ref, wg0_ref, cw_ref, cb_ref, wa_ref, ba_ref, wi_ref,
              bi_ref, lam_ref, out_ref, buf_ref, hc_ref, a_ref, b_ref):
    ts = x_ref.shape[1]
    c_dim = out_ref.shape[2]

    @pl.when(pl.program_id(1) == 0)
    def _():
        buf_ref[0:SUBLANES, :] = jnp.zeros((SUBLANES, c_dim), _F32)
        hc_ref[...] = jnp.zeros_like(hc_ref)

    h = _rms(x_ref[0], nm_ref[...]).astype(_BF)
    lx = _dot(h, wlx_ref[...])
    buf_ref[SUBLANES:SUBLANES + ts, :] = lx
    ext = buf_ref[...]
    cw = cw_ref[...]
    xc = cb_ref[...] + cw[3:4] * lx
    for j in range(CONV_WIDTH - 1):
        d = CONV_WIDTH - 1 - j
        xc = xc + cw[j:j + 1] * pltpu.roll(ext, d, 0)[SUBLANES:, :]
    buf_ref[0:SUBLANES, :] = lx[ts - SUBLANES:, :]

    xcb = xc.astype(_BF)
    r = _sigmoid(_dot(xcb, wa_ref[...]) + ba_ref[...])
    ig = _sigmoid(_dot(xcb, wi_ref[...]) + bi_ref[...])
    lam = lam_ref[...]
    softplus_neg_lam = jnp.maximum(-lam, 0.0) + jnp.log1p(jnp.exp(-jnp.abs(lam)))
    log_a = (-LRU_C) * r * softplus_neg_lam
    a = jnp.exp(log_a)
    y = jnp.tanh(-log_a) * (a * a + 1.0)
    u = jnp.where(y > 0.0, y * lax.rsqrt(y), 0.0) * (ig * xc)

    a = a.reshape(ts // SUBLANES, SUBLANES, c_dim)
    u = u.reshape(ts // SUBLANES, SUBLANES, c_dim)
    rid = lax.broadcasted_iota(jnp.int32, a.shape, 1)
    for d in (1, 2, 4):
        ok = rid >= d
        a_s = pltpu.roll(a, d, 1)
        u_s = pltpu.roll(u, d, 1)
        u = jnp.where(ok, a * u_s + u, u)
        a = jnp.where(ok, a * a_s, a)
    a_ref[...] = a.reshape(ts, c_dim)
    b_ref[...] = u.reshape(ts, c_dim)

    def step(g, hp):
        r0 = pl.multiple_of(g * SUBLANES, SUBLANES)
        hcur = a_ref[pl.ds(r0, SUBLANES), :] * hp + b_ref[pl.ds(r0, SUBLANES), :]
        b_ref[pl.ds(r0, SUBLANES), :] = hcur
        return jnp.broadcast_to(hcur[SUBLANES - 1:SUBLANES, :], (SUBLANES, c_dim))

    hc_ref[...] = lax.fori_loop(0, ts // SUBLANES, step, hc_ref[...])
    gate = _gelu(_dot(h, wlg_ref[...])) * _sigmoid(_dot(h, wg0_ref[...]))
    out_ref[0] = (b_ref[...] * gate).astype(_BF)


def _lru_branch(x, nm, wlx, wlg, wg0, cw, cb, wa, ba, wi, bi, lam, ts):
    b, s, d = x.shape
    c = wlx.shape[1]
    row = lambda v: v.reshape(1, -1)
    args = (x, row(nm), wlx, wlg, wg0, cw, row(cb), wa, row(ba), wi, row(bi), row(lam))
    in_specs = [pl.BlockSpec((1, ts, d), lambda i, j: (i, j, 0))]
    in_specs += [_full_spec(a.shape) for a in args[1:]]
    return pl.pallas_call(
        _lru_body,
        grid=(b, s // ts),
        in_specs=in_specs,
        out_specs=pl.BlockSpec((1, ts, c), lambda i, j: (i, j, 0)),
        out_shape=jax.ShapeDtypeStruct((b, s, c), _BF),
        scratch_shapes=[pltpu.VMEM((ts + SUBLANES, c), _F32), pltpu.VMEM((SUBLANES, c), _F32),
                        pltpu.VMEM((ts, c), _F32), pltpu.VMEM((ts, c), _F32)],
        compiler_params=_cparams("arbitrary", "arbitrary"),
        name="lru",
    )(*args)


def _rope128(x, cos, sin):
    return x * cos + pltpu.roll(x, LANES // 2, 1) * sin


def _rope64(x, cos, sin, first_half):
    rot = jnp.where(first_half, pltpu.roll(x, LANES - IDX_DIM // 2, 1), pltpu.roll(x, IDX_DIM // 2, 1))
    return x * cos + rot * sin


def _dsa_proj_body(x_ref, nm_ref, wq_ref, wk_ref, wv_ref, wqi_ref, wki_ref, wwi_ref, wg1_ref,
                   qn_ref, kn_ref, ikn_ref, c128_ref, s128_ref, c64_ref, s64_ref,
                   qT_ref, k_ref, vT_ref, qiT_ref, ki_ref, wT_ref, g1_ref):
    ts = x_ref.shape[1]
    h = _rms(x_ref[0], nm_ref[...]).astype(_BF)
    c128, s128 = c128_ref[...], s128_ref[...]
    c64, s64 = c64_ref[...], s64_ref[...]
    first_half = (lax.broadcasted_iota(jnp.int32, (ts, LANES), 1) & (IDX_DIM - 1)) < IDX_DIM // 2

    def head_norm_rope(t, g, scale):
        outs = []
        for i in range(t.shape[1] // LANES):
            th = _rms(t[:, i * LANES:(i + 1) * LANES], g)
            outs.append(_rope128(th, c128, s128) * scale)
        return jnp.concatenate(outs, axis=1)

    q = head_norm_rope(_dot(h, wq_ref[...]), qn_ref[...], float(LANES) ** -0.5)
    qT_ref[0] = q.T.astype(_BF)
    k_ref[0] = head_norm_rope(_dot(h, wk_ref[...]), kn_ref[...], 1.0).astype(_BF)
    vT = _dot(h, wv_ref[...]).T.astype(_BF)
    for g in range(ATT_KV_HEADS):
        vT_ref[0, 0, g, :LANES, :] = vT[g * LANES:(g + 1) * LANES, :]
        vT_ref[0, 0, g, LANES:, :] = jnp.ones((ONES_ROWS, ts), _BF)

    qi = _dot(h, wqi_ref[...])
    qi = jnp.concatenate(
        [_rope64(qi[:, i * LANES:(i + 1) * LANES], c64, s64, first_half) for i in range(qi.shape[1] // LANES)],
        axis=1) * (float(IDX_DIM) ** -0.5)
    qiT_ref[0] = qi.T.astype(_BF)
    ki = _rope64(_rms(_dot(h, wki_ref[...]), ikn_ref[...]), c64, s64, first_half)
    ki_ref[0] = ki[:, :IDX_DIM].astype(_BF)
    wi = _dot(h, wwi_ref[...]) * (float(IDX_HEADS) ** -0.5)
    wT_ref[0] = wi.T[:IDX_HEADS, :]
    g1_ref[0] = _sigmoid(_dot(h, wg1_ref[...])).astype(_BF)


def _dsa_proj(x, nm, wq, wk, wv, wqi, wki2, wwi, wg1, qn, kn, ikn2, c128, s128, c64, s64, ts):
    b, s, d = x.shape
    row = lambda v: v.reshape(1, -1)
    nq, nk, nqi = wq.shape[1], wk.shape[1], wqi.shape[1]
    args = (x, row(nm), wq, wk, wv, wqi, wki2, wwi, wg1, row(qn), row(kn), row(ikn2), c128, s128, c64, s64)
    tile = lambda n: pl.BlockSpec((1, ts, n), lambda i, j: (i, j, 0))
    tile_t = lambda n: pl.BlockSpec((1, n, ts), lambda i, j: (i, 0, j))
    tab = pl.BlockSpec((ts, LANES), lambda i, j: (j, 0))
    in_specs = [tile(d)] + [_full_spec(a.shape) for a in args[1:12]] + [tab] * 4
    out_shape = (
        jax.ShapeDtypeStruct((b, nq, s), _BF),
        jax.ShapeDtypeStruct((b, s, nk), _BF),
        jax.ShapeDtypeStruct((b, s // ts, ATT_KV_HEADS, LANES + ONES_ROWS, ts), _BF),
        jax.ShapeDtypeStruct((b, nqi, s), _BF),
        jax.ShapeDtypeStruct((b, s, IDX_DIM), _BF),
        jax.ShapeDtypeStruct((b, IDX_HEADS, s), _F32),
        jax.ShapeDtypeStruct((b, s, d), _BF),
    )
    out_specs = (tile_t(nq), tile(nk),
                 pl.BlockSpec((1, 1, ATT_KV_HEADS, LANES + ONES_ROWS, ts), lambda i, j: (i, j, 0, 0, 0)), tile_t(nqi),
                 tile(IDX_DIM), tile_t(IDX_HEADS), tile(d))
    return pl.pallas_call(
        _dsa_proj_body, grid=(b, s // ts), in_specs=in_specs, out_specs=out_specs, out_shape=out_shape,
        compiler_params=_cparams("arbitrary", "arbitrary"), name="dsa_proj",
    )(*args)


def _key_to_float(u):
    key = u ^ jnp.int32(INT_MIN)
    bits = key ^ ((key >> 31) & jnp.int32(0x7FFFFFFF))
    return jnp.where((u >> 23) == 0, -jnp.inf, lax.bitcast_convert_type(bits, _F32))


def _high_half(x):
    bits = lax.bitcast_convert_type(x, jnp.int32) & jnp.int32(-65536)
    return lax.bitcast_convert_type(bits, _F32).astype(_BF)


def _dsa_body(qT_ref, qiT_ref, wT_ref, k_ref, vT_ref, ki_ref, g1_ref, out_ref,
              sc_ref, sh_ref, m_ref, l_ref, acc_ref, bias_ref, s_ref, p_ref, *, topk, tk):
    tq = qT_ref.shape[2]
    dh = LANES
    q0 = pl.program_id(1) * tq
    n_chunks = (q0 + tq + tk - 1) // tk

    qicat = jnp.concatenate([qiT_ref[0, h * IDX_DIM:(h + 1) * IDX_DIM, :] for h in range(IDX_HEADS)], axis=1)
    w = wT_ref[0]
    qpos = q0 + lax.broadcasted_iota(jnp.int32, (tk, tq), 1)
    krow = lax.broadcasted_iota(jnp.int32, (tk, tq), 0)

    def score_chunk(c, carry):
        r0 = pl.multiple_of(c * tk, tk)
        lg = _dot(ki_ref[0, pl.ds(r0, tk), :], qicat)
        s = jnp.zeros((tk, tq), _F32)
        for h in range(IDX_HEADS):
            s = s + jnp.maximum(lg[:, h * tq:(h + 1) * tq], 0.0) * w[h:h + 1, :]
        s = jnp.where(krow + r0 <= qpos, s, -jnp.inf)
        sc_ref[pl.ds(r0, tk), :] = s
        sh_ref[pl.ds(r0, tk), :] = _high_half(s)
        return carry

    lax.fori_loop(0, n_chunks, score_chunk, 0)

    def make_count(nc, ref, rows, dtype):
        def count_ge(cand):
            one, zero = jnp.ones((), dtype), jnp.zeros((), dtype)
            parts = [jnp.zeros((rows, tq), dtype) for _ in range(4)]
            n = 0
            for c in range(nc):
                hit = jnp.where(ref[c * tk:(c + 1) * tk, :] >= cand, one, zero)
                for r in range(tk // rows):
                    parts[n % 4] = parts[n % 4] + hit[r * rows:(r + 1) * rows, :]
                    n += 1
            parts = [p.astype(_F32) for p in parts]
            return jnp.sum((parts[0] + parts[1]) + (parts[2] + parts[3]), axis=0, keepdims=True)
        return count_ge

    max_chunks = sc_ref.shape[0] // tk
    assert max_chunks * tk // (2 * SUBLANES) <= 4 * 64
    coarse = [make_count(nc, sh_ref, 2 * SUBLANES, _BF) for nc in range(1, max_chunks + 1)]
    fine = [make_count(nc, sc_ref, SUBLANES, _F32) for nc in range(1, max_chunks + 1)]

    def bisect_high(i, carry):
        u, n_ge = carry
        trial = u | lax.shift_left(jnp.int32(1), 31 - i)
        cnt = lax.switch(n_chunks - 1, coarse, _high_half(_key_to_float(trial)))
        keep = cnt >= float(topk)
        return jnp.where(keep, trial, u), jnp.where(keep, cnt, n_ge)

    def bisect_low(i, carry):
        u, n_ge = carry
        trial = u | lax.shift_left(jnp.int32(1), 31 - i)
        cnt = lax.switch(n_chunks - 1, fine, _key_to_float(trial))
        keep = cnt >= float(topk)
        return jnp.where(keep, trial, u), jnp.where(keep, cnt, n_ge)

    carry = lax.fori_loop(0, 16, bisect_high, (jnp.zeros((1, tq), jnp.int32), jnp.zeros((1, tq), _F32)))
    u, n_ge = lax.fori_loop(16, 32, bisect_low, carry)
    tau = jnp.maximum(_key_to_float(u), -FLT_MAX)

    @pl.when(jnp.max(jnp.where(tau > -FLT_MAX, n_ge, 0.0)) > float(topk))
    def _():
        def count_gt(c, cnt):
            r0 = pl.multiple_of(c * tk, tk)
            return cnt + jnp.sum(jnp.where(sc_ref[pl.ds(r0, tk), :] > tau, 1.0, 0.0), axis=0, keepdims=True)

        need = float(topk) - lax.fori_loop(0, n_chunks, count_gt, jnp.zeros((1, tq), _F32))
        before = (lax.broadcasted_iota(jnp.int32, (tk, tk), 1)
                  < lax.broadcasted_iota(jnp.int32, (tk, tk), 0)).astype(_BF)

        def strike(c, seen):
            r0 = pl.multiple_of(c * tk, tk)
            blk = sc_ref[pl.ds(r0, tk), :]
            tied = blk == tau
            tied_b = jnp.where(tied, 1.0, 0.0).astype(_BF)
            earlier = _dot(before, tied_b) + seen
            sc_ref[pl.ds(r0, tk), :] = jnp.where(tied & (earlier >= need), -jnp.inf, blk)
            return seen + jnp.sum(tied_b.astype(_F32), axis=0, keepdims=True)

        lax.fori_loop(0, n_chunks, strike, jnp.zeros((1, tq), _F32))

    m_ref[...] = jnp.full(m_ref.shape, MASK_VALUE, _F32)
    l_ref[...] = jnp.zeros_like(l_ref)
    acc_ref[...] = jnp.zeros_like(acc_ref)

    sub = 8 * SUBLANES
    groups = [slice(r * SUBLANES, (r + 1) * SUBLANES) for r in range(sub // SUBLANES)]

    last_chunk = sc_ref.shape[0] // tk - 1
    ahead = 3

    def logits(c, h):
        r0 = pl.multiple_of(c * tk, tk)
        g = h // ATT_GROUP
        s_ref[h] = _dot(k_ref[0, pl.ds(r0, tk), g * dh:(g + 1) * dh], qT_ref[0, h * dh:(h + 1) * dh, :])

    for h in range(ahead):
        logits(0, h)

    def att_chunk(c, carry):
        r0 = pl.multiple_of(c * tk, tk)
        bias_ref[...] = jnp.where(sc_ref[pl.ds(r0, tk), :] >= tau, 0.0, MASK_VALUE)
        for h in range(ATT_HEADS):
            if h + ahead < ATT_HEADS:
                logits(c, h + ahead)
            else:
                logits(jnp.minimum(c + 1, last_chunk), h + ahead - ATT_HEADS)
            m_old = m_ref[h:h + 1, :]
            mx = jnp.broadcast_to(m_old, (SUBLANES, tq))
            for i in range(tk // sub):
                rows = slice(i * sub, (i + 1) * sub)
                sb = s_ref[h, rows, :] + bias_ref[rows, :]
                s_ref[h, rows, :] = sb
                for gr in groups:
                    mx = jnp.maximum(mx, sb[gr, :])
            m_new = jnp.max(mx, axis=0, keepdims=True)
            for i in range(tk // sub):
                rows = slice(i * sub, (i + 1) * sub)
                p_ref[h, rows, :] = jnp.exp((s_ref[h, rows, :] - m_new).astype(_BF))
            alpha = jnp.exp(m_old - m_new)
            pv = _dot(vT_ref[0, c, h // ATT_GROUP], p_ref[h])
            l_ref[h:h + 1, :] = alpha * l_ref[h:h + 1, :] + pv[dh:dh + 1, :]
            acc_ref[h * dh:(h + 1) * dh, :] = alpha * acc_ref[h * dh:(h + 1) * dh, :] + pv[:dh, :]
            m_ref[h:h + 1, :] = m_new
        return carry

    lax.fori_loop(0, n_chunks, att_chunk, 0)
    outs = []
    for h in range(ATT_HEADS):
        o = acc_ref[h * dh:(h + 1) * dh, :] / l_ref[h:h + 1, :]
        outs.append(o.T)
    out_ref[0] = (jnp.concatenate(outs, axis=1) * g1_ref[0].astype(_F32)).astype(_BF)


def _dsa(qT, qiT, wT, k, vT, ki, g1, topk, tq, tk):
    b, nq, s = qT.shape
    d = g1.shape[2]
    in_specs = [
        pl.BlockSpec((1, nq, tq), lambda i, j: (i, 0, j)),
        pl.BlockSpec((1, qiT.shape[1], tq), lambda i, j: (i, 0, j)),
        pl.BlockSpec((1, IDX_HEADS, tq), lambda i, j: (i, 0, j)),
        pl.BlockSpec((1, s, k.shape[2]), lambda i, j: (i, 0, 0)),
        pl.BlockSpec((1,) + vT.shape[1:], lambda i, j: (i, 0, 0, 0, 0)),
        pl.BlockSpec((1, s, IDX_DIM), lambda i, j: (i, 0, 0)),
        pl.BlockSpec((1, tq, d), lambda i, j: (i, j, 0)),
    ]
    return pl.pallas_call(
        functools.partial(_dsa_body, topk=topk, tk=tk),
        grid=(b, s // tq), in_specs=in_specs,
        out_specs=pl.BlockSpec((1, tq, d), lambda i, j: (i, j, 0)),
        out_shape=jax.ShapeDtypeStruct((b, s, d), _BF),
        scratch_shapes=[pltpu.VMEM((s, tq), _F32), pltpu.VMEM((s, tq), _BF), pltpu.VMEM((ATT_HEADS, tq), _F32),
                        pltpu.VMEM((ATT_HEADS, tq), _F32), pltpu.VMEM((nq, tq), _F32),
                        pltpu.VMEM((tk, tq), _F32), pltpu.VMEM((ATT_HEADS, tk, tq), _F32),
                        pltpu.VMEM((ATT_HEADS, tk, tq), _BF)],
        compiler_params=_cparams("arbitrary", "arbitrary"), name="dsa",
    )(qT, qiT, wT, k, vT, ki, g1)


def _mem_kv_body(mem_ref, mn_ref, wkv_ref, mkn_ref, mkT_ref, mv_ref):
    d = mem_ref.shape[2]
    dm = d // MEM_HEADS
    m = _rms(mem_ref[0], mn_ref[...]).astype(_BF)
    kv = _dot(m, wkv_ref[...])
    mk = jnp.concatenate([_rms(kv[:, i * dm:(i + 1) * dm], mkn_ref[...]) for i in range(MEM_HEADS)], axis=1)
    mkT_ref[0] = mk.T.astype(_BF)
    mv_ref[0] = kv[:, d:].astype(_BF)


def _mem_kv(mem, mn, wkv, mkn):
    b, m, d = mem.shape
    row = lambda v: v.reshape(1, -1)
    return pl.pallas_call(
        _mem_kv_body, grid=(b,),
        in_specs=[pl.BlockSpec((1, m, d), lambda i: (i, 0, 0)), _full_spec((1, d)), _full_spec(wkv.shape),
                  _full_spec((1, d // MEM_HEADS))],
        out_specs=(pl.BlockSpec((1, d, m), lambda i: (i, 0, 0)), pl.BlockSpec((1, m, d), lambda i: (i, 0, 0))),
        out_shape=(jax.ShapeDtypeStruct((b, d, m), _BF), jax.ShapeDtypeStruct((b, m, d), _BF)),
        compiler_params=_cparams("arbitrary"), name="mem_kv",
    )(mem, row(mn), wkv, row(mkn))


def _mem_body(x_ref, nm_ref, wmq_ref, wg2_ref, mqn_ref, mkT_ref, mv_ref, out_ref):
    d = x_ref.shape[2]
    dm = d // MEM_HEADS
    h = _rms(x_ref[0], nm_ref[...]).astype(_BF)
    mq = _dot(h, wmq_ref[...])
    outs = []
    for i in range(MEM_HEADS):
        qh = (_rms(mq[:, i * dm:(i + 1) * dm], mqn_ref[...]) * (float(dm) ** -0.5)).astype(_BF)
        s = _dot(qh, mkT_ref[0, i * dm:(i + 1) * dm, :])
        p = jnp.exp(s - jnp.max(s, axis=-1, keepdims=True))
        o = _dot(p.astype(_BF), mv_ref[0, :, i * dm:(i + 1) * dm])
        outs.append(o / jnp.sum(p, axis=-1, keepdims=True))
    out_ref[0] = (jnp.concatenate(outs, axis=1) * _sigmoid(_dot(h, wg2_ref[...]))).astype(_BF)


def _mem_branch(x, nm, wmq, wg2, mqn, mkT, mv, ts):
    b, s, d = x.shape
    m = mv.shape[1]
    row = lambda v: v.reshape(1, -1)
    return pl.pallas_call(
        _mem_body, grid=(b, s // ts),
        in_specs=[pl.BlockSpec((1, ts, d), lambda i, j: (i, j, 0)), _full_spec((1, d)), _full_spec(wmq.shape),
                  _full_spec(wg2.shape), _full_spec((1, d // MEM_HEADS)),
                  pl.BlockSpec((1, d, m), lambda i, j: (i, 0, 0)), pl.BlockSpec((1, m, d), lambda i, j: (i, 0, 0))],
        out_specs=pl.BlockSpec((1, ts, d), lambda i, j: (i, j, 0)),
        out_shape=jax.ShapeDtypeStruct((b, s, d), _BF),
        compiler_params=_cparams("arbitrary", "arbitrary"), name="mem",
    )(x, row(nm), wmq, wg2, row(mqn), mkT, mv)


def _mid_body(x_ref, a_ref, b_ref, c_ref, wo_ref, nf_ref, wpqT_ref, sk_ref, x1_ref, h2T_ref, scT_ref):
    merged = (a_ref[...].astype(_F32) + b_ref[...].astype(_F32) + c_ref[...].astype(_F32)).astype(_BF)
    x1 = x_ref[...] + _dot(merged, wo_ref[...])
    x1_ref[...] = x1
    h2T = _rms(x1, nf_ref[...]).T.astype(_BF)
    h2T_ref[...] = h2T
    pqT = _dot(wpqT_ref[...], h2T).astype(_BF)
    for c in range(sk_ref.shape[0]):
        scT_ref[c * PEER_KEYS:(c + 1) * PEER_KEYS, :] = _dot(sk_ref[c], pqT[c * LANES:(c + 1) * LANES, :])


def _mid(x2, ma, mb, mc, wo, nf, wpqT, sk, ts):
    t, d = x2.shape
    nsc = sk.shape[0] * PEER_KEYS
    tile = pl.BlockSpec((ts, d), lambda i: (i, 0))
    return pl.pallas_call(
        _mid_body, grid=(t // ts,),
        in_specs=[tile, tile, tile, tile, _full_spec(wo.shape), _full_spec((1, d)), _full_spec(wpqT.shape),
                  _full_spec(sk.shape)],
        out_specs=(tile, pl.BlockSpec((d, ts), lambda i: (0, i)), pl.BlockSpec((nsc, ts), lambda i: (0, i))),
        out_shape=(jax.ShapeDtypeStruct((t, d), _F32), jax.ShapeDtypeStruct((d, t), _BF),
                   jax.ShapeDtypeStruct((nsc, t), _F32)),
        compiler_params=_cparams("arbitrary"), name="mid",
    )(x2, ma, mb, mc, wo, nf.reshape(1, -1), wpqT, sk)


def _top_values(s, n):
    groups = [s[r * SUBLANES:(r + 1) * SUBLANES, :] for r in range(s.shape[0] // SUBLANES)]
    size = pl.next_power_of_2(len(groups))
    groups += [jnp.full_like(groups[0], -jnp.inf)] * (size - len(groups))
    k = 2
    while k <= size:
        j = k // 2
        while j >= 1:
            for i in range(size):
                l = i ^ j
                if l > i:
                    hi, lo = jnp.maximum(groups[i], groups[l]), jnp.minimum(groups[i], groups[l])
                    groups[i], groups[l] = (hi, lo) if (i & k) == 0 else (lo, hi)
            j //= 2
        k *= 2
    vals = []
    for it in range(n):
        m = jnp.max(groups[0], axis=0, keepdims=True)
        vals.append(m)
        remaining = n - 1 - it
        head = groups[0] == m
        for d in range(min(remaining, size)):
            below = groups[d + 1] if d + 1 < size else jnp.full_like(m, -jnp.inf)
            groups[d] = jnp.where(head, below, groups[d])
    return vals


def _route_body(scT_ref, tau_ref, c_ref, top2_ref):
    def head(h, carry):
        r1 = pl.multiple_of(h * 2 * PEER_KEYS, PEER_KEYS)
        r2 = pl.multiple_of(h * 2 * PEER_KEYS + PEER_KEYS, PEER_KEYS)
        n = PEER_TOPK + 1
        v1 = _top_values(scT_ref[pl.ds(r1, PEER_KEYS), :], n)
        v2 = _top_values(scT_ref[pl.ds(r2, PEER_KEYS), :], n)
        edge = SUBLANES
        assert 2 * (edge + 1) > n
        pad = [jnp.full_like(v1[0], -jnp.inf)] * (-n % SUBLANES)
        col1 = jnp.concatenate(v1 + pad, axis=0)
        col2 = jnp.concatenate(v2 + pad, axis=0)
        cand = jnp.concatenate([v1[a] + col2[:edge] for a in range(edge)]
                               + [v1[0] + col2[edge:], v2[0] + col1[edge:]], axis=0)
        best = _top_values(cand, n)
        tau = best[PEER_TOPK - 1]
        top = v1[0] + v2[0]
        z = jnp.sum(jnp.where(cand >= tau, jnp.exp(cand - top), 0.0), axis=0, keepdims=True)
        tau_ref[pl.ds(h, 1), :] = 0.5 * (tau + best[PEER_TOPK])
        c_ref[pl.ds(h, 1), :] = top + jnp.log(z)
        top2_ref[pl.ds(pl.multiple_of(h * TOP_ROWS, SUBLANES), TOP_ROWS), :] = col2
        return carry

    lax.fori_loop(0, PEER_HEADS, head, 0, unroll=2)


def _route(scT, te):
    nsc, t = scT.shape
    spec = pl.BlockSpec((PEER_HEADS, te), lambda i: (0, i))
    return pl.pallas_call(
        _route_body, grid=(t // te,),
        in_specs=[pl.BlockSpec((nsc, te), lambda i: (0, i))],
        out_specs=(spec, spec, pl.BlockSpec((PEER_HEADS * TOP_ROWS, te), lambda i: (0, i))),
        out_shape=(jax.ShapeDtypeStruct((PEER_HEADS, t), _F32),) * 2
        + (jax.ShapeDtypeStruct((PEER_HEADS * TOP_ROWS, t), _F32),),
        compiler_params=_cparams("arbitrary"), name="route",
    )(scT)


def _experts_body(h2T_ref, scT_ref, tau_ref, c_ref, top2_ref, u_ref, vT_ref, x1_ref, out_ref,
                  acc_ref, e2_ref, rank_ref, cnt_ref, scale_ref, act_ref, g_ref, rows_ref, *, rows_per_step,
                  n_blocks):
    e = pl.program_id(1)
    nk = PEER_KEYS
    ts = act_ref.shape[1]

    def prepare_tile():
        acc_ref[...] = jnp.zeros_like(acc_ref)
        for h in range(PEER_HEADS):
            best = [top2_ref[h * TOP_ROWS + b:h * TOP_ROWS + b + 1, :] for b in range(PEER_TOPK + 1)]
            s1 = scT_ref[2 * h * nk:(2 * h + 1) * nk, :]
            s2 = scT_ref[(2 * h + 1) * nk:(2 * h + 2) * nk, :]
            e2_ref[h * nk:(h + 1) * nk, :] = jnp.exp(s2 - best[0]).astype(_BF)
            rank = jnp.zeros((nk, ts), _F32)
            for b in range(PEER_TOPK):
                rank = jnp.where(best[b] > s2, float(b + 1), rank)
            rank_ref[h * nk:(h + 1) * nk, :] = rank.astype(_BF)
            thr = tau_ref[h:h + 1, :] - s1
            cnt = jnp.zeros((nk, ts), _F32)
            for b in range(PEER_TOPK + 1):
                cnt = jnp.where(best[b] >= thr, float(b + 1), cnt)
            cnt_ref[h * nk:(h + 1) * nk, :] = cnt
            scale_ref[h * nk:(h + 1) * nk, :] = 0.5 * jnp.exp(s1 + best[0] - c_ref[h:h + 1, :])

    n_rows = rows_per_step * PEER_HEADS
    first_group = 2
    zero = jnp.zeros((), _BF)

    def first_matmul(i):
        rows = slice(i * first_group * nk, (i + 1) * first_group * nk)
        act_ref[rows, :] = _dot(u_ref[rows, :], h2T_ref[...])

    second_group = 4
    n_groups = rows_per_step // first_group

    def block_step():
        for i in range(rows_per_step):
            for h in range(PEER_HEADS):
                src = h * nk + e * rows_per_step + i
                r = i * PEER_HEADS + h
                rows_ref[r:r + 1, :] = cnt_ref[pl.ds(src, 1), :]
                rows_ref[n_rows + r:n_rows + r + 1, :] = scale_ref[pl.ds(src, 1), :]
        for g in range(n_groups):
            first_matmul(g)
        for i in range(rows_per_step):
            wgt = jnp.zeros((nk, ts), _BF)
            for h in range(PEER_HEADS):
                r = i * PEER_HEADS + h
                cnt = rows_ref[r:r + 1, :].astype(_BF)
                scale = rows_ref[n_rows + r:n_rows + r + 1, :].astype(_BF)
                hit = jnp.where(rank_ref[h * nk:(h + 1) * nk, :] < cnt, e2_ref[h * nk:(h + 1) * nk, :], zero)
                wgt = wgt + hit * scale
            g_ref[i * nk:(i + 1) * nk, :] = wgt
        for i in range(rows_per_step):
            x = act_ref[i * nk:(i + 1) * nk, :].astype(_BF)
            t = jnp.tanh(x * (jnp.asarray(GELU_C0, _BF) + jnp.asarray(GELU_C1, _BF) * (x * x)))
            g_ref[i * nk:(i + 1) * nk, :] = (x + x * t) * g_ref[i * nk:(i + 1) * nk, :]
            if i % second_group == second_group - 1:
                cols = slice((i + 1 - second_group) * nk, (i + 1) * nk)
                acc_ref[...] += _dot(vT_ref[0, :, cols], g_ref[cols, :])

    pl.when(e == 0)(prepare_tile)
    block_step()

    @pl.when(e == n_blocks - 1)
    def _():
        out_ref[...] = x1_ref[...] + acc_ref[...].T


def _experts(h2T, scT, tau, cn, top2, u, vT, x1, ts, rows_per_step):
    d, t = h2T.shape
    nsc = scT.shape[0]
    ne = u.shape[0]
    eb = rows_per_step * PEER_KEYS
    nb = ne // eb
    nhk = PEER_HEADS * PEER_KEYS
    col = lambda n: pl.BlockSpec((n, ts), lambda i, e: (0, i))
    tile = pl.BlockSpec((ts, d), lambda i, e: (i, 0))
    return pl.pallas_call(
        functools.partial(_experts_body, rows_per_step=rows_per_step, n_blocks=nb),
        grid=(t // ts, nb),
        in_specs=[col(d), col(nsc), col(PEER_HEADS), col(PEER_HEADS), col(top2.shape[0]),
                  pl.BlockSpec((eb, d), lambda i, e: (e, 0)), pl.BlockSpec((1, d, eb), lambda i, e: (e, 0, 0)), tile],
        out_specs=tile,
        out_shape=jax.ShapeDtypeStruct((t, d), _F32),
        scratch_shapes=[pltpu.VMEM((d, ts), _F32), pltpu.VMEM((nhk, ts), _BF), pltpu.VMEM((nhk, ts), _BF),
                        pltpu.VMEM((nhk, ts), _F32), pltpu.VMEM((nhk, ts), _F32),
                        pltpu.VMEM((eb, ts), _F32), pltpu.VMEM((eb, ts), _BF),
                        pltpu.VMEM((2 * rows_per_step * PEER_HEADS, ts), _F32)],
        compiler_params=_cparams("arbitrary", "arbitrary"), name="experts",
    )(h2T, scT, tau, cn, top2, u, vT, x1)


def _block_diag(w):
    n, c, d = w.shape
    return jnp.einsum("ncd,nm->ncmd", w, jnp.eye(n, dtype=w.dtype)).reshape(n * c, n * d)


def _rope_tables(s, dim, reps):
    half = dim // 2
    freq = ROPE_THETA ** (-jnp.arange(half, dtype=_F32) / half)
    ang = jnp.arange(s, dtype=_F32)[:, None] * freq[None, :]
    cos = jnp.tile(jnp.concatenate([jnp.cos(ang), jnp.cos(ang)], axis=1), (1, reps))
    sin = jnp.tile(jnp.concatenate([-jnp.sin(ang), jnp.sin(ang)], axis=1), (1, reps))
    return cos, sin


def _tile(n, pref):
    return pref if n % pref == 0 else n


def _layer(x, mem, p, l):
    b, s, d = x.shape
    bf = lambda a: a.astype(_BF)
    w_in = p["w_in"][l]
    n_q, n_kv = ATT_HEADS * LANES, ATT_KV_HEADS * LANES
    n_qi = IDX_HEADS * IDX_DIM
    sizes = (d, d, n_q, n_kv, n_kv, n_qi, IDX_DIM, IDX_HEADS, d, 3 * d)
    offs = [0]
    for n in sizes:
        offs.append(offs[-1] + n)
    wlx, wlg, wq, wk, wv, wqi, wki, wwi, wmq, wgates = (w_in[:, offs[i]:offs[i + 1]] for i in range(len(sizes)))
    wki2 = jnp.concatenate([wki, wki], axis=1)
    wwi_p = jnp.pad(wwi, ((0, 0), (0, LANES - IDX_HEADS)))
    wg0, wg1, wg2 = wgates[:, :d], wgates[:, d:2 * d], wgates[:, 2 * d:]
    nm = p["norm_mix"][l]

    ts_a = _tile(s, 512)
    m_lru = _lru_branch(x, nm, bf(wlx), bf(wlg), bf(wg0), p["conv_w"][l], p["conv_b"][l],
                        bf(_block_diag(p["lru_wa"][l])), p["lru_ba"][l].reshape(-1),
                        bf(_block_diag(p["lru_wi"][l])), p["lru_bi"][l].reshape(-1), p["lru_lambda"][l], ts_a)

    tk = _tile(s, 512)
    c128, s128 = _rope_tables(s, LANES, 1)
    c64, s64 = _rope_tables(s, IDX_DIM, LANES // IDX_DIM)
    ikn2 = jnp.concatenate([p["idx_k_norm"][l]] * 2)
    qT, k, vT, qiT, ki, wT, g1 = _dsa_proj(x, nm, bf(wq), bf(wk), bf(wv), bf(wqi), bf(wki2), bf(wwi_p), bf(wg1),
                                           p["q_norm"][l], p["k_norm"][l], ikn2, c128, s128, c64, s64, tk)
    m_att = _dsa(qT, qiT, wT, k, vT, ki, g1, min(TOPK_MAX, s // 4), _tile(s, 256), tk)

    mkT, mv = _mem_kv(mem, p["mem_norm"][l], bf(p["w_mem_kv"][l]), p["mem_k_norm"][l])
    m_mem = _mem_branch(x, nm, bf(wmq), bf(wg2), p["mem_q_norm"][l], mkT, mv, _tile(s, 512))

    t = b * s
    flat = lambda a: a.reshape(t, d)
    sk = bf(p["peer_subkeys"][l].reshape(2 * PEER_HEADS, PEER_KEYS, -1))
    x1, h2T, scT = _mid(flat(x), flat(m_lru), flat(m_att), flat(m_mem), bf(p["w_out"][l]), p["norm_ffn"][l],
                        bf(p["peer_wq"][l].T), sk, _tile(t, 512))
    tau, cn, top2 = _route(scT, _tile(t, 256))
    rows_per_step = 16
    vT = bf(p["peer_v"][l]).reshape(-1, rows_per_step * PEER_KEYS, d).transpose(0, 2, 1)
    out = _experts(h2T, scT, tau, cn, top2, bf(p["peer_u"][l]), vT, x1, _tile(t, 512), rows_per_step)
    return out.reshape(b, s, d)


def kernel(x, mem, norm_mix, w_in, conv_w, conv_b, lru_wa, lru_ba, lru_wi, lru_bi, lru_lambda, q_norm, k_norm,
           idx_k_norm, mem_norm, w_mem_kv, mem_q_norm, mem_k_norm, w_out, norm_ffn, peer_wq, peer_subkeys,
           peer_u, peer_v):
    p = dict(norm_mix=norm_mix, w_in=w_in, conv_w=conv_w, conv_b=conv_b, lru_wa=lru_wa, lru_ba=lru_ba,
             lru_wi=lru_wi, lru_bi=lru_bi, lru_lambda=lru_lambda, q_norm=q_norm, k_norm=k_norm,
             idx_k_norm=idx_k_norm, mem_norm=mem_norm, w_mem_kv=w_mem_kv, mem_q_norm=mem_q_norm,
             mem_k_norm=mem_k_norm, w_out=w_out, norm_ffn=norm_ffn, peer_wq=peer_wq, peer_subkeys=peer_subkeys,
             peer_u=peer_u, peer_v=peer_v)
    for l in range(norm_mix.shape[0]):
        x = _layer(x, mem, p, l)
    return x
```

```python
import functools

import jax
import jax.numpy as jnp
from jax import lax
from jax.experimental import pallas as pl
from jax.experimental.pallas import tpu as pltpu

_F32 = jnp.float32
_BF = jnp.bfloat16

EPS = 1e-6
ROPE_THETA = 10000.0
LRU_C = 8.0
CONV_WIDTH = 4
ATT_HEADS = 8
ATT_KV_HEADS = 2
ATT_GROUP = ATT_HEADS // ATT_KV_HEADS
IDX_HEADS = 8
IDX_DIM = 64
TOPK_MAX = 256
MEM_HEADS = 4
PEER_HEADS = 8
PEER_KEYS = 128
PEER_TOPK = 16
TOP_ROWS = 24

LANES = 128
SUBLANES = 8
ONES_ROWS = 2 * SUBLANES
VMEM_LIMIT_BYTES = 56 * 1024 * 1024
MASK_VALUE = -1e30
FLT_MAX = 3.4028234663852886e38
INT_MIN = -(2 ** 31)


def _cparams(*sem):
    return pltpu.CompilerParams(dimension_semantics=sem, vmem_limit_bytes=VMEM_LIMIT_BYTES)


def _rms(x, g):
    return x * lax.rsqrt(jnp.mean(x * x, axis=-1, keepdims=True) + EPS) * g


GELU_C0 = 0.7978845608028654
GELU_C1 = GELU_C0 * 0.044715


def _gelu(x):
    return 0.5 * x * (1.0 + jnp.tanh(x * (GELU_C0 + GELU_C1 * (x * x))))


def _sigmoid(x):
    return 0.5 * jnp.tanh(0.5 * x) + 0.5


def _dot(a, b):
    return jnp.dot(a, b, preferred_element_type=_F32)


def _full_spec(shape):
    n = len(shape)
    return pl.BlockSpec(shape, lambda *_: (0,) * n)


def _lru_body(x_ref, nm_ref, wlx_ref, wlg_ref, wg0_ref, cw_ref, cb_ref, wa_ref, ba_ref, wi_ref,
              bi_ref, lam_ref, out_ref, buf_ref, hc_ref, a_ref, b_ref):
    ts = x_ref.shape[1]
    c_dim = out_ref.shape[2]

    @pl.when(pl.program_id(1) == 0)
    def _():
        buf_ref[0:SUBLANES, :] = jnp.zeros((SUBLANES, c_dim), _F32)
        hc_ref[...] = jnp.zeros_like(hc_ref)

    h = _rms(x_ref[0], nm_ref[...]).astype(_BF)
    lx = _dot(h, wlx_ref[...])
    buf_ref[SUBLANES:SUBLANES + ts, :] = lx
    ext = buf_ref[...]
    cw = cw_ref[...]
    xc = cb_ref[...] + cw[3:4] * lx
    for j in range(CONV_WIDTH - 1):
        d = CONV_WIDTH - 1 - j
        xc = xc + cw[j:j + 1] * pltpu.roll(ext, d, 0)[SUBLANES:, :]
    buf_ref[0:SUBLANES, :] = lx[ts - SUBLANES:, :]

    xcb = xc.astype(_BF)
    r = _sigmoid(_dot(xcb, wa_ref[...]) + ba_ref[...])
    ig = _sigmoid(_dot(xcb, wi_ref[...]) + bi_ref[...])
    lam = lam_ref[...]
    softplus_neg_lam = jnp.maximum(-lam, 0.0) + jnp.log1p(jnp.exp(-jnp.abs(lam)))
    log_a = (-LRU_C) * r * softplus_neg_lam
    a = jnp.exp(log_a)
    y = jnp.tanh(-log_a) * (a * a + 1.0)
    u = jnp.where(y > 0.0, y * lax.rsqrt(y), 0.0) * (ig * xc)

    a = a.reshape(ts // SUBLANES, SUBLANES, c_dim)
    u = u.reshape(ts // SUBLANES, SUBLANES, c_dim)
    rid = lax.broadcasted_iota(jnp.int32, a.shape, 1)
    for d in (1, 2, 4):
        ok = rid >= d
        a_s = pltpu.roll(a, d, 1)
        u_s = pltpu.roll(u, d, 1)
        u = jnp.where(ok, a * u_s + u, u)
        a = jnp.where(ok, a * a_s, a)
    a_ref[...] = a.reshape(ts, c_dim)
    b_ref[...] = u.reshape(ts, c_dim)

    def step(g, hp):
        r0 = pl.multiple_of(g * SUBLANES, SUBLANES)
        hcur = a_ref[pl.ds(r0, SUBLANES), :] * hp + b_ref[pl.ds(r0, SUBLANES), :]
        b_ref[pl.ds(r0, SUBLANES), :] = hcur
        return jnp.broadcast_to(hcur[SUBLANES - 1:SUBLANES, :], (SUBLANES, c_dim))

    hc_ref[...] = lax.fori_loop(0, ts // SUBLANES, step, hc_ref[...])
    gate = _gelu(_dot(h, wlg_ref[...])) * _sigmoid(_dot(h, wg0_ref[...]))
    out_ref[0] = (b_ref[...] * gate).astype(_BF)


def _lru_branch(x, nm, wlx, wlg, wg0, cw, cb, wa, ba, wi, bi, lam, ts):
    b, s, d = x.shape
    c = wlx.shape[1]
    row = lambda v: v.reshape(1, -1)
    args = (x, row(nm), wlx, wlg, wg0, cw, row(cb), wa, row(ba), wi, row(bi), row(lam))
    in_specs = [pl.BlockSpec((1, ts, d), lambda i, j: (i, j, 0))]
    in_specs += [_full_spec(a.shape) for a in args[1:]]
    return pl.pallas_call(
        _lru_body,
        grid=(b, s // ts),
        in_specs=in_specs,
        out_specs=pl.BlockSpec((1, ts, c), lambda i, j: (i, j, 0)),
        out_shape=jax.ShapeDtypeStruct((b, s, c), _BF),
        scratch_shapes=[pltpu.VMEM((ts + SUBLANES, c), _F32), pltpu.VMEM((SUBLANES, c), _F32),
                        pltpu.VMEM((ts, c), _F32), pltpu.VMEM((ts, c), _F32)],
        compiler_params=_cparams("arbitrary", "arbitrary"),
        name="lru",
    )(*args)


def _rope128(x, cos, sin):
    return x * cos + pltpu.roll(x, LANES // 2, 1) * sin


def _rope64(x, cos, sin, first_half):
    rot = jnp.where(first_half, pltpu.roll(x, LANES - IDX_DIM // 2, 1), pltpu.roll(x, IDX_DIM // 2, 1))
    return x * cos + rot * sin


def _dsa_proj_body(x_ref, nm_ref, wq_ref, wk_ref, wv_ref, wqi_ref, wki_ref, wwi_ref, wg1_ref,
                   qn_ref, kn_ref, ikn_ref, c128_ref, s128_ref, c64_ref, s64_ref,
                   qT_ref, k_ref, vT_ref, qiT_ref, ki_ref, wT_ref, g1_ref):
    ts = x_ref.shape[1]
    h = _rms(x_ref[0], nm_ref[...]).astype(_BF)
    c128, s128 = c128_ref[...], s128_ref[...]
    c64, s64 = c64_ref[...], s64_ref[...]
    first_half = (lax.broadcasted_iota(jnp.int32, (ts, LANES), 1) & (IDX_DIM - 1)) < IDX_DIM // 2

    def head_norm_rope(t, g, scale):
        outs = []
        for i in range(t.shape[1] // LANES):
            th = _rms(t[:, i * LANES:(i + 1) * LANES], g)
            outs.append(_rope128(th, c128, s128) * scale)
        return jnp.concatenate(outs, axis=1)

    q = head_norm_rope(_dot(h, wq_ref[...]), qn_ref[...], float(LANES) ** -0.5)
    qT_ref[0] = q.T.astype(_BF)
    k_ref[0] = head_norm_rope(_dot(h, wk_ref[...]), kn_ref[...], 1.0).astype(_BF)
    vT = _dot(h, wv_ref[...]).T.astype(_BF)
    for g in range(ATT_KV_HEADS):
        vT_ref[0, 0, g, :LANES, :] = vT[g * LANES:(g + 1) * LANES, :]
        vT_ref[0, 0, g, LANES:, :] = jnp.ones((ONES_ROWS, ts), _BF)

    qi = _dot(h, wqi_ref[...])
    qi = jnp.concatenate(
        [_rope64(qi[:, i * LANES:(i + 1) * LANES], c64, s64, first_half) for i in range(qi.shape[1] // LANES)],
        axis=1) * (float(IDX_DIM) ** -0.5)
    qiT_ref[0] = qi.T.astype(_BF)
    ki = _rope64(_rms(_dot(h, wki_ref[...]), ikn_ref[...]), c64, s64, first_half)
    ki_ref[0] = ki[:, :IDX_DIM].astype(_BF)
    wi = _dot(h, wwi_ref[...]) * (float(IDX_HEADS) ** -0.5)
    wT_ref[0] = wi.T[:IDX_HEADS, :]
    g1_ref[0] = _sigmoid(_dot(h, wg1_ref[...])).astype(_BF)


def _dsa_proj(x, nm, wq, wk, wv, wqi, wki2, wwi, wg1, qn, kn, ikn2, c128, s128, c64, s64, ts):
    b, s, d = x.shape
    row = lambda v: v.reshape(1, -1)
    nq, nk, nqi = wq.shape[1], wk.shape[1], wqi.shape[1]
    args = (x, row(nm), wq, wk, wv, wqi, wki2, wwi, wg1, row(qn), row(kn), row(ikn2), c128, s128, c64, s64)
    tile = lambda n: pl.BlockSpec((1, ts, n), lambda i, j: (i, j, 0))
    tile_t = lambda n: pl.BlockSpec((1, n, ts), lambda i, j: (i, 0, j))
    tab = pl.BlockSpec((ts, LANES), lambda i, j: (j, 0))
    in_specs = [tile(d)] + [_full_spec(a.shape) for a in args[1:12]] + [tab] * 4
    out_shape = (
        jax.ShapeDtypeStruct((b, nq, s), _BF),
        jax.ShapeDtypeStruct((b, s, nk), _BF),
        jax.ShapeDtypeStruct((b, s // ts, ATT_KV_HEADS, LANES + ONES_ROWS, ts), _BF),
        jax.ShapeDtypeStruct((b, nqi, s), _BF),
        jax.ShapeDtypeStruct((b, s, IDX_DIM), _BF),
        jax.ShapeDtypeStruct((b, IDX_HEADS, s), _F32),
        jax.ShapeDtypeStruct((b, s, d), _BF),
    )
    out_specs = (tile_t(nq), tile(nk),
                 pl.BlockSpec((1, 1, ATT_KV_HEADS, LANES + ONES_ROWS, ts), lambda i, j: (i, j, 0, 0, 0)), tile_t(nqi),
                 tile(IDX_DIM), tile_t(IDX_HEADS), tile(d))
    return pl.pallas_call(
        _dsa_proj_body, grid=(b, s // ts), in_specs=in_specs, out_specs=out_specs, out_shape=out_shape,
        compiler_params=_cparams("arbitrary", "arbitrary"), name="dsa_proj",
    )(*args)


def _key_to_float(u):
    key = u ^ jnp.int32(INT_MIN)
    bits = key ^ ((key >> 31) & jnp.int32(0x7FFFFFFF))
    return jnp.where((u >> 23) == 0, -jnp.inf, lax.bitcast_convert_type(bits, _F32))


def _high_half(x):
    bits = lax.bitcast_convert_type(x, jnp.int32) & jnp.int32(-65536)
    return lax.bitcast_convert_type(bits, _F32).astype(_BF)


def _dsa_body(qT_ref, qiT_ref, wT_ref, k_ref, vT_ref, ki_ref, g1_ref, out_ref,
              sc_ref, sh_ref, m_ref, l_ref, acc_ref, bias_ref, s_ref, p_ref, *, topk, tk):
    tq = qT_ref.shape[2]
    dh = LANES
    q0 = pl.program_id(1) * tq
    n_chunks = (q0 + tq + tk - 1) // tk

    qicat = jnp.concatenate([qiT_ref[0, h * IDX_DIM:(h + 1) * IDX_DIM, :] for h in range(IDX_HEADS)], axis=1)
    w = wT_ref[0]
    qpos = q0 + lax.broadcasted_iota(jnp.int32, (tk, tq), 1)
    krow = lax.broadcasted_iota(jnp.int32, (tk, tq), 0)

    def score_chunk(c, carry):
        r0 = pl.multiple_of(c * tk, tk)
        lg = _dot(ki_ref[0, pl.ds(r0, tk), :], qicat)
        s = jnp.zeros((tk, tq), _F32)
        for h in range(IDX_HEADS):
            s = s + jnp.maximum(lg[:, h * tq:(h + 1) * tq], 0.0) * w[h:h + 1, :]
        s = jnp.where(krow + r0 <= qpos, s, -jnp.inf)
        sc_ref[pl.ds(r0, tk), :] = s
        sh_ref[pl.ds(r0, tk), :] = _high_half(s)
        return carry

    lax.fori_loop(0, n_chunks, score_chunk, 0)

    def make_count(nc, ref, rows, dtype):
        def count_ge(cand):
            one, zero = jnp.ones((), dtype), jnp.zeros((), dtype)
            parts = [jnp.zeros((rows, tq), dtype) for _ in range(4)]
            n = 0
            for c in range(nc):
                hit = jnp.where(ref[c * tk:(c + 1) * tk, :] >= cand, one, zero)
                for r in range(tk // rows):
                    parts[n % 4] = parts[n % 4] + hit[r * rows:(r + 1) * rows, :]
                    n += 1
            parts = [p.astype(_F32) for p in parts]
            return jnp.sum((parts[0] + parts[1]) + (parts[2] + parts[3]), axis=0, keepdims=True)
        return count_ge

    max_chunks = sc_ref.shape[0] // tk
    assert max_chunks * tk // (2 * SUBLANES) <= 4 * 64
    coarse = [make_count(nc, sh_ref, 2 * SUBLANES, _BF) for nc in range(1, max_chunks + 1)]
    fine = [make_count(nc, sc_ref, SUBLANES, _F32) for nc in range(1, max_chunks + 1)]

    def bisect_high(i, carry):
        u, n_ge = carry
        trial = u | lax.shift_left(jnp.int32(1), 31 - i)
        cnt = lax.switch(n_chunks - 1, coarse, _high_half(_key_to_float(trial)))
        keep = cnt >= float(topk)
        return jnp.where(keep, trial, u), jnp.where(keep, cnt, n_ge)

    def bisect_low(i, carry):
        u, n_ge = carry
        trial = u | lax.shift_left(jnp.int32(1), 31 - i)
        cnt = lax.switch(n_chunks - 1, fine, _key_to_float(trial))
        keep = cnt >= float(topk)
        return jnp.where(keep, trial, u), jnp.where(keep, cnt, n_ge)

    carry = lax.fori_loop(0, 16, bisect_high, (jnp.zeros((1, tq), jnp.int32), jnp.zeros((1, tq), _F32)))
    u, n_ge = lax.fori_loop(16, 32, bisect_low, carry)
    tau = jnp.maximum(_key_to_float(u), -FLT_MAX)

    @pl.when(jnp.max(jnp.where(tau > -FLT_MAX, n_ge, 0.0)) > float(topk))
    def _():
        def count_gt(c, cnt):
            r0 = pl.multiple_of(c * tk, tk)
            return cnt + jnp.sum(jnp.where(sc_ref[pl.ds(r0, tk), :] > tau, 1.0, 0.0), axis=0, keepdims=True)

        need = float(topk) - lax.fori_loop(0, n_chunks, count_gt, jnp.zeros((1, tq), _F32))
        before = (lax.broadcasted_iota(jnp.int32, (tk, tk), 1)
                  < lax.broadcasted_iota(jnp.int32, (tk, tk), 0)).astype(_BF)

        def strike(c, seen):
            r0 = pl.multiple_of(c * tk, tk)
            blk = sc_ref[pl.ds(r0, tk), :]
            tied = blk == tau
            tied_b = jnp.where(tied, 1.0, 0.0).astype(_BF)
            earlier = _dot(before, tied_b) + seen
            sc_ref[pl.ds(r0, tk), :] = jnp.where(tied & (earlier >= need), -jnp.inf, blk)
            return seen + jnp.sum(tied_b.astype(_F32), axis=0, keepdims=True)

        lax.fori_loop(0, n_chunks, strike, jnp.zeros((1, tq), _F32))

    m_ref[...] = jnp.full(m_ref.shape, MASK_VALUE, _F32)
    l_ref[...] = jnp.zeros_like(l_ref)
    acc_ref[...] = jnp.zeros_like(acc_ref)

    sub = 8 * SUBLANES
    groups = [slice(r * SUBLANES, (r + 1) * SUBLANES) for r in range(sub // SUBLANES)]

    last_chunk = sc_ref.shape[0] // tk - 1
    ahead = 3

    def logits(c, h):
        r0 = pl.multiple_of(c * tk, tk)
        g = h // ATT_GROUP
        s_ref[h] = _dot(k_ref[0, pl.ds(r0, tk), g * dh:(g + 1) * dh], qT_ref[0, h * dh:(h + 1) * dh, :])

    for h in range(ahead):
        logits(0, h)

    def att_chunk(c, carry):
        r0 = pl.multiple_of(c * tk, tk)
        bias_ref[...] = jnp.where(sc_ref[pl.ds(r0, tk), :] >= tau, 0.0, MASK_VALUE)
        for h in range(ATT_HEADS):
            if h + ahead < ATT_HEADS:
                logits(c, h + ahead)
            else:
                logits(jnp.minimum(c + 1, last_chunk), h + ahead - ATT_HEADS)
            m_old = m_ref[h:h + 1, :]
            mx = jnp.broadcast_to(m_old, (SUBLANES, tq))
            for i in range(tk // sub):
                rows = slice(i * sub, (i + 1) * sub)
                sb = s_ref[h, rows, :] + bias_ref[rows, :]
                s_ref[h, rows, :] = sb
                for gr in groups:
                    mx = jnp.maximum(mx, sb[gr, :])
            m_new = jnp.max(mx, axis=0, keepdims=True)
            for i in range(tk // sub):
                rows = slice(i * sub, (i + 1) * sub)
                p_ref[h, rows, :] = jnp.exp((s_ref[h, rows, :] - m_new).astype(_BF))
            alpha = jnp.exp(m_old - m_new)
            pv = _dot(vT_ref[0, c, h // ATT_GROUP], p_ref[h])
            l_ref[h:h + 1, :] = alpha * l_ref[h:h + 1, :] + pv[dh:dh + 1, :]
            acc_ref[h * dh:(h + 1) * dh, :] = alpha * acc_ref[h * dh:(h + 1) * dh, :] + pv[:dh, :]
            m_ref[h:h + 1, :] = m_new
        return carry

    lax.fori_loop(0, n_chunks, att_chunk, 0)
    outs = []
    for h in range(ATT_HEADS):
        o = acc_ref[h * dh:(h + 1) * dh, :] / l_ref[h:h + 1, :]
        outs.append(o.T)
    out_ref[0] = (jnp.concatenate(outs, axis=1) * g1_ref[0].astype(_F32)).astype(_BF)


def _dsa(qT, qiT, wT, k, vT, ki, g1, topk, tq, tk):
    b, nq, s = qT.shape
    d = g1.shape[2]
    in_specs = [
        pl.BlockSpec((1, nq, tq), lambda i, j: (i, 0, j)),
        pl.BlockSpec((1, qiT.shape[1], tq), lambda i, j: (i, 0, j)),
        pl.BlockSpec((1, IDX_HEADS, tq), lambda i, j: (i, 0, j)),
        pl.BlockSpec((1, s, k.shape[2]), lambda i, j: (i, 0, 0)),
        pl.BlockSpec((1,) + vT.shape[1:], lambda i, j: (i, 0, 0, 0, 0)),
        pl.BlockSpec((1, s, IDX_DIM), lambda i, j: (i, 0, 0)),
        pl.BlockSpec((1, tq, d), lambda i, j: (i, j, 0)),
    ]
    return pl.pallas_call(
        functools.partial(_dsa_body, topk=topk, tk=tk),
        grid=(b, s // tq), in_specs=in_specs,
        out_specs=pl.BlockSpec((1, tq, d), lambda i, j: (i, j, 0)),
        out_shape=jax.ShapeDtypeStruct((b, s, d), _BF),
        scratch_shapes=[pltpu.VMEM((s, tq), _F32), pltpu.VMEM((s, tq), _BF), pltpu.VMEM((ATT_HEADS, tq), _F32),
                        pltpu.VMEM((ATT_HEADS, tq), _F32), pltpu.VMEM((nq, tq), _F32),
                        pltpu.VMEM((tk, tq), _F32), pltpu.VMEM((ATT_HEADS, tk, tq), _F32),
                        pltpu.VMEM((ATT_HEADS, tk, tq), _BF)],
        compiler_params=_cparams("arbitrary", "arbitrary"), name="dsa",
    )(qT, qiT, wT, k, vT, ki, g1)


def _mem_kv_body(mem_ref, mn_ref, wkv_ref, mkn_ref, mkT_ref, mv_ref):
    d = mem_ref.shape[2]
    dm = d // MEM_HEADS
    m = _rms(mem_ref[0], mn_ref[...]).astype(_BF)
    kv = _dot(m, wkv_ref[...])
    mk = jnp.concatenate([_rms(kv[:, i * dm:(i + 1) * dm], mkn_ref[...]) for i in range(MEM_HEADS)], axis=1)
    mkT_ref[0] = mk.T.astype(_BF)
    mv_ref[0] = kv[:, d:].astype(_BF)


def _mem_kv(mem, mn, wkv, mkn):
    b, m, d = mem.shape
    row = lambda v: v.reshape(1, -1)
    return pl.pallas_call(
        _mem_kv_body, grid=(b,),
        in_specs=[pl.BlockSpec((1, m, d), lambda i: (i, 0, 0)), _full_spec((1, d)), _full_spec(wkv.shape),
                  _full_spec((1, d // MEM_HEADS))],
        out_specs=(pl.BlockSpec((1, d, m), lambda i: (i, 0, 0)), pl.BlockSpec((1, m, d), lambda i: (i, 0, 0))),
        out_shape=(jax.ShapeDtypeStruct((b, d, m), _BF), jax.ShapeDtypeStruct((b, m, d), _BF)),
        compiler_params=_cparams("arbitrary"), name="mem_kv",
    )(mem, row(mn), wkv, row(mkn))


def _mem_body(x_ref, nm_ref, wmq_ref, wg2_ref, mqn_ref, mkT_ref, mv_ref, out_ref):
    d = x_ref.shape[2]
    dm = d // MEM_HEADS
    h = _rms(x_ref[0], nm_ref[...]).astype(_BF)
    mq = _dot(h, wmq_ref[...])
    outs = []
    for i in range(MEM_HEADS):
        qh = (_rms(mq[:, i * dm:(i + 1) * dm], mqn_ref[...]) * (float(dm) ** -0.5)).astype(_BF)
        s = _dot(qh, mkT_ref[0, i * dm:(i + 1) * dm, :])
        p = jnp.exp(s - jnp.max(s, axis=-1, keepdims=True))
        o = _dot(p.astype(_BF), mv_ref[0, :, i * dm:(i + 1) * dm])
        outs.append(o / jnp.sum(p, axis=-1, keepdims=True))
    out_ref[0] = (jnp.concatenate(outs, axis=1) * _sigmoid(_dot(h, wg2_ref[...]))).astype(_BF)


def _mem_branch(x, nm, wmq, wg2, mqn, mkT, mv, ts):
    b, s, d = x.shape
    m = mv.shape[1]
    row = lambda v: v.reshape(1, -1)
    return pl.pallas_call(
        _mem_body, grid=(b, s // ts),
        in_specs=[pl.BlockSpec((1, ts, d), lambda i, j: (i, j, 0)), _full_spec((1, d)), _full_spec(wmq.shape),
                  _full_spec(wg2.shape), _full_spec((1, d // MEM_HEADS)),
                  pl.BlockSpec((1, d, m), lambda i, j: (i, 0, 0)), pl.BlockSpec((1, m, d), lambda i, j: (i, 0, 0))],
        out_specs=pl.BlockSpec((1, ts, d), lambda i, j: (i, j, 0)),
        out_shape=jax.ShapeDtypeStruct((b, s, d), _BF),
        compiler_params=_cparams("arbitrary", "arbitrary"), name="mem",
    )(x, row(nm), wmq, wg2, row(mqn), mkT, mv)


def _mid_body(x_ref, a_ref, b_ref, c_ref, wo_ref, nf_ref, wpqT_ref, sk_ref, x1_ref, h2T_ref, scT_ref):
    merged = (a_ref[...].astype(_F32) + b_ref[...].astype(_F32) + c_ref[...].astype(_F32)).astype(_BF)
    x1 = x_ref[...] + _dot(merged, wo_ref[...])
    x1_ref[...] = x1
    h2T = _rms(x1, nf_ref[...]).T.astype(_BF)
    h2T_ref[...] = h2T
    pqT = _dot(wpqT_ref[...], h2T).astype(_BF)
    for c in range(sk_ref.shape[0]):
        scT_ref[c * PEER_KEYS:(c + 1) * PEER_KEYS, :] = _dot(sk_ref[c], pqT[c * LANES:(c + 1) * LANES, :])


def _mid(x2, ma, mb, mc, wo, nf, wpqT, sk, ts):
    t, d = x2.shape
    nsc = sk.shape[0] * PEER_KEYS
    tile = pl.BlockSpec((ts, d), lambda i: (i, 0))
    return pl.pallas_call(
        _mid_body, grid=(t // ts,),
        in_specs=[tile, tile, tile, tile, _full_spec(wo.shape), _full_spec((1, d)), _full_spec(wpqT.shape),
                  _full_spec(sk.shape)],
        out_specs=(tile, pl.BlockSpec((d, ts), lambda i: (0, i)), pl.BlockSpec((nsc, ts), lambda i: (0, i))),
        out_shape=(jax.ShapeDtypeStruct((t, d), _F32), jax.ShapeDtypeStruct((d, t), _BF),
                   jax.ShapeDtypeStruct((nsc, t), _F32)),
        compiler_params=_cparams("arbitrary"), name="mid",
    )(x2, ma, mb, mc, wo, nf.reshape(1, -1), wpqT, sk)


def _top_values(s, n):
    groups = [s[r * SUBLANES:(r + 1) * SUBLANES, :] for r in range(s.shape[0] // SUBLANES)]
    size = pl.next_power_of_2(len(groups))
    groups += [jnp.full_like(groups[0], -jnp.inf)] * (size - len(groups))
    k = 2
    while k <= size:
        j = k // 2
        while j >= 1:
            for i in range(size):
                l = i ^ j
                if l > i:
                    hi, lo = jnp.maximum(groups[i], groups[l]), jnp.minimum(groups[i], groups[l])
                    groups[i], groups[l] = (hi, lo) if (i & k) == 0 else (lo, hi)
            j //= 2
        k *= 2
    vals = []
    for it in range(n):
        m = jnp.max(groups[0], axis=0, keepdims=True)
        vals.append(m)
        remaining = n - 1 - it
        head = groups[0] == m
        for d in range(min(remaining, size)):
            below = groups[d + 1] if d + 1 < size else jnp.full_like(m, -jnp.inf)
            groups[d] = jnp.where(head, below, groups[d])
    return vals


def _route_body(scT_ref, tau_ref, c_ref, top2_ref):
    def head(h, carry):
        r1 = pl.multiple_of(h * 2 * PEER_KEYS, PEER_KEYS)
        r2 = pl.multiple_of(h * 2 * PEER_KEYS + PEER_KEYS, PEER_KEYS)
        n = PEER_TOPK + 1
        v1 = _top_values(scT_ref[pl.ds(r1, PEER_KEYS), :], n)
        v2 = _top_values(scT_ref[pl.ds(r2, PEER_KEYS), :], n)
        edge = SUBLANES
        assert 2 * (edge + 1) > n
        pad = [jnp.full_like(v1[0], -jnp.inf)] * (-n % SUBLANES)
        col1 = jnp.concatenate(v1 + pad, axis=0)
        col2 = jnp.concatenate(v2 + pad, axis=0)
        cand = jnp.concatenate([v1[a] + col2[:edge] for a in range(edge)]
                               + [v1[0] + col2[edge:], v2[0] + col1[edge:]], axis=0)
        best = _top_values(cand, n)
        tau = best[PEER_TOPK - 1]
        top = v1[0] + v2[0]
        z = jnp.sum(jnp.where(cand >= tau, jnp.exp(cand - top), 0.0), axis=0, keepdims=True)
        tau_ref[pl.ds(h, 1), :] = 0.5 * (tau + best[PEER_TOPK])
        c_ref[pl.ds(h, 1), :] = top + jnp.log(z)
        top2_ref[pl.ds(pl.multiple_of(h * TOP_ROWS, SUBLANES), TOP_ROWS), :] = col2
        return carry

    lax.fori_loop(0, PEER_HEADS, head, 0, unroll=2)


def _route(scT, te):
    nsc, t = scT.shape
    spec = pl.BlockSpec((PEER_HEADS, te), lambda i: (0, i))
    return pl.pallas_call(
        _route_body, grid=(t // te,),
        in_specs=[pl.BlockSpec((nsc, te), lambda i: (0, i))],
        out_specs=(spec, spec, pl.BlockSpec((PEER_HEADS * TOP_ROWS, te), lambda i: (0, i))),
        out_shape=(jax.ShapeDtypeStruct((PEER_HEADS, t), _F32),) * 2
        + (jax.ShapeDtypeStruct((PEER_HEADS * TOP_ROWS, t), _F32),),
        compiler_params=_cparams("arbitrary"), name="route",
    )(scT)


def _experts_body(h2T_ref, scT_ref, tau_ref, c_ref, top2_ref, u_ref, vT_ref, x1_ref, out_ref,
                  acc_ref, e2_ref, rank_ref, cnt_ref, scale_ref, act_ref, g_ref, rows_ref, *, rows_per_step,
                  n_blocks):
    e = pl.program_id(1)
    nk = PEER_KEYS
    ts = act_ref.shape[1]

    def prepare_tile():
        acc_ref[...] = jnp.zeros_like(acc_ref)
        for h in range(PEER_HEADS):
            best = [top2_ref[h * TOP_ROWS + b:h * TOP_ROWS + b + 1, :] for b in range(PEER_TOPK + 1)]
            s1 = scT_ref[2 * h * nk:(2 * h + 1) * nk, :]
            s2 = scT_ref[(2 * h + 1) * nk:(2 * h + 2) * nk, :]
            e2_ref[h * nk:(h + 1) * nk, :] = jnp.exp(s2 - best[0]).astype(_BF)
            rank = jnp.zeros((nk, ts), _F32)
            for b in range(PEER_TOPK):
                rank = jnp.where(best[b] > s2, float(b + 1), rank)
            rank_ref[h * nk:(h + 1) * nk, :] = rank.astype(_BF)
            thr = tau_ref[h:h + 1, :] - s1
            cnt = jnp.zeros((nk, ts), _F32)
            for b in range(PEER_TOPK + 1):
                cnt = jnp.where(best[b] >= thr, float(b + 1), cnt)
            cnt_ref[h * nk:(h + 1) * nk, :] = cnt
            scale_ref[h * nk:(h + 1) * nk, :] = 0.5 * jnp.exp(s1 + best[0] - c_ref[h:h + 1, :])

    n_rows = rows_per_step * PEER_HEADS
    first_group = 2
    zero = jnp.zeros((), _BF)

    def first_matmul(i):
        rows = slice(i * first_group * nk, (i + 1) * first_group * nk)
        act_ref[rows, :] = _dot(u_ref[rows, :], h2T_ref[...])

    second_group = 4
    n_groups = rows_per_step // first_group

    def block_step():
        for i in range(rows_per_step):
            for h in range(PEER_HEADS):
                src = h * nk + e * rows_per_step + i
                r = i * PEER_HEADS + h
                rows_ref[r:r + 1, :] = cnt_ref[pl.ds(src, 1), :]
                rows_ref[n_rows + r:n_rows + r + 1, :] = scale_ref[pl.ds(src, 1), :]
        for g in range(n_groups):
            first_matmul(g)
        for i in range(rows_per_step):
            wgt = jnp.zeros((nk, ts), _BF)
            for h in range(PEER_HEADS):
                r = i * PEER_HEADS + h
                cnt = rows_ref[r:r + 1, :].astype(_BF)
                scale = rows_ref[n_rows + r:n_rows + r + 1, :].astype(_BF)
                hit = jnp.where(rank_ref[h * nk:(h + 1) * nk, :] < cnt, e2_ref[h * nk:(h + 1) * nk, :], zero)
                wgt = wgt + hit * scale
            g_ref[i * nk:(i + 1) * nk, :] = wgt
        for i in range(rows_per_step):
            x = act_ref[i * nk:(i + 1) * nk, :].astype(_BF)
            t = jnp.tanh(x * (jnp.asarray(GELU_C0, _BF) + jnp.asarray(GELU_C1, _BF) * (x * x)))
            g_ref[i * nk:(i + 1) * nk, :] = (x + x * t) * g_ref[i * nk:(i + 1) * nk, :]
            if i % second_group == second_group - 1:
                cols = slice((i + 1 - second_group) * nk, (i + 1) * nk)
                acc_ref[...] += _dot(vT_ref[0, :, cols], g_ref[cols, :])

    pl.when(e == 0)(prepare_tile)
    block_step()

    @pl.when(e == n_blocks - 1)
    def _():
        out_ref[...] = x1_ref[...] + acc_ref[...].T


def _experts(h2T, scT, tau, cn, top2, u, vT, x1, ts, rows_per_step):
    d, t = h2T.shape
    nsc = scT.shape[0]
    ne = u.shape[0]
    eb = rows_per_step * PEER_KEYS
    nb = ne // eb
    nhk = PEER_HEADS * PEER_KEYS
    col = lambda n: pl.BlockSpec((n, ts), lambda i, e: (0, i))
    tile = pl.BlockSpec((ts, d), lambda i, e: (i, 0))
    return pl.pallas_call(
        functools.partial(_experts_body, rows_per_step=rows_per_step, n_blocks=nb),
        grid=(t // ts, nb),
        in_specs=[col(d), col(nsc), col(PEER_HEADS), col(PEER_HEADS), col(top2.shape[0]),
                  pl.BlockSpec((eb, d), lambda i, e: (e, 0)), pl.BlockSpec((1, d, eb), lambda i, e: (e, 0, 0)), tile],
        out_specs=tile,
        out_shape=jax.ShapeDtypeStruct((t, d), _F32),
        scratch_shapes=[pltpu.VMEM((d, ts), _F32), pltpu.VMEM((nhk, ts), _BF), pltpu.VMEM((nhk, ts), _BF),
                        pltpu.VMEM((nhk, ts), _F32), pltpu.VMEM((nhk, ts), _F32),
                        pltpu.VMEM((eb, ts), _F32), pltpu.VMEM((eb, ts), _BF),
                        pltpu.VMEM((2 * rows_per_step * PEER_HEADS, ts), _F32)],
        compiler_params=pltpu.CompilerParams(
            dimension_semantics=("arbitrary", "arbitrary"), vmem_limit_bytes=VMEM_LIMIT_BYTES,
            allow_input_fusion=[False, False, False, False, False, True, False, False]),
        name="experts",
    )(h2T, scT, tau, cn, top2, u, vT, x1)


def _block_diag(w):
    n, c, d = w.shape
    return jnp.einsum("ncd,nm->ncmd", w, jnp.eye(n, dtype=w.dtype)).reshape(n * c, n * d)


def _rope_tables(s, dim, reps):
    half = dim // 2
    freq = ROPE_THETA ** (-jnp.arange(half, dtype=_F32) / half)
    ang = jnp.arange(s, dtype=_F32)[:, None] * freq[None, :]
    cos = jnp.tile(jnp.concatenate([jnp.cos(ang), jnp.cos(ang)], axis=1), (1, reps))
    sin = jnp.tile(jnp.concatenate([-jnp.sin(ang), jnp.sin(ang)], axis=1), (1, reps))
    return cos, sin


def _tile(n, pref):
    return pref if n % pref == 0 else n


def _layer(x, mem, p, l):
    b, s, d = x.shape
    bf = lambda a: a.astype(_BF)
    w_in = p["w_in"][l]
    n_q, n_kv = ATT_HEADS * LANES, ATT_KV_HEADS * LANES
    n_qi = IDX_HEADS * IDX_DIM
    sizes = (d, d, n_q, n_kv, n_kv, n_qi, IDX_DIM, IDX_HEADS, d, 3 * d)
    offs = [0]
    for n in sizes:
        offs.append(offs[-1] + n)
    wlx, wlg, wq, wk, wv, wqi, wki, wwi, wmq, wgates = (w_in[:, offs[i]:offs[i + 1]] for i in range(len(sizes)))
    wki2 = jnp.concatenate([wki, wki], axis=1)
    wwi_p = jnp.pad(wwi, ((0, 0), (0, LANES - IDX_HEADS)))
    wg0, wg1, wg2 = wgates[:, :d], wgates[:, d:2 * d], wgates[:, 2 * d:]
    nm = p["norm_mix"][l]

    ts_a = _tile(s, 512)
    m_lru = _lru_branch(x, nm, bf(wlx), bf(wlg), bf(wg0), p["conv_w"][l], p["conv_b"][l],
                        bf(_block_diag(p["lru_wa"][l])), p["lru_ba"][l].reshape(-1),
                        bf(_block_diag(p["lru_wi"][l])), p["lru_bi"][l].reshape(-1), p["lru_lambda"][l], ts_a)

    tk = _tile(s, 512)
    c128, s128 = _rope_tables(s, LANES, 1)
    c64, s64 = _rope_tables(s, IDX_DIM, LANES // IDX_DIM)
    ikn2 = jnp.concatenate([p["idx_k_norm"][l]] * 2)
    qT, k, vT, qiT, ki, wT, g1 = _dsa_proj(x, nm, bf(wq), bf(wk), bf(wv), bf(wqi), bf(wki2), bf(wwi_p), bf(wg1),
                                           p["q_norm"][l], p["k_norm"][l], ikn2, c128, s128, c64, s64, tk)
    m_att = _dsa(qT, qiT, wT, k, vT, ki, g1, min(TOPK_MAX, s // 4), _tile(s, 256), tk)

    mkT, mv = _mem_kv(mem, p["mem_norm"][l], bf(p["w_mem_kv"][l]), p["mem_k_norm"][l])
    m_mem = _mem_branch(x, nm, bf(wmq), bf(wg2), p["mem_q_norm"][l], mkT, mv, _tile(s, 512))

    t = b * s
    flat = lambda a: a.reshape(t, d)
    sk = bf(p["peer_subkeys"][l].reshape(2 * PEER_HEADS, PEER_KEYS, -1))
    x1, h2T, scT = _mid(flat(x), flat(m_lru), flat(m_att), flat(m_mem), bf(p["w_out"][l]), p["norm_ffn"][l],
                        bf(p["peer_wq"][l].T), sk, _tile(t, 512))
    tau, cn, top2 = _route(scT, _tile(t, 256))
    rows_per_step = 16
    vT = bf(p["peer_v"][l]).reshape(-1, rows_per_step * PEER_KEYS, d).transpose(0, 2, 1)
    out = _experts(h2T, scT, tau, cn, top2, bf(p["peer_u"][l]), vT, x1, _tile(t, 512), rows_per_step)
    return out.reshape(b, s, d)


def kernel(x, mem, norm_mix, w_in, conv_w, conv_b, lru_wa, lru_ba, lru_wi, lru_bi, lru_lambda, q_norm, k_norm,
           idx_k_norm, mem_norm, w_mem_kv, mem_q_norm, mem_k_norm, w_out, norm_ffn, peer_wq, peer_subkeys,
           peer_u, peer_v):
    p = dict(norm_mix=norm_mix, w_in=w_in, conv_w=conv_w, conv_b=conv_b, lru_wa=lru_wa, lru_ba=lru_ba,
             lru_wi=lru_wi, lru_bi=lru_bi, lru_lambda=lru_lambda, q_norm=q_norm, k_norm=k_norm,
             idx_k_norm=idx_k_norm, mem_norm=mem_norm, w_mem_kv=w_mem_kv, mem_q_norm=mem_q_norm,
             mem_k_norm=mem_k_norm, w_out=w_out, norm_ffn=norm_ffn, peer_wq=peer_wq, peer_subkeys=peer_subkeys,
             peer_u=peer_u, peer_v=peer_v)
    for l in range(norm_mix.shape[0]):
        x = _layer(x, mem, p, l)
    return x
```
